```python
import math
import jax, jax.numpy as jnp
from jax import lax
import numpy as np

D_MODEL = 1024
BATCH = 4
SEQ = 4096
DEPTH = 2
DEC_BATCH = 32
DEC_SEQ = 16
PAST_LEN = 4096

CHUNK = 64
N_A_LAYERS = max(1, DEPTH // 2)
N_B_LAYERS = DEPTH - N_A_LAYERS
RMS_EPS = 1e-6
FFN_DIM = 2816
FFN_RES = 0.5
MLA_HEADS = 16
MLA_Q_LORA = 384
MLA_KV_LORA = 256
MLA_NOPE = 64
MLA_ROPE = 32
MLA_V = 64
ROPE_BASE = 10000.0
Q_BLOCK = 128
SWA_HEADS = 16
SWA_KV_HEADS = 4
SWA_HEAD_DIM = 64
SWA_REP = SWA_HEADS // SWA_KV_HEADS
WINDOW = 128
WINDOW_CHUNKS = WINDOW // CHUNK
N_BUCKETS = 32
MAX_DISTANCE = 128
NEG_INF = -1e30

kernel_name = "yoco_mla_swa_sink_macaron_step"


def rms_norm(x, g):
    xf = x.astype(jnp.float32)
    y = xf * lax.rsqrt(jnp.mean(xf * xf, axis=-1, keepdims=True) + RMS_EPS)
    return (y * g.astype(jnp.float32)).astype(x.dtype)


def swiglu(x, w_gate, w_up, w_down):
    return (jax.nn.silu(x @ w_gate) * (x @ w_up)) @ w_down


def rope_angles(pos):
    inv = ROPE_BASE ** (-jnp.arange(0, MLA_ROPE, 2, dtype=jnp.float32) / MLA_ROPE)
    ang = pos.astype(jnp.float32)[:, None] * inv[None, :]
    return jnp.cos(ang), jnp.sin(ang)


def apply_rope(x, cos, sin):
    xf = x.astype(jnp.float32)
    x1, x2 = jnp.split(xf, 2, axis=-1)
    return jnp.concatenate([x1 * cos - x2 * sin, x1 * sin + x2 * cos], axis=-1).astype(x.dtype)


def rel_bucket(rel):
    half = N_BUCKETS // 2
    max_exact = half // 2
    base = jnp.where(rel > 0, half, 0)
    n = jnp.abs(rel)
    nf = jnp.maximum(n, 1).astype(jnp.float32)
    large = max_exact + (jnp.log(nf / max_exact) / math.log(MAX_DISTANCE / max_exact)
                         * (half - max_exact)).astype(jnp.int32)
    large = jnp.minimum(large, half - 1)
    return base + jnp.where(n < max_exact, n, large)


def relative_bias(rel, table):
    return jnp.moveaxis(table[rel_bucket(rel)].astype(jnp.float32), -1, 0)[None]


def sink_attention(q, k, v, bias, valid, sinks):
    n_blk, _, n_q, n_k = bias.shape
    bias = bias.reshape(n_blk, SWA_KV_HEADS, SWA_REP, n_q, n_k)
    s = jnp.einsum('bnqgrd,bnkgd->bngrqk', q, k).astype(jnp.float32) * (SWA_HEAD_DIM ** -0.5) + bias[None]
    s = jnp.where(valid[None, :, None, None], s, NEG_INF)
    sink = sinks.astype(jnp.float32).reshape(1, 1, SWA_KV_HEADS, SWA_REP, 1, 1)
    m = jnp.maximum(jnp.max(s, axis=-1, keepdims=True), sink)
    p = jnp.exp(s - m)
    p = p / (jnp.sum(p, axis=-1, keepdims=True) + jnp.exp(sink - m))
    return jnp.einsum('bngrqk,bnkgd->bnqgrd', p.astype(v.dtype), v)


def swa_prompt(q, k, v, rel_table, sinks):
    b_, s_, _ = q.shape
    n_chunks = s_ // CHUNK
    pad = WINDOW_CHUNKS * CHUNK
    band_len = pad + CHUNK
    qb = q.reshape(b_, n_chunks, CHUNK, SWA_KV_HEADS, SWA_REP, SWA_HEAD_DIM)

    def band(t):
        t = jnp.pad(t, ((0, 0), (pad, 0), (0, 0), (0, 0)))
        t = t.reshape(b_, n_chunks + WINDOW_CHUNKS, CHUNK, SWA_KV_HEADS, SWA_HEAD_DIM)
        return jnp.concatenate([t[:, j:j + n_chunks] for j in range(WINDOW_CHUNKS + 1)], axis=2)

    off_k = jnp.arange(band_len) - pad
    bias = relative_bias(off_k[None, :] - jnp.arange(CHUNK)[:, None], rel_table)
    kpos = jnp.arange(n_chunks)[:, None] * CHUNK + off_k[None, :]
    valid = (kpos >= 0)[:, None, :]
    o = sink_attention(qb, band(k), band(v), bias, valid, sinks)
    return o.reshape(b_, s_, SWA_HEADS * SWA_HEAD_DIM)


def swa_step(q, k_all, v_all, qpos, kpos, rel_table, sinks):
    b_, s_, _ = q.shape
    qc, kc = qpos // CHUNK, kpos // CHUNK
    valid = ((kc[None, :] <= qc[:, None]) & (kc[None, :] >= qc[:, None] - WINDOW_CHUNKS))[None]
    bias = relative_bias(kpos[None, :] - qpos[:, None], rel_table)
    qb = q.reshape(b_, 1, s_, SWA_KV_HEADS, SWA_REP, SWA_HEAD_DIM)
    o = sink_attention(qb, k_all[:, None], v_all[:, None], bias, valid, sinks)
    return o.reshape(b_, s_, SWA_HEADS * SWA_HEAD_DIM)


def mla_attention(q_lat, q_rope, c_kv, k_rope, qpos, kpos):
    scale = (MLA_NOPE + MLA_ROPE) ** -0.5
    kc = kpos // CHUNK

    def block(args):
        ql, qr, qp = args
        s = (jnp.einsum('bqhc,bkc->bhqk', ql, c_kv)
             + jnp.einsum('bqhr,bkr->bhqk', qr, k_rope)).astype(jnp.float32) * scale
        mask = kc[None, :] <= (qp // CHUNK)[:, None]
        p = jax.nn.softmax(jnp.where(mask[None, None], s, NEG_INF), axis=-1)
        return jnp.einsum('bhqk,bkc->bqhc', p.astype(c_kv.dtype), c_kv)

    sq = q_lat.shape[1]
    if sq > Q_BLOCK:
        nb = sq // Q_BLOCK
        split = lambda t: jnp.moveaxis(t.reshape(t.shape[0], nb, Q_BLOCK, *t.shape[2:]), 1, 0)
        out = lax.map(block, (split(q_lat), split(q_rope), qpos.reshape(nb, Q_BLOCK)))
        return jnp.moveaxis(out, 0, 1).reshape(q_lat.shape)
    return block((q_lat, q_rope, qpos))


def mla_mixer(h, pos, past_len, cache_lat, cache_rope, w_dq, q_norm, w_uq, w_dkv, kv_norm, w_uk, w_uv, w_o):
    b_, s_, _ = h.shape
    cq = rms_norm(h @ w_dq, q_norm)
    q = (cq @ w_uq).reshape(b_, s_, MLA_HEADS, MLA_NOPE + MLA_ROPE)
    q_nope, q_rope = q[..., :MLA_NOPE], q[..., MLA_NOPE:]
    ckv = h @ w_dkv
    lat = rms_norm(ckv[..., :MLA_KV_LORA], kv_norm)
    cos, sin = rope_angles(pos)
    k_rope = apply_rope(ckv[..., MLA_KV_LORA:], cos, sin)
    q_rope = apply_rope(q_rope, cos[:, None], sin[:, None])
    q_lat = jnp.einsum('bshn,chn->bshc', q_nope, w_uk)
    if cache_lat is None:
        all_lat, all_rope, kpos = lat, k_rope, pos
    else:
        all_lat = jnp.concatenate([cache_lat, lat], axis=1)
        all_rope = jnp.concatenate([cache_rope, k_rope], axis=1)
        kpos = jnp.arange(past_len + s_)
    o_lat = mla_attention(q_lat, q_rope, all_lat, all_rope, pos, kpos)
    o = jnp.einsum('bshc,chv->bshv', o_lat, w_uv).reshape(b_, s_, MLA_HEADS * MLA_V)
    return o @ w_o, lat, k_rope


def trunk(x, past_len, cache_lat, cache_rope, cache_k, cache_v, P):
    b_, s_, _ = x.shape
    pos = past_len + jnp.arange(s_)
    new_lat, new_rope = [], []
    kv_k = kv_v = None
    for layer in range(DEPTH):
        x = x + FFN_RES * swiglu(rms_norm(x, P['ffn_norm1'][layer]), P['ffn1_w_gate'][layer],
                                 P['ffn1_w_up'][layer], P['ffn1_w_down'][layer])
        h = rms_norm(x, P['mix_norm'][layer])
        if layer < N_A_LAYERS:
            a = layer
            y, lat, kr = mla_mixer(
                h, pos, past_len,
                None if cache_lat is None else cache_lat[a],
                None if cache_rope is None else cache_rope[a],
                P['mla_w_dq'][a], P['mla_q_norm'][a], P['mla_w_uq'][a], P['mla_w_dkv'][a],
                P['mla_kv_norm'][a], P['mla_w_uk'][a], P['mla_w_uv'][a], P['mla_w_o'][a])
            new_lat.append(lat)
            new_rope.append(kr)
        else:
            bl = layer - N_A_LAYERS
            q = h @ P['swa_w_q'][bl]
            if cache_k is None:
                o = swa_prompt(q, kv_k, kv_v, P['rel_bias'], P['swa_sinks'][bl])
            else:
                w_c = cache_k.shape[1]
                k_all = jnp.concatenate([cache_k, kv_k], axis=1)
                v_all = jnp.concatenate([cache_v, kv_v], axis=1)
                kpos = jnp.arange(past_len - w_c, past_len + s_)
                o = swa_step(q, k_all, v_all, pos, kpos, P['rel_bias'], P['swa_sinks'][bl])
            y = o @ P['swa_w_o'][bl]
        x = x + y
        x = x + FFN_RES * swiglu(rms_norm(x, P['ffn_norm2'][layer]), P['ffn2_w_gate'][layer],
                                 P['ffn2_w_up'][layer], P['ffn2_w_down'][layer])
        if layer == N_A_LAYERS - 1:
            kv = (rms_norm(x, P['kv_norm']) @ P['w_kv_shared']).reshape(
                b_, s_, 2, SWA_KV_HEADS, SWA_HEAD_DIM)
            kv_k, kv_v = kv[:, :, 0], kv[:, :, 1]
    y = rms_norm(x, P['final_norm'])
    return y, jnp.stack(new_lat), jnp.stack(new_rope), kv_k, kv_v


def setup_inputs(seed: int = 0) -> dict:
    key = jax.random.key(seed)
    ks = iter(jax.random.split(key, 32))
    nrm = lambda shape, scale: jax.random.normal(next(ks), shape, jnp.float32) * scale
    gain = lambda shape: 1.0 + nrm(shape, 0.02)
    swa_cache = min(WINDOW, PAST_LEN)
    D, F = D_MODEL, FFN_DIM
    return {
        "x_prompt": nrm((BATCH, SEQ, D), 1.0),
        "x_sample": nrm((DEC_BATCH, DEC_SEQ, D), 1.0),
        "cache_mla_latent": nrm((N_A_LAYERS, DEC_BATCH, PAST_LEN, MLA_KV_LORA), 1.0),
        "cache_mla_krope": nrm((N_A_LAYERS, DEC_BATCH, PAST_LEN, MLA_ROPE), 1.0),
        "cache_swa_k": nrm((DEC_BATCH, swa_cache, SWA_KV_HEADS, SWA_HEAD_DIM), 1.0),
        "cache_swa_v": nrm((DEC_BATCH, swa_cache, SWA_KV_HEADS, SWA_HEAD_DIM), 1.0),
        "ffn_norm1": gain((DEPTH, D)),
        "ffn1_w_gate": nrm((DEPTH, D, F), D ** -0.5),
        "ffn1_w_up": nrm((DEPTH, D, F), D ** -0.5),
        "ffn1_w_down": nrm((DEPTH, F, D), F ** -0.5),
        "mix_norm": gain((DEPTH, D)),
        "ffn_norm2": gain((DEPTH, D)),
        "ffn2_w_gate": nrm((DEPTH, D, F), D ** -0.5),
        "ffn2_w_up": nrm((DEPTH, D, F), D ** -0.5),
        "ffn2_w_down": nrm((DEPTH, F, D), F ** -0.5),
        "mla_w_dq": nrm((N_A_LAYERS, D, MLA_Q_LORA), D ** -0.5),
        "mla_q_norm": gain((N_A_LAYERS, MLA_Q_LORA)),
        "mla_w_uq": nrm((N_A_LAYERS, MLA_Q_LORA, MLA_HEADS * (MLA_NOPE + MLA_ROPE)), MLA_Q_LORA ** -0.5),
        "mla_w_dkv": nrm((N_A_LAYERS, D, MLA_KV_LORA + MLA_ROPE), D ** -0.5),
        "mla_kv_norm": gain((N_A_LAYERS, MLA_KV_LORA)),
        "mla_w_uk": nrm((N_A_LAYERS, MLA_KV_LORA, MLA_HEADS, MLA_NOPE), MLA_KV_LORA ** -0.5),
        "mla_w_uv": nrm((N_A_LAYERS, MLA_KV_LORA, MLA_HEADS, MLA_V), MLA_KV_LORA ** -0.5),
        "mla_w_o": nrm((N_A_LAYERS, MLA_HEADS * MLA_V, D), (MLA_HEADS * MLA_V) ** -0.5),
        "kv_norm": gain((D,)),
        "w_kv_shared": nrm((D, 2 * SWA_KV_HEADS * SWA_HEAD_DIM), D ** -0.5),
        "swa_w_q": nrm((N_B_LAYERS, D, SWA_HEADS * SWA_HEAD_DIM), D ** -0.5),
        "swa_sinks": nrm((N_B_LAYERS, SWA_HEADS), 1.0),
        "swa_w_o": nrm((N_B_LAYERS, SWA_HEADS * SWA_HEAD_DIM, D), (SWA_HEADS * SWA_HEAD_DIM) ** -0.5),
        "rel_bias": nrm((N_BUCKETS, SWA_HEADS), 0.3),
        "final_norm": gain((D,)),
    }


def reference(x_prompt, x_sample, cache_mla_latent, cache_mla_krope, cache_swa_k, cache_swa_v,
              ffn_norm1, ffn1_w_gate, ffn1_w_up, ffn1_w_down, mix_norm,
              ffn_norm2, ffn2_w_gate, ffn2_w_up, ffn2_w_down,
              mla_w_dq, mla_q_norm, mla_w_uq, mla_w_dkv, mla_kv_norm, mla_w_uk, mla_w_uv, mla_w_o,
              kv_norm, w_kv_shared, swa_w_q, swa_sinks, swa_w_o, rel_bias, final_norm):
    P = dict(ffn_norm1=ffn_norm1, ffn1_w_gate=ffn1_w_gate, ffn1_w_up=ffn1_w_up, ffn1_w_down=ffn1_w_down,
             mix_norm=mix_norm, ffn_norm2=ffn_norm2, ffn2_w_gate=ffn2_w_gate, ffn2_w_up=ffn2_w_up,
             ffn2_w_down=ffn2_w_down, mla_w_dq=mla_w_dq, mla_q_norm=mla_q_norm, mla_w_uq=mla_w_uq,
             mla_w_dkv=mla_w_dkv, mla_kv_norm=mla_kv_norm, mla_w_uk=mla_w_uk, mla_w_uv=mla_w_uv,
             mla_w_o=mla_w_o, kv_norm=kv_norm, w_kv_shared=w_kv_shared, swa_w_q=swa_w_q,
             swa_sinks=swa_sinks, swa_w_o=swa_w_o, rel_bias=rel_bias, final_norm=final_norm)
    y_prompt, lat_p, rope_p, k_p, v_p = trunk(x_prompt, 0, None, None, None, None, P)
    s_p = x_prompt.shape[1]
    keep = min(WINDOW, s_p)
    new_swa_k_prompt = k_p[:, s_p - keep:]
    new_swa_v_prompt = v_p[:, s_p - keep:]
    past_len = cache_mla_latent.shape[2]
    y_sample, lat_s, rope_s, k_s, v_s = trunk(x_sample, past_len, cache_mla_latent, cache_mla_krope,
                                              cache_swa_k, cache_swa_v, P)
    return (y_prompt, y_sample, lat_p, rope_p, new_swa_k_prompt, new_swa_v_prompt,
            lat_s, rope_s, k_s, v_s)
```

```python
import functools
import math

import jax
import jax.numpy as jnp
from jax import lax
from jax.experimental import pallas as pl
from jax.experimental.pallas import tpu as pltpu

F32 = jnp.float32
BF16 = jnp.bfloat16

CHUNK = 64
RMS_EPS = 1e-6
FFN_RES = 0.5
ROPE_BASE = 10000.0
WINDOW = 128
WINDOW_CHUNKS = WINDOW // CHUNK
N_BUCKETS = 32
MAX_DISTANCE = 128
NEG_INF = -1e30
MLA_NOPE = 64
MLA_ROPE = 32
SWA_HEAD_DIM = 64

LANES = 128
ROW_TILE = 512
VMEM_LIMIT = 56 * 1024 * 1024


def _params(n_axes, vmem=VMEM_LIMIT):
    return pltpu.CompilerParams(dimension_semantics=("arbitrary",) * n_axes, vmem_limit_bytes=vmem)


def _rms(xf, g):
    return xf * lax.rsqrt(jnp.mean(xf * xf, axis=-1, keepdims=True) + RMS_EPS) * g


def _dot(a, b):
    return jnp.dot(a, b, preferred_element_type=F32)


def _dot_nt(a, b):
    return lax.dot_general(a, b, (((1,), (1,)), ((), ())), preferred_element_type=F32)


def _row_spec(tm, ncols):
    return pl.BlockSpec((tm, ncols), lambda i: (i, 0))


def _const_spec(shape):
    nd = len(shape)
    return pl.BlockSpec(shape, lambda i: (0,) * nd, pipeline_mode=pl.Buffered(1))


def _row_tile(t):
    tm = min(ROW_TILE, t)
    assert t % tm == 0, (t, tm)
    return tm


def _ffn_kernel(x_ref, g_ref, wg_ref, wu_ref, wd_ref, *rest, final_norm):
    o_ref = rest[-1]
    x = x_ref[...]
    h = _rms(x, g_ref[...]).astype(BF16)
    gate = _dot(h, wg_ref[...])
    up = _dot(h, wu_ref[...])
    a = (gate * jax.nn.sigmoid(gate) * up).astype(BF16)
    out = x + FFN_RES * _dot(a, wd_ref[...])
    if final_norm:
        out = _rms(out, rest[0][...])
    o_ref[...] = out


def _ffn(x, g, wg, wu, wd, final_g=None):
    t, d = x.shape
    f = wg.shape[1]
    tm = _row_tile(t)
    ins = [x, g, wg, wu, wd]
    specs = [_row_spec(tm, d), _const_spec((1, d)), _const_spec((d, f)), _const_spec((d, f)),
             _const_spec((f, d))]
    if final_g is not None:
        ins.append(final_g)
        specs.append(_const_spec((1, d)))
    return pl.pallas_call(
        functools.partial(_ffn_kernel, final_norm=final_g is not None),
        grid=(t // tm,), in_specs=specs, out_specs=_row_spec(tm, d),
        out_shape=jax.ShapeDtypeStruct((t, d), F32), compiler_params=_params(1), name="ffn",
    )(*ins)


def _norm_linear_kernel(x_ref, g_ref, *refs, n, scales):
    h = _rms(x_ref[...], g_ref[...]).astype(BF16)
    for w_ref, o_ref, sc in zip(refs[:n], refs[n:], scales):
        y = _dot(h, w_ref[...])
        if sc != 1.0:
            y = y * sc
        o_ref[...] = y.astype(o_ref.dtype)


def _norm_linear(x, g, ws, dtypes, scales, name):
    t, d = x.shape
    tm = _row_tile(t)
    n = len(ws)
    return pl.pallas_call(
        functools.partial(_norm_linear_kernel, n=n, scales=tuple(scales)),
        grid=(t // tm,),
        in_specs=[_row_spec(tm, d), _const_spec((1, d))] + [_const_spec(w.shape) for w in ws],
        out_specs=[_row_spec(tm, w.shape[1]) for w in ws],
        out_shape=[jax.ShapeDtypeStruct((t, w.shape[1]), dt) for w, dt in zip(ws, dtypes)],
        compiler_params=_params(1), name=name,
    )(x, g, *ws)


def _lin_res_kernel(a_ref, w_ref, x_ref, o_ref):
    o_ref[...] = x_ref[...] + _dot(a_ref[...], w_ref[...])


def _lin_res(a, w, x, name):
    t, d = x.shape
    k = a.shape[1]
    tm = _row_tile(t)
    return pl.pallas_call(
        _lin_res_kernel, grid=(t // tm,),
        in_specs=[_row_spec(tm, k), _const_spec((k, d)), _row_spec(tm, d)],
        out_specs=_row_spec(tm, d), out_shape=jax.ShapeDtypeStruct((t, d), F32),
        compiler_params=_params(1), name=name,
    )(a, w, x)


def _mla_common(x_ref, mg_ref, wdq_ref, qn_ref, wlat_ref, kvn_ref, wkr_ref, wkrr_ref, tab_k):
    h = _rms(x_ref[...], mg_ref[...]).astype(BF16)
    cq = _rms(_dot(h, wdq_ref[...]), qn_ref[...]).astype(BF16)
    lat = _rms(_dot(h, wlat_ref[...]), kvn_ref[...])
    ck, sk = tab_k
    kr = _dot(h, wkr_ref[...]) * ck + _dot(h, wkrr_ref[...]) * sk
    return cq, lat, kr


def _mla_proj_prompt_kernel(x_ref, mg_ref, wdq_ref, qn_ref, wq_ref, wqr_ref, wlat_ref, kvn_ref,
                            wkr_ref, wkrr_ref, wuk_ref, wuv_ref, tab_ref,
                            q_ref, k_ref, v_ref, lat_ref, kr_ref, *, heads):
    cq_t, sq_t = tab_ref[:, 0:LANES], tab_ref[:, LANES:2 * LANES]
    tab_k = (tab_ref[:, 2 * LANES:3 * LANES], tab_ref[:, 3 * LANES:4 * LANES])
    cq, lat, kr = _mla_common(x_ref, mg_ref, wdq_ref, qn_ref, wlat_ref, kvn_ref, wkr_ref, wkrr_ref, tab_k)
    lat_ref[...] = lat
    kr_ref[...] = kr[:, MLA_NOPE:MLA_NOPE + MLA_ROPE]
    qa = _dot(cq, wq_ref[...])
    qb = _dot(cq, wqr_ref[...])
    latb = lat.astype(BF16)
    kn = _dot(latb, wuk_ref[...])
    for hh in range(heads):
        sl = slice(hh * LANES, (hh + 1) * LANES)
        q_ref[:, sl] = (qa[:, sl] * cq_t + qb[:, sl] * sq_t).astype(BF16)
        k_ref[:, sl] = (kn[:, sl] + kr).astype(BF16)
    v_ref[...] = _dot(latb, wuv_ref[...]).astype(BF16)


def _mla_proj_decode_kernel(x_ref, mg_ref, wdq_ref, qn_ref, wq_ref, wqrope_ref, wqroper_ref, wlat_ref,
                            kvn_ref, wkr_ref, wkrr_ref, wukd_ref, tab_ref, tabr_ref,
                            ql_ref, qr_ref, lat_ref, kr_ref, *, heads, scale, c_dim):
    tab_k = (tab_ref[:, 2 * LANES:3 * LANES], tab_ref[:, 3 * LANES:4 * LANES])
    cq, lat, kr = _mla_common(x_ref, mg_ref, wdq_ref, qn_ref, wlat_ref, kvn_ref, wkr_ref, wkrr_ref, tab_k)
    lat_ref[...] = lat
    kr_ref[...] = kr[:, MLA_NOPE:MLA_NOPE + MLA_ROPE]
    qp = (_dot(cq, wq_ref[...]) * scale).astype(BF16)
    for hh in range(heads):
        ql_ref[:, hh * c_dim:(hh + 1) * c_dim] = _dot(qp[:, hh * LANES:(hh + 1) * LANES], wukd_ref[hh]).astype(BF16)
    nr = heads * MLA_ROPE
    qr_ref[...] = (_dot(cq, wqrope_ref[...]) * tabr_ref[:, 0:nr]
                   + _dot(cq, wqroper_ref[...]) * tabr_ref[:, nr:2 * nr]).astype(BF16)


def _mla_attn_kernel(q_ref, k_ref, v_ref, o_ref, *, blk):
    i = pl.program_id(2)
    q = q_ref[...]

    def step(j, carry, masked):
        m, l, acc = carry
        start = pl.multiple_of(j * blk, blk)
        s = _dot_nt(q, k_ref[pl.ds(start, blk), :])
        if masked:
            qc = lax.broadcasted_iota(jnp.int32, (blk, 1), 0) // CHUNK
            kc = lax.broadcasted_iota(jnp.int32, (1, blk), 1) // CHUNK
            s = jnp.where(kc <= qc, s, NEG_INF)
        m_new = jnp.maximum(m, jnp.max(s, axis=-1, keepdims=True))
        alpha = jnp.exp(m - m_new)
        p = jnp.exp(s - m_new)
        l = alpha * l + jnp.sum(p, axis=-1, keepdims=True)
        acc = alpha * acc + _dot(p.astype(BF16), v_ref[pl.ds(start, blk), :])
        return m_new, l, acc

    init = (jnp.full((blk, 1), NEG_INF, F32), jnp.zeros((blk, 1), F32), jnp.zeros((blk, LANES), F32))
    carry = lax.fori_loop(0, i, functools.partial(step, masked=False), init)
    m, l, acc = step(i, carry, True)
    o_ref[...] = (acc / l).astype(BF16)


def _mla_attn_prompt(q, k, v, batch, seq, heads, blk=256):
    nq = seq // blk
    return pl.pallas_call(
        functools.partial(_mla_attn_kernel, blk=blk),
        grid=(batch, heads, nq),
        in_specs=[pl.BlockSpec((blk, LANES), lambda b, h, i: (b * nq + i, h)),
                  pl.BlockSpec((seq, LANES), lambda b, h, i: (b, h)),
                  pl.BlockSpec((seq, LANES), lambda b, h, i: (b, h))],
        out_specs=pl.BlockSpec((blk, LANES), lambda b, h, i: (b * nq + i, h)),
        out_shape=jax.ShapeDtypeStruct(q.shape, BF16),
        compiler_params=_params(3), name="mla_attn_prompt",
    )(q, k, v)


def _mla_attn_decode_kernel(ql_ref, qr_ref, cl_ref, cr_ref, nl_ref, nr_ref, o_ref, m_sc, l_sc, acc_sc):
    j = pl.program_id(1)

    @pl.when(j == 0)
    def _():
        m_sc[...] = jnp.full(m_sc.shape, NEG_INF, F32)
        l_sc[...] = jnp.zeros(l_sc.shape, F32)
        acc_sc[...] = jnp.zeros(acc_sc.shape, F32)

    def update(kl, kr):
        s = _dot_nt(ql_ref[...], kl) + _dot_nt(qr_ref[...], kr)
        m = m_sc[...]
        m_new = jnp.maximum(m, jnp.max(s, axis=-1, keepdims=True))
        alpha = jnp.exp(m - m_new)
        p = jnp.exp(s - m_new)
        l_sc[...] = alpha * l_sc[...] + jnp.sum(p, axis=-1, keepdims=True)
        acc_sc[...] = alpha * acc_sc[...] + _dot(p.astype(BF16), kl)
        m_sc[...] = m_new

    update(cl_ref[...].astype(BF16), cr_ref[...].astype(BF16))

    @pl.when(j == pl.num_programs(1) - 1)
    def _():
        update(nl_ref[...].astype(BF16), nr_ref[...].astype(BF16))
        o_ref[...] = (acc_sc[...] / l_sc[...]).astype(BF16)


def _mla_attn_decode(ql, qr, cache_lat, cache_rope, new_lat, new_rope, rows, n_new, blk=1024):
    batch, past, c_dim = cache_lat.shape
    r_dim = cache_rope.shape[2]
    blk = min(blk, past)
    assert past % blk == 0
    return pl.pallas_call(
        _mla_attn_decode_kernel,
        grid=(batch, past // blk),
        in_specs=[pl.BlockSpec((rows, c_dim), lambda b, j: (b, 0)),
                  pl.BlockSpec((rows, r_dim), lambda b, j: (b, 0)),
                  pl.BlockSpec((None, blk, c_dim), lambda b, j: (b, j, 0)),
                  pl.BlockSpec((None, blk, r_dim), lambda b, j: (b, j, 0)),
                  pl.BlockSpec((n_new, c_dim), lambda b, j: (b, 0)),
                  pl.BlockSpec((n_new, r_dim), lambda b, j: (b, 0))],
        out_specs=pl.BlockSpec((rows, c_dim), lambda b, j: (b, 0)),
        out_shape=jax.ShapeDtypeStruct(ql.shape, BF16),
        scratch_shapes=[pltpu.VMEM((rows, 1), F32), pltpu.VMEM((rows, 1), F32), pltpu.VMEM((rows, c_dim), F32)],
        compiler_params=_params(2), name="mla_attn_decode",
    )(ql, qr, cache_lat, cache_rope, new_lat, new_rope)


def _mla_out_decode_kernel(ol_ref, wuv_ref, wo_ref, x_ref, o_ref, *, heads, c_dim):
    parts = [_dot(ol_ref[:, hh * c_dim:(hh + 1) * c_dim], wuv_ref[hh]).astype(BF16) for hh in range(heads)]
    o_ref[...] = x_ref[...] + _dot(jnp.concatenate(parts, axis=1), wo_ref[...])


def _bias_kernel(tab_ref, idx_ref, o_ref, *, heads):
    idx = idx_ref[...]
    for hh in range(heads):
        acc = jnp.zeros(idx.shape, F32)
        for b in range(N_BUCKETS):
            acc = jnp.where(idx == b, tab_ref[b, hh], acc)
        o_ref[hh] = acc


def _bias_table(rel_bias, idx):
    heads = rel_bias.shape[1]
    return pl.pallas_call(
        functools.partial(_bias_kernel, heads=heads),
        in_specs=[pl.BlockSpec(memory_space=pltpu.SMEM), pl.BlockSpec(idx.shape, lambda: (0, 0))],
        out_specs=pl.BlockSpec((heads,) + idx.shape, lambda: (0, 0, 0)),
        out_shape=jax.ShapeDtypeStruct((heads,) + idx.shape, F32), name="rel_bias_table",
    )(rel_bias, idx)


def _rel_bucket(rel):
    half = N_BUCKETS // 2
    max_exact = half // 2
    base = jnp.where(rel > 0, half, 0)
    n = jnp.abs(rel)
    nf = jnp.maximum(n, 1).astype(jnp.float32)
    large = max_exact + (jnp.log(nf / max_exact) / math.log(MAX_DISTANCE / max_exact)
                         * (half - max_exact)).astype(jnp.int32)
    large = jnp.minimum(large, half - 1)
    return base + jnp.where(n < max_exact, n, large)


def _sink_softmax(s, sink):
    m = jnp.maximum(jnp.max(s, axis=-1, keepdims=True), sink)
    p = jnp.exp(s - m)
    return p / (jnp.sum(p, axis=-1, keepdims=True) + jnp.exp(sink - m))


def _swa_prompt_kernel(sink_ref, q_ref, kp_ref, kc_ref, vp_ref, vc_ref, bias_ref, o_ref, *, groups, rep, blk):
    i = pl.program_id(1)
    gw = rep * SWA_HEAD_DIM
    row_chunk = lax.broadcasted_iota(jnp.int32, (blk, 1), 0) // CHUNK
    key_chunk = lax.broadcasted_iota(jnp.int32, (1, 2 * blk), 1) // CHUNK
    first = blk // CHUNK
    valid = ((key_chunk >= row_chunk) & (key_chunk <= row_chunk + WINDOW_CHUNKS)
             & ((i > 0) | (key_chunk >= first)))
    lane_head = lax.broadcasted_iota(jnp.int32, (1, gw), 1) // SWA_HEAD_DIM
    for g in range(groups):
        sl = slice(g * gw, (g + 1) * gw)
        qg = q_ref[:, sl]
        qs = jnp.concatenate([jnp.where(lane_head == r, qg, jnp.zeros_like(qg)) for r in range(rep)], axis=0)
        kg = jnp.concatenate([kp_ref[:, sl], kc_ref[:, sl]], axis=0)
        vg = jnp.concatenate([vp_ref[:, sl], vc_ref[:, sl]], axis=0)
        s = _dot_nt(qs, kg)
        ps = []
        for r in range(rep):
            hh = g * rep + r
            sh = jnp.where(valid, s[r * blk:(r + 1) * blk] + bias_ref[hh], NEG_INF)
            ps.append(_sink_softmax(sh, sink_ref[hh]).astype(BF16))
        res = _dot(jnp.concatenate(ps, axis=0), vg)
        og = jnp.zeros((blk, gw), F32)
        for r in range(rep):
            og = jnp.where(lane_head == r, res[r * blk:(r + 1) * blk], og)
        o_ref[:, sl] = og.astype(BF16)


def _swa_prompt(q, k4, v4, bias, sinks, batch, seq, groups, rep, blk=128):
    nq = seq // blk
    width = q.shape[1]
    cur = lambda b, i: (b * nq + i, 0)
    prev = lambda b, i: (b * nq + jnp.maximum(i - 1, 0), 0)
    return pl.pallas_call(
        functools.partial(_swa_prompt_kernel, groups=groups, rep=rep, blk=blk),
        grid=(batch, nq),
        in_specs=[pl.BlockSpec(memory_space=pltpu.SMEM),
                  pl.BlockSpec((blk, width), cur),
                  pl.BlockSpec((blk, width), prev), pl.BlockSpec((blk, width), cur),
                  pl.BlockSpec((blk, width), prev), pl.BlockSpec((blk, width), cur),
                  pl.BlockSpec(bias.shape, lambda b, i: (0, 0, 0), pipeline_mode=pl.Buffered(1))],
        out_specs=pl.BlockSpec((blk, width), cur),
        out_shape=jax.ShapeDtypeStruct(q.shape, BF16),
        compiler_params=_params(2), name="swa_prompt",
    )(sinks, q, k4, k4, v4, v4, bias)


def _swa_decode_kernel(q_ref, ck_ref, cv_ref, nkv_ref, bias_ref, sink_ref, o_ref, *, w_c, kv_w):
    q = q_ref[...]
    sink = sink_ref[...]
    nk = nkv_ref[:, 0:kv_w].astype(BF16)
    nv = nkv_ref[:, kv_w:2 * kv_w].astype(BF16)
    s1 = _dot_nt(q, ck_ref[...].astype(BF16)) + bias_ref[:, 0:w_c]
    s2 = _dot_nt(q, nk) + bias_ref[:, w_c:]
    m = jnp.maximum(jnp.maximum(jnp.max(s1, axis=-1, keepdims=True), jnp.max(s2, axis=-1, keepdims=True)), sink)
    p1 = jnp.exp(s1 - m)
    p2 = jnp.exp(s2 - m)
    den = jnp.sum(p1, axis=-1, keepdims=True) + jnp.sum(p2, axis=-1, keepdims=True) + jnp.exp(sink - m)
    p1 = (p1 / den).astype(BF16)
    p2 = (p2 / den).astype(BF16)
    o_ref[...] = (_dot(p1, cv_ref[...].astype(BF16)) + _dot(p2, nv)).astype(BF16)


def _rope_tables(pos, scale):
    inv = ROPE_BASE ** (-jnp.arange(0, MLA_ROPE, 2, dtype=jnp.float32) / MLA_ROPE)
    ang = pos.astype(jnp.float32)[:, None] * inv[None, :]
    cos, sin = jnp.cos(ang), jnp.sin(ang)
    n = pos.shape[0]
    c2 = jnp.concatenate([cos, cos], axis=1)
    s2 = jnp.concatenate([sin, sin], axis=1)
    z32 = jnp.zeros((n, LANES - MLA_NOPE - MLA_ROPE), F32)
    z64 = jnp.zeros((n, MLA_NOPE), F32)
    cq = jnp.concatenate([jnp.full((n, MLA_NOPE), scale, F32), c2 * scale, z32], axis=1)
    sq = jnp.concatenate([z64, s2 * scale, z32], axis=1)
    ck = jnp.concatenate([z64, c2, z32], axis=1)
    sk = jnp.concatenate([z64, s2, z32], axis=1)
    return jnp.concatenate([cq, sq, ck, sk], axis=1), c2, s2


def _prep_weights(mla_w_dq, mla_w_uq, mla_w_dkv, mla_w_uk, mla_w_uv, mla_w_o, w_kv_shared, swa_w_q, swa_w_o):
    ql, qcols = mla_w_uq.shape
    c_dim, heads, nope = mla_w_uk.shape
    rope = qcols // heads - nope
    half = rope // 2
    d = mla_w_dq.shape[0]
    pad = LANES - nope - rope
    wq3 = mla_w_uq.reshape(ql, heads, nope + rope)
    w_nope, r1, r2 = wq3[:, :, :nope], wq3[:, :, nope:nope + half], wq3[:, :, nope + half:]
    zq = lambda n: jnp.zeros((ql, heads, n), F32)
    w = {}
    w["wq"] = jnp.concatenate([w_nope, r1, r2, zq(pad)], axis=-1).reshape(ql, heads * LANES)
    w["wq_rot"] = jnp.concatenate([zq(nope), -r2, r1, zq(pad)], axis=-1).reshape(ql, heads * LANES)
    w["wq_rope"] = jnp.concatenate([r1, r2], axis=-1).reshape(ql, heads * rope)
    w["wq_rope_rot"] = jnp.concatenate([-r2, r1], axis=-1).reshape(ql, heads * rope)
    w["wdq"] = mla_w_dq
    w["wlat"] = mla_w_dkv[:, :c_dim]
    k1, k2 = mla_w_dkv[:, c_dim:c_dim + half], mla_w_dkv[:, c_dim + half:]
    zk = lambda n: jnp.zeros((d, n), F32)
    w["wkr"] = jnp.concatenate([zk(nope), k1, k2, zk(pad)], axis=1)
    w["wkr_rot"] = jnp.concatenate([zk(nope), -k2, k1, zk(pad)], axis=1)
    zc = jnp.zeros((c_dim, heads, LANES - nope), F32)
    w["wuk"] = jnp.concatenate([mla_w_uk, zc], axis=-1).reshape(c_dim, heads * LANES)
    w["wuv"] = jnp.concatenate([mla_w_uv, zc], axis=-1).reshape(c_dim, heads * LANES)
    w["wuk_dec"] = jnp.concatenate([jnp.transpose(mla_w_uk, (1, 2, 0)),
                                    jnp.zeros((heads, LANES - nope, c_dim), F32)], axis=1)
    w["wuv_dec"] = jnp.concatenate([jnp.transpose(mla_w_uv, (1, 0, 2)),
                                    jnp.zeros((heads, c_dim, LANES - nope), F32)], axis=-1)
    vdim = mla_w_uv.shape[2]
    w["wo"] = jnp.concatenate([mla_w_o.reshape(heads, vdim, d), jnp.zeros((heads, LANES - vdim, d), F32)],
                              axis=1).reshape(heads * LANES, d)
    kvw = w_kv_shared.shape[1] // 2
    groups = kvw // SWA_HEAD_DIM
    s_heads = swa_w_q.shape[1] // SWA_HEAD_DIM
    rep = s_heads // groups
    wk = w_kv_shared[:, :kvw].reshape(d, groups, 1, SWA_HEAD_DIM)
    wv = w_kv_shared[:, kvw:].reshape(d, groups, 1, SWA_HEAD_DIM)
    w["wkv"] = w_kv_shared
    w["wk4"] = jnp.broadcast_to(wk, (d, groups, rep, SWA_HEAD_DIM)).reshape(d, s_heads * SWA_HEAD_DIM)
    w["wv4"] = jnp.broadcast_to(wv, (d, groups, rep, SWA_HEAD_DIM)).reshape(d, s_heads * SWA_HEAD_DIM)
    w["swa_wq"] = swa_w_q
    w["swa_wo"] = swa_w_o
    onehot = (jnp.arange(s_heads)[:, None] // rep == jnp.arange(groups)[None, :]).astype(F32)
    wq_h = swa_w_q.reshape(d, s_heads, SWA_HEAD_DIM)
    w["swa_wq_dec"] = (wq_h[:, :, None, :] * onehot[None, :, :, None]).reshape(d, s_heads * kvw)
    wo_h = swa_w_o.reshape(s_heads, SWA_HEAD_DIM, d)
    w["swa_wo_dec"] = (wo_h[:, None, :, :] * onehot[:, :, None, None]).reshape(s_heads * kvw, d)
    return {k: v.astype(BF16) for k, v in w.items()}, dict(heads=heads, c_dim=c_dim, groups=groups, rep=rep,
                                                            s_heads=s_heads, kvw=kvw)


def _mla_proj_prompt(x, mg, qn, kvn, w, tab, seq, heads, c_dim):
    t, d = x.shape
    tm = _row_tile(t)
    assert seq % tm == 0
    nb = seq // tm
    hw = heads * LANES
    consts = [mg, w["wdq"], qn, w["wq"], w["wq_rot"], w["wlat"], kvn, w["wkr"], w["wkr_rot"], w["wuk"], w["wuv"]]
    return pl.pallas_call(
        functools.partial(_mla_proj_prompt_kernel, heads=heads),
        grid=(t // tm,),
        in_specs=[_row_spec(tm, d)] + [_const_spec(c.shape) for c in consts]
                 + [pl.BlockSpec((tm, 4 * LANES), lambda i: (i % nb, 0))],
        out_specs=[_row_spec(tm, hw), _row_spec(tm, hw), _row_spec(tm, hw), _row_spec(tm, c_dim),
                   _row_spec(tm, MLA_ROPE)],
        out_shape=[jax.ShapeDtypeStruct((t, hw), BF16)] * 3
                  + [jax.ShapeDtypeStruct((t, c_dim), F32), jax.ShapeDtypeStruct((t, MLA_ROPE), F32)],
        compiler_params=_params(1), name="mla_proj_prompt",
    )(x, *consts, tab)


def _mla_proj_decode(x, mg, qn, kvn, w, tab, tabr, heads, c_dim, scale):
    t, d = x.shape
    tm = _row_tile(t)
    assert tab.shape[0] == tm and t == tm
    consts = [mg, w["wdq"], qn, w["wq"], w["wq_rope"], w["wq_rope_rot"], w["wlat"], kvn, w["wkr"], w["wkr_rot"],
              w["wuk_dec"], tab, tabr]
    return pl.pallas_call(
        functools.partial(_mla_proj_decode_kernel, heads=heads, scale=scale, c_dim=c_dim),
        grid=(t // tm,),
        in_specs=[_row_spec(tm, d)] + [_const_spec(c.shape) for c in consts],
        out_specs=[_row_spec(tm, heads * c_dim), _row_spec(tm, heads * MLA_ROPE), _row_spec(tm, c_dim),
                   _row_spec(tm, MLA_ROPE)],
        out_shape=[jax.ShapeDtypeStruct((t, heads * c_dim), BF16), jax.ShapeDtypeStruct((t, heads * MLA_ROPE), BF16),
                   jax.ShapeDtypeStruct((t, c_dim), F32), jax.ShapeDtypeStruct((t, MLA_ROPE), F32)],
        compiler_params=_params(1), name="mla_proj_decode",
    )(x, *consts)


def _trunk_prompt(x3, P, w, meta):
    batch, seq, d = x3.shape
    heads, c_dim = meta["heads"], meta["c_dim"]
    x = x3.reshape(batch * seq, d)
    row = lambda v: v.reshape(1, -1)
    scale = (MLA_NOPE + MLA_ROPE) ** -0.5
    x = _ffn(x, row(P["ffn_norm1"][0]), w["f1g"][0], w["f1u"][0], w["f1d"][0])
    tab, _, _ = _rope_tables(jnp.arange(seq), scale)
    q, k, v, lat, kr = _mla_proj_prompt(x, row(P["mix_norm"][0]), row(P["mla_q_norm"][0]), row(P["mla_kv_norm"][0]),
                                         w, tab, seq, heads, c_dim)
    o = _mla_attn_prompt(q, k, v, batch, seq, heads)
    x = _lin_res(o, w["wo"], x, "mla_out_prompt")
    x = _ffn(x, row(P["ffn_norm2"][0]), w["f2g"][0], w["f2u"][0], w["f2d"][0])
    kv, k4, v4 = _norm_linear(x, row(P["kv_norm"]), [w["wkv"], w["wk4"], w["wv4"]], [F32, BF16, BF16],
                              [1.0, 1.0, 1.0], "kv_shared_prompt")
    x = _ffn(x, row(P["ffn_norm1"][1]), w["f1g"][1], w["f1u"][1], w["f1d"][1])
    (qs,) = _norm_linear(x, row(P["mix_norm"][1]), [w["swa_wq"]], [BF16], [SWA_HEAD_DIM ** -0.5], "swa_q_prompt")
    blk = 2 * CHUNK
    rel = (jnp.arange(2 * blk) - blk)[None, :] - jnp.arange(blk)[:, None]
    bias = _bias_table(P["rel_bias"], _rel_bucket(rel).astype(jnp.int32))
    o = _swa_prompt(qs, k4, v4, bias, P["swa_sinks"][0], batch, seq, meta["groups"], meta["rep"], blk)
    x = _lin_res(o, w["swa_wo"], x, "swa_out_prompt")
    y = _ffn(x, row(P["ffn_norm2"][1]), w["f2g"][1], w["f2u"][1], w["f2d"][1], row(P["final_norm"]))
    kvw = meta["kvw"]
    keep = min(WINDOW, seq)
    kv3 = kv.reshape(batch, seq, 2 * kvw)[:, seq - keep:]
    new_k = kv3[:, :, :kvw].reshape(batch, keep, meta["groups"], SWA_HEAD_DIM)
    new_v = kv3[:, :, kvw:].reshape(batch, keep, meta["groups"], SWA_HEAD_DIM)
    return (y.reshape(batch, seq, d), lat.reshape(1, batch, seq, c_dim), kr.reshape(1, batch, seq, MLA_ROPE),
            new_k, new_v)


def _trunk_decode(x3, cache_lat, cache_rope, cache_k, cache_v, P, w, meta):
    batch, n_new, d = x3.shape
    heads, c_dim, kvw, s_heads = meta["heads"], meta["c_dim"], meta["kvw"], meta["s_heads"]
    past = cache_lat.shape[1]
    w_c = cache_k.shape[1]
    qpos = past + jnp.arange(n_new)
    assert past % CHUNK == 0 and n_new <= CHUNK and w_c <= WINDOW_CHUNKS * CHUNK and w_c <= past
    t = batch * n_new
    x = x3.reshape(t, d)
    row = lambda v: v.reshape(1, -1)
    scale = (MLA_NOPE + MLA_ROPE) ** -0.5
    x = _ffn(x, row(P["ffn_norm1"][0]), w["f1g"][0], w["f1u"][0], w["f1d"][0])
    tab, c2, s2 = _rope_tables(qpos, scale)
    tab = jnp.tile(tab, (batch, 1))
    tabr = jnp.tile(jnp.concatenate([jnp.tile(c2 * scale, (1, heads)), jnp.tile(s2 * scale, (1, heads))], axis=1),
                    (batch, 1))
    ql, qr, lat, kr = _mla_proj_decode(x, row(P["mix_norm"][0]), row(P["mla_q_norm"][0]), row(P["mla_kv_norm"][0]),
                                       w, tab, tabr, heads, c_dim, scale)
    rows = n_new * heads
    ol = _mla_attn_decode(ql.reshape(t * heads, c_dim), qr.reshape(t * heads, MLA_ROPE), cache_lat, cache_rope,
                          lat, kr, rows, n_new)
    tm = _row_tile(t)
    x = pl.pallas_call(
        functools.partial(_mla_out_decode_kernel, heads=heads, c_dim=c_dim),
        grid=(t // tm,),
        in_specs=[_row_spec(tm, heads * c_dim), _const_spec(w["wuv_dec"].shape), _const_spec(w["wo"].shape),
                  _row_spec(tm, d)],
        out_specs=_row_spec(tm, d), out_shape=jax.ShapeDtypeStruct((t, d), F32),
        compiler_params=_params(1), name="mla_out_decode",
    )(ol.reshape(t, heads * c_dim), w["wuv_dec"], w["wo"], x)
    x = _ffn(x, row(P["ffn_norm2"][0]), w["f2g"][0], w["f2u"][0], w["f2d"][0])
    (kv,) = _norm_linear(x, row(P["kv_norm"]), [w["wkv"]], [F32], [1.0], "kv_shared_decode")
    x = _ffn(x, row(P["ffn_norm1"][1]), w["f1g"][1], w["f1u"][1], w["f1d"][1])
    (qs,) = _norm_linear(x, row(P["mix_norm"][1]), [w["swa_wq_dec"]], [BF16], [SWA_HEAD_DIM ** -0.5], "swa_q_decode")
    kpos = jnp.arange(past - w_c, past + n_new)
    bias_h = _bias_table(P["rel_bias"], _rel_bucket(kpos[None, :] - qpos[:, None]).astype(jnp.int32))
    bias = jnp.transpose(bias_h, (1, 0, 2)).reshape(n_new * s_heads, w_c + n_new)
    sink_rows = jnp.tile(P["swa_sinks"][0], n_new).reshape(n_new * s_heads, 1)
    srows = n_new * s_heads
    o = pl.pallas_call(
        functools.partial(_swa_decode_kernel, w_c=w_c, kv_w=kvw),
        grid=(batch,),
        in_specs=[pl.BlockSpec((srows, kvw), lambda b: (b, 0)),
                  pl.BlockSpec((None, w_c, kvw), lambda b: (b, 0, 0)),
                  pl.BlockSpec((None, w_c, kvw), lambda b: (b, 0, 0)),
                  pl.BlockSpec((n_new, 2 * kvw), lambda b: (b, 0)),
                  _const_spec(bias.shape), _const_spec(sink_rows.shape)],
        out_specs=pl.BlockSpec((srows, kvw), lambda b: (b, 0)),
        out_shape=jax.ShapeDtypeStruct((t * s_heads, kvw), BF16),
        compiler_params=_params(1), name="swa_decode",
    )(qs.reshape(t * s_heads, kvw), cache_k.reshape(batch, w_c, kvw), cache_v.reshape(batch, w_c, kvw), kv,
      bias, sink_rows)
    x = _lin_res(o.reshape(t, s_heads * kvw), w["swa_wo_dec"], x, "swa_out_decode")
    y = _ffn(x, row(P["ffn_norm2"][1]), w["f2g"][1], w["f2u"][1], w["f2d"][1], row(P["final_norm"]))
    kv3 = kv.reshape(batch, n_new, 2 * kvw)
    new_k = kv3[:, :, :kvw].reshape(batch, n_new, meta["groups"], SWA_HEAD_DIM)
    new_v = kv3[:, :, kvw:].reshape(batch, n_new, meta["groups"], SWA_HEAD_DIM)
    return (y.reshape(batch, n_new, d), lat.reshape(1, batch, n_new, c_dim), kr.reshape(1, batch, n_new, MLA_ROPE),
            new_k, new_v)


def kernel(x_prompt, x_sample, cache_mla_latent, cache_mla_krope, cache_swa_k, cache_swa_v, ffn_norm1, ffn1_w_gate, ffn1_w_up, ffn1_w_down, mix_norm, ffn_norm2, ffn2_w_gate, ffn2_w_up, ffn2_w_down, mla_w_dq, mla_q_norm, mla_w_uq, mla_w_dkv, mla_kv_norm, mla_w_uk, mla_w_uv, mla_w_o, kv_norm, w_kv_shared, swa_w_q, swa_sinks, swa_w_o, rel_bias, final_norm):
    assert ffn_norm1.shape[0] == 2 and mla_w_dq.shape[0] == 1 and swa_w_q.shape[0] == 1
    w, meta = _prep_weights(mla_w_dq[0], mla_w_uq[0], mla_w_dkv[0], mla_w_uk[0], mla_w_uv[0], mla_w_o[0],
                            w_kv_shared, swa_w_q[0], swa_w_o[0])
    w.update(f1g=ffn1_w_gate.astype(BF16), f1u=ffn1_w_up.astype(BF16), f1d=ffn1_w_down.astype(BF16),
             f2g=ffn2_w_gate.astype(BF16), f2u=ffn2_w_up.astype(BF16), f2d=ffn2_w_down.astype(BF16))
    P = dict(ffn_norm1=ffn_norm1, mix_norm=mix_norm, ffn_norm2=ffn_norm2, mla_q_norm=mla_q_norm,
             mla_kv_norm=mla_kv_norm, kv_norm=kv_norm, swa_sinks=swa_sinks, rel_bias=rel_bias, final_norm=final_norm)
    y_p, lat_p, rope_p, k_p, v_p = _trunk_prompt(x_prompt, P, w, meta)
    y_s, lat_s, rope_s, k_s, v_s = _trunk_decode(x_sample, cache_mla_latent.reshape(cache_mla_latent.shape[1:]),
                                                 cache_mla_krope.reshape(cache_mla_krope.shape[1:]),
                                                 cache_swa_k, cache_swa_v, P, w, meta)
    return (y_p, y_s, lat_p, rope_p, k_p, v_p, lat_s, rope_s, k_s, v_s)
```

```python
import functools
import math

import jax
import jax.numpy as jnp
from jax import lax
from jax.experimental import pallas as pl
from jax.experimental.pallas import tpu as pltpu

F32 = jnp.float32
BF16 = jnp.bfloat16

CHUNK = 64
RMS_EPS = 1e-6
FFN_RES = 0.5
ROPE_BASE = 10000.0
WINDOW = 128
WINDOW_CHUNKS = WINDOW // CHUNK
N_BUCKETS = 32
MAX_DISTANCE = 128
NEG_INF = -1e30
MLA_NOPE = 64
MLA_ROPE = 32
MLA_V = 64
SWA_HEAD_DIM = 64

LANES = 128
ROW_TILE = 512
MLA_BLK = 256
MLA_HEADS_PER_STEP = 4
VMEM_LIMIT = 56 * 1024 * 1024


def _params(n_axes, vmem=VMEM_LIMIT):
    return pltpu.CompilerParams(dimension_semantics=("arbitrary",) * n_axes, vmem_limit_bytes=vmem)


def _rms(xf, g):
    return xf * lax.rsqrt(jnp.mean(xf * xf, axis=-1, keepdims=True) + RMS_EPS) * g


def _dot(a, b):
    return jnp.dot(a, b, preferred_element_type=F32)


def _dot_nt(a, b):
    return lax.dot_general(a, b, (((1,), (1,)), ((), ())), preferred_element_type=F32)


def _row_spec(tm, ncols):
    return pl.BlockSpec((tm, ncols), lambda i: (i, 0))


def _const_spec(shape):
    nd = len(shape)
    return pl.BlockSpec(shape, lambda i: (0,) * nd, pipeline_mode=pl.Buffered(1))


def _row_tile(t):
    tm = min(ROW_TILE, t)
    assert t % tm == 0, (t, tm)
    return tm


def _ffn_kernel(x_ref, g_ref, wg_ref, wu_ref, wd_ref, *rest, final_norm):
    o_ref = rest[-1]
    x = x_ref[...]
    h = _rms(x, g_ref[...]).astype(BF16)
    gate = _dot(h, wg_ref[...])
    up = _dot(h, wu_ref[...])
    a = (gate * jax.nn.sigmoid(gate) * up).astype(BF16)
    out = x + FFN_RES * _dot(a, wd_ref[...])
    if final_norm:
        out = _rms(out, rest[0][...])
    o_ref[...] = out


def _ffn(x, g, wg, wu, wd, final_g=None):
    t, d = x.shape
    f = wg.shape[1]
    tm = _row_tile(t)
    ins = [x, g, wg, wu, wd]
    specs = [_row_spec(tm, d), _const_spec((1, d)), _const_spec((d, f)), _const_spec((d, f)),
             _const_spec((f, d))]
    if final_g is not None:
        ins.append(final_g)
        specs.append(_const_spec((1, d)))
    return pl.pallas_call(
        functools.partial(_ffn_kernel, final_norm=final_g is not None),
        grid=(t // tm,), in_specs=specs, out_specs=_row_spec(tm, d),
        out_shape=jax.ShapeDtypeStruct((t, d), F32), compiler_params=_params(1), name="ffn",
    )(*ins)


def _norm_linear_kernel(x_ref, g_ref, *refs, n, scales):
    h = _rms(x_ref[...], g_ref[...]).astype(BF16)
    for w_ref, o_ref, sc in zip(refs[:n], refs[n:], scales):
        y = _dot(h, w_ref[...])
        if sc != 1.0:
            y = y * sc
        o_ref[...] = y.astype(o_ref.dtype)


def _norm_linear(x, g, ws, dtypes, scales, name):
    t, d = x.shape
    tm = _row_tile(t)
    n = len(ws)
    return pl.pallas_call(
        functools.partial(_norm_linear_kernel, n=n, scales=tuple(scales)),
        grid=(t // tm,),
        in_specs=[_row_spec(tm, d), _const_spec((1, d))] + [_const_spec(w.shape) for w in ws],
        out_specs=[_row_spec(tm, w.shape[1]) for w in ws],
        out_shape=[jax.ShapeDtypeStruct((t, w.shape[1]), dt) for w, dt in zip(ws, dtypes)],
        compiler_params=_params(1), name=name,
    )(x, g, *ws)


def _lin_res_kernel(a_ref, w_ref, x_ref, o_ref):
    o_ref[...] = x_ref[...] + _dot(a_ref[...], w_ref[...])


def _lin_res(a, w, x, name):
    t, d = x.shape
    k = a.shape[1]
    tm = _row_tile(t)
    return pl.pallas_call(
        _lin_res_kernel, grid=(t // tm,),
        in_specs=[_row_spec(tm, k), _const_spec((k, d)), _row_spec(tm, d)],
        out_specs=_row_spec(tm, d), out_shape=jax.ShapeDtypeStruct((t, d), F32),
        compiler_params=_params(1), name=name,
    )(a, w, x)


def _mla_common(x_ref, mg_ref, wdq_ref, qn_ref, wlat_ref, kvn_ref, wkr_ref, wkrr_ref, tab_k):
    h = _rms(x_ref[...], mg_ref[...]).astype(BF16)
    cq = _rms(_dot(h, wdq_ref[...]), qn_ref[...]).astype(BF16)
    lat = _rms(_dot(h, wlat_ref[...]), kvn_ref[...])
    ck, sk = tab_k
    kr = _dot(h, wkr_ref[...]) * ck + _dot(h, wkrr_ref[...]) * sk
    return cq, lat, kr


def _mla_proj_prompt_kernel(x_ref, mg_ref, wdq_ref, qn_ref, wq_ref, wqr_ref, wlat_ref, kvn_ref,
                            wkr_ref, wkrr_ref, wuk_ref, wuvt_ref, tab_ref,
                            q_ref, k_ref, vt_ref, lat_ref, kr_ref, *, heads):
    cq_t, sq_t = tab_ref[:, 0:LANES], tab_ref[:, LANES:2 * LANES]
    tab_k = (tab_ref[:, 2 * LANES:3 * LANES], tab_ref[:, 3 * LANES:4 * LANES])
    cq, lat, kr = _mla_common(x_ref, mg_ref, wdq_ref, qn_ref, wlat_ref, kvn_ref, wkr_ref, wkrr_ref, tab_k)
    lat_ref[...] = lat
    kr_ref[...] = kr[:, MLA_NOPE:MLA_NOPE + MLA_ROPE]
    qa = _dot(cq, wq_ref[...])
    qb = _dot(cq, wqr_ref[...])
    latb = lat.astype(BF16)
    kn = _dot(latb, wuk_ref[...])
    for hh in range(heads):
        sl = slice(hh * LANES, (hh + 1) * LANES)
        q_ref[:, sl] = (qa[:, sl] * cq_t + qb[:, sl] * sq_t).astype(BF16)
        k_ref[:, sl] = (kn[:, sl] + kr).astype(BF16)
    vt = _dot_nt(wuvt_ref[...], latb)
    ones_row = lax.broadcasted_iota(jnp.int32, vt.shape, 0) % LANES == MLA_V
    vt_ref[...] = jnp.where(ones_row, 1.0, vt).astype(BF16)


def _mla_proj_decode_kernel(x_ref, mg_ref, wdq_ref, qn_ref, wq_ref, wqrope_ref, wqroper_ref, wlat_ref,
                            kvn_ref, wkr_ref, wkrr_ref, wukd_ref, tab_ref, tabr_ref,
                            ql_ref, qr_ref, lat_ref, kr_ref, *, heads, scale, c_dim):
    tab_k = (tab_ref[:, 2 * LANES:3 * LANES], tab_ref[:, 3 * LANES:4 * LANES])
    cq, lat, kr = _mla_common(x_ref, mg_ref, wdq_ref, qn_ref, wlat_ref, kvn_ref, wkr_ref, wkrr_ref, tab_k)
    lat_ref[...] = lat
    kr_ref[...] = kr[:, MLA_NOPE:MLA_NOPE + MLA_ROPE]
    qp = (_dot(cq, wq_ref[...]) * scale).astype(BF16)
    for hh in range(heads):
        ql_ref[:, hh * c_dim:(hh + 1) * c_dim] = _dot(qp[:, hh * LANES:(hh + 1) * LANES], wukd_ref[hh]).astype(BF16)
    nr = heads * MLA_ROPE
    qr_ref[...] = (_dot(cq, wqrope_ref[...]) * tabr_ref[:, 0:nr]
                   + _dot(cq, wqroper_ref[...]) * tabr_ref[:, nr:2 * nr]).astype(BF16)


def _mla_attn_kernel(q_ref, k_ref, vt_ref, o_ref, *, blk, hp):
    i = pl.program_id(2)
    qs = [q_ref[:, hh * LANES:(hh + 1) * LANES] for hh in range(hp)]

    sls = [slice(hh * LANES, (hh + 1) * LANES) for hh in range(hp)]

    def scores(j):
        start = pl.multiple_of(j * blk, blk)
        return tuple(_dot_nt(k_ref[pl.ds(start, blk), sls[hh]], qs[hh]) for hh in range(hp))

    def update(j, ss, state):
        start = pl.multiple_of(j * blk, blk)
        out = []
        for hh in range(hp):
            m, acc = state[hh]
            m_new = jnp.maximum(m, jnp.max(ss[hh], axis=0, keepdims=True))
            alpha = jnp.exp(m - m_new)
            p = jnp.exp(ss[hh] - m_new).astype(BF16)
            acc = alpha * acc + _dot(vt_ref[sls[hh], pl.ds(start, blk)], p)
            out.append((m_new, acc))
        return tuple(out)

    def step(j, carry):
        ss, state = carry
        nxt = scores(j + 1)
        return nxt, update(j, ss, state)

    init = tuple((jnp.full((1, blk), NEG_INF, F32), jnp.zeros((LANES, blk), F32)) for _ in range(hp))
    ss, state = lax.fori_loop(0, i, step, (scores(0), init))
    kc = lax.broadcasted_iota(jnp.int32, (blk, 1), 0) // CHUNK
    qc = lax.broadcasted_iota(jnp.int32, (1, blk), 1) // CHUNK
    state = update(i, tuple(jnp.where(kc <= qc, s, NEG_INF) for s in ss), state)
    for hh in range(hp):
        _, acc = state[hh]
        o = acc / acc[MLA_V:MLA_V + 1, :]
        o_ref[:, hh * LANES:(hh + 1) * LANES] = o.T.astype(BF16)


def _mla_attn_prompt(q, k, vt, batch, seq, heads, blk=MLA_BLK, hp=MLA_HEADS_PER_STEP):
    nq = seq // blk
    assert seq % blk == 0 and blk % CHUNK == 0 and heads % hp == 0
    w = hp * LANES
    return pl.pallas_call(
        functools.partial(_mla_attn_kernel, blk=blk, hp=hp),
        grid=(batch, heads // hp, nq),
        in_specs=[pl.BlockSpec((blk, w), lambda b, h, i: (b * nq + i, h)),
                  pl.BlockSpec((seq, w), lambda b, h, i: (b, h)),
                  pl.BlockSpec((w, seq), lambda b, h, i: (h, b))],
        out_specs=pl.BlockSpec((blk, w), lambda b, h, i: (b * nq + i, h)),
        out_shape=jax.ShapeDtypeStruct(q.shape, BF16),
        compiler_params=_params(3), name="mla_attn_prompt",
    )(q, k, vt)


def _mla_attn_decode_kernel(ql_ref, qr_ref, cl_ref, cr_ref, nl_ref, nr_ref, o_ref, m_sc, l_sc, acc_sc):
    j = pl.program_id(1)

    @pl.when(j == 0)
    def _():
        m_sc[...] = jnp.full(m_sc.shape, NEG_INF, F32)
        l_sc[...] = jnp.zeros(l_sc.shape, F32)
        acc_sc[...] = jnp.zeros(acc_sc.shape, F32)

    def update(kl, kr):
        s = _dot_nt(ql_ref[...], kl) + _dot_nt(qr_ref[...], kr)
        m = m_sc[...]
        m_new = jnp.maximum(m, jnp.max(s, axis=-1, keepdims=True))
        alpha = jnp.exp(m - m_new)
        p = jnp.exp(s - m_new)
        l_sc[...] = alpha * l_sc[...] + jnp.sum(p, axis=-1, keepdims=True)
        acc_sc[...] = alpha * acc_sc[...] + _dot(p.astype(BF16), kl)
        m_sc[...] = m_new

    update(cl_ref[...].astype(BF16), cr_ref[...].astype(BF16))

    @pl.when(j == pl.num_programs(1) - 1)
    def _():
        update(nl_ref[...].astype(BF16), nr_ref[...].astype(BF16))
        o_ref[...] = (acc_sc[...] / l_sc[...]).astype(BF16)


def _mla_attn_decode(ql, qr, cache_lat, cache_rope, new_lat, new_rope, rows, n_new, blk=1024):
    batch, past, c_dim = cache_lat.shape
    r_dim = cache_rope.shape[2]
    blk = min(blk, past)
    assert past % blk == 0
    return pl.pallas_call(
        _mla_attn_decode_kernel,
        grid=(batch, past // blk),
        in_specs=[pl.BlockSpec((rows, c_dim), lambda b, j: (b, 0)),
                  pl.BlockSpec((rows, r_dim), lambda b, j: (b, 0)),
                  pl.BlockSpec((None, blk, c_dim), lambda b, j: (b, j, 0)),
                  pl.BlockSpec((None, blk, r_dim), lambda b, j: (b, j, 0)),
                  pl.BlockSpec((n_new, c_dim), lambda b, j: (b, 0)),
                  pl.BlockSpec((n_new, r_dim), lambda b, j: (b, 0))],
        out_specs=pl.BlockSpec((rows, c_dim), lambda b, j: (b, 0)),
        out_shape=jax.ShapeDtypeStruct(ql.shape, BF16),
        scratch_shapes=[pltpu.VMEM((rows, 1), F32), pltpu.VMEM((rows, 1), F32), pltpu.VMEM((rows, c_dim), F32)],
        compiler_params=_params(2), name="mla_attn_decode",
    )(ql, qr, cache_lat, cache_rope, new_lat, new_rope)


def _mla_out_decode_kernel(ol_ref, wuv_ref, wo_ref, x_ref, o_ref, *, heads, c_dim):
    parts = [_dot(ol_ref[:, hh * c_dim:(hh + 1) * c_dim], wuv_ref[hh]).astype(BF16) for hh in range(heads)]
    o_ref[...] = x_ref[...] + _dot(jnp.concatenate(parts, axis=1), wo_ref[...])


def _bias_kernel(tab_ref, idx_ref, o_ref, *, heads):
    idx = idx_ref[...]
    for hh in range(heads):
        acc = jnp.zeros(idx.shape, F32)
        for b in range(N_BUCKETS):
            acc = jnp.where(idx == b, tab_ref[b, hh], acc)
        o_ref[hh] = acc


def _bias_table(rel_bias, idx):
    heads = rel_bias.shape[1]
    return pl.pallas_call(
        functools.partial(_bias_kernel, heads=heads),
        in_specs=[pl.BlockSpec(memory_space=pltpu.SMEM), pl.BlockSpec(idx.shape, lambda: (0, 0))],
        out_specs=pl.BlockSpec((heads,) + idx.shape, lambda: (0, 0, 0)),
        out_shape=jax.ShapeDtypeStruct((heads,) + idx.shape, F32), name="rel_bias_table",
    )(rel_bias, idx)


def _rel_bucket(rel):
    half = N_BUCKETS // 2
    max_exact = half // 2
    base = jnp.where(rel > 0, half, 0)
    n = jnp.abs(rel)
    nf = jnp.maximum(n, 1).astype(jnp.float32)
    large = max_exact + (jnp.log(nf / max_exact) / math.log(MAX_DISTANCE / max_exact)
                         * (half - max_exact)).astype(jnp.int32)
    large = jnp.minimum(large, half - 1)
    return base + jnp.where(n < max_exact, n, large)


def _sink_softmax(s, sink):
    m = jnp.maximum(jnp.max(s, axis=-1, keepdims=True), sink)
    p = jnp.exp(s - m)
    return p / (jnp.sum(p, axis=-1, keepdims=True) + jnp.exp(sink - m))


def _swa_prompt_kernel(sink_ref, q_ref, kp_ref, kc_ref, vp_ref, vc_ref, bias_ref, o_ref, *, groups, rep, blk):
    i = pl.program_id(1)
    gw = rep * SWA_HEAD_DIM
    row_chunk = lax.broadcasted_iota(jnp.int32, (blk, 1), 0) // CHUNK
    key_chunk = lax.broadcasted_iota(jnp.int32, (1, 2 * blk), 1) // CHUNK
    first = blk // CHUNK
    valid = ((key_chunk >= row_chunk) & (key_chunk <= row_chunk + WINDOW_CHUNKS)
             & ((i > 0) | (key_chunk >= first)))
    lane_head = lax.broadcasted_iota(jnp.int32, (1, gw), 1) // SWA_HEAD_DIM
    for g in range(groups):
        sl = slice(g * gw, (g + 1) * gw)
        qg = q_ref[:, sl]
        qs = jnp.concatenate([jnp.where(lane_head == r, qg, jnp.zeros_like(qg)) for r in range(rep)], axis=0)
        kg = jnp.concatenate([kp_ref[:, sl], kc_ref[:, sl]], axis=0)
        vg = jnp.concatenate([vp_ref[:, sl], vc_ref[:, sl]], axis=0)
        s = _dot_nt(qs, kg)
        ps = []
        for r in range(rep):
            hh = g * rep + r
            sh = jnp.where(valid, s[r * blk:(r + 1) * blk] + bias_ref[hh], NEG_INF)
            ps.append(_sink_softmax(sh, sink_ref[hh]).astype(BF16))
        res = _dot(jnp.concatenate(ps, axis=0), vg)
        og = jnp.zeros((blk, gw), F32)
        for r in range(rep):
            og = jnp.where(lane_head == r, res[r * blk:(r + 1) * blk], og)
        o_ref[:, sl] = og.astype(BF16)


def _swa_prompt(q, k4, v4, bias, sinks, batch, seq, groups, rep, blk=128):
    nq = seq // blk
    width = q.shape[1]
    cur = lambda b, i: (b * nq + i, 0)
    prev = lambda b, i: (b * nq + jnp.maximum(i - 1, 0), 0)
    return pl.pallas_call(
        functools.partial(_swa_prompt_kernel, groups=groups, rep=rep, blk=blk),
        grid=(batch, nq),
        in_specs=[pl.BlockSpec(memory_space=pltpu.SMEM),
                  pl.BlockSpec((blk, width), cur),
                  pl.BlockSpec((blk, width), prev), pl.BlockSpec((blk, width), cur),
                  pl.BlockSpec((blk, width), prev), pl.BlockSpec((blk, width), cur),
                  pl.BlockSpec(bias.shape, lambda b, i: (0, 0, 0), pipeline_mode=pl.Buffered(1))],
        out_specs=pl.BlockSpec((blk, width), cur),
        out_shape=jax.ShapeDtypeStruct(q.shape, BF16),
        compiler_params=_params(2), name="swa_prompt",
    )(sinks, q, k4, k4, v4, v4, bias)


def _swa_decode_kernel(q_ref, ck_ref, cv_ref, nkv_ref, bias_ref, sink_ref, o_ref, *, w_c, kv_w):
    q = q_ref[...]
    sink = sink_ref[...]
    nk = nkv_ref[:, 0:kv_w].astype(BF16)
    nv = nkv_ref[:, kv_w:2 * kv_w].astype(BF16)
    s1 = _dot_nt(q, ck_ref[...].astype(BF16)) + bias_ref[:, 0:w_c]
    s2 = _dot_nt(q, nk) + bias_ref[:, w_c:]
    m = jnp.maximum(jnp.maximum(jnp.max(s1, axis=-1, keepdims=True), jnp.max(s2, axis=-1, keepdims=True)), sink)
    p1 = jnp.exp(s1 - m)
    p2 = jnp.exp(s2 - m)
    den = jnp.sum(p1, axis=-1, keepdims=True) + jnp.sum(p2, axis=-1, keepdims=True) + jnp.exp(sink - m)
    p1 = (p1 / den).astype(BF16)
    p2 = (p2 / den).astype(BF16)
    o_ref[...] = (_dot(p1, cv_ref[...].astype(BF16)) + _dot(p2, nv)).astype(BF16)


def _rope_tables(pos, scale):
    inv = ROPE_BASE ** (-jnp.arange(0, MLA_ROPE, 2, dtype=jnp.float32) / MLA_ROPE)
    ang = pos.astype(jnp.float32)[:, None] * inv[None, :]
    cos, sin = jnp.cos(ang), jnp.sin(ang)
    n = pos.shape[0]
    c2 = jnp.concatenate([cos, cos], axis=1)
    s2 = jnp.concatenate([sin, sin], axis=1)
    z32 = jnp.zeros((n, LANES - MLA_NOPE - MLA_ROPE), F32)
    z64 = jnp.zeros((n, MLA_NOPE), F32)
    cq = jnp.concatenate([jnp.full((n, MLA_NOPE), scale, F32), c2 * scale, z32], axis=1)
    sq = jnp.concatenate([z64, s2 * scale, z32], axis=1)
    ck = jnp.concatenate([z64, c2, z32], axis=1)
    sk = jnp.concatenate([z64, s2, z32], axis=1)
    return jnp.concatenate([cq, sq, ck, sk], axis=1), c2, s2


def _prep_weights(mla_w_dq, mla_w_uq, mla_w_dkv, mla_w_uk, mla_w_uv, mla_w_o, w_kv_shared, swa_w_q, swa_w_o):
    ql, qcols = mla_w_uq.shape
    c_dim, heads, nope = mla_w_uk.shape
    rope = qcols // heads - nope
    half = rope // 2
    d = mla_w_dq.shape[0]
    pad = LANES - nope - rope
    wq3 = mla_w_uq.reshape(ql, heads, nope + rope)
    w_nope, r1, r2 = wq3[:, :, :nope], wq3[:, :, nope:nope + half], wq3[:, :, nope + half:]
    zq = lambda n: jnp.zeros((ql, heads, n), F32)
    w = {}
    w["wq"] = jnp.concatenate([w_nope, r1, r2, zq(pad)], axis=-1).reshape(ql, heads * LANES)
    w["wq_rot"] = jnp.concatenate([zq(nope), -r2, r1, zq(pad)], axis=-1).reshape(ql, heads * LANES)
    w["wq_rope"] = jnp.concatenate([r1, r2], axis=-1).reshape(ql, heads * rope)
    w["wq_rope_rot"] = jnp.concatenate([-r2, r1], axis=-1).reshape(ql, heads * rope)
    w["wdq"] = mla_w_dq
    w["wlat"] = mla_w_dkv[:, :c_dim]
    k1, k2 = mla_w_dkv[:, c_dim:c_dim + half], mla_w_dkv[:, c_dim + half:]
    zk = lambda n: jnp.zeros((d, n), F32)
    w["wkr"] = jnp.concatenate([zk(nope), k1, k2, zk(pad)], axis=1)
    w["wkr_rot"] = jnp.concatenate([zk(nope), -k2, k1, zk(pad)], axis=1)
    zc = jnp.zeros((c_dim, heads, LANES - nope), F32)
    w["wuk"] = jnp.concatenate([mla_w_uk, zc], axis=-1).reshape(c_dim, heads * LANES)
    w["wuv_t"] = jnp.concatenate([mla_w_uv, zc], axis=-1).reshape(c_dim, heads * LANES).T
    w["wuk_dec"] = jnp.concatenate([jnp.transpose(mla_w_uk, (1, 2, 0)),
                                    jnp.zeros((heads, LANES - nope, c_dim), F32)], axis=1)
    w["wuv_dec"] = jnp.concatenate([jnp.transpose(mla_w_uv, (1, 0, 2)),
                                    jnp.zeros((heads, c_dim, LANES - nope), F32)], axis=-1)
    vdim = mla_w_uv.shape[2]
    w["wo"] = jnp.concatenate([mla_w_o.reshape(heads, vdim, d), jnp.zeros((heads, LANES - vdim, d), F32)],
                              axis=1).reshape(heads * LANES, d)
    kvw = w_kv_shared.shape[1] // 2
    groups = kvw // SWA_HEAD_DIM
    s_heads = swa_w_q.shape[1] // SWA_HEAD_DIM
    rep = s_heads // groups
    wk = w_kv_shared[:, :kvw].reshape(d, groups, 1, SWA_HEAD_DIM)
    wv = w_kv_shared[:, kvw:].reshape(d, groups, 1, SWA_HEAD_DIM)
    w["wkv"] = w_kv_shared
    w["wk4"] = jnp.broadcast_to(wk, (d, groups, rep, SWA_HEAD_DIM)).reshape(d, s_heads * SWA_HEAD_DIM)
    w["wv4"] = jnp.broadcast_to(wv, (d, groups, rep, SWA_HEAD_DIM)).reshape(d, s_heads * SWA_HEAD_DIM)
    w["swa_wq"] = swa_w_q
    w["swa_wo"] = swa_w_o
    onehot = (jnp.arange(s_heads)[:, None] // rep == jnp.arange(groups)[None, :]).astype(F32)
    wq_h = swa_w_q.reshape(d, s_heads, SWA_HEAD_DIM)
    w["swa_wq_dec"] = (wq_h[:, :, None, :] * onehot[None, :, :, None]).reshape(d, s_heads * kvw)
    wo_h = swa_w_o.reshape(s_heads, SWA_HEAD_DIM, d)
    w["swa_wo_dec"] = (wo_h[:, None, :, :] * onehot[:, :, None, None]).reshape(s_heads * kvw, d)
    return {k: v.astype(BF16) for k, v in w.items()}, dict(heads=heads, c_dim=c_dim, groups=groups, rep=rep,
                                                            s_heads=s_heads, kvw=kvw)


def _mla_proj_prompt(x, mg, qn, kvn, w, tab, seq, heads, c_dim):
    t, d = x.shape
    tm = _row_tile(t)
    assert seq % tm == 0
    nb = seq // tm
    hw = heads * LANES
    consts = [mg, w["wdq"], qn, w["wq"], w["wq_rot"], w["wlat"], kvn, w["wkr"], w["wkr_rot"], w["wuk"], w["wuv_t"]]
    return pl.pallas_call(
        functools.partial(_mla_proj_prompt_kernel, heads=heads),
        grid=(t // tm,),
        in_specs=[_row_spec(tm, d)] + [_const_spec(c.shape) for c in consts]
                 + [pl.BlockSpec((tm, 4 * LANES), lambda i: (i % nb, 0))],
        out_specs=[_row_spec(tm, hw), _row_spec(tm, hw), pl.BlockSpec((hw, tm), lambda i: (0, i)),
                   _row_spec(tm, c_dim), _row_spec(tm, MLA_ROPE)],
        out_shape=[jax.ShapeDtypeStruct((t, hw), BF16), jax.ShapeDtypeStruct((t, hw), BF16),
                   jax.ShapeDtypeStruct((hw, t), BF16),
                   jax.ShapeDtypeStruct((t, c_dim), F32), jax.ShapeDtypeStruct((t, MLA_ROPE), F32)],
        compiler_params=_params(1), name="mla_proj_prompt",
    )(x, *consts, tab)


def _mla_proj_decode(x, mg, qn, kvn, w, tab, tabr, heads, c_dim, scale):
    t, d = x.shape
    tm = _row_tile(t)
    assert tab.shape[0] == tm and t == tm
    consts = [mg, w["wdq"], qn, w["wq"], w["wq_rope"], w["wq_rope_rot"], w["wlat"], kvn, w["wkr"], w["wkr_rot"],
              w["wuk_dec"], tab, tabr]
    return pl.pallas_call(
        functools.partial(_mla_proj_decode_kernel, heads=heads, scale=scale, c_dim=c_dim),
        grid=(t // tm,),
        in_specs=[_row_spec(tm, d)] + [_const_spec(c.shape) for c in consts],
        out_specs=[_row_spec(tm, heads * c_dim), _row_spec(tm, heads * MLA_ROPE), _row_spec(tm, c_dim),
                   _row_spec(tm, MLA_ROPE)],
        out_shape=[jax.ShapeDtypeStruct((t, heads * c_dim), BF16), jax.ShapeDtypeStruct((t, heads * MLA_ROPE), BF16),
                   jax.ShapeDtypeStruct((t, c_dim), F32), jax.ShapeDtypeStruct((t, MLA_ROPE), F32)],
        compiler_params=_params(1), name="mla_proj_decode",
    )(x, *consts)


def _trunk_prompt(x3, P, w, meta):
    batch, seq, d = x3.shape
    heads, c_dim = meta["heads"], meta["c_dim"]
    x = x3.reshape(batch * seq, d)
    row = lambda v: v.reshape(1, -1)
    scale = (MLA_NOPE + MLA_ROPE) ** -0.5
    x = _ffn(x, row(P["ffn_norm1"][0]), w["f1g"][0], w["f1u"][0], w["f1d"][0])
    tab, _, _ = _rope_tables(jnp.arange(seq), scale)
    q, k, v, lat, kr = _mla_proj_prompt(x, row(P["mix_norm"][0]), row(P["mla_q_norm"][0]), row(P["mla_kv_norm"][0]),
                                         w, tab, seq, heads, c_dim)
    o = _mla_attn_prompt(q, k, v, batch, seq, heads)
    x = _lin_res(o, w["wo"], x, "mla_out_prompt")
    x = _ffn(x, row(P["ffn_norm2"][0]), w["f2g"][0], w["f2u"][0], w["f2d"][0])
    kv, k4, v4 = _norm_linear(x, row(P["kv_norm"]), [w["wkv"], w["wk4"], w["wv4"]], [F32, BF16, BF16],
                              [1.0, 1.0, 1.0], "kv_shared_prompt")
    x = _ffn(x, row(P["ffn_norm1"][1]), w["f1g"][1], w["f1u"][1], w["f1d"][1])
    (qs,) = _norm_linear(x, row(P["mix_norm"][1]), [w["swa_wq"]], [BF16], [SWA_HEAD_DIM ** -0.5], "swa_q_prompt")
    blk = 2 * CHUNK
    rel = (jnp.arange(2 * blk) - blk)[None, :] - jnp.arange(blk)[:, None]
    bias = _bias_table(P["rel_bias"], _rel_bucket(rel).astype(jnp.int32))
    o = _swa_prompt(qs, k4, v4, bias, P["swa_sinks"][0], batch, seq, meta["groups"], meta["rep"], blk)
    x = _lin_res(o, w["swa_wo"], x, "swa_out_prompt")
    y = _ffn(x, row(P["ffn_norm2"][1]), w["f2g"][1], w["f2u"][1], w["f2d"][1], row(P["final_norm"]))
    kvw = meta["kvw"]
    keep = min(WINDOW, seq)
    kv3 = kv.reshape(batch, seq, 2 * kvw)[:, seq - keep:]
    new_k = kv3[:, :, :kvw].reshape(batch, keep, meta["groups"], SWA_HEAD_DIM)
    new_v = kv3[:, :, kvw:].reshape(batch, keep, meta["groups"], SWA_HEAD_DIM)
    return (y.reshape(batch, seq, d), lat.reshape(1, batch, seq, c_dim), kr.reshape(1, batch, seq, MLA_ROPE),
            new_k, new_v)


def _trunk_decode(x3, cache_lat, cache_rope, cache_k, cache_v, P, w, meta):
    batch, n_new, d = x3.shape
    heads, c_dim, kvw, s_heads = meta["heads"], meta["c_dim"], meta["kvw"], meta["s_heads"]
    past = cache_lat.shape[1]
    w_c = cache_k.shape[1]
    qpos = past + jnp.arange(n_new)
    assert past % CHUNK == 0 and n_new <= CHUNK and w_c <= WINDOW_CHUNKS * CHUNK and w_c <= past
    t = batch * n_new
    x = x3.reshape(t, d)
    row = lambda v: v.reshape(1, -1)
    scale = (MLA_NOPE + MLA_ROPE) ** -0.5
    x = _ffn(x, row(P["ffn_norm1"][0]), w["f1g"][0], w["f1u"][0], w["f1d"][0])
    tab, c2, s2 = _rope_tables(qpos, scale)
    tab = jnp.tile(tab, (batch, 1))
    tabr = jnp.tile(jnp.concatenate([jnp.tile(c2 * scale, (1, heads)), jnp.tile(s2 * scale, (1, heads))], axis=1),
                    (batch, 1))
    ql, qr, lat, kr = _mla_proj_decode(x, row(P["mix_norm"][0]), row(P["mla_q_norm"][0]), row(P["mla_kv_norm"][0]),
                                       w, tab, tabr, heads, c_dim, scale)
    rows = n_new * heads
    ol = _mla_attn_decode(ql.reshape(t * heads, c_dim), qr.reshape(t * heads, MLA_ROPE), cache_lat, cache_rope,
                          lat, kr, rows, n_new)
    tm = _row_tile(t)
    x = pl.pallas_call(
        functools.partial(_mla_out_decode_kernel, heads=heads, c_dim=c_dim),
        grid=(t // tm,),
        in_specs=[_row_spec(tm, heads * c_dim), _const_spec(w["wuv_dec"].shape), _const_spec(w["wo"].shape),
                  _row_spec(tm, d)],
        out_specs=_row_spec(tm, d), out_shape=jax.ShapeDtypeStruct((t, d), F32),
        compiler_params=_params(1), name="mla_out_decode",
    )(ol.reshape(t, heads * c_dim), w["wuv_dec"], w["wo"], x)
    x = _ffn(x, row(P["ffn_norm2"][0]), w["f2g"][0], w["f2u"][0], w["f2d"][0])
    (kv,) = _norm_linear(x, row(P["kv_norm"]), [w["wkv"]], [F32], [1.0], "kv_shared_decode")
    x = _ffn(x, row(P["ffn_norm1"][1]), w["f1g"][1], w["f1u"][1], w["f1d"][1])
    (qs,) = _norm_linear(x, row(P["mix_norm"][1]), [w["swa_wq_dec"]], [BF16], [SWA_HEAD_DIM ** -0.5], "swa_q_decode")
    kpos = jnp.arange(past - w_c, past + n_new)
    bias_h = _bias_table(P["rel_bias"], _rel_bucket(kpos[None, :] - qpos[:, None]).astype(jnp.int32))
    bias = jnp.transpose(bias_h, (1, 0, 2)).reshape(n_new * s_heads, w_c + n_new)
    sink_rows = jnp.tile(P["swa_sinks"][0], n_new).reshape(n_new * s_heads, 1)
    srows = n_new * s_heads
    o = pl.pallas_call(
        functools.partial(_swa_decode_kernel, w_c=w_c, kv_w=kvw),
        grid=(batch,),
        in_specs=[pl.BlockSpec((srows, kvw), lambda b: (b, 0)),
                  pl.BlockSpec((None, w_c, kvw), lambda b: (b, 0, 0)),
                  pl.BlockSpec((None, w_c, kvw), lambda b: (b, 0, 0)),
                  pl.BlockSpec((n_new, 2 * kvw), lambda b: (b, 0)),
                  _const_spec(bias.shape), _const_spec(sink_rows.shape)],
        out_specs=pl.BlockSpec((srows, kvw), lambda b: (b, 0)),
        out_shape=jax.ShapeDtypeStruct((t * s_heads, kvw), BF16),
        compiler_params=_params(1), name="swa_decode",
    )(qs.reshape(t * s_heads, kvw), cache_k.reshape(batch, w_c, kvw), cache_v.reshape(batch, w_c, kvw), kv,
      bias, sink_rows)
    x = _lin_res(o.reshape(t, s_heads * kvw), w["swa_wo_dec"], x, "swa_out_decode")
    y = _ffn(x, row(P["ffn_norm2"][1]), w["f2g"][1], w["f2u"][1], w["f2d"][1], row(P["final_norm"]))
    kv3 = kv.reshape(batch, n_new, 2 * kvw)
    new_k = kv3[:, :, :kvw].reshape(batch, n_new, meta["groups"], SWA_HEAD_DIM)
    new_v = kv3[:, :, kvw:].reshape(batch, n_new, meta["groups"], SWA_HEAD_DIM)
    return (y.reshape(batch, n_new, d), lat.reshape(1, batch, n_new, c_dim), kr.reshape(1, batch, n_new, MLA_ROPE),
            new_k, new_v)


def kernel(x_prompt, x_sample, cache_mla_latent, cache_mla_krope, cache_swa_k, cache_swa_v, ffn_norm1, ffn1_w_gate, ffn1_w_up, ffn1_w_down, mix_norm, ffn_norm2, ffn2_w_gate, ffn2_w_up, ffn2_w_down, mla_w_dq, mla_q_norm, mla_w_uq, mla_w_dkv, mla_kv_norm, mla_w_uk, mla_w_uv, mla_w_o, kv_norm, w_kv_shared, swa_w_q, swa_sinks, swa_w_o, rel_bias, final_norm):
    assert ffn_norm1.shape[0] == 2 and mla_w_dq.shape[0] == 1 and swa_w_q.shape[0] == 1
    w, meta = _prep_weights(mla_w_dq[0], mla_w_uq[0], mla_w_dkv[0], mla_w_uk[0], mla_w_uv[0], mla_w_o[0],
                            w_kv_shared, swa_w_q[0], swa_w_o[0])
    w.update(f1g=ffn1_w_gate.astype(BF16), f1u=ffn1_w_up.astype(BF16), f1d=ffn1_w_down.astype(BF16),
             f2g=ffn2_w_gate.astype(BF16), f2u=ffn2_w_up.astype(BF16), f2d=ffn2_w_down.astype(BF16))
    P = dict(ffn_norm1=ffn_norm1, mix_norm=mix_norm, ffn_norm2=ffn_norm2, mla_q_norm=mla_q_norm,
             mla_kv_norm=mla_kv_norm, kv_norm=kv_norm, swa_sinks=swa_sinks, rel_bias=rel_bias, final_norm=final_norm)
    y_p, lat_p, rope_p, k_p, v_p = _trunk_prompt(x_prompt, P, w, meta)
    y_s, lat_s, rope_s, k_s, v_s = _trunk_decode(x_sample, cache_mla_latent.reshape(cache_mla_latent.shape[1:]),
                                                 cache_mla_krope.reshape(cache_mla_krope.shape[1:]),
                                                 cache_swa_k, cache_swa_v, P, w, meta)
    return (y_p, y_s, lat_p, rope_p, k_p, v_p, lat_s, rope_s, k_s, v_s)
```

```python
import functools
import math

import jax
import jax.numpy as jnp
from jax import lax
from jax.experimental import pallas as pl
from jax.experimental.pallas import tpu as pltpu

F32 = jnp.float32
BF16 = jnp.bfloat16

CHUNK = 64
RMS_EPS = 1e-6
FFN_RES = 0.5
ROPE_BASE = 10000.0
WINDOW = 128
WINDOW_CHUNKS = WINDOW // CHUNK
N_BUCKETS = 32
MAX_DISTANCE = 128
NEG_INF = -1e30
MLA_NOPE = 64
MLA_ROPE = 32
MLA_V = 64
SWA_HEAD_DIM = 64

LANES = 128
ROW_TILE = 512
MLA_BQ = 512
MLA_BK = 512
MLA_HEADS_PER_STEP = 4
VMEM_LIMIT = 56 * 1024 * 1024


def _params(n_axes, vmem=VMEM_LIMIT):
    return pltpu.CompilerParams(dimension_semantics=("arbitrary",) * n_axes, vmem_limit_bytes=vmem)


def _rms(xf, g):
    return xf * lax.rsqrt(jnp.mean(xf * xf, axis=-1, keepdims=True) + RMS_EPS) * g


def _dot(a, b):
    return jnp.dot(a, b, preferred_element_type=F32)


def _dot_nt(a, b):
    return lax.dot_general(a, b, (((1,), (1,)), ((), ())), preferred_element_type=F32)


def _row_spec(tm, ncols):
    return pl.BlockSpec((tm, ncols), lambda i: (i, 0))


def _const_spec(shape):
    nd = len(shape)
    return pl.BlockSpec(shape, lambda i: (0,) * nd, pipeline_mode=pl.Buffered(1))


def _row_tile(t):
    tm = min(ROW_TILE, t)
    assert t % tm == 0, (t, tm)
    return tm


def _ffn_kernel(x_ref, g_ref, wg_ref, wu_ref, wd_ref, *rest, final_norm):
    o_ref = rest[-1]
    x = x_ref[...]
    h = _rms(x, g_ref[...]).astype(BF16)
    gate = _dot(h, wg_ref[...])
    up = _dot(h, wu_ref[...])
    a = (gate * jax.nn.sigmoid(gate) * up).astype(BF16)
    out = x + FFN_RES * _dot(a, wd_ref[...])
    if final_norm:
        out = _rms(out, rest[0][...])
    o_ref[...] = out


def _ffn(x, g, wg, wu, wd, final_g=None):
    t, d = x.shape
    f = wg.shape[1]
    tm = _row_tile(t)
    ins = [x, g, wg, wu, wd]
    specs = [_row_spec(tm, d), _const_spec((1, d)), _const_spec((d, f)), _const_spec((d, f)),
             _const_spec((f, d))]
    if final_g is not None:
        ins.append(final_g)
        specs.append(_const_spec((1, d)))
    return pl.pallas_call(
        functools.partial(_ffn_kernel, final_norm=final_g is not None),
        grid=(t // tm,), in_specs=specs, out_specs=_row_spec(tm, d),
        out_shape=jax.ShapeDtypeStruct((t, d), F32), compiler_params=_params(1), name="ffn",
    )(*ins)


def _norm_linear_kernel(x_ref, g_ref, *refs, n, scales):
    h = _rms(x_ref[...], g_ref[...]).astype(BF16)
    for w_ref, o_ref, sc in zip(refs[:n], refs[n:], scales):
        y = _dot(h, w_ref[...])
        if sc != 1.0:
            y = y * sc
        o_ref[...] = y.astype(o_ref.dtype)


def _norm_linear(x, g, ws, dtypes, scales, name):
    t, d = x.shape
    tm = _row_tile(t)
    n = len(ws)
    return pl.pallas_call(
        functools.partial(_norm_linear_kernel, n=n, scales=tuple(scales)),
        grid=(t // tm,),
        in_specs=[_row_spec(tm, d), _const_spec((1, d))] + [_const_spec(w.shape) for w in ws],
        out_specs=[_row_spec(tm, w.shape[1]) for w in ws],
        out_shape=[jax.ShapeDtypeStruct((t, w.shape[1]), dt) for w, dt in zip(ws, dtypes)],
        compiler_params=_params(1), name=name,
    )(x, g, *ws)


def _lin_res_kernel(a_ref, w_ref, x_ref, o_ref):
    o_ref[...] = x_ref[...] + _dot(a_ref[...], w_ref[...])


def _lin_res(a, w, x, name):
    t, d = x.shape
    k = a.shape[1]
    tm = _row_tile(t)
    return pl.pallas_call(
        _lin_res_kernel, grid=(t // tm,),
        in_specs=[_row_spec(tm, k), _const_spec((k, d)), _row_spec(tm, d)],
        out_specs=_row_spec(tm, d), out_shape=jax.ShapeDtypeStruct((t, d), F32),
        compiler_params=_params(1), name=name,
    )(a, w, x)


def _mla_common(x_ref, mg_ref, wdq_ref, qn_ref, wlat_ref, kvn_ref, wkr_ref, wkrr_ref, tab_k):
    h = _rms(x_ref[...], mg_ref[...]).astype(BF16)
    cq = _rms(_dot(h, wdq_ref[...]), qn_ref[...]).astype(BF16)
    lat = _rms(_dot(h, wlat_ref[...]), kvn_ref[...])
    ck, sk = tab_k
    kr = _dot(h, wkr_ref[...]) * ck + _dot(h, wkrr_ref[...]) * sk
    return cq, lat, kr


def _mla_proj_prompt_kernel(x_ref, mg_ref, wdq_ref, qn_ref, wq_ref, wqr_ref, wlat_ref, kvn_ref,
                            wkr_ref, wkrr_ref, wuk_ref, wuvt_ref, tab_ref,
                            q_ref, k_ref, vt_ref, lat_ref, kr_ref, *, heads):
    cq_t, sq_t = tab_ref[:, 0:LANES], tab_ref[:, LANES:2 * LANES]
    tab_k = (tab_ref[:, 2 * LANES:3 * LANES], tab_ref[:, 3 * LANES:4 * LANES])
    cq, lat, kr = _mla_common(x_ref, mg_ref, wdq_ref, qn_ref, wlat_ref, kvn_ref, wkr_ref, wkrr_ref, tab_k)
    lat_ref[...] = lat
    kr_ref[...] = kr[:, MLA_NOPE:MLA_NOPE + MLA_ROPE]
    qa = _dot(cq, wq_ref[...])
    qb = _dot(cq, wqr_ref[...])
    latb = lat.astype(BF16)
    kn = _dot(latb, wuk_ref[...])
    for hh in range(heads):
        sl = slice(hh * LANES, (hh + 1) * LANES)
        q_ref[:, sl] = (qa[:, sl] * cq_t + qb[:, sl] * sq_t).astype(BF16)
        k_ref[:, sl] = (kn[:, sl] + kr).astype(BF16)
    vt = _dot_nt(wuvt_ref[...], latb)
    ones_row = lax.broadcasted_iota(jnp.int32, vt.shape, 0) % LANES == MLA_V
    vt_ref[...] = jnp.where(ones_row, 1.0, vt).astype(BF16)


def _mla_proj_decode_kernel(x_ref, mg_ref, wdq_ref, qn_ref, wq_ref, wqrope_ref, wqroper_ref, wlat_ref,
                            kvn_ref, wkr_ref, wkrr_ref, wukd_ref, tab_ref, tabr_ref,
                            ql_ref, qr_ref, lat_ref, kr_ref, *, heads, scale, c_dim):
    tab_k = (tab_ref[:, 2 * LANES:3 * LANES], tab_ref[:, 3 * LANES:4 * LANES])
    cq, lat, kr = _mla_common(x_ref, mg_ref, wdq_ref, qn_ref, wlat_ref, kvn_ref, wkr_ref, wkrr_ref, tab_k)
    lat_ref[...] = lat
    kr_ref[...] = kr[:, MLA_NOPE:MLA_NOPE + MLA_ROPE]
    qp = (_dot(cq, wq_ref[...]) * scale).astype(BF16)
    for hh in range(heads):
        ql_ref[:, hh * c_dim:(hh + 1) * c_dim] = _dot(qp[:, hh * LANES:(hh + 1) * LANES], wukd_ref[hh]).astype(BF16)
    nr = heads * MLA_ROPE
    qr_ref[...] = (_dot(cq, wqrope_ref[...]) * tabr_ref[:, 0:nr]
                   + _dot(cq, wqroper_ref[...]) * tabr_ref[:, nr:2 * nr]).astype(BF16)


def _mla_attn_kernel(q_ref, k_ref, vt_ref, o_ref, m_sc, acc_sc, *, bq, bk, hp):
    i = pl.program_id(2)
    qs = [q_ref[:, hh * LANES:(hh + 1) * LANES] for hh in range(hp)]
    sls = [slice(hh * LANES, (hh + 1) * LANES) for hh in range(hp)]

    def block(j, masked):
        start = pl.multiple_of(j * bk, bk)
        ss = [_dot_nt(k_ref[pl.ds(start, bk), sls[hh]], qs[hh]) for hh in range(hp)]
        if masked:
            kc = (start + lax.broadcasted_iota(jnp.int32, (bk, 1), 0)) // CHUNK
            qc = (i * bq + lax.broadcasted_iota(jnp.int32, (1, bq), 1)) // CHUNK
            ss = [jnp.where(kc <= qc, s, NEG_INF) for s in ss]
        for hh in range(hp):
            m = m_sc[hh]
            m_new = jnp.maximum(m, jnp.max(ss[hh], axis=0, keepdims=True))
            alpha = jnp.exp2(m - m_new)
            p = jnp.exp2(ss[hh] - m_new).astype(BF16)
            acc_sc[hh] = alpha * acc_sc[hh] + _dot(vt_ref[sls[hh], pl.ds(start, bk)], p)
            m_sc[hh] = m_new

    m_sc[...] = jnp.full(m_sc.shape, NEG_INF, F32)
    acc_sc[...] = jnp.zeros(acc_sc.shape, F32)
    n_full = (i * bq) // bk

    @pl.loop(0, n_full)
    def _(j):
        block(j, False)

    for t in range(max(1, bq // bk)):
        block(n_full + t, True)
    for hh in range(hp):
        acc = acc_sc[hh]
        o = acc / acc[MLA_V:MLA_V + 1, :]
        o_ref[:, hh * LANES:(hh + 1) * LANES] = o.T.astype(BF16)


def _mla_attn_prompt(q, k, vt, batch, seq, heads, bq=MLA_BQ, bk=MLA_BK, hp=MLA_HEADS_PER_STEP):
    nq = seq // bq
    assert seq % bq == 0 and seq % bk == 0 and bq % CHUNK == 0 and heads % hp == 0
    assert bq % bk == 0 or bk % bq == 0
    w = hp * LANES
    return pl.pallas_call(
        functools.partial(_mla_attn_kernel, bq=bq, bk=bk, hp=hp),
        grid=(batch, heads // hp, nq),
        in_specs=[pl.BlockSpec((bq, w), lambda b, h, i: (b * nq + i, h)),
                  pl.BlockSpec((seq, w), lambda b, h, i: (b, h)),
                  pl.BlockSpec((w, seq), lambda b, h, i: (h, b))],
        out_specs=pl.BlockSpec((bq, w), lambda b, h, i: (b * nq + i, h)),
        out_shape=jax.ShapeDtypeStruct(q.shape, BF16),
        scratch_shapes=[pltpu.VMEM((hp, 1, bq), F32), pltpu.VMEM((hp, LANES, bq), F32)],
        compiler_params=_params(3), name="mla_attn_prompt",
    )(q, k, vt)


def _mla_attn_decode_kernel(ql_ref, qr_ref, cl_ref, cr_ref, nl_ref, nr_ref, o_ref, m_sc, l_sc, acc_sc):
    j = pl.program_id(1)

    @pl.when(j == 0)
    def _():
        m_sc[...] = jnp.full(m_sc.shape, NEG_INF, F32)
        l_sc[...] = jnp.zeros(l_sc.shape, F32)
        acc_sc[...] = jnp.zeros(acc_sc.shape, F32)

    def update(kl, kr):
        s = _dot_nt(ql_ref[...], kl) + _dot_nt(qr_ref[...], kr)
        m = m_sc[...]
        m_new = jnp.maximum(m, jnp.max(s, axis=-1, keepdims=True))
        alpha = jnp.exp(m - m_new)
        p = jnp.exp(s - m_new)
        l_sc[...] = alpha * l_sc[...] + jnp.sum(p, axis=-1, keepdims=True)
        acc_sc[...] = alpha * acc_sc[...] + _dot(p.astype(BF16), kl)
        m_sc[...] = m_new

    update(cl_ref[...].astype(BF16), cr_ref[...].astype(BF16))

    @pl.when(j == pl.num_programs(1) - 1)
    def _():
        update(nl_ref[...].astype(BF16), nr_ref[...].astype(BF16))
        o_ref[...] = (acc_sc[...] / l_sc[...]).astype(BF16)


def _mla_attn_decode(ql, qr, cache_lat, cache_rope, new_lat, new_rope, rows, n_new, blk=1024):
    batch, past, c_dim = cache_lat.shape
    r_dim = cache_rope.shape[2]
    blk = min(blk, past)
    assert past % blk == 0
    return pl.pallas_call(
        _mla_attn_decode_kernel,
        grid=(batch, past // blk),
        in_specs=[pl.BlockSpec((rows, c_dim), lambda b, j: (b, 0)),
                  pl.BlockSpec((rows, r_dim), lambda b, j: (b, 0)),
                  pl.BlockSpec((None, blk, c_dim), lambda b, j: (b, j, 0)),
                  pl.BlockSpec((None, blk, r_dim), lambda b, j: (b, j, 0)),
                  pl.BlockSpec((n_new, c_dim), lambda b, j: (b, 0)),
                  pl.BlockSpec((n_new, r_dim), lambda b, j: (b, 0))],
        out_specs=pl.BlockSpec((rows, c_dim), lambda b, j: (b, 0)),
        out_shape=jax.ShapeDtypeStruct(ql.shape, BF16),
        scratch_shapes=[pltpu.VMEM((rows, 1), F32), pltpu.VMEM((rows, 1), F32), pltpu.VMEM((rows, c_dim), F32)],
        compiler_params=_params(2), name="mla_attn_decode",
    )(ql, qr, cache_lat, cache_rope, new_lat, new_rope)


def _mla_out_decode_kernel(ol_ref, wuv_ref, wo_ref, x_ref, o_ref, *, heads, c_dim):
    parts = [_dot(ol_ref[:, hh * c_dim:(hh + 1) * c_dim], wuv_ref[hh]).astype(BF16) for hh in range(heads)]
    o_ref[...] = x_ref[...] + _dot(jnp.concatenate(parts, axis=1), wo_ref[...])


def _bias_kernel(tab_ref, idx_ref, o_ref, *, heads):
    idx = idx_ref[...]
    for hh in range(heads):
        acc = jnp.zeros(idx.shape, F32)
        for b in range(N_BUCKETS):
            acc = jnp.where(idx == b, tab_ref[b, hh], acc)
        o_ref[hh] = acc


def _bias_table(rel_bias, idx):
    heads = rel_bias.shape[1]
    return pl.pallas_call(
        functools.partial(_bias_kernel, heads=heads),
        in_specs=[pl.BlockSpec(memory_space=pltpu.SMEM), pl.BlockSpec(idx.shape, lambda: (0, 0))],
        out_specs=pl.BlockSpec((heads,) + idx.shape, lambda: (0, 0, 0)),
        out_shape=jax.ShapeDtypeStruct((heads,) + idx.shape, F32), name="rel_bias_table",
    )(rel_bias, idx)


def _rel_bucket(rel):
    half = N_BUCKETS // 2
    max_exact = half // 2
    base = jnp.where(rel > 0, half, 0)
    n = jnp.abs(rel)
    nf = jnp.maximum(n, 1).astype(jnp.float32)
    large = max_exact + (jnp.log(nf / max_exact) / math.log(MAX_DISTANCE / max_exact)
                         * (half - max_exact)).astype(jnp.int32)
    large = jnp.minimum(large, half - 1)
    return base + jnp.where(n < max_exact, n, large)


def _sink_softmax(s, sink):
    m = jnp.maximum(jnp.max(s, axis=-1, keepdims=True), sink)
    p = jnp.exp(s - m)
    return p / (jnp.sum(p, axis=-1, keepdims=True) + jnp.exp(sink - m))


def _swa_prompt_kernel(sink_ref, q_ref, kp_ref, kc_ref, vp_ref, vc_ref, bias_ref, o_ref, *, groups, rep, blk):
    i = pl.program_id(1)
    gw = rep * SWA_HEAD_DIM
    row_chunk = lax.broadcasted_iota(jnp.int32, (blk, 1), 0) // CHUNK
    key_chunk = lax.broadcasted_iota(jnp.int32, (1, 2 * blk), 1) // CHUNK
    first = blk // CHUNK
    valid = ((key_chunk >= row_chunk) & (key_chunk <= row_chunk + WINDOW_CHUNKS)
             & ((i > 0) | (key_chunk >= first)))
    lane_head = lax.broadcasted_iota(jnp.int32, (1, gw), 1) // SWA_HEAD_DIM
    for g in range(groups):
        sl = slice(g * gw, (g + 1) * gw)
        qg = q_ref[:, sl]
        qs = jnp.concatenate([jnp.where(lane_head == r, qg, jnp.zeros_like(qg)) for r in range(rep)], axis=0)
        kg = jnp.concatenate([kp_ref[:, sl], kc_ref[:, sl]], axis=0)
        vg = jnp.concatenate([vp_ref[:, sl], vc_ref[:, sl]], axis=0)
        s = _dot_nt(qs, kg)
        ps = []
        for r in range(rep):
            hh = g * rep + r
            sh = jnp.where(valid, s[r * blk:(r + 1) * blk] + bias_ref[hh], NEG_INF)
            ps.append(_sink_softmax(sh, sink_ref[hh]).astype(BF16))
        res = _dot(jnp.concatenate(ps, axis=0), vg)
        og = jnp.zeros((blk, gw), F32)
        for r in range(rep):
            og = jnp.where(lane_head == r, res[r * blk:(r + 1) * blk], og)
        o_ref[:, sl] = og.astype(BF16)


def _swa_prompt(q, k4, v4, bias, sinks, batch, seq, groups, rep, blk=128):
    nq = seq // blk
    width = q.shape[1]
    cur = lambda b, i: (b * nq + i, 0)
    prev = lambda b, i: (b * nq + jnp.maximum(i - 1, 0), 0)
    return pl.pallas_call(
        functools.partial(_swa_prompt_kernel, groups=groups, rep=rep, blk=blk),
        grid=(batch, nq),
        in_specs=[pl.BlockSpec(memory_space=pltpu.SMEM),
                  pl.BlockSpec((blk, width), cur),
                  pl.BlockSpec((blk, width), prev), pl.BlockSpec((blk, width), cur),
                  pl.BlockSpec((blk, width), prev), pl.BlockSpec((blk, width), cur),
                  pl.BlockSpec(bias.shape, lambda b, i: (0, 0, 0), pipeline_mode=pl.Buffered(1))],
        out_specs=pl.BlockSpec((blk, width), cur),
        out_shape=jax.ShapeDtypeStruct(q.shape, BF16),
        compiler_params=_params(2), name="swa_prompt",
    )(sinks, q, k4, k4, v4, v4, bias)


def _swa_decode_kernel(q_ref, ck_ref, cv_ref, nkv_ref, bias_ref, sink_ref, o_ref, *, w_c, kv_w):
    q = q_ref[...]
    sink = sink_ref[...]
    nk = nkv_ref[:, 0:kv_w].astype(BF16)
    nv = nkv_ref[:, kv_w:2 * kv_w].astype(BF16)
    s1 = _dot_nt(q, ck_ref[...].astype(BF16)) + bias_ref[:, 0:w_c]
    s2 = _dot_nt(q, nk) + bias_ref[:, w_c:]
    m = jnp.maximum(jnp.maximum(jnp.max(s1, axis=-1, keepdims=True), jnp.max(s2, axis=-1, keepdims=True)), sink)
    p1 = jnp.exp(s1 - m)
    p2 = jnp.exp(s2 - m)
    den = jnp.sum(p1, axis=-1, keepdims=True) + jnp.sum(p2, axis=-1, keepdims=True) + jnp.exp(sink - m)
    p1 = (p1 / den).astype(BF16)
    p2 = (p2 / den).astype(BF16)
    o_ref[...] = (_dot(p1, cv_ref[...].astype(BF16)) + _dot(p2, nv)).astype(BF16)


def _rope_tables(pos, scale):
    inv = ROPE_BASE ** (-jnp.arange(0, MLA_ROPE, 2, dtype=jnp.float32) / MLA_ROPE)
    ang = pos.astype(jnp.float32)[:, None] * inv[None, :]
    cos, sin = jnp.cos(ang), jnp.sin(ang)
    n = pos.shape[0]
    c2 = jnp.concatenate([cos, cos], axis=1)
    s2 = jnp.concatenate([sin, sin], axis=1)
    z32 = jnp.zeros((n, LANES - MLA_NOPE - MLA_ROPE), F32)
    z64 = jnp.zeros((n, MLA_NOPE), F32)
    cq = jnp.concatenate([jnp.full((n, MLA_NOPE), scale, F32), c2 * scale, z32], axis=1)
    sq = jnp.concatenate([z64, s2 * scale, z32], axis=1)
    ck = jnp.concatenate([z64, c2, z32], axis=1)
    sk = jnp.concatenate([z64, s2, z32], axis=1)
    return jnp.concatenate([cq, sq, ck, sk], axis=1), c2, s2


def _prep_weights(mla_w_dq, mla_w_uq, mla_w_dkv, mla_w_uk, mla_w_uv, mla_w_o, w_kv_shared, swa_w_q, swa_w_o):
    ql, qcols = mla_w_uq.shape
    c_dim, heads, nope = mla_w_uk.shape
    rope = qcols // heads - nope
    half = rope // 2
    d = mla_w_dq.shape[0]
    pad = LANES - nope - rope
    wq3 = mla_w_uq.reshape(ql, heads, nope + rope)
    w_nope, r1, r2 = wq3[:, :, :nope], wq3[:, :, nope:nope + half], wq3[:, :, nope + half:]
    zq = lambda n: jnp.zeros((ql, heads, n), F32)
    w = {}
    w["wq"] = jnp.concatenate([w_nope, r1, r2, zq(pad)], axis=-1).reshape(ql, heads * LANES)
    w["wq_rot"] = jnp.concatenate([zq(nope), -r2, r1, zq(pad)], axis=-1).reshape(ql, heads * LANES)
    w["wq_rope"] = jnp.concatenate([r1, r2], axis=-1).reshape(ql, heads * rope)
    w["wq_rope_rot"] = jnp.concatenate([-r2, r1], axis=-1).reshape(ql, heads * rope)
    w["wdq"] = mla_w_dq
    w["wlat"] = mla_w_dkv[:, :c_dim]
    k1, k2 = mla_w_dkv[:, c_dim:c_dim + half], mla_w_dkv[:, c_dim + half:]
    zk = lambda n: jnp.zeros((d, n), F32)
    w["wkr"] = jnp.concatenate([zk(nope), k1, k2, zk(pad)], axis=1)
    w["wkr_rot"] = jnp.concatenate([zk(nope), -k2, k1, zk(pad)], axis=1)
    zc = jnp.zeros((c_dim, heads, LANES - nope), F32)
    w["wuk"] = jnp.concatenate([mla_w_uk, zc], axis=-1).reshape(c_dim, heads * LANES)
    w["wuv_t"] = jnp.concatenate([mla_w_uv, zc], axis=-1).reshape(c_dim, heads * LANES).T
    w["wuk_dec"] = jnp.concatenate([jnp.transpose(mla_w_uk, (1, 2, 0)),
                                    jnp.zeros((heads, LANES - nope, c_dim), F32)], axis=1)
    w["wuv_dec"] = jnp.concatenate([jnp.transpose(mla_w_uv, (1, 0, 2)),
                                    jnp.zeros((heads, c_dim, LANES - nope), F32)], axis=-1)
    vdim = mla_w_uv.shape[2]
    w["wo"] = jnp.concatenate([mla_w_o.reshape(heads, vdim, d), jnp.zeros((heads, LANES - vdim, d), F32)],
                              axis=1).reshape(heads * LANES, d)
    kvw = w_kv_shared.shape[1] // 2
    groups = kvw // SWA_HEAD_DIM
    s_heads = swa_w_q.shape[1] // SWA_HEAD_DIM
    rep = s_heads // groups
    wk = w_kv_shared[:, :kvw].reshape(d, groups, 1, SWA_HEAD_DIM)
    wv = w_kv_shared[:, kvw:].reshape(d, groups, 1, SWA_HEAD_DIM)
    w["wkv"] = w_kv_shared
    w["wk4"] = jnp.broadcast_to(wk, (d, groups, rep, SWA_HEAD_DIM)).reshape(d, s_heads * SWA_HEAD_DIM)
    w["wv4"] = jnp.broadcast_to(wv, (d, groups, rep, SWA_HEAD_DIM)).reshape(d, s_heads * SWA_HEAD_DIM)
    w["swa_wq"] = swa_w_q
    w["swa_wo"] = swa_w_o
    onehot = (jnp.arange(s_heads)[:, None] // rep == jnp.arange(groups)[None, :]).astype(F32)
    wq_h = swa_w_q.reshape(d, s_heads, SWA_HEAD_DIM)
    w["swa_wq_dec"] = (wq_h[:, :, None, :] * onehot[None, :, :, None]).reshape(d, s_heads * kvw)
    wo_h = swa_w_o.reshape(s_heads, SWA_HEAD_DIM, d)
    w["swa_wo_dec"] = (wo_h[:, None, :, :] * onehot[:, :, None, None]).reshape(s_heads * kvw, d)
    return {k: v.astype(BF16) for k, v in w.items()}, dict(heads=heads, c_dim=c_dim, groups=groups, rep=rep,
                                                            s_heads=s_heads, kvw=kvw)


def _mla_proj_prompt(x, mg, qn, kvn, w, tab, seq, heads, c_dim):
    t, d = x.shape
    tm = _row_tile(t)
    assert seq % tm == 0
    nb = seq // tm
    hw = heads * LANES
    consts = [mg, w["wdq"], qn, w["wq"], w["wq_rot"], w["wlat"], kvn, w["wkr"], w["wkr_rot"], w["wuk"], w["wuv_t"]]
    return pl.pallas_call(
        functools.partial(_mla_proj_prompt_kernel, heads=heads),
        grid=(t // tm,),
        in_specs=[_row_spec(tm, d)] + [_const_spec(c.shape) for c in consts]
                 + [pl.BlockSpec((tm, 4 * LANES), lambda i: (i % nb, 0))],
        out_specs=[_row_spec(tm, hw), _row_spec(tm, hw), pl.BlockSpec((hw, tm), lambda i: (0, i)),
                   _row_spec(tm, c_dim), _row_spec(tm, MLA_ROPE)],
        out_shape=[jax.ShapeDtypeStruct((t, hw), BF16), jax.ShapeDtypeStruct((t, hw), BF16),
                   jax.ShapeDtypeStruct((hw, t), BF16),
                   jax.ShapeDtypeStruct((t, c_dim), F32), jax.ShapeDtypeStruct((t, MLA_ROPE), F32)],
        compiler_params=_params(1), name="mla_proj_prompt",
    )(x, *consts, tab)


def _mla_proj_decode(x, mg, qn, kvn, w, tab, tabr, heads, c_dim, scale):
    t, d = x.shape
    tm = _row_tile(t)
    assert tab.shape[0] == tm and t == tm
    consts = [mg, w["wdq"], qn, w["wq"], w["wq_rope"], w["wq_rope_rot"], w["wlat"], kvn, w["wkr"], w["wkr_rot"],
              w["wuk_dec"], tab, tabr]
    return pl.pallas_call(
        functools.partial(_mla_proj_decode_kernel, heads=heads, scale=scale, c_dim=c_dim),
        grid=(t // tm,),
        in_specs=[_row_spec(tm, d)] + [_const_spec(c.shape) for c in consts],
        out_specs=[_row_spec(tm, heads * c_dim), _row_spec(tm, heads * MLA_ROPE), _row_spec(tm, c_dim),
                   _row_spec(tm, MLA_ROPE)],
        out_shape=[jax.ShapeDtypeStruct((t, heads * c_dim), BF16), jax.ShapeDtypeStruct((t, heads * MLA_ROPE), BF16),
                   jax.ShapeDtypeStruct((t, c_dim), F32), jax.ShapeDtypeStruct((t, MLA_ROPE), F32)],
        compiler_params=_params(1), name="mla_proj_decode",
    )(x, *consts)


def _trunk_prompt(x3, P, w, meta):
    batch, seq, d = x3.shape
    heads, c_dim = meta["heads"], meta["c_dim"]
    x = x3.reshape(batch * seq, d)
    row = lambda v: v.reshape(1, -1)
    scale = (MLA_NOPE + MLA_ROPE) ** -0.5 * math.log2(math.e)
    x = _ffn(x, row(P["ffn_norm1"][0]), w["f1g"][0], w["f1u"][0], w["f1d"][0])
    tab, _, _ = _rope_tables(jnp.arange(seq), scale)
    q, k, v, lat, kr = _mla_proj_prompt(x, row(P["mix_norm"][0]), row(P["mla_q_norm"][0]), row(P["mla_kv_norm"][0]),
                                         w, tab, seq, heads, c_dim)
    o = _mla_attn_prompt(q, k, v, batch, seq, heads)
    x = _lin_res(o, w["wo"], x, "mla_out_prompt")
    x = _ffn(x, row(P["ffn_norm2"][0]), w["f2g"][0], w["f2u"][0], w["f2d"][0])
    kv, k4, v4 = _norm_linear(x, row(P["kv_norm"]), [w["wkv"], w["wk4"], w["wv4"]], [F32, BF16, BF16],
                              [1.0, 1.0, 1.0], "kv_shared_prompt")
    x = _ffn(x, row(P["ffn_norm1"][1]), w["f1g"][1], w["f1u"][1], w["f1d"][1])
    (qs,) = _norm_linear(x, row(P["mix_norm"][1]), [w["swa_wq"]], [BF16], [SWA_HEAD_DIM ** -0.5], "swa_q_prompt")
    blk = 2 * CHUNK
    rel = (jnp.arange(2 * blk) - blk)[None, :] - jnp.arange(blk)[:, None]
    bias = _bias_table(P["rel_bias"], _rel_bucket(rel).astype(jnp.int32))
    o = _swa_prompt(qs, k4, v4, bias, P["swa_sinks"][0], batch, seq, meta["groups"], meta["rep"], blk)
    x = _lin_res(o, w["swa_wo"], x, "swa_out_prompt")
    y = _ffn(x, row(P["ffn_norm2"][1]), w["f2g"][1], w["f2u"][1], w["f2d"][1], row(P["final_norm"]))
    kvw = meta["kvw"]
    keep = min(WINDOW, seq)
    kv3 = kv.reshape(batch, seq, 2 * kvw)[:, seq - keep:]
    new_k = kv3[:, :, :kvw].reshape(batch, keep, meta["groups"], SWA_HEAD_DIM)
    new_v = kv3[:, :, kvw:].reshape(batch, keep, meta["groups"], SWA_HEAD_DIM)
    return (y.reshape(batch, seq, d), lat.reshape(1, batch, seq, c_dim), kr.reshape(1, batch, seq, MLA_ROPE),
            new_k, new_v)


def _trunk_decode(x3, cache_lat, cache_rope, cache_k, cache_v, P, w, meta):
    batch, n_new, d = x3.shape
    heads, c_dim, kvw, s_heads = meta["heads"], meta["c_dim"], meta["kvw"], meta["s_heads"]
    past = cache_lat.shape[1]
    w_c = cache_k.shape[1]
    qpos = past + jnp.arange(n_new)
    assert past % CHUNK == 0 and n_new <= CHUNK and w_c <= WINDOW_CHUNKS * CHUNK and w_c <= past
    t = batch * n_new
    x = x3.reshape(t, d)
    row = lambda v: v.reshape(1, -1)
    scale = (MLA_NOPE + MLA_ROPE) ** -0.5
    x = _ffn(x, row(P["ffn_norm1"][0]), w["f1g"][0], w["f1u"][0], w["f1d"][0])
    tab, c2, s2 = _rope_tables(qpos, scale)
    tab = jnp.tile(tab, (batch, 1))
    tabr = jnp.tile(jnp.concatenate([jnp.tile(c2 * scale, (1, heads)), jnp.tile(s2 * scale, (1, heads))], axis=1),
                    (batch, 1))
    ql, qr, lat, kr = _mla_proj_decode(x, row(P["mix_norm"][0]), row(P["mla_q_norm"][0]), row(P["mla_kv_norm"][0]),
                                       w, tab, tabr, heads, c_dim, scale)
    rows = n_new * heads
    ol = _mla_attn_decode(ql.reshape(t * heads, c_dim), qr.reshape(t * heads, MLA_ROPE), cache_lat, cache_rope,
                          lat, kr, rows, n_new)
    tm = _row_tile(t)
    x = pl.pallas_call(
        functools.partial(_mla_out_decode_kernel, heads=heads, c_dim=c_dim),
        grid=(t // tm,),
        in_specs=[_row_spec(tm, heads * c_dim), _const_spec(w["wuv_dec"].shape), _const_spec(w["wo"].shape),
                  _row_spec(tm, d)],
        out_specs=_row_spec(tm, d), out_shape=jax.ShapeDtypeStruct((t, d), F32),
        compiler_params=_params(1), name="mla_out_decode",
    )(ol.reshape(t, heads * c_dim), w["wuv_dec"], w["wo"], x)
    x = _ffn(x, row(P["ffn_norm2"][0]), w["f2g"][0], w["f2u"][0], w["f2d"][0])
    (kv,) = _norm_linear(x, row(P["kv_norm"]), [w["wkv"]], [F32], [1.0], "kv_shared_decode")
    x = _ffn(x, row(P["ffn_norm1"][1]), w["f1g"][1], w["f1u"][1], w["f1d"][1])
    (qs,) = _norm_linear(x, row(P["mix_norm"][1]), [w["swa_wq_dec"]], [BF16], [SWA_HEAD_DIM ** -0.5], "swa_q_decode")
    kpos = jnp.arange(past - w_c, past + n_new)
    bias_h = _bias_table(P["rel_bias"], _rel_bucket(kpos[None, :] - qpos[:, None]).astype(jnp.int32))
    bias = jnp.transpose(bias_h, (1, 0, 2)).reshape(n_new * s_heads, w_c + n_new)
    sink_rows = jnp.tile(P["swa_sinks"][0], n_new).reshape(n_new * s_heads, 1)
    srows = n_new * s_heads
    o = pl.pallas_call(
        functools.partial(_swa_decode_kernel, w_c=w_c, kv_w=kvw),
        grid=(batch,),
        in_specs=[pl.BlockSpec((srows, kvw), lambda b: (b, 0)),
                  pl.BlockSpec((None, w_c, kvw), lambda b: (b, 0, 0)),
                  pl.BlockSpec((None, w_c, kvw), lambda b: (b, 0, 0)),
                  pl.BlockSpec((n_new, 2 * kvw), lambda b: (b, 0)),
                  _const_spec(bias.shape), _const_spec(sink_rows.shape)],
        out_specs=pl.BlockSpec((srows, kvw), lambda b: (b, 0)),
        out_shape=jax.ShapeDtypeStruct((t * s_heads, kvw), BF16),
        compiler_params=_params(1), name="swa_decode",
    )(qs.reshape(t * s_heads, kvw), cache_k.reshape(batch, w_c, kvw), cache_v.reshape(batch, w_c, kvw), kv,
      bias, sink_rows)
    x = _lin_res(o.reshape(t, s_heads * kvw), w["swa_wo_dec"], x, "swa_out_decode")
    y = _ffn(x, row(P["ffn_norm2"][1]), w["f2g"][1], w["f2u"][1], w["f2d"][1], row(P["final_norm"]))
    kv3 = kv.reshape(batch, n_new, 2 * kvw)
    new_k = kv3[:, :, :kvw].reshape(batch, n_new, meta["groups"], SWA_HEAD_DIM)
    new_v = kv3[:, :, kvw:].reshape(batch, n_new, meta["groups"], SWA_HEAD_DIM)
    return (y.reshape(batch, n_new, d), lat.reshape(1, batch, n_new, c_dim), kr.reshape(1, batch, n_new, MLA_ROPE),
            new_k, new_v)


def kernel(x_prompt, x_sample, cache_mla_latent, cache_mla_krope, cache_swa_k, cache_swa_v, ffn_norm1, ffn1_w_gate, ffn1_w_up, ffn1_w_down, mix_norm, ffn_norm2, ffn2_w_gate, ffn2_w_up, ffn2_w_down, mla_w_dq, mla_q_norm, mla_w_uq, mla_w_dkv, mla_kv_norm, mla_w_uk, mla_w_uv, mla_w_o, kv_norm, w_kv_shared, swa_w_q, swa_sinks, swa_w_o, rel_bias, final_norm):
    assert ffn_norm1.shape[0] == 2 and mla_w_dq.shape[0] == 1 and swa_w_q.shape[0] == 1
    w, meta = _prep_weights(mla_w_dq[0], mla_w_uq[0], mla_w_dkv[0], mla_w_uk[0], mla_w_uv[0], mla_w_o[0],
                            w_kv_shared, swa_w_q[0], swa_w_o[0])
    per_layer = lambda a: [a[layer].astype(BF16) for layer in range(a.shape[0])]
    w.update(f1g=per_layer(ffn1_w_gate), f1u=per_layer(ffn1_w_up), f1d=per_layer(ffn1_w_down),
             f2g=per_layer(ffn2_w_gate), f2u=per_layer(ffn2_w_up), f2d=per_layer(ffn2_w_down))
    P = dict(ffn_norm1=ffn_norm1, mix_norm=mix_norm, ffn_norm2=ffn_norm2, mla_q_norm=mla_q_norm,
             mla_kv_norm=mla_kv_norm, kv_norm=kv_norm, swa_sinks=swa_sinks, rel_bias=rel_bias, final_norm=final_norm)
    y_p, lat_p, rope_p, k_p, v_p = _trunk_prompt(x_prompt, P, w, meta)
    y_s, lat_s, rope_s, k_s, v_s = _trunk_decode(x_sample, cache_mla_latent.reshape(cache_mla_latent.shape[1:]),
                                                 cache_mla_krope.reshape(cache_mla_krope.shape[1:]),
                                                 cache_swa_k, cache_swa_v, P, w, meta)
    return (y_p, y_s, lat_p, rope_p, k_p, v_p, lat_s, rope_s, k_s, v_s)
```

```python
import functools
import math

import jax
import jax.numpy as jnp
from jax import lax
from jax.experimental import pallas as pl
from jax.experimental.pallas import tpu as pltpu

F32 = jnp.float32
BF16 = jnp.bfloat16

CHUNK = 64
RMS_EPS = 1e-6
FFN_RES = 0.5
ROPE_BASE = 10000.0
WINDOW = 128
WINDOW_CHUNKS = WINDOW // CHUNK
N_BUCKETS = 32
MAX_DISTANCE = 128
NEG_INF = -1e30
MLA_NOPE = 64
MLA_ROPE = 32
MLA_V = 64
SWA_HEAD_DIM = 64

LANES = 128
ROW_TILE = 512
MLA_BQ = 512
MLA_BK = 512
MLA_DECODE_CHUNK = 1024
MLA_HEADS_PER_STEP = 4
VMEM_LIMIT = 56 * 1024 * 1024


def _params(n_axes, vmem=VMEM_LIMIT):
    return pltpu.CompilerParams(dimension_semantics=("arbitrary",) * n_axes, vmem_limit_bytes=vmem)


def _rms(xf, g):
    return xf * lax.rsqrt(jnp.mean(xf * xf, axis=-1, keepdims=True) + RMS_EPS) * g


def _dot(a, b):
    return jnp.dot(a, b, preferred_element_type=F32)


def _dot_nt(a, b):
    return lax.dot_general(a, b, (((1,), (1,)), ((), ())), preferred_element_type=F32)


def _row_spec(tm, ncols):
    return pl.BlockSpec((tm, ncols), lambda i: (i, 0))


def _const_spec(shape):
    nd = len(shape)
    return pl.BlockSpec(shape, lambda i: (0,) * nd, pipeline_mode=pl.Buffered(1))


def _row_tile(t):
    tm = min(ROW_TILE, t)
    assert t % tm == 0, (t, tm)
    return tm


def _ffn_kernel(x_ref, g_ref, wg_ref, wu_ref, wd_ref, *rest, final_norm):
    o_ref = rest[-1]
    x = x_ref[...]
    h = _rms(x, g_ref[...]).astype(BF16)
    gate = _dot(h, wg_ref[...])
    up = _dot(h, wu_ref[...])
    a = (gate * jax.nn.sigmoid(gate) * up).astype(BF16)
    out = x + FFN_RES * _dot(a, wd_ref[...])
    if final_norm:
        out = _rms(out, rest[0][...])
    o_ref[...] = out


def _layer_spec(shape, layer):
    nd = len(shape) - 1
    return pl.BlockSpec((None,) + tuple(shape[1:]), lambda i: (layer,) + (0,) * nd, pipeline_mode=pl.Buffered(1))


def _ffn(x, g, wg, wu, wd, layer, final_g=None):
    t, d = x.shape
    tm = _row_tile(t)
    ins = [x, g, wg, wu, wd]
    specs = [_row_spec(tm, d), _const_spec((1, d)), _layer_spec(wg.shape, layer), _layer_spec(wu.shape, layer),
             _layer_spec(wd.shape, layer)]
    if final_g is not None:
        ins.append(final_g)
        specs.append(_const_spec((1, d)))
    return pl.pallas_call(
        functools.partial(_ffn_kernel, final_norm=final_g is not None),
        grid=(t // tm,), in_specs=specs, out_specs=_row_spec(tm, d),
        out_shape=jax.ShapeDtypeStruct((t, d), F32), compiler_params=_params(1), name="ffn",
    )(*ins)


def _norm_linear_kernel(x_ref, g_ref, *refs, n, scales):
    h = _rms(x_ref[...], g_ref[...]).astype(BF16)
    for w_ref, o_ref, sc in zip(refs[:n], refs[n:], scales):
        y = _dot(h, w_ref[...])
        if sc != 1.0:
            y = y * sc
        o_ref[...] = y.astype(o_ref.dtype)


def _norm_linear(x, g, ws, dtypes, scales, name):
    t, d = x.shape
    tm = _row_tile(t)
    n = len(ws)
    return pl.pallas_call(
        functools.partial(_norm_linear_kernel, n=n, scales=tuple(scales)),
        grid=(t // tm,),
        in_specs=[_row_spec(tm, d), _const_spec((1, d))] + [_const_spec(w.shape) for w in ws],
        out_specs=[_row_spec(tm, w.shape[1]) for w in ws],
        out_shape=[jax.ShapeDtypeStruct((t, w.shape[1]), dt) for w, dt in zip(ws, dtypes)],
        compiler_params=_params(1), name=name,
    )(x, g, *ws)


def _lin_res_kernel(a_ref, w_ref, x_ref, o_ref):
    o_ref[...] = x_ref[...] + _dot(a_ref[...], w_ref[...])


def _lin_res(a, w, x, name):
    t, d = x.shape
    k = a.shape[1]
    tm = _row_tile(t)
    return pl.pallas_call(
        _lin_res_kernel, grid=(t // tm,),
        in_specs=[_row_spec(tm, k), _const_spec((k, d)), _row_spec(tm, d)],
        out_specs=_row_spec(tm, d), out_shape=jax.ShapeDtypeStruct((t, d), F32),
        compiler_params=_params(1), name=name,
    )(a, w, x)


def _mla_proj_kernel(x_ref, mg_ref, wdq_ref, qn_ref, wq_ref, wqr_ref, wlat_ref, kvn_ref, wkr_ref, wkrr_ref, tab_ref,
                     *refs, heads, c_dim, absorbed):
    cos_t, sin_t = tab_ref[:, 0:LANES], tab_ref[:, LANES:2 * LANES]
    nope_lane = lax.broadcasted_iota(jnp.int32, (1, LANES), 1) < MLA_NOPE
    cos_q = jnp.where(nope_lane, 1.0, cos_t)
    h = _rms(x_ref[...], mg_ref[...]).astype(BF16)
    cq = _rms(_dot(h, wdq_ref[...]), qn_ref[...]).astype(BF16)
    lat = _rms(_dot(h, wlat_ref[...]), kvn_ref[...])
    kr = _dot(h, wkr_ref[...]) * cos_t + _dot(h, wkrr_ref[...]) * sin_t
    qa = _dot(cq, wq_ref[...])
    qb = _dot(cq, wqr_ref[...])
    latb = lat.astype(BF16)
    if absorbed:
        wukd_ref, q_ref, ql_ref, lat_ref, kr_ref = refs
    else:
        wuk_ref, wuvt_ref, q_ref, k_ref, vt_ref, lat_ref, kr_ref = refs
        kn = _dot(latb, wuk_ref[...])
    lat_ref[...] = lat
    kr_ref[...] = kr[:, MLA_NOPE:MLA_NOPE + MLA_ROPE]
    for hh in range(heads):
        sl = slice(hh * LANES, (hh + 1) * LANES)
        qh = (qa[:, sl] * cos_q + qb[:, sl] * sin_t).astype(BF16)
        q_ref[:, sl] = qh
        if absorbed:
            ql_ref[:, hh * c_dim:(hh + 1) * c_dim] = _dot(qh, wukd_ref[hh]).astype(BF16)
        else:
            k_ref[:, sl] = (kn[:, sl] + kr).astype(BF16)
    if not absorbed:
        vt = _dot_nt(wuvt_ref[...], latb)
        ones_row = lax.broadcasted_iota(jnp.int32, vt.shape, 0) % LANES == MLA_V
        vt_ref[...] = jnp.where(ones_row, 1.0, vt).astype(BF16)


def _mla_attn_kernel(q_ref, k_ref, vt_ref, o_ref, m_sc, acc_sc, *, bq, bk, hp):
    i = pl.program_id(2)
    qs = [q_ref[:, hh * LANES:(hh + 1) * LANES] for hh in range(hp)]
    sls = [slice(hh * LANES, (hh + 1) * LANES) for hh in range(hp)]

    def block(j, masked):
        start = pl.multiple_of(j * bk, bk)
        ss = [_dot_nt(k_ref[pl.ds(start, bk), sls[hh]], qs[hh]) for hh in range(hp)]
        if masked:
            kc = (start + lax.broadcasted_iota(jnp.int32, (bk, 1), 0)) // CHUNK
            qc = (i * bq + lax.broadcasted_iota(jnp.int32, (1, bq), 1)) // CHUNK
            ss = [jnp.where(kc <= qc, s, NEG_INF) for s in ss]
        for hh in range(hp):
            m = m_sc[hh]
            m_new = jnp.maximum(m, jnp.max(ss[hh], axis=0, keepdims=True))
            alpha = jnp.exp2(m - m_new)
            p = jnp.exp2(ss[hh] - m_new).astype(BF16)
            acc_sc[hh] = alpha * acc_sc[hh] + _dot(vt_ref[sls[hh], pl.ds(start, bk)], p)
            m_sc[hh] = m_new

    m_sc[...] = jnp.full(m_sc.shape, NEG_INF, F32)
    acc_sc[...] = jnp.zeros(acc_sc.shape, F32)
    n_full = (i * bq) // bk

    @pl.loop(0, n_full)
    def _(j):
        block(j, False)

    for t in range(max(1, bq // bk)):
        block(n_full + t, True)
    for hh in range(hp):
        acc = acc_sc[hh]
        o = acc / acc[MLA_V:MLA_V + 1, :]
        o_ref[:, hh * LANES:(hh + 1) * LANES] = o.T.astype(BF16)


def _mla_attn_prompt(q, k, vt, batch, seq, heads, bq=MLA_BQ, bk=MLA_BK, hp=MLA_HEADS_PER_STEP):
    nq = seq // bq
    assert seq % bq == 0 and seq % bk == 0 and bq % CHUNK == 0 and heads % hp == 0
    assert bq % bk == 0 or bk % bq == 0
    w = hp * LANES
    return pl.pallas_call(
        functools.partial(_mla_attn_kernel, bq=bq, bk=bk, hp=hp),
        grid=(batch, heads // hp, nq),
        in_specs=[pl.BlockSpec((bq, w), lambda b, h, i: (b * nq + i, h)),
                  pl.BlockSpec((seq, w), lambda b, h, i: (b, h)),
                  pl.BlockSpec((w, seq), lambda b, h, i: (h, b))],
        out_specs=pl.BlockSpec((bq, w), lambda b, h, i: (b * nq + i, h)),
        out_shape=jax.ShapeDtypeStruct(q.shape, BF16),
        scratch_shapes=[pltpu.VMEM((hp, 1, bq), F32), pltpu.VMEM((hp, LANES, bq), F32)],
        compiler_params=_params(3), name="mla_attn_prompt",
    )(q, k, vt)


def _mla_attn_decode_kernel(ql_ref, q_ref, cl_ref, cr_ref, nl_ref, nr_ref, o_ref, *, heads, c_dim):
    n = ql_ref.shape[0]
    ql = jnp.concatenate([ql_ref[:, hh * c_dim:(hh + 1) * c_dim] for hh in range(heads)], axis=0)
    qr = jnp.concatenate([q_ref[:, hh * LANES + MLA_NOPE:hh * LANES + MLA_NOPE + MLA_ROPE] for hh in range(heads)],
                         axis=0)
    past = cl_ref.shape[0]
    chunk = min(past, MLA_DECODE_CHUNK)
    parts = [(cl_ref[c:c + chunk, :], cr_ref[c:c + chunk, :]) for c in range(0, past, chunk)]
    parts.append((nl_ref[...], nr_ref[...]))
    m = jnp.full((heads * n, 1), NEG_INF, F32)
    l = jnp.zeros((heads * n, 1), F32)
    acc = jnp.zeros((heads * n, c_dim), F32)
    for kl, kr in parts:
        kl, kr = kl.astype(BF16), kr.astype(BF16)
        s = _dot_nt(ql, kl) + _dot_nt(qr, kr)
        m_new = jnp.maximum(m, jnp.max(s, axis=-1, keepdims=True))
        alpha = jnp.exp2(m - m_new)
        p = jnp.exp2(s - m_new)
        l = alpha * l + jnp.sum(p, axis=-1, keepdims=True)
        acc = alpha * acc + _dot(p.astype(BF16), kl)
        m = m_new
    o = acc / l
    for hh in range(heads):
        o_ref[:, hh * c_dim:(hh + 1) * c_dim] = o[hh * n:(hh + 1) * n].astype(BF16)


def _mla_attn_decode(ql, q, cache_lat, cache_rope, new_lat, new_rope, heads, n_new):
    _, batch, past, c_dim = cache_lat.shape
    r_dim = cache_rope.shape[3]
    return pl.pallas_call(
        functools.partial(_mla_attn_decode_kernel, heads=heads, c_dim=c_dim),
        grid=(batch,),
        in_specs=[pl.BlockSpec((n_new, heads * c_dim), lambda b: (b, 0)),
                  pl.BlockSpec((n_new, heads * LANES), lambda b: (b, 0)),
                  pl.BlockSpec((None, None, past, c_dim), lambda b: (0, b, 0, 0)),
                  pl.BlockSpec((None, None, past, r_dim), lambda b: (0, b, 0, 0)),
                  pl.BlockSpec((n_new, c_dim), lambda b: (b, 0)),
                  pl.BlockSpec((n_new, r_dim), lambda b: (b, 0))],
        out_specs=pl.BlockSpec((n_new, heads * c_dim), lambda b: (b, 0)),
        out_shape=jax.ShapeDtypeStruct(ql.shape, BF16),
        compiler_params=_params(1), name="mla_attn_decode",
    )(ql, q, cache_lat, cache_rope, new_lat, new_rope)


def _mla_out_decode_kernel(ol_ref, wuv_ref, wo_ref, x_ref, o_ref, *, heads, c_dim):
    parts = [_dot(ol_ref[:, hh * c_dim:(hh + 1) * c_dim], wuv_ref[hh]).astype(BF16) for hh in range(heads)]
    o_ref[...] = x_ref[...] + _dot(jnp.concatenate(parts, axis=1), wo_ref[...])


def _bias_kernel(tab_ref, idx_ref, o_ref, *, heads):
    idx = idx_ref[...]
    for hh in range(heads):
        acc = jnp.zeros(idx.shape, F32)
        for b in range(N_BUCKETS):
            acc = jnp.where(idx == b, tab_ref[b, hh], acc)
        o_ref[hh] = acc


def _bias_table(rel_bias, idx):
    heads = rel_bias.shape[1]
    return pl.pallas_call(
        functools.partial(_bias_kernel, heads=heads),
        in_specs=[pl.BlockSpec(memory_space=pltpu.SMEM), pl.BlockSpec(idx.shape, lambda: (0, 0))],
        out_specs=pl.BlockSpec((heads,) + idx.shape, lambda: (0, 0, 0)),
        out_shape=jax.ShapeDtypeStruct((heads,) + idx.shape, F32), name="rel_bias_table",
    )(rel_bias, idx)


def _rel_bucket(rel):
    half = N_BUCKETS // 2
    max_exact = half // 2
    base = jnp.where(rel > 0, half, 0)
    n = jnp.abs(rel)
    nf = jnp.maximum(n, 1).astype(jnp.float32)
    large = max_exact + (jnp.log(nf / max_exact) / math.log(MAX_DISTANCE / max_exact)
                         * (half - max_exact)).astype(jnp.int32)
    large = jnp.minimum(large, half - 1)
    return base + jnp.where(n < max_exact, n, large)


def _sink_softmax(s, sink):
    m = jnp.maximum(jnp.max(s, axis=-1, keepdims=True), sink)
    p = jnp.exp(s - m)
    return p / (jnp.sum(p, axis=-1, keepdims=True) + jnp.exp(sink - m))


def _swa_prompt_kernel(sink_ref, q_ref, kp_ref, kc_ref, vp_ref, vc_ref, bias_ref, o_ref, *, groups, rep, blk):
    i = pl.program_id(1)
    gw = rep * SWA_HEAD_DIM
    row_chunk = lax.broadcasted_iota(jnp.int32, (blk, 1), 0) // CHUNK
    key_chunk = lax.broadcasted_iota(jnp.int32, (1, 2 * blk), 1) // CHUNK
    first = blk // CHUNK
    valid = ((key_chunk >= row_chunk) & (key_chunk <= row_chunk + WINDOW_CHUNKS)
             & ((i > 0) | (key_chunk >= first)))
    lane_head = lax.broadcasted_iota(jnp.int32, (1, gw), 1) // SWA_HEAD_DIM
    for g in range(groups):
        sl = slice(g * gw, (g + 1) * gw)
        qg = q_ref[:, sl]
        qs = jnp.concatenate([jnp.where(lane_head == r, qg, jnp.zeros_like(qg)) for r in range(rep)], axis=0)
        kg = jnp.concatenate([kp_ref[:, sl], kc_ref[:, sl]], axis=0)
        vg = jnp.concatenate([vp_ref[:, sl], vc_ref[:, sl]], axis=0)
        s = _dot_nt(qs, kg)
        ps = []
        for r in range(rep):
            hh = g * rep + r
            sh = jnp.where(valid, s[r * blk:(r + 1) * blk] + bias_ref[hh], NEG_INF)
            ps.append(_sink_softmax(sh, sink_ref[hh]).astype(BF16))
        res = _dot(jnp.concatenate(ps, axis=0), vg)
        og = jnp.zeros((blk, gw), F32)
        for r in range(rep):
            og = jnp.where(lane_head == r, res[r * blk:(r + 1) * blk], og)
        o_ref[:, sl] = og.astype(BF16)


def _swa_prompt(q, k4, v4, bias, sinks, batch, seq, groups, rep, blk=128):
    nq = seq // blk
    width = q.shape[1]
    cur = lambda b, i: (b * nq + i, 0)
    prev = lambda b, i: (b * nq + jnp.maximum(i - 1, 0), 0)
    return pl.pallas_call(
        functools.partial(_swa_prompt_kernel, groups=groups, rep=rep, blk=blk),
        grid=(batch, nq),
        in_specs=[pl.BlockSpec(memory_space=pltpu.SMEM),
                  pl.BlockSpec((blk, width), cur),
                  pl.BlockSpec((blk, width), prev), pl.BlockSpec((blk, width), cur),
                  pl.BlockSpec((blk, width), prev), pl.BlockSpec((blk, width), cur),
                  pl.BlockSpec(bias.shape, lambda b, i: (0, 0, 0), pipeline_mode=pl.Buffered(1))],
        out_specs=pl.BlockSpec((blk, width), cur),
        out_shape=jax.ShapeDtypeStruct(q.shape, BF16),
        compiler_params=_params(2), name="swa_prompt",
    )(sinks, q, k4, k4, v4, v4, bias)


def _swa_decode_kernel(q_ref, ck_ref, cv_ref, nkv_ref, bias_ref, sink_ref, o_ref, *, w_c, kv_w, s_heads):
    n = q_ref.shape[0]
    q = jnp.concatenate([q_ref[:, hh * kv_w:(hh + 1) * kv_w] for hh in range(s_heads)], axis=0)
    sink = sink_ref[...]
    nk = nkv_ref[:, 0:kv_w].astype(BF16)
    nv = nkv_ref[:, kv_w:2 * kv_w].astype(BF16)
    s1 = _dot_nt(q, ck_ref[...].astype(BF16)) + bias_ref[:, 0:w_c]
    s2 = _dot_nt(q, nk) + bias_ref[:, w_c:]
    m = jnp.maximum(jnp.maximum(jnp.max(s1, axis=-1, keepdims=True), jnp.max(s2, axis=-1, keepdims=True)), sink)
    p1 = jnp.exp(s1 - m)
    p2 = jnp.exp(s2 - m)
    den = jnp.sum(p1, axis=-1, keepdims=True) + jnp.sum(p2, axis=-1, keepdims=True) + jnp.exp(sink - m)
    p1 = (p1 / den).astype(BF16)
    p2 = (p2 / den).astype(BF16)
    o = _dot(p1, cv_ref[...].astype(BF16)) + _dot(p2, nv)
    for hh in range(s_heads):
        o_ref[:, hh * kv_w:(hh + 1) * kv_w] = o[hh * n:(hh + 1) * n].astype(BF16)


def _rope_table(pos):
    inv = ROPE_BASE ** (-jnp.arange(0, MLA_ROPE, 2, dtype=jnp.float32) / MLA_ROPE)
    ang = pos.astype(jnp.float32)[:, None] * inv[None, :]
    cos, sin = jnp.cos(ang), jnp.sin(ang)
    widths = ((0, 0), (MLA_NOPE, LANES - MLA_NOPE - MLA_ROPE))
    return jnp.concatenate([jnp.pad(jnp.concatenate([cos, cos], axis=1), widths),
                            jnp.pad(jnp.concatenate([sin, sin], axis=1), widths)], axis=1)


def _prep_weights(mla_w_dq, mla_w_uq, mla_w_dkv, mla_w_uk, mla_w_uv, mla_w_o, w_kv_shared, swa_w_q, swa_w_o, scale):
    ql, qcols = mla_w_uq.shape
    c_dim, heads, nope = mla_w_uk.shape
    rope = qcols // heads - nope
    half = rope // 2
    d = mla_w_dq.shape[0]
    pad = LANES - nope - rope
    wq3 = mla_w_uq.reshape(ql, heads, nope + rope)
    w_nope, r1, r2 = wq3[:, :, :nope], wq3[:, :, nope:nope + half], wq3[:, :, nope + half:]
    zq = lambda n: jnp.zeros((ql, heads, n), F32)
    w = {}
    w["wq"] = jnp.concatenate([w_nope, r1, r2, zq(pad)], axis=-1).reshape(ql, heads * LANES) * scale
    w["wq_rot"] = jnp.concatenate([zq(nope), -r2, r1, zq(pad)], axis=-1).reshape(ql, heads * LANES) * scale
    w["wdq"] = mla_w_dq
    w["wlat"] = mla_w_dkv[:, :c_dim]
    k1, k2 = mla_w_dkv[:, c_dim:c_dim + half], mla_w_dkv[:, c_dim + half:]
    zk = lambda n: jnp.zeros((d, n), F32)
    w["wkr"] = jnp.concatenate([zk(nope), k1, k2, zk(pad)], axis=1)
    w["wkr_rot"] = jnp.concatenate([zk(nope), -k2, k1, zk(pad)], axis=1)
    zc = jnp.zeros((c_dim, heads, LANES - nope), F32)
    w["wuk"] = jnp.concatenate([mla_w_uk, zc], axis=-1).reshape(c_dim, heads * LANES)
    w["wuv_t"] = jnp.concatenate([mla_w_uv, zc], axis=-1).reshape(c_dim, heads * LANES).T
    w["wuk_dec"] = jnp.concatenate([jnp.transpose(mla_w_uk, (1, 2, 0)),
                                    jnp.zeros((heads, LANES - nope, c_dim), F32)], axis=1)
    w["wuv_dec"] = jnp.concatenate([jnp.transpose(mla_w_uv, (1, 0, 2)),
                                    jnp.zeros((heads, c_dim, LANES - nope), F32)], axis=-1)
    vdim = mla_w_uv.shape[2]
    w["wo"] = jnp.concatenate([mla_w_o.reshape(heads, vdim, d), jnp.zeros((heads, LANES - vdim, d), F32)],
                              axis=1).reshape(heads * LANES, d)
    kvw = w_kv_shared.shape[1] // 2
    groups = kvw // SWA_HEAD_DIM
    s_heads = swa_w_q.shape[1] // SWA_HEAD_DIM
    rep = s_heads // groups
    wk = w_kv_shared[:, :kvw].reshape(d, groups, 1, SWA_HEAD_DIM)
    wv = w_kv_shared[:, kvw:].reshape(d, groups, 1, SWA_HEAD_DIM)
    w["wkv"] = w_kv_shared
    w["wk4"] = jnp.broadcast_to(wk, (d, groups, rep, SWA_HEAD_DIM)).reshape(d, s_heads * SWA_HEAD_DIM)
    w["wv4"] = jnp.broadcast_to(wv, (d, groups, rep, SWA_HEAD_DIM)).reshape(d, s_heads * SWA_HEAD_DIM)
    w["swa_wq"] = swa_w_q
    w["swa_wo"] = swa_w_o
    onehot = (jnp.arange(s_heads)[:, None] // rep == jnp.arange(groups)[None, :]).astype(F32)
    wq_h = swa_w_q.reshape(d, s_heads, SWA_HEAD_DIM)
    w["swa_wq_dec"] = (wq_h[:, :, None, :] * onehot[None, :, :, None]).reshape(d, s_heads * kvw)
    wo_h = swa_w_o.reshape(s_heads, SWA_HEAD_DIM, d)
    w["swa_wo_dec"] = (wo_h[:, None, :, :] * onehot[:, :, None, None]).reshape(s_heads * kvw, d)
    return {k: v.astype(BF16) for k, v in w.items()}, dict(heads=heads, c_dim=c_dim, groups=groups, rep=rep,
                                                            s_heads=s_heads, kvw=kvw)


def _mla_proj(x, mg, qn, kvn, w, tab, heads, c_dim, absorbed):
    t, d = x.shape
    tm = _row_tile(t)
    assert tab.shape[0] % tm == 0
    nb = tab.shape[0] // tm
    hw = heads * LANES
    consts = [mg, w["wdq"], qn, w["wq"], w["wq_rot"], w["wlat"], kvn, w["wkr"], w["wkr_rot"]]
    extra = [w["wuk_dec"]] if absorbed else [w["wuk"], w["wuv_t"]]
    common_out = [(_row_spec(tm, c_dim), jax.ShapeDtypeStruct((t, c_dim), F32)),
                  (_row_spec(tm, MLA_ROPE), jax.ShapeDtypeStruct((t, MLA_ROPE), F32))]
    q_out = (_row_spec(tm, hw), jax.ShapeDtypeStruct((t, hw), BF16))
    if absorbed:
        outs = [q_out, (_row_spec(tm, heads * c_dim), jax.ShapeDtypeStruct((t, heads * c_dim), BF16))] + common_out
    else:
        outs = [q_out, q_out, (pl.BlockSpec((hw, tm), lambda i: (0, i)), jax.ShapeDtypeStruct((hw, t), BF16))] + common_out
    return pl.pallas_call(
        functools.partial(_mla_proj_kernel, heads=heads, c_dim=c_dim, absorbed=absorbed),
        grid=(t // tm,),
        in_specs=[_row_spec(tm, d)] + [_const_spec(c.shape) for c in consts]
                 + [pl.BlockSpec((tm, 2 * LANES), lambda i: (i % nb, 0))] + [_const_spec(c.shape) for c in extra],
        out_specs=[o[0] for o in outs], out_shape=[o[1] for o in outs],
        compiler_params=_params(1), name="mla_proj_absorbed" if absorbed else "mla_proj",
    )(x, *consts, tab, *extra)


def _trunk_prompt(x3, P, w, meta):
    batch, seq, d = x3.shape
    heads, c_dim = meta["heads"], meta["c_dim"]
    x = x3.reshape(batch * seq, d)
    row = lambda v: v.reshape(1, -1)
    x = _ffn(x, row(P["ffn_norm1"][0]), w["f1g"], w["f1u"], w["f1d"], 0)
    q, k, v, lat, kr = _mla_proj(x, row(P["mix_norm"][0]), row(P["mla_q_norm"][0]), row(P["mla_kv_norm"][0]),
                                 w, _rope_table(jnp.arange(seq)), heads, c_dim, absorbed=False)
    o = _mla_attn_prompt(q, k, v, batch, seq, heads)
    x = _lin_res(o, w["wo"], x, "mla_out_prompt")
    x = _ffn(x, row(P["ffn_norm2"][0]), w["f2g"], w["f2u"], w["f2d"], 0)
    kv, k4, v4 = _norm_linear(x, row(P["kv_norm"]), [w["wkv"], w["wk4"], w["wv4"]], [F32, BF16, BF16],
                              [1.0, 1.0, 1.0], "kv_shared_prompt")
    x = _ffn(x, row(P["ffn_norm1"][1]), w["f1g"], w["f1u"], w["f1d"], 1)
    (qs,) = _norm_linear(x, row(P["mix_norm"][1]), [w["swa_wq"]], [BF16], [SWA_HEAD_DIM ** -0.5], "swa_q_prompt")
    blk = 2 * CHUNK
    rel = (jnp.arange(2 * blk) - blk)[None, :] - jnp.arange(blk)[:, None]
    bias = _bias_table(P["rel_bias"], _rel_bucket(rel).astype(jnp.int32))
    o = _swa_prompt(qs, k4, v4, bias, P["swa_sinks"][0], batch, seq, meta["groups"], meta["rep"], blk)
    x = _lin_res(o, w["swa_wo"], x, "swa_out_prompt")
    y = _ffn(x, row(P["ffn_norm2"][1]), w["f2g"], w["f2u"], w["f2d"], 1, row(P["final_norm"]))
    kvw = meta["kvw"]
    keep = min(WINDOW, seq)
    kv3 = kv.reshape(batch, seq, 2 * kvw)[:, seq - keep:]
    new_k = kv3[:, :, :kvw].reshape(batch, keep, meta["groups"], SWA_HEAD_DIM)
    new_v = kv3[:, :, kvw:].reshape(batch, keep, meta["groups"], SWA_HEAD_DIM)
    return (y.reshape(batch, seq, d), lat.reshape(1, batch, seq, c_dim), kr.reshape(1, batch, seq, MLA_ROPE),
            new_k, new_v)


def _trunk_decode(x3, cache_lat, cache_rope, cache_k, cache_v, P, w, meta):
    batch, n_new, d = x3.shape
    heads, c_dim, kvw, s_heads = meta["heads"], meta["c_dim"], meta["kvw"], meta["s_heads"]
    past = cache_lat.shape[2]
    w_c = cache_k.shape[1]
    qpos = past + jnp.arange(n_new)
    assert past % CHUNK == 0 and n_new <= CHUNK and w_c <= WINDOW_CHUNKS * CHUNK and w_c <= past
    t = batch * n_new
    x = x3.reshape(t, d)
    row = lambda v: v.reshape(1, -1)
    x = _ffn(x, row(P["ffn_norm1"][0]), w["f1g"], w["f1u"], w["f1d"], 0)
    tm = _row_tile(t)
    assert tm % n_new == 0
    tab = jnp.tile(_rope_table(qpos), (tm // n_new, 1))
    q, ql, lat, kr = _mla_proj(x, row(P["mix_norm"][0]), row(P["mla_q_norm"][0]), row(P["mla_kv_norm"][0]),
                               w, tab, heads, c_dim, absorbed=True)
    ol = _mla_attn_decode(ql, q, cache_lat, cache_rope, lat, kr, heads, n_new)
    x = pl.pallas_call(
        functools.partial(_mla_out_decode_kernel, heads=heads, c_dim=c_dim),
        grid=(t // tm,),
        in_specs=[_row_spec(tm, heads * c_dim), _const_spec(w["wuv_dec"].shape), _const_spec(w["wo"].shape),
                  _row_spec(tm, d)],
        out_specs=_row_spec(tm, d), out_shape=jax.ShapeDtypeStruct((t, d), F32),
        compiler_params=_params(1), name="mla_out_decode",
    )(ol, w["wuv_dec"], w["wo"], x)
    x = _ffn(x, row(P["ffn_norm2"][0]), w["f2g"], w["f2u"], w["f2d"], 0)
    (kv,) = _norm_linear(x, row(P["kv_norm"]), [w["wkv"]], [F32], [1.0], "kv_shared_decode")
    x = _ffn(x, row(P["ffn_norm1"][1]), w["f1g"], w["f1u"], w["f1d"], 1)
    (qs,) = _norm_linear(x, row(P["mix_norm"][1]), [w["swa_wq_dec"]], [BF16], [SWA_HEAD_DIM ** -0.5], "swa_q_decode")
    kpos = jnp.arange(past - w_c, past + n_new)
    bias_h = _bias_table(P["rel_bias"], _rel_bucket(kpos[None, :] - qpos[:, None]).astype(jnp.int32))
    bias = bias_h.reshape(s_heads * n_new, w_c + n_new)
    sink_rows = jnp.repeat(P["swa_sinks"][0], n_new).reshape(s_heads * n_new, 1)
    o = pl.pallas_call(
        functools.partial(_swa_decode_kernel, w_c=w_c, kv_w=kvw, s_heads=s_heads),
        grid=(batch,),
        in_specs=[pl.BlockSpec((n_new, s_heads * kvw), lambda b: (b, 0)),
                  pl.BlockSpec((None, w_c, kvw), lambda b: (b, 0, 0)),
                  pl.BlockSpec((None, w_c, kvw), lambda b: (b, 0, 0)),
                  pl.BlockSpec((n_new, 2 * kvw), lambda b: (b, 0)),
                  _const_spec(bias.shape), _const_spec(sink_rows.shape)],
        out_specs=pl.BlockSpec((n_new, s_heads * kvw), lambda b: (b, 0)),
        out_shape=jax.ShapeDtypeStruct((t, s_heads * kvw), BF16),
        compiler_params=_params(1), name="swa_decode",
    )(qs, cache_k.reshape(batch, w_c, kvw), cache_v.reshape(batch, w_c, kvw), kv, bias, sink_rows)
    x = _lin_res(o, w["swa_wo_dec"], x, "swa_out_decode")
    y = _ffn(x, row(P["ffn_norm2"][1]), w["f2g"], w["f2u"], w["f2d"], 1, row(P["final_norm"]))
    kv3 = kv.reshape(batch, n_new, 2 * kvw)
    new_k = kv3[:, :, :kvw].reshape(batch, n_new, meta["groups"], SWA_HEAD_DIM)
    new_v = kv3[:, :, kvw:].reshape(batch, n_new, meta["groups"], SWA_HEAD_DIM)
    return (y.reshape(batch, n_new, d), lat.reshape(1, batch, n_new, c_dim), kr.reshape(1, batch, n_new, MLA_ROPE),
            new_k, new_v)


def kernel(x_prompt, x_sample, cache_mla_latent, cache_mla_krope, cache_swa_k, cache_swa_v, ffn_norm1, ffn1_w_gate, ffn1_w_up, ffn1_w_down, mix_norm, ffn_norm2, ffn2_w_gate, ffn2_w_up, ffn2_w_down, mla_w_dq, mla_q_norm, mla_w_uq, mla_w_dkv, mla_kv_norm, mla_w_uk, mla_w_uv, mla_w_o, kv_norm, w_kv_shared, swa_w_q, swa_sinks, swa_w_o, rel_bias, final_norm):
    assert ffn_norm1.shape[0] == 2 and mla_w_dq.shape[0] == 1 and swa_w_q.shape[0] == 1
    scale = (MLA_NOPE + MLA_ROPE) ** -0.5 * math.log2(math.e)
    w, meta = _prep_weights(mla_w_dq[0], mla_w_uq[0], mla_w_dkv[0], mla_w_uk[0], mla_w_uv[0], mla_w_o[0],
                            w_kv_shared, swa_w_q[0], swa_w_o[0], scale)
    w.update(f1g=ffn1_w_gate.astype(BF16), f1u=ffn1_w_up.astype(BF16), f1d=ffn1_w_down.astype(BF16),
             f2g=ffn2_w_gate.astype(BF16), f2u=ffn2_w_up.astype(BF16), f2d=ffn2_w_down.astype(BF16))
    P = dict(ffn_norm1=ffn_norm1, mix_norm=mix_norm, ffn_norm2=ffn_norm2, mla_q_norm=mla_q_norm,
             mla_kv_norm=mla_kv_norm, kv_norm=kv_norm, swa_sinks=swa_sinks, rel_bias=rel_bias, final_norm=final_norm)
    y_p, lat_p, rope_p, k_p, v_p = _trunk_prompt(x_prompt, P, w, meta)
    y_s, lat_s, rope_s, k_s, v_s = _trunk_decode(x_sample, cache_mla_latent, cache_mla_krope,
                                                 cache_swa_k, cache_swa_v, P, w, meta)
    return (y_p, y_s, lat_p, rope_p, k_p, v_p, lat_s, rope_s, k_s, v_s)
```

```python
import functools
import math

import jax
import jax.numpy as jnp
from jax import lax
from jax.experimental import pallas as pl
from jax.experimental.pallas import tpu as pltpu

F32 = jnp.float32
BF16 = jnp.bfloat16

CHUNK = 64
RMS_EPS = 1e-6
FFN_RES = 0.5
ROPE_BASE = 10000.0
WINDOW = 128
WINDOW_CHUNKS = WINDOW // CHUNK
N_BUCKETS = 32
MAX_DISTANCE = 128
NEG_INF = -1e30
LOG2E = math.log2(math.e)
MLA_NOPE = 64
MLA_ROPE = 32
MLA_V = 64
SWA_HEAD_DIM = 64

LANES = 128
ROW_TILE = 256
MLA_BQ = 512
MLA_BK = 512
MLA_DECODE_CHUNK = 1024
MLA_HEADS_PER_STEP = 4
VMEM_LIMIT = 60 * 1024 * 1024


def _params(n_axes, vmem=VMEM_LIMIT):
    return pltpu.CompilerParams(dimension_semantics=("arbitrary",) * n_axes, vmem_limit_bytes=vmem)


def _rms(xf, g):
    return xf * lax.rsqrt(jnp.mean(xf * xf, axis=-1, keepdims=True) + RMS_EPS) * g


def _dot(a, b):
    return jnp.dot(a, b, preferred_element_type=F32)


def _dot_nt(a, b):
    return lax.dot_general(a, b, (((1,), (1,)), ((), ())), preferred_element_type=F32)


def _row_tile(t):
    tm = min(ROW_TILE, t)
    assert t % tm == 0, (t, tm)
    return tm


def _rows(a, tm):
    return a, pl.BlockSpec((tm, a.shape[1]), lambda i: (i, 0))


def _const(a):
    nd = a.ndim
    return a, pl.BlockSpec(a.shape, lambda i: (0,) * nd, pipeline_mode=pl.Buffered(1))


def _layer(a, layer):
    nd = a.ndim - 1
    return a, pl.BlockSpec((None,) + tuple(a.shape[1:]), lambda i: (layer,) + (0,) * nd,
                           pipeline_mode=pl.Buffered(1))


def _cycle(a, tm):
    assert a.shape[0] % tm == 0
    nb = a.shape[0] // tm
    return a, pl.BlockSpec((tm, a.shape[1]), lambda i: (i % nb, 0))


def _rows_out(t, tm, ncols, dtype):
    return jax.ShapeDtypeStruct((t, ncols), dtype), pl.BlockSpec((tm, ncols), lambda i: (i, 0))


def _rows_call(body, name, t, tm, ins, outs):
    return pl.pallas_call(
        body, grid=(t // tm,),
        in_specs=[s for _, s in ins], out_specs=[s for _, s in outs], out_shape=[o for o, _ in outs],
        compiler_params=_params(1), name=name,
    )(*[a for a, _ in ins])


def _ffn_ins(P, w, which, layer):
    return [_const(P["ffn_norm%d" % which][layer].reshape(1, -1)), _layer(w["f%dg" % which], layer),
            _layer(w["f%du" % which], layer), _layer(w["f%dd" % which], layer)]


def _ffn_apply(x, g_ref, wg_ref, wu_ref, wd_ref):
    h = _rms(x, g_ref[...]).astype(BF16)
    gate = _dot(h, wg_ref[...])
    up = _dot(h, wu_ref[...])
    a = (gate * jax.nn.sigmoid(gate) * up).astype(BF16)
    return x + FFN_RES * _dot(a, wd_ref[...])


def _mla_proj_apply(x, mg_ref, wdq_ref, qn_ref, wq_ref, wqr_ref, wlat_ref, kvn_ref, wkr_ref, wkrr_ref, tab_ref,
                    refs, heads, c_dim, absorbed):
    cos_t, sin_t = tab_ref[:, 0:LANES], tab_ref[:, LANES:2 * LANES]
    nope_lane = lax.broadcasted_iota(jnp.int32, (1, LANES), 1) < MLA_NOPE
    cos_q = jnp.where(nope_lane, 1.0, cos_t)
    h = _rms(x, mg_ref[...]).astype(BF16)
    cq = _rms(_dot(h, wdq_ref[...]), qn_ref[...]).astype(BF16)
    lat = _rms(_dot(h, wlat_ref[...]), kvn_ref[...])
    kr = _dot(h, wkr_ref[...]) * cos_t + _dot(h, wkrr_ref[...]) * sin_t
    qa = _dot(cq, wq_ref[...])
    qb = _dot(cq, wqr_ref[...])
    latb = lat.astype(BF16)
    if absorbed:
        wukd_ref, q_ref, ql_ref, lat_ref, kr_ref = refs
    else:
        wuk_ref, wuvt_ref, q_ref, k_ref, vt_ref, lat_ref, kr_ref = refs
        kn = _dot(latb, wuk_ref[...])
    lat_ref[...] = lat
    kr_ref[...] = kr[:, MLA_NOPE:MLA_NOPE + MLA_ROPE]
    for hh in range(heads):
        sl = slice(hh * LANES, (hh + 1) * LANES)
        qh = (qa[:, sl] * cos_q + qb[:, sl] * sin_t).astype(BF16)
        q_ref[:, sl] = qh
        if absorbed:
            ql_ref[:, hh * c_dim:(hh + 1) * c_dim] = _dot(qh, wukd_ref[hh]).astype(BF16)
        else:
            k_ref[:, sl] = (kn[:, sl] + kr).astype(BF16)
    if not absorbed:
        vt = _dot_nt(wuvt_ref[...], latb)
        ones_row = lax.broadcasted_iota(jnp.int32, vt.shape, 0) % LANES == MLA_V
        vt_ref[...] = jnp.where(ones_row, 1.0, vt).astype(BF16)


def _rows_a_kernel(x_ref, fg, wg, wu, wd, mg, wdq, qn, wq, wqr, wlat, kvn, wkr, wkrr, tab, *refs,
                   heads, c_dim, absorbed):
    n_extra = 1 if absorbed else 2
    x1_ref = refs[n_extra]
    x1 = _ffn_apply(x_ref[...], fg, wg, wu, wd)
    x1_ref[...] = x1
    _mla_proj_apply(x1, mg, wdq, qn, wq, wqr, wlat, kvn, wkr, wkrr, tab, refs[:n_extra] + refs[n_extra + 1:],
                    heads, c_dim, absorbed)


def _rows_b_kernel(a_ref, *refs, dec_heads, c_dim, tiled_kv):
    if dec_heads:
        wuv_ref, refs = refs[0], refs[1:]
        a = jnp.concatenate([_dot(a_ref[:, hh * c_dim:(hh + 1) * c_dim], wuv_ref[hh]).astype(BF16)
                             for hh in range(dec_heads)], axis=1)
    else:
        a = a_ref[...]
    wo, x_ref, fg, wg, wu, wd, kvg, wkv = refs[:8]
    x = _ffn_apply(x_ref[...] + _dot(a, wo[...]), fg, wg, wu, wd)
    kv = _dot(_rms(x, kvg[...]).astype(BF16), wkv[...])
    if tiled_kv:
        repk, repv, x_out, kv_out, k4_out, v4_out = refs[8:]
        half = kv.shape[1] // 2
        k4_out[...] = _dot(kv[:, :half].astype(BF16), repk[...]).astype(BF16)
        v4_out[...] = _dot(kv[:, half:].astype(BF16), repv[...]).astype(BF16)
    else:
        x_out, kv_out = refs[8:]
    x_out[...] = x
    kv_out[...] = kv


def _rows_c_kernel(x_ref, fg, wg, wu, wd, mg, wq, x_out, q_out, *, scale):
    x = _ffn_apply(x_ref[...], fg, wg, wu, wd)
    x_out[...] = x
    q_out[...] = (_dot(_rms(x, mg[...]).astype(BF16), wq[...]) * scale).astype(BF16)


def _rows_d_kernel(a_ref, wo, x_ref, fg, wg, wu, wd, fin_g, y_out):
    x = _ffn_apply(x_ref[...] + _dot(a_ref[...], wo[...]), fg, wg, wu, wd)
    y_out[...] = _rms(x, fin_g[...])


def _mla_attn_kernel(q_ref, k_ref, vt_ref, o_ref, m_sc, acc_sc, *, bq, bk, hp):
    i = pl.program_id(2)
    qs = [q_ref[:, hh * LANES:(hh + 1) * LANES] for hh in range(hp)]
    sls = [slice(hh * LANES, (hh + 1) * LANES) for hh in range(hp)]

    def block(j, masked):
        start = pl.multiple_of(j * bk, bk)
        ss = [_dot_nt(k_ref[pl.ds(start, bk), sls[hh]], qs[hh]) for hh in range(hp)]
        if masked:
            kc = (start + lax.broadcasted_iota(jnp.int32, (bk, 1), 0)) // CHUNK
            qc = (i * bq + lax.broadcasted_iota(jnp.int32, (1, bq), 1)) // CHUNK
            ss = [jnp.where(kc <= qc, s, NEG_INF) for s in ss]
        for hh in range(hp):
            m = m_sc[hh]
            m_new = jnp.maximum(m, jnp.max(ss[hh], axis=0, keepdims=True))
            alpha = jnp.exp2(m - m_new)
            p = jnp.exp2(ss[hh] - m_new).astype(BF16)
            acc_sc[hh] = alpha * acc_sc[hh] + _dot(vt_ref[sls[hh], pl.ds(start, bk)], p)
            m_sc[hh] = m_new

    m_sc[...] = jnp.full(m_sc.shape, NEG_INF, F32)
    acc_sc[...] = jnp.zeros(acc_sc.shape, F32)
    n_full = (i * bq) // bk

    @pl.loop(0, n_full)
    def _(j):
        block(j, False)

    for t in range(max(1, bq // bk)):
        block(n_full + t, True)
    for pair in range(hp // 2):
        halves = []
        for hh in (2 * pair, 2 * pair + 1):
            acc = acc_sc[hh]
            halves.append(acc[:MLA_V] * (1.0 / acc[MLA_V:MLA_V + 1]))
        o_ref[:, pair * LANES:(pair + 1) * LANES] = jnp.concatenate(halves, axis=0).T.astype(BF16)


def _mla_attn_prompt(q, k, vt, batch, seq, heads, bq=MLA_BQ, bk=MLA_BK, hp=MLA_HEADS_PER_STEP):
    bq, bk = min(bq, seq), min(bk, seq)
    nq = seq // bq
    assert seq % bq == 0 and seq % bk == 0 and bq % CHUNK == 0 and heads % hp == 0 and hp % 2 == 0
    assert bq % bk == 0 or bk % bq == 0
    w = hp * LANES
    return pl.pallas_call(
        functools.partial(_mla_attn_kernel, bq=bq, bk=bk, hp=hp),
        grid=(batch, heads // hp, nq),
        in_specs=[pl.BlockSpec((bq, w), lambda b, h, i: (b * nq + i, h)),
                  pl.BlockSpec((seq, w), lambda b, h, i: (b, h)),
                  pl.BlockSpec((w, seq), lambda b, h, i: (h, b))],
        out_specs=pl.BlockSpec((bq, hp * MLA_V), lambda b, h, i: (b * nq + i, h)),
        out_shape=jax.ShapeDtypeStruct((q.shape[0], heads * MLA_V), BF16),
        scratch_shapes=[pltpu.VMEM((hp, 1, bq), F32), pltpu.VMEM((hp, LANES, bq), F32)],
        compiler_params=_params(3), name="mla_attn_prompt",
    )(q, k, vt)


def _mla_attn_decode_kernel(ql_ref, q_ref, cl_ref, crt_ref, nl_ref, nr_ref, o_ref, *, heads, c_dim):
    n = ql_ref.shape[0]
    ql = jnp.concatenate([ql_ref[:, hh * c_dim:(hh + 1) * c_dim] for hh in range(heads)], axis=0)
    qr = jnp.concatenate([q_ref[:, hh * LANES + MLA_NOPE:hh * LANES + MLA_NOPE + MLA_ROPE] for hh in range(heads)],
                         axis=0)
    past = cl_ref.shape[0]
    chunk = min(past, MLA_DECODE_CHUNK)
    parts = [(cl_ref[c:c + chunk, :], crt_ref[:, c:c + chunk], _dot) for c in range(0, past, chunk)]
    parts.append((nl_ref[...], nr_ref[...], _dot_nt))
    m = jnp.full((heads * n, 1), NEG_INF, F32)
    l = jnp.zeros((heads * n, 1), F32)
    acc = jnp.zeros((heads * n, c_dim), F32)
    for kl, kr, rope_dot in parts:
        kl, kr = kl.astype(BF16), kr.astype(BF16)
        s = _dot_nt(ql, kl) + rope_dot(qr, kr)
        m_new = jnp.maximum(m, jnp.max(s, axis=-1, keepdims=True))
        alpha = jnp.exp2(m - m_new)
        p = jnp.exp2(s - m_new)
        l = alpha * l + jnp.sum(p, axis=-1, keepdims=True)
        acc = alpha * acc + _dot(p.astype(BF16), kl)
        m = m_new
    o = acc / l
    for hh in range(heads):
        o_ref[:, hh * c_dim:(hh + 1) * c_dim] = o[hh * n:(hh + 1) * n].astype(BF16)


def _mla_attn_decode(ql, q, cache_lat, cache_rope_t, new_lat, new_rope, heads, n_new):
    _, batch, past, c_dim = cache_lat.shape
    r_dim = cache_rope_t.shape[2]
    return pl.pallas_call(
        functools.partial(_mla_attn_decode_kernel, heads=heads, c_dim=c_dim),
        grid=(batch,),
        in_specs=[pl.BlockSpec((n_new, heads * c_dim), lambda b: (b, 0)),
                  pl.BlockSpec((n_new, heads * LANES), lambda b: (b, 0)),
                  pl.BlockSpec((None, None, past, c_dim), lambda b: (0, b, 0, 0)),
                  pl.BlockSpec((None, None, r_dim, past), lambda b: (0, b, 0, 0)),
                  pl.BlockSpec((n_new, c_dim), lambda b: (b, 0)),
                  pl.BlockSpec((n_new, r_dim), lambda b: (b, 0))],
        out_specs=pl.BlockSpec((n_new, heads * c_dim), lambda b: (b, 0)),
        out_shape=jax.ShapeDtypeStruct(ql.shape, BF16),
        compiler_params=_params(1), name="mla_attn_decode",
    )(ql, q, cache_lat, cache_rope_t, new_lat, new_rope)


def _bias_kernel(tab_ref, idx_ref, o_ref, *, heads):
    idx = idx_ref[...]
    for hh in range(heads):
        acc = jnp.zeros(idx.shape, F32)
        for b in range(N_BUCKETS):
            acc = jnp.where(idx == b, tab_ref[b, hh] * LOG2E, acc)
        o_ref[hh] = acc


def _bias_table(rel_bias, idx):
    heads = rel_bias.shape[1]
    return pl.pallas_call(
        functools.partial(_bias_kernel, heads=heads),
        in_specs=[pl.BlockSpec(memory_space=pltpu.SMEM), pl.BlockSpec(idx.shape, lambda: (0, 0))],
        out_specs=pl.BlockSpec((heads,) + idx.shape, lambda: (0, 0, 0)),
        out_shape=jax.ShapeDtypeStruct((heads,) + idx.shape, F32), name="rel_bias_table",
    )(rel_bias, idx)


def _rel_bucket(rel):
    half = N_BUCKETS // 2
    max_exact = half // 2
    base = jnp.where(rel > 0, half, 0)
    n = jnp.abs(rel)
    nf = jnp.maximum(n, 1).astype(jnp.float32)
    large = max_exact + (jnp.log(nf / max_exact) / math.log(MAX_DISTANCE / max_exact)
                         * (half - max_exact)).astype(jnp.int32)
    large = jnp.minimum(large, half - 1)
    return base + jnp.where(n < max_exact, n, large)


def _sink_softmax(s, sink):
    m = jnp.maximum(jnp.max(s, axis=-1, keepdims=True), sink)
    p = jnp.exp2(s - m)
    return p * (1.0 / (jnp.sum(p, axis=-1, keepdims=True) + jnp.exp2(sink - m)))


def _swa_prompt_kernel(sink_ref, q_ref, kp_ref, kc_ref, vp_ref, vc_ref, bias_ref, o_ref, *, groups, rep, blk):
    i = pl.program_id(1)
    gw = rep * SWA_HEAD_DIM
    row_chunk = lax.broadcasted_iota(jnp.int32, (blk, 1), 0) // CHUNK
    key_chunk = lax.broadcasted_iota(jnp.int32, (1, 2 * blk), 1) // CHUNK
    first = blk // CHUNK
    valid = ((key_chunk >= row_chunk) & (key_chunk <= row_chunk + WINDOW_CHUNKS)
             & ((i > 0) | (key_chunk >= first)))
    lane_head = lax.broadcasted_iota(jnp.int32, (1, gw), 1) // SWA_HEAD_DIM
    for g in range(groups):
        sl = slice(g * gw, (g + 1) * gw)
        qg = q_ref[:, sl]
        qs = jnp.concatenate([jnp.where(lane_head == r, qg, jnp.zeros_like(qg)) for r in range(rep)], axis=0)
        kg = jnp.concatenate([kp_ref[:, sl], kc_ref[:, sl]], axis=0)
        vg = jnp.concatenate([vp_ref[:, sl], vc_ref[:, sl]], axis=0)
        s = _dot_nt(qs, kg)
        ps = []
        for r in range(rep):
            hh = g * rep + r
            sh = jnp.where(valid, s[r * blk:(r + 1) * blk] + bias_ref[hh], NEG_INF)
            ps.append(_sink_softmax(sh, sink_ref[hh] * LOG2E).astype(BF16))
        res = _dot(jnp.concatenate(ps, axis=0), vg)
        og = jnp.zeros((blk, gw), F32)
        for r in range(rep):
            og = jnp.where(lane_head == r, res[r * blk:(r + 1) * blk], og)
        o_ref[:, sl] = og.astype(BF16)


def _swa_prompt(q, k4, v4, bias, sinks, batch, seq, groups, rep, blk=128):
    nq = seq // blk
    width = q.shape[1]
    cur = lambda b, i: (b * nq + i, 0)
    prev = lambda b, i: (b * nq + jnp.maximum(i - 1, 0), 0)
    return pl.pallas_call(
        functools.partial(_swa_prompt_kernel, groups=groups, rep=rep, blk=blk),
        grid=(batch, nq),
        in_specs=[pl.BlockSpec(memory_space=pltpu.SMEM),
                  pl.BlockSpec((blk, width), cur),
                  pl.BlockSpec((blk, width), prev), pl.BlockSpec((blk, width), cur),
                  pl.BlockSpec((blk, width), prev), pl.BlockSpec((blk, width), cur),
                  pl.BlockSpec(bias.shape, lambda b, i: (0, 0, 0), pipeline_mode=pl.Buffered(1))],
        out_specs=pl.BlockSpec((blk, width), cur),
        out_shape=jax.ShapeDtypeStruct(q.shape, BF16),
        compiler_params=_params(2), name="swa_prompt",
    )(sinks, q, k4, k4, v4, v4, bias)


def _swa_decode_kernel(q_ref, ck_ref, cv_ref, nkv_ref, bias_ref, sink_ref, o_ref, *, w_c, kv_w, s_heads):
    n = q_ref.shape[0]
    q = jnp.concatenate([q_ref[:, hh * kv_w:(hh + 1) * kv_w] for hh in range(s_heads)], axis=0)
    sink = sink_ref[...] * LOG2E
    nk = nkv_ref[:, 0:kv_w].astype(BF16)
    nv = nkv_ref[:, kv_w:2 * kv_w].astype(BF16)
    s1 = _dot_nt(q, ck_ref[...].astype(BF16)) + bias_ref[:, 0:w_c]
    s2 = _dot_nt(q, nk) + bias_ref[:, w_c:]
    m = jnp.maximum(jnp.maximum(jnp.max(s1, axis=-1, keepdims=True), jnp.max(s2, axis=-1, keepdims=True)), sink)
    p1 = jnp.exp2(s1 - m)
    p2 = jnp.exp2(s2 - m)
    inv = 1.0 / (jnp.sum(p1, axis=-1, keepdims=True) + jnp.sum(p2, axis=-1, keepdims=True) + jnp.exp2(sink - m))
    p1 = (p1 * inv).astype(BF16)
    p2 = (p2 * inv).astype(BF16)
    o = _dot(p1, cv_ref[...].astype(BF16)) + _dot(p2, nv)
    for hh in range(s_heads):
        o_ref[:, hh * kv_w:(hh + 1) * kv_w] = o[hh * n:(hh + 1) * n].astype(BF16)


def _rope_table(pos):
    inv = ROPE_BASE ** (-jnp.arange(0, MLA_ROPE, 2, dtype=jnp.float32) / MLA_ROPE)
    ang = pos.astype(jnp.float32)[:, None] * inv[None, :]
    cos, sin = jnp.cos(ang), jnp.sin(ang)
    widths = ((0, 0), (MLA_NOPE, LANES - MLA_NOPE - MLA_ROPE))
    return jnp.concatenate([jnp.pad(jnp.concatenate([cos, cos], axis=1), widths),
                            jnp.pad(jnp.concatenate([sin, sin], axis=1), widths)], axis=1)


def _prep_weights(mla_w_dq, mla_w_uq, mla_w_dkv, mla_w_uk, mla_w_uv, mla_w_o, w_kv_shared, swa_w_q, swa_w_o, scale):
    ql, qcols = mla_w_uq.shape
    c_dim, heads, nope = mla_w_uk.shape
    rope = qcols // heads - nope
    half = rope // 2
    d = mla_w_dq.shape[0]
    pad = LANES - nope - rope
    wq3 = mla_w_uq.reshape(ql, heads, nope + rope)
    w_nope, r1, r2 = wq3[:, :, :nope], wq3[:, :, nope:nope + half], wq3[:, :, nope + half:]
    zq = lambda n: jnp.zeros((ql, heads, n), F32)
    w = {}
    w["wq"] = jnp.concatenate([w_nope, r1, r2, zq(pad)], axis=-1).reshape(ql, heads * LANES) * scale
    w["wq_rot"] = jnp.concatenate([zq(nope), -r2, r1, zq(pad)], axis=-1).reshape(ql, heads * LANES) * scale
    w["wdq"] = mla_w_dq
    w["wlat"] = mla_w_dkv[:, :c_dim]
    k1, k2 = mla_w_dkv[:, c_dim:c_dim + half], mla_w_dkv[:, c_dim + half:]
    zk = lambda n: jnp.zeros((d, n), F32)
    w["wkr"] = jnp.concatenate([zk(nope), k1, k2, zk(pad)], axis=1)
    w["wkr_rot"] = jnp.concatenate([zk(nope), -k2, k1, zk(pad)], axis=1)
    zc = jnp.zeros((c_dim, heads, LANES - nope), F32)
    w["wuk"] = jnp.concatenate([mla_w_uk, zc], axis=-1).reshape(c_dim, heads * LANES)
    w["wuv_t"] = jnp.concatenate([mla_w_uv, zc], axis=-1).reshape(c_dim, heads * LANES).T
    w["wuk_dec"] = jnp.concatenate([jnp.transpose(mla_w_uk, (1, 2, 0)),
                                    jnp.zeros((heads, LANES - nope, c_dim), F32)], axis=1)
    w["wuv_dec"] = jnp.concatenate([jnp.transpose(mla_w_uv, (1, 0, 2)),
                                    jnp.zeros((heads, c_dim, LANES - nope), F32)], axis=-1)
    vdim = mla_w_uv.shape[2]
    assert vdim == MLA_V and nope == MLA_NOPE and rope == MLA_ROPE
    w["wo"] = mla_w_o
    w["wo_pad"] = jnp.concatenate([mla_w_o.reshape(heads, vdim, d), jnp.zeros((heads, LANES - vdim, d), F32)],
                                  axis=1).reshape(heads * LANES, d)
    kvw = w_kv_shared.shape[1] // 2
    groups = kvw // SWA_HEAD_DIM
    s_heads = swa_w_q.shape[1] // SWA_HEAD_DIM
    rep = s_heads // groups
    w["wkv"] = w_kv_shared
    w["rep_kv"] = jnp.kron(jnp.eye(groups, dtype=F32), jnp.tile(jnp.eye(SWA_HEAD_DIM, dtype=F32), (1, rep)))
    w["swa_wq"] = swa_w_q
    w["swa_wo"] = swa_w_o
    onehot = (jnp.arange(s_heads)[:, None] // rep == jnp.arange(groups)[None, :]).astype(F32)
    wq_h = swa_w_q.reshape(d, s_heads, SWA_HEAD_DIM)
    w["swa_wq_dec"] = (wq_h[:, :, None, :] * onehot[None, :, :, None]).reshape(d, s_heads * kvw)
    wo_h = swa_w_o.reshape(s_heads, SWA_HEAD_DIM, d)
    w["swa_wo_dec"] = (wo_h[:, None, :, :] * onehot[:, :, None, None]).reshape(s_heads * kvw, d)
    return {k: v.astype(BF16) for k, v in w.items()}, dict(heads=heads, c_dim=c_dim, groups=groups, rep=rep,
                                                            s_heads=s_heads, kvw=kvw)


def _rows_a(x, P, w, tab, heads, c_dim, absorbed):
    t, d = x.shape
    tm = _row_tile(t)
    hw = heads * LANES
    row = lambda v: v.reshape(1, -1)
    ins = ([_rows(x, tm)] + _ffn_ins(P, w, 1, 0)
           + [_const(row(P["mix_norm"][0])), _const(w["wdq"]), _const(row(P["mla_q_norm"][0])), _const(w["wq"]),
              _const(w["wq_rot"]), _const(w["wlat"]), _const(row(P["mla_kv_norm"][0])), _const(w["wkr"]),
              _const(w["wkr_rot"]), _cycle(tab, tm)]
           + ([_const(w["wuk_dec"])] if absorbed else [_const(w["wuk"]), _const(w["wuv_t"])]))
    outs = [_rows_out(t, tm, d, F32), _rows_out(t, tm, hw, BF16)]
    if absorbed:
        outs.append(_rows_out(t, tm, heads * c_dim, BF16))
    else:
        outs += [_rows_out(t, tm, hw, BF16),
                 (jax.ShapeDtypeStruct((hw, t), BF16), pl.BlockSpec((hw, tm), lambda i: (0, i)))]
    outs += [_rows_out(t, tm, c_dim, F32), _rows_out(t, tm, MLA_ROPE, F32)]
    return _rows_call(functools.partial(_rows_a_kernel, heads=heads, c_dim=c_dim, absorbed=absorbed),
                      "rows_a_absorbed" if absorbed else "rows_a", t, tm, ins, outs)


def _rows_b(a, x, P, w, meta, decode):
    t, d = x.shape
    tm = _row_tile(t)
    kvw = meta["kvw"]
    ins = [_rows(a, tm)]
    if decode:
        ins += [_const(w["wuv_dec"]), _const(w["wo_pad"])]
    else:
        ins += [_const(w["wo"])]
    ins += [_rows(x, tm)] + _ffn_ins(P, w, 2, 0) + [_const(P["kv_norm"].reshape(1, -1)), _const(w["wkv"])]
    outs = [_rows_out(t, tm, d, F32), _rows_out(t, tm, 2 * kvw, F32)]
    if not decode:
        ins += [_const(w["rep_kv"]), _const(w["rep_kv"])]
        outs += [_rows_out(t, tm, meta["s_heads"] * SWA_HEAD_DIM, BF16)] * 2
    return _rows_call(functools.partial(_rows_b_kernel, dec_heads=meta["heads"] if decode else 0,
                                        c_dim=meta["c_dim"], tiled_kv=not decode),
                      "rows_b_decode" if decode else "rows_b", t, tm, ins, outs)


def _rows_c(x, P, w, wq, name):
    t, d = x.shape
    tm = _row_tile(t)
    ins = [_rows(x, tm)] + _ffn_ins(P, w, 1, 1) + [_const(P["mix_norm"][1].reshape(1, -1)), _const(wq)]
    outs = [_rows_out(t, tm, d, F32), _rows_out(t, tm, wq.shape[1], BF16)]
    return _rows_call(functools.partial(_rows_c_kernel, scale=SWA_HEAD_DIM ** -0.5 * LOG2E), name, t, tm, ins, outs)


def _rows_d(a, x, P, w, wo, name):
    t, d = x.shape
    tm = _row_tile(t)
    ins = ([_rows(a, tm), _const(wo), _rows(x, tm)] + _ffn_ins(P, w, 2, 1)
           + [_const(P["final_norm"].reshape(1, -1))])
    return _rows_call(_rows_d_kernel, name, t, tm, ins, [_rows_out(t, tm, d, F32)])[0]


def _trunk_prompt(x3, P, w, meta):
    batch, seq, d = x3.shape
    heads, c_dim, kvw = meta["heads"], meta["c_dim"], meta["kvw"]
    x = x3.reshape(batch * seq, d)
    x, q, k, vt, lat, kr = _rows_a(x, P, w, _rope_table(jnp.arange(seq)), heads, c_dim, absorbed=False)
    o = _mla_attn_prompt(q, k, vt, batch, seq, heads)
    x, kv, k4, v4 = _rows_b(o, x, P, w, meta, decode=False)
    x, qs = _rows_c(x, P, w, w["swa_wq"], "rows_c")
    blk = 2 * CHUNK
    rel = (jnp.arange(2 * blk) - blk)[None, :] - jnp.arange(blk)[:, None]
    bias = _bias_table(P["rel_bias"], _rel_bucket(rel).astype(jnp.int32))
    o = _swa_prompt(qs, k4, v4, bias, P["swa_sinks"][0], batch, seq, meta["groups"], meta["rep"], blk)
    y = _rows_d(o, x, P, w, w["swa_wo"], "rows_d")
    keep = min(WINDOW, seq)
    kv3 = kv.reshape(batch, seq, 2 * kvw)[:, seq - keep:]
    new_k = kv3[:, :, :kvw].reshape(batch, keep, meta["groups"], SWA_HEAD_DIM)
    new_v = kv3[:, :, kvw:].reshape(batch, keep, meta["groups"], SWA_HEAD_DIM)
    return (y.reshape(batch, seq, d), lat.reshape(1, batch, seq, c_dim), kr.reshape(1, batch, seq, MLA_ROPE),
            new_k, new_v)


def _trunk_decode(x3, cache_lat, cache_rope, cache_k, cache_v, P, w, meta):
    batch, n_new, d = x3.shape
    heads, c_dim, kvw, s_heads = meta["heads"], meta["c_dim"], meta["kvw"], meta["s_heads"]
    past = cache_lat.shape[2]
    w_c = cache_k.shape[1]
    qpos = past + jnp.arange(n_new)
    assert past % CHUNK == 0 and n_new <= CHUNK and w_c <= WINDOW_CHUNKS * CHUNK and w_c <= past
    t = batch * n_new
    x = x3.reshape(t, d)
    tm = _row_tile(t)
    assert tm % n_new == 0
    tab = jnp.tile(_rope_table(qpos), (tm // n_new, 1))
    x, q, ql, lat, kr = _rows_a(x, P, w, tab, heads, c_dim, absorbed=True)
    ol = _mla_attn_decode(ql, q, cache_lat, jnp.swapaxes(cache_rope, 2, 3), lat, kr, heads, n_new)
    x, kv = _rows_b(ol, x, P, w, meta, decode=True)
    x, qs = _rows_c(x, P, w, w["swa_wq_dec"], "rows_c_decode")
    kpos = jnp.arange(past - w_c, past + n_new)
    bias_h = _bias_table(P["rel_bias"], _rel_bucket(kpos[None, :] - qpos[:, None]).astype(jnp.int32))
    bias = bias_h.reshape(s_heads * n_new, w_c + n_new)
    sink_rows = jnp.repeat(P["swa_sinks"][0], n_new).reshape(s_heads * n_new, 1)
    const2 = lambda a: pl.BlockSpec(a.shape, lambda b: (0, 0), pipeline_mode=pl.Buffered(1))
    o = pl.pallas_call(
        functools.partial(_swa_decode_kernel, w_c=w_c, kv_w=kvw, s_heads=s_heads),
        grid=(batch,),
        in_specs=[pl.BlockSpec((n_new, s_heads * kvw), lambda b: (b, 0)),
                  pl.BlockSpec((None, w_c, kvw), lambda b: (b, 0, 0)),
                  pl.BlockSpec((None, w_c, kvw), lambda b: (b, 0, 0)),
                  pl.BlockSpec((n_new, 2 * kvw), lambda b: (b, 0)),
                  const2(bias), const2(sink_rows)],
        out_specs=pl.BlockSpec((n_new, s_heads * kvw), lambda b: (b, 0)),
        out_shape=jax.ShapeDtypeStruct((t, s_heads * kvw), BF16),
        compiler_params=_params(1), name="swa_decode",
    )(qs, cache_k.reshape(batch, w_c, kvw), cache_v.reshape(batch, w_c, kvw), kv, bias, sink_rows)
    y = _rows_d(o, x, P, w, w["swa_wo_dec"], "rows_d_decode")
    kv3 = kv.reshape(batch, n_new, 2 * kvw)
    new_k = kv3[:, :, :kvw].reshape(batch, n_new, meta["groups"], SWA_HEAD_DIM)
    new_v = kv3[:, :, kvw:].reshape(batch, n_new, meta["groups"], SWA_HEAD_DIM)
    return (y.reshape(batch, n_new, d), lat.reshape(1, batch, n_new, c_dim), kr.reshape(1, batch, n_new, MLA_ROPE),
            new_k, new_v)


def kernel(x_prompt, x_sample, cache_mla_latent, cache_mla_krope, cache_swa_k, cache_swa_v, ffn_norm1, ffn1_w_gate, ffn1_w_up, ffn1_w_down, mix_norm, ffn_norm2, ffn2_w_gate, ffn2_w_up, ffn2_w_down, mla_w_dq, mla_q_norm, mla_w_uq, mla_w_dkv, mla_kv_norm, mla_w_uk, mla_w_uv, mla_w_o, kv_norm, w_kv_shared, swa_w_q, swa_sinks, swa_w_o, rel_bias, final_norm):
    assert ffn_norm1.shape[0] == 2 and mla_w_dq.shape[0] == 1 and swa_w_q.shape[0] == 1
    scale = (MLA_NOPE + MLA_ROPE) ** -0.5 * LOG2E
    w, meta = _prep_weights(mla_w_dq[0], mla_w_uq[0], mla_w_dkv[0], mla_w_uk[0], mla_w_uv[0], mla_w_o[0],
                            w_kv_shared, swa_w_q[0], swa_w_o[0], scale)
    w.update(f1g=ffn1_w_gate.astype(BF16), f1u=ffn1_w_up.astype(BF16), f1d=ffn1_w_down.astype(BF16),
             f2g=ffn2_w_gate.astype(BF16), f2u=ffn2_w_up.astype(BF16), f2d=ffn2_w_down.astype(BF16))
    P = dict(ffn_norm1=ffn_norm1, mix_norm=mix_norm, ffn_norm2=ffn_norm2, mla_q_norm=mla_q_norm,
             mla_kv_norm=mla_kv_norm, kv_norm=kv_norm, swa_sinks=swa_sinks, rel_bias=rel_bias, final_norm=final_norm)
    y_p, lat_p, rope_p, k_p, v_p = _trunk_prompt(x_prompt, P, w, meta)
    y_s, lat_s, rope_s, k_s, v_s = _trunk_decode(x_sample, cache_mla_latent, cache_mla_krope,
                                                 cache_swa_k, cache_swa_v, P, w, meta)
    return (y_p, y_s, lat_p, rope_p, k_p, v_p, lat_s, rope_s, k_s, v_s)
```

```python
import functools
import math

import jax
import jax.numpy as jnp
from jax import lax
from jax.experimental import pallas as pl
from jax.experimental.pallas import tpu as pltpu

F32 = jnp.float32
BF16 = jnp.bfloat16

CHUNK = 64
RMS_EPS = 1e-6
FFN_RES = 0.5
ROPE_BASE = 10000.0
WINDOW = 128
WINDOW_CHUNKS = WINDOW // CHUNK
N_BUCKETS = 32
MAX_DISTANCE = 128
NEG_INF = -1e30
LOG2E = math.log2(math.e)
MLA_NOPE = 64
MLA_ROPE = 32
MLA_V = 64
MLA_VT_ROWS = 80
SWA_HEAD_DIM = 64

LANES = 128
ROW_TILE = 256
MLA_BQ = 512
MLA_BK = 512
MLA_DECODE_CHUNK = 1024
MLA_HEADS_PER_STEP = 4
VMEM_LIMIT = 60 * 1024 * 1024


def _params(n_axes, vmem=VMEM_LIMIT):
    return pltpu.CompilerParams(dimension_semantics=("arbitrary",) * n_axes, vmem_limit_bytes=vmem)


def _rms(xf, g):
    return xf * lax.rsqrt(jnp.mean(xf * xf, axis=-1, keepdims=True) + RMS_EPS) * g


def _dot(a, b):
    return jnp.dot(a, b, preferred_element_type=F32)


def _dot_nt(a, b):
    return lax.dot_general(a, b, (((1,), (1,)), ((), ())), preferred_element_type=F32)


def _row_tile(t):
    tm = min(ROW_TILE, t)
    assert t % tm == 0, (t, tm)
    return tm


def _rows(a, tm):
    return a, pl.BlockSpec((tm, a.shape[1]), lambda i: (i, 0))


def _const(a):
    nd = a.ndim
    return a, pl.BlockSpec(a.shape, lambda i: (0,) * nd, pipeline_mode=pl.Buffered(1))


def _layer(a, layer):
    nd = a.ndim - 1
    return a, pl.BlockSpec((None,) + tuple(a.shape[1:]), lambda i: (layer,) + (0,) * nd,
                           pipeline_mode=pl.Buffered(1))


def _cycle(a, tm):
    assert a.shape[0] % tm == 0
    nb = a.shape[0] // tm
    return a, pl.BlockSpec((tm, a.shape[1]), lambda i: (i % nb, 0))


def _rows_out(t, tm, ncols, dtype):
    return jax.ShapeDtypeStruct((t, ncols), dtype), pl.BlockSpec((tm, ncols), lambda i: (i, 0))


def _rows_call(body, name, t, tm, ins, outs):
    return pl.pallas_call(
        body, grid=(t // tm,),
        in_specs=[s for _, s in ins], out_specs=[s for _, s in outs], out_shape=[o for o, _ in outs],
        compiler_params=_params(1), name=name,
    )(*[a for a, _ in ins])


def _ffn_ins(P, w, which, layer):
    return [_const(P["ffn_norm%d" % which][layer].reshape(1, -1)), _layer(w["f%dg" % which], layer),
            _layer(w["f%du" % which], layer), _layer(w["f%dd" % which], layer)]


def _ffn_apply(x, g_ref, wg_ref, wu_ref, wd_ref):
    h = _rms(x, g_ref[...]).astype(BF16)
    gate = _dot(h, wg_ref[...])
    up = _dot(h, wu_ref[...])
    a = (gate * jax.nn.sigmoid(gate) * up).astype(BF16)
    return x + FFN_RES * _dot(a, wd_ref[...])


def _mla_proj_apply(x, mg_ref, wdq_ref, qn_ref, wq_ref, wqr_ref, wlat_ref, kvn_ref, wkr_ref, wkrr_ref, tab_ref,
                    refs, heads, c_dim, absorbed):
    cos_t, sin_t = tab_ref[:, 0:LANES], tab_ref[:, LANES:2 * LANES]
    nope_lane = lax.broadcasted_iota(jnp.int32, (1, LANES), 1) < MLA_NOPE
    cos_q = jnp.where(nope_lane, 1.0, cos_t)
    h = _rms(x, mg_ref[...]).astype(BF16)
    cq = _rms(_dot(h, wdq_ref[...]), qn_ref[...]).astype(BF16)
    lat = _rms(_dot(h, wlat_ref[...]), kvn_ref[...])
    kr = _dot(h, wkr_ref[...]) * cos_t + _dot(h, wkrr_ref[...]) * sin_t
    qa = _dot(cq, wq_ref[...])
    qb = _dot(cq, wqr_ref[...])
    latb = lat.astype(BF16)
    if absorbed:
        wukd_ref, q_ref, ql_ref, lat_ref, kr_ref = refs
    else:
        wuk_ref, wuvt_ref, q_ref, k_ref, vt_ref, lat_ref, kr_ref = refs
        kn = _dot(latb, wuk_ref[...])
    lat_ref[...] = lat
    kr_ref[...] = kr[:, MLA_NOPE:MLA_NOPE + MLA_ROPE]
    for hh in range(heads):
        sl = slice(hh * LANES, (hh + 1) * LANES)
        qh = (qa[:, sl] * cos_q + qb[:, sl] * sin_t).astype(BF16)
        q_ref[:, sl] = qh
        if absorbed:
            ql_ref[:, hh * c_dim:(hh + 1) * c_dim] = _dot(qh, wukd_ref[hh]).astype(BF16)
        else:
            k_ref[:, sl] = (kn[:, sl] + kr).astype(BF16)
    if not absorbed:
        vt = _dot_nt(wuvt_ref[...], latb)
        ones_row = lax.broadcasted_iota(jnp.int32, vt.shape, 0) % MLA_VT_ROWS == MLA_V
        vt_ref[...] = jnp.where(ones_row, 1.0, vt).astype(BF16)


def _rows_a_kernel(x_ref, fg, wg, wu, wd, mg, wdq, qn, wq, wqr, wlat, kvn, wkr, wkrr, tab, *refs,
                   heads, c_dim, absorbed):
    n_extra = 1 if absorbed else 2
    x1_ref = refs[n_extra]
    x1 = _ffn_apply(x_ref[...], fg, wg, wu, wd)
    x1_ref[...] = x1
    _mla_proj_apply(x1, mg, wdq, qn, wq, wqr, wlat, kvn, wkr, wkrr, tab, refs[:n_extra] + refs[n_extra + 1:],
                    heads, c_dim, absorbed)


def _rows_b_kernel(a_ref, *refs, dec_heads, c_dim, tiled_kv):
    if dec_heads:
        wuv_ref, refs = refs[0], refs[1:]
        a = jnp.concatenate([_dot(a_ref[:, hh * c_dim:(hh + 1) * c_dim], wuv_ref[hh]).astype(BF16)
                             for hh in range(dec_heads)], axis=1)
    else:
        a = a_ref[...]
    wo, x_ref, fg, wg, wu, wd, kvg, wkv = refs[:8]
    x = _ffn_apply(x_ref[...] + _dot(a, wo[...]), fg, wg, wu, wd)
    kv = _dot(_rms(x, kvg[...]).astype(BF16), wkv[...])
    if tiled_kv:
        repk, eye, x_out, kv_out, k2_out, vt_out = refs[8:]
        half = kv.shape[1] // 2
        k2_out[...] = _dot(kv[:, :half].astype(BF16), repk[...]).astype(BF16)
        vt_out[...] = _dot_nt(eye[...], kv[:, half:].astype(BF16)).astype(BF16)
    else:
        x_out, kv_out = refs[8:]
    x_out[...] = x
    kv_out[...] = kv


def _rows_c_kernel(x_ref, fg, wg, wu, wd, mg, wq, x_out, q_out, *, scale, transposed):
    x = _ffn_apply(x_ref[...], fg, wg, wu, wd)
    x_out[...] = x
    h = _rms(x, mg[...]).astype(BF16)
    q = _dot_nt(wq[...], h) if transposed else _dot(h, wq[...])
    q_out[...] = (q * scale).astype(BF16)


def _rows_d_kernel(a_ref, wo, x_ref, fg, wg, wu, wd, fin_g, y_out):
    x = _ffn_apply(x_ref[...] + _dot(a_ref[...], wo[...]), fg, wg, wu, wd)
    y_out[...] = _rms(x, fin_g[...])


def _mla_attn_kernel(q_ref, k_ref, vt_ref, o_ref, m_sc, acc_sc, *, bq, bk, hp):
    i = pl.program_id(2)
    qs = [q_ref[:, hh * LANES:(hh + 1) * LANES] for hh in range(hp)]
    sls = [slice(hh * LANES, (hh + 1) * LANES) for hh in range(hp)]

    def block(j, masked):
        start = pl.multiple_of(j * bk, bk)
        ss = [_dot_nt(k_ref[pl.ds(start, bk), sls[hh]], qs[hh]) for hh in range(hp)]
        if masked:
            kc = (start + lax.broadcasted_iota(jnp.int32, (bk, 1), 0)) // CHUNK
            qc = (i * bq + lax.broadcasted_iota(jnp.int32, (1, bq), 1)) // CHUNK
            ss = [jnp.where(kc <= qc, s, NEG_INF) for s in ss]
        for hh in range(hp):
            m = m_sc[hh]
            m_new = jnp.maximum(m, jnp.max(ss[hh], axis=0, keepdims=True))
            alpha = jnp.exp2(m - m_new)
            p = jnp.exp2(ss[hh] - m_new).astype(BF16)
            vt = vt_ref[hh * MLA_VT_ROWS:(hh + 1) * MLA_VT_ROWS, pl.ds(start, bk)]
            acc_sc[hh] = alpha * acc_sc[hh] + _dot(vt, p)
            m_sc[hh] = m_new

    m_sc[...] = jnp.full(m_sc.shape, NEG_INF, F32)
    acc_sc[...] = jnp.zeros(acc_sc.shape, F32)
    n_full = (i * bq) // bk

    @pl.loop(0, n_full)
    def _(j):
        block(j, False)

    for t in range(max(1, bq // bk)):
        block(n_full + t, True)
    for pair in range(hp // 2):
        halves = []
        for hh in (2 * pair, 2 * pair + 1):
            acc = acc_sc[hh]
            halves.append(acc[:MLA_V] * (1.0 / acc[MLA_V:MLA_V + 1]))
        o_ref[:, pair * LANES:(pair + 1) * LANES] = jnp.concatenate(halves, axis=0).T.astype(BF16)


def _mla_attn_prompt(q, k, vt, batch, seq, heads, bq=MLA_BQ, bk=MLA_BK, hp=MLA_HEADS_PER_STEP):
    bq, bk = min(bq, seq), min(bk, seq)
    nq = seq // bq
    assert seq % bq == 0 and seq % bk == 0 and bq % CHUNK == 0 and heads % hp == 0 and hp % 2 == 0
    assert bq % bk == 0 or bk % bq == 0
    w = hp * LANES
    return pl.pallas_call(
        functools.partial(_mla_attn_kernel, bq=bq, bk=bk, hp=hp),
        grid=(batch, heads // hp, nq),
        in_specs=[pl.BlockSpec((bq, w), lambda b, h, i: (b * nq + i, h)),
                  pl.BlockSpec((seq, w), lambda b, h, i: (b, h)),
                  pl.BlockSpec((hp * MLA_VT_ROWS, seq), lambda b, h, i: (h, b))],
        out_specs=pl.BlockSpec((bq, hp * MLA_V), lambda b, h, i: (b * nq + i, h)),
        out_shape=jax.ShapeDtypeStruct((q.shape[0], heads * MLA_V), BF16),
        scratch_shapes=[pltpu.VMEM((hp, 1, bq), F32), pltpu.VMEM((hp, MLA_VT_ROWS, bq), F32)],
        compiler_params=_params(3), name="mla_attn_prompt",
    )(q, k, vt)


def _mla_attn_decode_kernel(ql_ref, q_ref, cl_ref, crt_ref, nl_ref, nr_ref, o_ref, *, heads, c_dim):
    n = ql_ref.shape[0]
    ql = jnp.concatenate([ql_ref[:, hh * c_dim:(hh + 1) * c_dim] for hh in range(heads)], axis=0)
    qr = jnp.concatenate([q_ref[:, hh * LANES + MLA_NOPE:hh * LANES + MLA_NOPE + MLA_ROPE] for hh in range(heads)],
                         axis=0)
    past = cl_ref.shape[0]
    chunk = min(past, MLA_DECODE_CHUNK)
    parts = [(cl_ref[c:c + chunk, :], crt_ref[:, c:c + chunk], _dot) for c in range(0, past, chunk)]
    parts.append((nl_ref[...], nr_ref[...], _dot_nt))
    m = jnp.full((heads * n, 1), NEG_INF, F32)
    l = jnp.zeros((heads * n, 1), F32)
    acc = jnp.zeros((heads * n, c_dim), F32)
    for kl, kr, rope_dot in parts:
        kl, kr = kl.astype(BF16), kr.astype(BF16)
        s = _dot_nt(ql, kl) + rope_dot(qr, kr)
        m_new = jnp.maximum(m, jnp.max(s, axis=-1, keepdims=True))
        alpha = jnp.exp2(m - m_new)
        p = jnp.exp2(s - m_new)
        l = alpha * l + jnp.sum(p, axis=-1, keepdims=True)
        acc = alpha * acc + _dot(p.astype(BF16), kl)
        m = m_new
    o = acc / l
    for hh in range(heads):
        o_ref[:, hh * c_dim:(hh + 1) * c_dim] = o[hh * n:(hh + 1) * n].astype(BF16)


def _mla_attn_decode(ql, q, cache_lat, cache_rope_t, new_lat, new_rope, heads, n_new):
    _, batch, past, c_dim = cache_lat.shape
    r_dim = cache_rope_t.shape[2]
    return pl.pallas_call(
        functools.partial(_mla_attn_decode_kernel, heads=heads, c_dim=c_dim),
        grid=(batch,),
        in_specs=[pl.BlockSpec((n_new, heads * c_dim), lambda b: (b, 0)),
                  pl.BlockSpec((n_new, heads * LANES), lambda b: (b, 0)),
                  pl.BlockSpec((None, None, past, c_dim), lambda b: (0, b, 0, 0)),
                  pl.BlockSpec((None, None, r_dim, past), lambda b: (0, b, 0, 0)),
                  pl.BlockSpec((n_new, c_dim), lambda b: (b, 0)),
                  pl.BlockSpec((n_new, r_dim), lambda b: (b, 0))],
        out_specs=pl.BlockSpec((n_new, heads * c_dim), lambda b: (b, 0)),
        out_shape=jax.ShapeDtypeStruct(ql.shape, BF16),
        compiler_params=_params(1), name="mla_attn_decode",
    )(ql, q, cache_lat, cache_rope_t, new_lat, new_rope)


def _bias_kernel(tab_ref, idx_ref, o_ref, *, heads):
    idx = idx_ref[...]
    for hh in range(heads):
        acc = jnp.zeros(idx.shape, F32)
        for b in range(N_BUCKETS):
            acc = jnp.where(idx == b, tab_ref[b, hh] * LOG2E, acc)
        o_ref[hh] = acc


def _bias_table(rel_bias, idx):
    heads = rel_bias.shape[1]
    return pl.pallas_call(
        functools.partial(_bias_kernel, heads=heads),
        in_specs=[pl.BlockSpec(memory_space=pltpu.SMEM), pl.BlockSpec(idx.shape, lambda: (0, 0))],
        out_specs=pl.BlockSpec((heads,) + idx.shape, lambda: (0, 0, 0)),
        out_shape=jax.ShapeDtypeStruct((heads,) + idx.shape, F32), name="rel_bias_table",
    )(rel_bias, idx)


def _rel_bucket(rel):
    half = N_BUCKETS // 2
    max_exact = half // 2
    base = jnp.where(rel > 0, half, 0)
    n = jnp.abs(rel)
    nf = jnp.maximum(n, 1).astype(jnp.float32)
    large = max_exact + (jnp.log(nf / max_exact) / math.log(MAX_DISTANCE / max_exact)
                         * (half - max_exact)).astype(jnp.int32)
    large = jnp.minimum(large, half - 1)
    return base + jnp.where(n < max_exact, n, large)


def _swa_prompt_kernel(sink_ref, qt_ref, kp_ref, kc_ref, vtp_ref, vtc_ref, bias_ref, o_ref, *, s_heads, rep, blk):
    i = pl.program_id(1)
    key_chunk = lax.broadcasted_iota(jnp.int32, (2 * blk, 1), 0) // CHUNK
    row_chunk = lax.broadcasted_iota(jnp.int32, (1, blk), 1) // CHUNK
    first = blk // CHUNK
    valid = ((key_chunk >= row_chunk) & (key_chunk <= row_chunk + WINDOW_CHUNKS)
             & ((i > 0) | (key_chunk >= first)))
    dh = SWA_HEAD_DIM
    ss = []
    for hh in range(s_heads):
        g, pair, slot = hh // rep, hh // 2, hh % 2
        ksl = slice(g * 4 * dh + slot * 2 * dh, g * 4 * dh + (slot + 1) * 2 * dh)
        k = jnp.concatenate([kp_ref[:, ksl], kc_ref[:, ksl]], axis=0)
        ss.append(_dot(k, qt_ref[pair * 2 * dh:(pair + 1) * 2 * dh, :]))
    ps, invs = [], []
    for hh, s in enumerate(ss):
        s = jnp.where(valid, s + bias_ref[hh], NEG_INF)
        sink = sink_ref[hh] * LOG2E
        m = jnp.maximum(jnp.max(s, axis=0, keepdims=True), sink)
        p = jnp.exp2(s - m)
        invs.append(1.0 / (jnp.sum(p, axis=0, keepdims=True) + jnp.exp2(sink - m)))
        ps.append(p.astype(BF16))
    outs = []
    for hh, p in enumerate(ps):
        g = hh // rep
        vt = jnp.concatenate([vtp_ref[g * dh:(g + 1) * dh, :], vtc_ref[g * dh:(g + 1) * dh, :]], axis=1)
        outs.append(_dot(vt, p) * invs[hh])
    for pair in range(s_heads // 2):
        o_ref[:, pair * 2 * dh:(pair + 1) * 2 * dh] = (
            jnp.concatenate(outs[2 * pair:2 * pair + 2], axis=0).T.astype(BF16))


def _swa_prompt(qt, k2, vt, bias_t, sinks, batch, seq, s_heads, rep, blk=128):
    nq = seq // blk
    kw = k2.shape[1]
    cur = lambda b, i: (b * nq + i, 0)
    prev = lambda b, i: (b * nq + jnp.maximum(i - 1, 0), 0)
    cur_t = lambda b, i: (0, b * nq + i)
    prev_t = lambda b, i: (0, b * nq + jnp.maximum(i - 1, 0))
    assert rep == 4 and kw == (s_heads // rep) * 4 * SWA_HEAD_DIM
    return pl.pallas_call(
        functools.partial(_swa_prompt_kernel, s_heads=s_heads, rep=rep, blk=blk),
        grid=(batch, nq),
        in_specs=[pl.BlockSpec(memory_space=pltpu.SMEM),
                  pl.BlockSpec((qt.shape[0], blk), cur_t),
                  pl.BlockSpec((blk, kw), prev), pl.BlockSpec((blk, kw), cur),
                  pl.BlockSpec((vt.shape[0], blk), prev_t), pl.BlockSpec((vt.shape[0], blk), cur_t),
                  pl.BlockSpec(bias_t.shape, lambda b, i: (0, 0, 0), pipeline_mode=pl.Buffered(1))],
        out_specs=pl.BlockSpec((blk, qt.shape[0]), cur),
        out_shape=jax.ShapeDtypeStruct((qt.shape[1], qt.shape[0]), BF16),
        compiler_params=_params(2), name="swa_prompt",
    )(sinks, qt, k2, k2, vt, vt, bias_t)


def _swa_decode_kernel(q_ref, ck_ref, cv_ref, nkv_ref, bias_ref, sink_ref, o_ref, *, w_c, kv_w, s_heads):
    n = q_ref.shape[0]
    q = jnp.concatenate([q_ref[:, hh * kv_w:(hh + 1) * kv_w] for hh in range(s_heads)], axis=0)
    sink = sink_ref[...] * LOG2E
    nk = nkv_ref[:, 0:kv_w].astype(BF16)
    nv = nkv_ref[:, kv_w:2 * kv_w].astype(BF16)
    s1 = _dot_nt(q, ck_ref[...].astype(BF16)) + bias_ref[:, 0:w_c]
    s2 = _dot_nt(q, nk) + bias_ref[:, w_c:]
    m = jnp.maximum(jnp.maximum(jnp.max(s1, axis=-1, keepdims=True), jnp.max(s2, axis=-1, keepdims=True)), sink)
    p1 = jnp.exp2(s1 - m)
    p2 = jnp.exp2(s2 - m)
    inv = 1.0 / (jnp.sum(p1, axis=-1, keepdims=True) + jnp.sum(p2, axis=-1, keepdims=True) + jnp.exp2(sink - m))
    p1 = (p1 * inv).astype(BF16)
    p2 = (p2 * inv).astype(BF16)
    o = _dot(p1, cv_ref[...].astype(BF16)) + _dot(p2, nv)
    for hh in range(s_heads):
        o_ref[:, hh * kv_w:(hh + 1) * kv_w] = o[hh * n:(hh + 1) * n].astype(BF16)


def _rope_table(pos):
    inv = ROPE_BASE ** (-jnp.arange(0, MLA_ROPE, 2, dtype=jnp.float32) / MLA_ROPE)
    ang = pos.astype(jnp.float32)[:, None] * inv[None, :]
    cos, sin = jnp.cos(ang), jnp.sin(ang)
    widths = ((0, 0), (MLA_NOPE, LANES - MLA_NOPE - MLA_ROPE))
    return jnp.concatenate([jnp.pad(jnp.concatenate([cos, cos], axis=1), widths),
                            jnp.pad(jnp.concatenate([sin, sin], axis=1), widths)], axis=1)


def _prep_weights(mla_w_dq, mla_w_uq, mla_w_dkv, mla_w_uk, mla_w_uv, mla_w_o, w_kv_shared, swa_w_q, swa_w_o, scale):
    ql, qcols = mla_w_uq.shape
    c_dim, heads, nope = mla_w_uk.shape
    rope = qcols // heads - nope
    half = rope // 2
    d = mla_w_dq.shape[0]
    pad = LANES - nope - rope
    wq3 = mla_w_uq.reshape(ql, heads, nope + rope)
    w_nope, r1, r2 = wq3[:, :, :nope], wq3[:, :, nope:nope + half], wq3[:, :, nope + half:]
    zq = lambda n: jnp.zeros((ql, heads, n), F32)
    w = {}
    w["wq"] = jnp.concatenate([w_nope, r1, r2, zq(pad)], axis=-1).reshape(ql, heads * LANES) * scale
    w["wq_rot"] = jnp.concatenate([zq(nope), -r2, r1, zq(pad)], axis=-1).reshape(ql, heads * LANES) * scale
    w["wdq"] = mla_w_dq
    w["wlat"] = mla_w_dkv[:, :c_dim]
    k1, k2 = mla_w_dkv[:, c_dim:c_dim + half], mla_w_dkv[:, c_dim + half:]
    zk = lambda n: jnp.zeros((d, n), F32)
    w["wkr"] = jnp.concatenate([zk(nope), k1, k2, zk(pad)], axis=1)
    w["wkr_rot"] = jnp.concatenate([zk(nope), -k2, k1, zk(pad)], axis=1)
    zc = jnp.zeros((c_dim, heads, LANES - nope), F32)
    w["wuk"] = jnp.concatenate([mla_w_uk, zc], axis=-1).reshape(c_dim, heads * LANES)
    w["wuv_t"] = jnp.pad(mla_w_uv, ((0, 0), (0, 0), (0, MLA_VT_ROWS - mla_w_uv.shape[2]))).reshape(
        c_dim, heads * MLA_VT_ROWS).T
    w["wuk_dec"] = jnp.concatenate([jnp.transpose(mla_w_uk, (1, 2, 0)),
                                    jnp.zeros((heads, LANES - nope, c_dim), F32)], axis=1)
    w["wuv_dec"] = jnp.concatenate([jnp.transpose(mla_w_uv, (1, 0, 2)),
                                    jnp.zeros((heads, c_dim, LANES - nope), F32)], axis=-1)
    vdim = mla_w_uv.shape[2]
    assert vdim == MLA_V and nope == MLA_NOPE and rope == MLA_ROPE
    w["wo"] = mla_w_o
    w["wo_pad"] = jnp.concatenate([mla_w_o.reshape(heads, vdim, d), jnp.zeros((heads, LANES - vdim, d), F32)],
                                  axis=1).reshape(heads * LANES, d)
    kvw = w_kv_shared.shape[1] // 2
    groups = kvw // SWA_HEAD_DIM
    s_heads = swa_w_q.shape[1] // SWA_HEAD_DIM
    rep = s_heads // groups
    w["wkv"] = w_kv_shared
    eye = jnp.eye(SWA_HEAD_DIM, dtype=F32)
    zero = jnp.zeros_like(eye)
    w["rep_k2"] = jnp.kron(jnp.eye(groups, dtype=F32), jnp.concatenate([eye, zero, zero, eye], axis=1))
    w["eye_kv"] = jnp.eye(kvw, dtype=F32)
    w["swa_wq_t"] = swa_w_q.T
    w["swa_wo"] = swa_w_o
    onehot = (jnp.arange(s_heads)[:, None] // rep == jnp.arange(groups)[None, :]).astype(F32)
    wq_h = swa_w_q.reshape(d, s_heads, SWA_HEAD_DIM)
    w["swa_wq_dec"] = (wq_h[:, :, None, :] * onehot[None, :, :, None]).reshape(d, s_heads * kvw)
    wo_h = swa_w_o.reshape(s_heads, SWA_HEAD_DIM, d)
    w["swa_wo_dec"] = (wo_h[:, None, :, :] * onehot[:, :, None, None]).reshape(s_heads * kvw, d)
    return {k: v.astype(BF16) for k, v in w.items()}, dict(heads=heads, c_dim=c_dim, groups=groups, rep=rep,
                                                            s_heads=s_heads, kvw=kvw)


def _rows_a(x, P, w, tab, heads, c_dim, absorbed):
    t, d = x.shape
    tm = _row_tile(t)
    hw = heads * LANES
    row = lambda v: v.reshape(1, -1)
    ins = ([_rows(x, tm)] + _ffn_ins(P, w, 1, 0)
           + [_const(row(P["mix_norm"][0])), _const(w["wdq"]), _const(row(P["mla_q_norm"][0])), _const(w["wq"]),
              _const(w["wq_rot"]), _const(w["wlat"]), _const(row(P["mla_kv_norm"][0])), _const(w["wkr"]),
              _const(w["wkr_rot"]), _cycle(tab, tm)]
           + ([_const(w["wuk_dec"])] if absorbed else [_const(w["wuk"]), _const(w["wuv_t"])]))
    outs = [_rows_out(t, tm, d, F32), _rows_out(t, tm, hw, BF16)]
    if absorbed:
        outs.append(_rows_out(t, tm, heads * c_dim, BF16))
    else:
        outs += [_rows_out(t, tm, hw, BF16),
                 (jax.ShapeDtypeStruct((heads * MLA_VT_ROWS, t), BF16),
                  pl.BlockSpec((heads * MLA_VT_ROWS, tm), lambda i: (0, i)))]
    outs += [_rows_out(t, tm, c_dim, F32), _rows_out(t, tm, MLA_ROPE, F32)]
    return _rows_call(functools.partial(_rows_a_kernel, heads=heads, c_dim=c_dim, absorbed=absorbed),
                      "rows_a_absorbed" if absorbed else "rows_a", t, tm, ins, outs)


def _rows_b(a, x, P, w, meta, decode):
    t, d = x.shape
    tm = _row_tile(t)
    kvw = meta["kvw"]
    ins = [_rows(a, tm)]
    if decode:
        ins += [_const(w["wuv_dec"]), _const(w["wo_pad"])]
    else:
        ins += [_const(w["wo"])]
    ins += [_rows(x, tm)] + _ffn_ins(P, w, 2, 0) + [_const(P["kv_norm"].reshape(1, -1)), _const(w["wkv"])]
    outs = [_rows_out(t, tm, d, F32), _rows_out(t, tm, 2 * kvw, F32)]
    if not decode:
        ins += [_const(w["rep_k2"]), _const(w["eye_kv"])]
        outs += [_rows_out(t, tm, w["rep_k2"].shape[1], BF16),
                 (jax.ShapeDtypeStruct((kvw, t), BF16), pl.BlockSpec((kvw, tm), lambda i: (0, i)))]
    return _rows_call(functools.partial(_rows_b_kernel, dec_heads=meta["heads"] if decode else 0,
                                        c_dim=meta["c_dim"], tiled_kv=not decode),
                      "rows_b_decode" if decode else "rows_b", t, tm, ins, outs)


def _rows_c(x, P, w, wq, name, transposed):
    t, d = x.shape
    tm = _row_tile(t)
    ins = [_rows(x, tm)] + _ffn_ins(P, w, 1, 1) + [_const(P["mix_norm"][1].reshape(1, -1)), _const(wq)]
    if transposed:
        q_out = (jax.ShapeDtypeStruct((wq.shape[0], t), BF16), pl.BlockSpec((wq.shape[0], tm), lambda i: (0, i)))
    else:
        q_out = _rows_out(t, tm, wq.shape[1], BF16)
    return _rows_call(functools.partial(_rows_c_kernel, scale=SWA_HEAD_DIM ** -0.5 * LOG2E, transposed=transposed),
                      name, t, tm, ins, [_rows_out(t, tm, d, F32), q_out])


def _rows_d(a, x, P, w, wo, name):
    t, d = x.shape
    tm = _row_tile(t)
    ins = ([_rows(a, tm), _const(wo), _rows(x, tm)] + _ffn_ins(P, w, 2, 1)
           + [_const(P["final_norm"].reshape(1, -1))])
    return _rows_call(_rows_d_kernel, name, t, tm, ins, [_rows_out(t, tm, d, F32)])[0]


def _trunk_prompt(x3, P, w, meta):
    batch, seq, d = x3.shape
    heads, c_dim, kvw = meta["heads"], meta["c_dim"], meta["kvw"]
    x = x3.reshape(batch * seq, d)
    x, q, k, vt, lat, kr = _rows_a(x, P, w, _rope_table(jnp.arange(seq)), heads, c_dim, absorbed=False)
    o = _mla_attn_prompt(q, k, vt, batch, seq, heads)
    x, kv, k2, v_t = _rows_b(o, x, P, w, meta, decode=False)
    x, q_t = _rows_c(x, P, w, w["swa_wq_t"], "rows_c", transposed=True)
    blk = 2 * CHUNK
    rel = (jnp.arange(2 * blk) - blk)[:, None] - jnp.arange(blk)[None, :]
    bias_t = _bias_table(P["rel_bias"], _rel_bucket(rel).astype(jnp.int32))
    o = _swa_prompt(q_t, k2, v_t, bias_t, P["swa_sinks"][0], batch, seq, meta["s_heads"], meta["rep"], blk)
    y = _rows_d(o, x, P, w, w["swa_wo"], "rows_d")
    keep = min(WINDOW, seq)
    kv3 = kv.reshape(batch, seq, 2 * kvw)[:, seq - keep:]
    new_k = kv3[:, :, :kvw].reshape(batch, keep, meta["groups"], SWA_HEAD_DIM)
    new_v = kv3[:, :, kvw:].reshape(batch, keep, meta["groups"], SWA_HEAD_DIM)
    return (y.reshape(batch, seq, d), lat.reshape(1, batch, seq, c_dim), kr.reshape(1, batch, seq, MLA_ROPE),
            new_k, new_v)


def _trunk_decode(x3, cache_lat, cache_rope, cache_k, cache_v, P, w, meta):
    batch, n_new, d = x3.shape
    heads, c_dim, kvw, s_heads = meta["heads"], meta["c_dim"], meta["kvw"], meta["s_heads"]
    past = cache_lat.shape[2]
    w_c = cache_k.shape[1]
    qpos = past + jnp.arange(n_new)
    assert past % CHUNK == 0 and n_new <= CHUNK and w_c <= WINDOW_CHUNKS * CHUNK and w_c <= past
    t = batch * n_new
    x = x3.reshape(t, d)
    tm = _row_tile(t)
    assert tm % n_new == 0
    tab = jnp.tile(_rope_table(qpos), (tm // n_new, 1))
    x, q, ql, lat, kr = _rows_a(x, P, w, tab, heads, c_dim, absorbed=True)
    ol = _mla_attn_decode(ql, q, cache_lat, jnp.swapaxes(cache_rope, 2, 3), lat, kr, heads, n_new)
    x, kv = _rows_b(ol, x, P, w, meta, decode=True)
    x, qs = _rows_c(x, P, w, w["swa_wq_dec"], "rows_c_decode", transposed=False)
    kpos = jnp.arange(past - w_c, past + n_new)
    bias_h = _bias_table(P["rel_bias"], _rel_bucket(kpos[None, :] - qpos[:, None]).astype(jnp.int32))
    bias = bias_h.reshape(s_heads * n_new, w_c + n_new)
    sink_rows = jnp.repeat(P["swa_sinks"][0], n_new).reshape(s_heads * n_new, 1)
    const2 = lambda a: pl.BlockSpec(a.shape, lambda b: (0, 0), pipeline_mode=pl.Buffered(1))
    o = pl.pallas_call(
        functools.partial(_swa_decode_kernel, w_c=w_c, kv_w=kvw, s_heads=s_heads),
        grid=(batch,),
        in_specs=[pl.BlockSpec((n_new, s_heads * kvw), lambda b: (b, 0)),
                  pl.BlockSpec((None, w_c, kvw), lambda b: (b, 0, 0)),
                  pl.BlockSpec((None, w_c, kvw), lambda b: (b, 0, 0)),
                  pl.BlockSpec((n_new, 2 * kvw), lambda b: (b, 0)),
                  const2(bias), const2(sink_rows)],
        out_specs=pl.BlockSpec((n_new, s_heads * kvw), lambda b: (b, 0)),
        out_shape=jax.ShapeDtypeStruct((t, s_heads * kvw), BF16),
        compiler_params=_params(1), name="swa_decode",
    )(qs, cache_k.reshape(batch, w_c, kvw), cache_v.reshape(batch, w_c, kvw), kv, bias, sink_rows)
    y = _rows_d(o, x, P, w, w["swa_wo_dec"], "rows_d_decode")
    kv3 = kv.reshape(batch, n_new, 2 * kvw)
    new_k = kv3[:, :, :kvw].reshape(batch, n_new, meta["groups"], SWA_HEAD_DIM)
    new_v = kv3[:, :, kvw:].reshape(batch, n_new, meta["groups"], SWA_HEAD_DIM)
    return (y.reshape(batch, n_new, d), lat.reshape(1, batch, n_new, c_dim), kr.reshape(1, batch, n_new, MLA_ROPE),
            new_k, new_v)


def kernel(x_prompt, x_sample, cache_mla_latent, cache_mla_krope, cache_swa_k, cache_swa_v, ffn_norm1, ffn1_w_gate, ffn1_w_up, ffn1_w_down, mix_norm, ffn_norm2, ffn2_w_gate, ffn2_w_up, ffn2_w_down, mla_w_dq, mla_q_norm, mla_w_uq, mla_w_dkv, mla_kv_norm, mla_w_uk, mla_w_uv, mla_w_o, kv_norm, w_kv_shared, swa_w_q, swa_sinks, swa_w_o, rel_bias, final_norm):
    assert ffn_norm1.shape[0] == 2 and mla_w_dq.shape[0] == 1 and swa_w_q.shape[0] == 1
    scale = (MLA_NOPE + MLA_ROPE) ** -0.5 * LOG2E
    w, meta = _prep_weights(mla_w_dq[0], mla_w_uq[0], mla_w_dkv[0], mla_w_uk[0], mla_w_uv[0], mla_w_o[0],
                            w_kv_shared, swa_w_q[0], swa_w_o[0], scale)
    w.update(f1g=ffn1_w_gate.astype(BF16), f1u=ffn1_w_up.astype(BF16), f1d=ffn1_w_down.astype(BF16),
             f2g=ffn2_w_gate.astype(BF16), f2u=ffn2_w_up.astype(BF16), f2d=ffn2_w_down.astype(BF16))
    P = dict(ffn_norm1=ffn_norm1, mix_norm=mix_norm, ffn_norm2=ffn_norm2, mla_q_norm=mla_q_norm,
             mla_kv_norm=mla_kv_norm, kv_norm=kv_norm, swa_sinks=swa_sinks, rel_bias=rel_bias, final_norm=final_norm)
    y_p, lat_p, rope_p, k_p, v_p = _trunk_prompt(x_prompt, P, w, meta)
    y_s, lat_s, rope_s, k_s, v_s = _trunk_decode(x_sample, cache_mla_latent, cache_mla_krope,
                                                 cache_swa_k, cache_swa_v, P, w, meta)
    return (y_p, y_s, lat_p, rope_p, k_p, v_p, lat_s, rope_s, k_s, v_s)
```

```python
import functools
import math

import jax
import jax.numpy as jnp
from jax import lax
from jax.experimental import pallas as pl
from jax.experimental.pallas import tpu as pltpu

F32 = jnp.float32
BF16 = jnp.bfloat16

CHUNK = 64
RMS_EPS = 1e-6
FFN_RES = 0.5
ROPE_BASE = 10000.0
WINDOW = 128
WINDOW_CHUNKS = WINDOW // CHUNK
N_BUCKETS = 32
MAX_DISTANCE = 128
NEG_INF = -1e30
LOG2E = math.log2(math.e)
MLA_NOPE = 64
MLA_ROPE = 32
MLA_V = 64
MLA_VT_ROWS = 80
SWA_HEAD_DIM = 64

LANES = 128
ROW_TILE = 256
ROW_TILE_LIGHT = 512
MLA_BQ = 512
MLA_BK = 512
MLA_DIAG = 256
MLA_DECODE_CHUNK = 1024
MLA_HEADS_PER_STEP = 4
VMEM_LIMIT = 60 * 1024 * 1024


def _params(n_axes, vmem=VMEM_LIMIT):
    return pltpu.CompilerParams(dimension_semantics=("arbitrary",) * n_axes, vmem_limit_bytes=vmem)


def _rms(xf, g):
    return xf * lax.rsqrt(jnp.mean(xf * xf, axis=-1, keepdims=True) + RMS_EPS) * g


def _dot(a, b):
    return jnp.dot(a, b, preferred_element_type=F32)


def _dot_nt(a, b):
    return lax.dot_general(a, b, (((1,), (1,)), ((), ())), preferred_element_type=F32)


def _row_tile(t, tile=None):
    tm = min(tile or ROW_TILE, t)
    assert t % tm == 0, (t, tm)
    return tm


def _rows(a, tm):
    return a, pl.BlockSpec((tm, a.shape[1]), lambda i: (i, 0))


def _const(a):
    nd = a.ndim
    return a, pl.BlockSpec(a.shape, lambda i: (0,) * nd, pipeline_mode=pl.Buffered(1))


def _layer(a, layer):
    nd = a.ndim - 1
    return a, pl.BlockSpec((None,) + tuple(a.shape[1:]), lambda i: (layer,) + (0,) * nd,
                           pipeline_mode=pl.Buffered(1))


def _cycle(a, tm):
    assert a.shape[0] % tm == 0
    nb = a.shape[0] // tm
    return a, pl.BlockSpec((tm, a.shape[1]), lambda i: (i % nb, 0))


def _rows_out(t, tm, ncols, dtype):
    return jax.ShapeDtypeStruct((t, ncols), dtype), pl.BlockSpec((tm, ncols), lambda i: (i, 0))


def _rows_call(body, name, t, tm, ins, outs):
    return pl.pallas_call(
        body, grid=(t // tm,),
        in_specs=[s for _, s in ins], out_specs=[s for _, s in outs], out_shape=[o for o, _ in outs],
        compiler_params=_params(1), name=name,
    )(*[a for a, _ in ins])


def _ffn_ins(P, w, which, layer):
    return [_const(P["ffn_norm%d" % which][layer].reshape(1, -1)), _layer(w["f%dg" % which], layer),
            _layer(w["f%du" % which], layer), _layer(w["f%dd" % which], layer)]


def _ffn_apply(x, g_ref, wg_ref, wu_ref, wd_ref):
    h = _rms(x, g_ref[...]).astype(BF16)
    gate = _dot(h, wg_ref[...])
    up = _dot(h, wu_ref[...])
    a = (gate * jax.nn.sigmoid(gate) * up).astype(BF16)
    return x + FFN_RES * _dot(a, wd_ref[...])


def _mla_proj_apply(x, mg_ref, wdq_ref, qn_ref, wq_ref, wqr_ref, wlat_ref, kvn_ref, wkr_ref, wkrr_ref, tab_ref,
                    refs, heads, c_dim, absorbed):
    cos_t, sin_t = tab_ref[:, 0:LANES], tab_ref[:, LANES:2 * LANES]
    nope_lane = lax.broadcasted_iota(jnp.int32, (1, LANES), 1) < MLA_NOPE
    cos_q = jnp.where(nope_lane, 1.0, cos_t)
    h = _rms(x, mg_ref[...]).astype(BF16)
    cq = _rms(_dot(h, wdq_ref[...]), qn_ref[...]).astype(BF16)
    lat = _rms(_dot(h, wlat_ref[...]), kvn_ref[...])
    kr = _dot(h, wkr_ref[...]) * cos_t + _dot(h, wkrr_ref[...]) * sin_t
    qa = _dot(cq, wq_ref[...])
    qb = _dot(cq, wqr_ref[...])
    latb = lat.astype(BF16)
    if absorbed:
        wukd_ref, q_ref, ql_ref, lat_ref, kr_ref = refs
    else:
        wuk_ref, wuvt_ref, q_ref, k_ref, vt_ref, lat_ref, kr_ref = refs
        kn = _dot(latb, wuk_ref[...])
    lat_ref[...] = lat
    kr_ref[...] = kr[:, MLA_NOPE:MLA_NOPE + MLA_ROPE]
    for hh in range(heads):
        sl = slice(hh * LANES, (hh + 1) * LANES)
        qh = (qa[:, sl] * cos_q + qb[:, sl] * sin_t).astype(BF16)
        q_ref[:, sl] = qh
        if absorbed:
            ql_ref[:, hh * c_dim:(hh + 1) * c_dim] = _dot(qh, wukd_ref[hh]).astype(BF16)
        else:
            k_ref[:, sl] = (kn[:, sl] + kr).astype(BF16)
    if not absorbed:
        vt = _dot_nt(wuvt_ref[...], latb)
        ones_row = lax.broadcasted_iota(jnp.int32, vt.shape, 0) % MLA_VT_ROWS == MLA_V
        vt_ref[...] = jnp.where(ones_row, 1.0, vt).astype(BF16)


def _rows_a_kernel(x_ref, fg, wg, wu, wd, mg, wdq, qn, wq, wqr, wlat, kvn, wkr, wkrr, tab, *refs,
                   heads, c_dim, absorbed):
    n_extra = 1 if absorbed else 2
    x1_ref = refs[n_extra]
    x1 = _ffn_apply(x_ref[...], fg, wg, wu, wd)
    x1_ref[...] = x1
    _mla_proj_apply(x1, mg, wdq, qn, wq, wqr, wlat, kvn, wkr, wkrr, tab, refs[:n_extra] + refs[n_extra + 1:],
                    heads, c_dim, absorbed)


def _rows_b_kernel(a_ref, *refs, dec_heads, c_dim, tiled_kv):
    if dec_heads:
        wuv_ref, refs = refs[0], refs[1:]
        a = jnp.concatenate([_dot(a_ref[:, hh * c_dim:(hh + 1) * c_dim], wuv_ref[hh]).astype(BF16)
                             for hh in range(dec_heads)], axis=1)
    else:
        a = a_ref[...]
    wo, x_ref, fg, wg, wu, wd, kvg, wkv = refs[:8]
    x = _ffn_apply(x_ref[...] + _dot(a, wo[...]), fg, wg, wu, wd)
    kv = _dot(_rms(x, kvg[...]).astype(BF16), wkv[...])
    if tiled_kv:
        repk, eye, x_out, kv_out, k2_out, vt_out = refs[8:]
        half = kv.shape[1] // 2
        k2_out[...] = _dot(kv[:, :half].astype(BF16), repk[...]).astype(BF16)
        vt_out[...] = _dot_nt(eye[...], kv[:, half:].astype(BF16)).astype(BF16)
    else:
        x_out, kv_out = refs[8:]
    x_out[...] = x
    kv_out[...] = kv


def _rows_c_kernel(x_ref, fg, wg, wu, wd, mg, wq, x_out, q_out, *, scale, transposed):
    x = _ffn_apply(x_ref[...], fg, wg, wu, wd)
    x_out[...] = x
    h = _rms(x, mg[...]).astype(BF16)
    q = _dot_nt(wq[...], h) if transposed else _dot(h, wq[...])
    q_out[...] = (q * scale).astype(BF16)


def _rows_d_kernel(a_ref, wo, x_ref, fg, wg, wu, wd, fin_g, y_out):
    x = _ffn_apply(x_ref[...] + _dot(a_ref[...], wo[...]), fg, wg, wu, wd)
    y_out[...] = _rms(x, fin_g[...])


def _mla_attn_kernel(q_ref, k_ref, vt_ref, o_ref, m_sc, acc_sc, *, bq, bk, hp):
    i = pl.program_id(2)
    qs = [q_ref[:, hh * LANES:(hh + 1) * LANES] for hh in range(hp)]
    sls = [slice(hh * LANES, (hh + 1) * LANES) for hh in range(hp)]

    def block(start, width, q_lo, masked):
        ql = slice(q_lo, bq)
        ss = [_dot_nt(k_ref[pl.ds(start, width), sls[hh]], qs[hh][ql]) for hh in range(hp)]
        if masked:
            kc = (start + lax.broadcasted_iota(jnp.int32, (width, 1), 0)) // CHUNK
            qc = (i * bq + q_lo + lax.broadcasted_iota(jnp.int32, (1, bq - q_lo), 1)) // CHUNK
            ss = [jnp.where(kc <= qc, s, NEG_INF) for s in ss]
        for hh in range(hp):
            m = m_sc[hh, :, ql]
            m_new = jnp.maximum(m, jnp.max(ss[hh], axis=0, keepdims=True))
            alpha = jnp.exp2(m - m_new)
            p = jnp.exp2(ss[hh] - m_new).astype(BF16)
            vt = vt_ref[hh * MLA_VT_ROWS:(hh + 1) * MLA_VT_ROWS, pl.ds(start, width)]
            acc_sc[hh, :, ql] = alpha * acc_sc[hh, :, ql] + _dot(vt, p)
            m_sc[hh, :, ql] = m_new

    m_sc[...] = jnp.full(m_sc.shape, NEG_INF, F32)
    acc_sc[...] = jnp.zeros(acc_sc.shape, F32)
    n_full = (i * bq) // bk

    @pl.loop(0, n_full)
    def _(j):
        block(pl.multiple_of(j * bk, bk), bk, 0, False)

    diag = min(MLA_DIAG, bq)
    for t in range(bq // diag):
        block(pl.multiple_of(i * bq + t * diag, diag), diag, t * diag, True)
    for pair in range(hp // 2):
        halves = []
        for hh in (2 * pair, 2 * pair + 1):
            acc = acc_sc[hh]
            halves.append(acc[:MLA_V] * (1.0 / acc[MLA_V:MLA_V + 1]))
        o_ref[:, pair * LANES:(pair + 1) * LANES] = jnp.concatenate(halves, axis=0).T.astype(BF16)


def _mla_attn_prompt(q, k, vt, batch, seq, heads, bq=MLA_BQ, bk=MLA_BK, hp=MLA_HEADS_PER_STEP):
    bq, bk = min(bq, seq), min(bk, seq)
    nq = seq // bq
    assert seq % bq == 0 and seq % bk == 0 and bq % CHUNK == 0 and heads % hp == 0 and hp % 2 == 0
    assert bq % bk == 0 and bq % min(MLA_DIAG, bq) == 0 and MLA_DIAG % CHUNK == 0
    w = hp * LANES
    return pl.pallas_call(
        functools.partial(_mla_attn_kernel, bq=bq, bk=bk, hp=hp),
        grid=(batch, heads // hp, nq),
        in_specs=[pl.BlockSpec((bq, w), lambda b, h, i: (b * nq + i, h)),
                  pl.BlockSpec((seq, w), lambda b, h, i: (b, h)),
                  pl.BlockSpec((hp * MLA_VT_ROWS, seq), lambda b, h, i: (h, b))],
        out_specs=pl.BlockSpec((bq, hp * MLA_V), lambda b, h, i: (b * nq + i, h)),
        out_shape=jax.ShapeDtypeStruct((q.shape[0], heads * MLA_V), BF16),
        scratch_shapes=[pltpu.VMEM((hp, 1, bq), F32), pltpu.VMEM((hp, MLA_VT_ROWS, bq), F32)],
        compiler_params=_params(3), name="mla_attn_prompt",
    )(q, k, vt)


def _mla_attn_decode_kernel(ql_ref, q_ref, cl_ref, crt_ref, nl_ref, nr_ref, o_ref, *, heads, c_dim):
    n = ql_ref.shape[0]
    ql = jnp.concatenate([ql_ref[:, hh * c_dim:(hh + 1) * c_dim] for hh in range(heads)], axis=0)
    qr = jnp.concatenate([q_ref[:, hh * LANES + MLA_NOPE:hh * LANES + MLA_NOPE + MLA_ROPE] for hh in range(heads)],
                         axis=0)
    past = cl_ref.shape[0]
    chunk = min(past, MLA_DECODE_CHUNK)
    parts = [(cl_ref[c:c + chunk, :], crt_ref[:, c:c + chunk], _dot) for c in range(0, past, chunk)]
    parts.append((nl_ref[...], nr_ref[...], _dot_nt))
    m = jnp.full((heads * n, 1), NEG_INF, F32)
    l = jnp.zeros((heads * n, 1), F32)
    acc = jnp.zeros((heads * n, c_dim), F32)
    for kl, kr, rope_dot in parts:
        kl, kr = kl.astype(BF16), kr.astype(BF16)
        s = _dot_nt(ql, kl) + rope_dot(qr, kr)
        m_new = jnp.maximum(m, jnp.max(s, axis=-1, keepdims=True))
        alpha = jnp.exp2(m - m_new)
        p = jnp.exp2(s - m_new)
        l = alpha * l + jnp.sum(p, axis=-1, keepdims=True)
        acc = alpha * acc + _dot(p.astype(BF16), kl)
        m = m_new
    o = acc / l
    for hh in range(heads):
        o_ref[:, hh * c_dim:(hh + 1) * c_dim] = o[hh * n:(hh + 1) * n].astype(BF16)


def _mla_attn_decode(ql, q, cache_lat, cache_rope_t, new_lat, new_rope, heads, n_new):
    _, batch, past, c_dim = cache_lat.shape
    r_dim = cache_rope_t.shape[2]
    return pl.pallas_call(
        functools.partial(_mla_attn_decode_kernel, heads=heads, c_dim=c_dim),
        grid=(batch,),
        in_specs=[pl.BlockSpec((n_new, heads * c_dim), lambda b: (b, 0)),
                  pl.BlockSpec((n_new, heads * LANES), lambda b: (b, 0)),
                  pl.BlockSpec((None, None, past, c_dim), lambda b: (0, b, 0, 0)),
                  pl.BlockSpec((None, None, r_dim, past), lambda b: (0, b, 0, 0)),
                  pl.BlockSpec((n_new, c_dim), lambda b: (b, 0)),
                  pl.BlockSpec((n_new, r_dim), lambda b: (b, 0))],
        out_specs=pl.BlockSpec((n_new, heads * c_dim), lambda b: (b, 0)),
        out_shape=jax.ShapeDtypeStruct(ql.shape, BF16),
        compiler_params=_params(1), name="mla_attn_decode",
    )(ql, q, cache_lat, cache_rope_t, new_lat, new_rope)


def _bias_kernel(tab_ref, idx_ref, o_ref, *, heads):
    idx = idx_ref[...]
    for hh in range(heads):
        acc = jnp.zeros(idx.shape, F32)
        for b in range(N_BUCKETS):
            acc = jnp.where(idx == b, tab_ref[b, hh] * LOG2E, acc)
        o_ref[hh] = acc


def _bias_table(rel_bias, idx):
    heads = rel_bias.shape[1]
    return pl.pallas_call(
        functools.partial(_bias_kernel, heads=heads),
        in_specs=[pl.BlockSpec(memory_space=pltpu.SMEM), pl.BlockSpec(idx.shape, lambda: (0, 0))],
        out_specs=pl.BlockSpec((heads,) + idx.shape, lambda: (0, 0, 0)),
        out_shape=jax.ShapeDtypeStruct((heads,) + idx.shape, F32), name="rel_bias_table",
    )(rel_bias, idx)


def _rel_bucket(rel):
    half = N_BUCKETS // 2
    max_exact = half // 2
    base = jnp.where(rel > 0, half, 0)
    n = jnp.abs(rel)
    nf = jnp.maximum(n, 1).astype(jnp.float32)
    large = max_exact + (jnp.log(nf / max_exact) / math.log(MAX_DISTANCE / max_exact)
                         * (half - max_exact)).astype(jnp.int32)
    large = jnp.minimum(large, half - 1)
    return base + jnp.where(n < max_exact, n, large)


def _swa_prompt_kernel(sink_ref, qt_ref, kp_ref, kc_ref, vtp_ref, vtc_ref, bias_ref, o_ref, *, s_heads, rep, blk):
    i = pl.program_id(1)
    key_chunk = lax.broadcasted_iota(jnp.int32, (2 * blk, 1), 0) // CHUNK
    row_chunk = lax.broadcasted_iota(jnp.int32, (1, blk), 1) // CHUNK
    first = blk // CHUNK
    valid = ((key_chunk >= row_chunk) & (key_chunk <= row_chunk + WINDOW_CHUNKS)
             & ((i > 0) | (key_chunk >= first)))
    dh = SWA_HEAD_DIM
    ss = []
    for hh in range(s_heads):
        g, pair, slot = hh // rep, hh // 2, hh % 2
        ksl = slice(g * 4 * dh + slot * 2 * dh, g * 4 * dh + (slot + 1) * 2 * dh)
        k = jnp.concatenate([kp_ref[:, ksl], kc_ref[:, ksl]], axis=0)
        ss.append(_dot(k, qt_ref[pair * 2 * dh:(pair + 1) * 2 * dh, :]))
    ps, invs = [], []
    for hh, s in enumerate(ss):
        s = jnp.where(valid, s + bias_ref[hh], NEG_INF)
        sink = sink_ref[hh] * LOG2E
        m = jnp.maximum(jnp.max(s, axis=0, keepdims=True), sink)
        p = jnp.exp2(s - m)
        invs.append(1.0 / (jnp.sum(p, axis=0, keepdims=True) + jnp.exp2(sink - m)))
        ps.append(p.astype(BF16))
    outs = []
    for hh, p in enumerate(ps):
        g = hh // rep
        vt = jnp.concatenate([vtp_ref[g * dh:(g + 1) * dh, :], vtc_ref[g * dh:(g + 1) * dh, :]], axis=1)
        outs.append(_dot(vt, p) * invs[hh])
    for pair in range(s_heads // 2):
        o_ref[:, pair * 2 * dh:(pair + 1) * 2 * dh] = (
            jnp.concatenate(outs[2 * pair:2 * pair + 2], axis=0).T.astype(BF16))


def _swa_prompt(qt, k2, vt, bias_t, sinks, batch, seq, s_heads, rep, blk=128):
    nq = seq // blk
    kw = k2.shape[1]
    cur = lambda b, i: (b * nq + i, 0)
    prev = lambda b, i: (b * nq + jnp.maximum(i - 1, 0), 0)
    cur_t = lambda b, i: (0, b * nq + i)
    prev_t = lambda b, i: (0, b * nq + jnp.maximum(i - 1, 0))
    assert rep == 4 and kw == (s_heads // rep) * 4 * SWA_HEAD_DIM
    return pl.pallas_call(
        functools.partial(_swa_prompt_kernel, s_heads=s_heads, rep=rep, blk=blk),
        grid=(batch, nq),
        in_specs=[pl.BlockSpec(memory_space=pltpu.SMEM),
                  pl.BlockSpec((qt.shape[0], blk), cur_t),
                  pl.BlockSpec((blk, kw), prev), pl.BlockSpec((blk, kw), cur),
                  pl.BlockSpec((vt.shape[0], blk), prev_t), pl.BlockSpec((vt.shape[0], blk), cur_t),
                  pl.BlockSpec(bias_t.shape, lambda b, i: (0, 0, 0), pipeline_mode=pl.Buffered(1))],
        out_specs=pl.BlockSpec((blk, qt.shape[0]), cur),
        out_shape=jax.ShapeDtypeStruct((qt.shape[1], qt.shape[0]), BF16),
        compiler_params=_params(2), name="swa_prompt",
    )(sinks, qt, k2, k2, vt, vt, bias_t)


def _swa_decode_kernel(q_ref, ck_ref, cv_ref, nkv_ref, rep_ref, bias_ref, sink_ref, o_ref, *, kv_w, groups, rep):
    n = q_ref.shape[0]
    gw = rep * SWA_HEAD_DIM
    k = jnp.concatenate([ck_ref[...], nkv_ref[:, 0:kv_w]], axis=0).astype(BF16)
    v = jnp.concatenate([cv_ref[...], nkv_ref[:, kv_w:2 * kv_w]], axis=0).astype(BF16)
    k4 = _dot(k, rep_ref[...]).astype(BF16)
    v4 = _dot(v, rep_ref[...]).astype(BF16)
    lane_head = lax.broadcasted_iota(jnp.int32, (1, gw), 1) // SWA_HEAD_DIM
    sls = [slice(g * gw, (g + 1) * gw) for g in range(groups)]
    ss = []
    for g in range(groups):
        qg = q_ref[:, sls[g]]
        qs = jnp.concatenate([jnp.where(lane_head == r, qg, jnp.zeros_like(qg)) for r in range(rep)], axis=0)
        ss.append(_dot_nt(qs, k4[:, sls[g]]))
    ps = []
    for g in range(groups):
        rows = slice(g * rep * n, (g + 1) * rep * n)
        s = ss[g] + bias_ref[rows, :]
        sink = sink_ref[rows, :] * LOG2E
        m = jnp.maximum(jnp.max(s, axis=-1, keepdims=True), sink)
        p = jnp.exp2(s - m)
        inv = 1.0 / (jnp.sum(p, axis=-1, keepdims=True) + jnp.exp2(sink - m))
        ps.append((p * inv).astype(BF16))
    for g in range(groups):
        res = _dot(ps[g], v4[:, sls[g]])
        og = jnp.zeros((n, gw), F32)
        for r in range(rep):
            og = jnp.where(lane_head == r, res[r * n:(r + 1) * n], og)
        o_ref[:, sls[g]] = og.astype(BF16)


def _rope_table(pos):
    inv = ROPE_BASE ** (-jnp.arange(0, MLA_ROPE, 2, dtype=jnp.float32) / MLA_ROPE)
    ang = pos.astype(jnp.float32)[:, None] * inv[None, :]
    cos, sin = jnp.cos(ang), jnp.sin(ang)
    widths = ((0, 0), (MLA_NOPE, LANES - MLA_NOPE - MLA_ROPE))
    return jnp.concatenate([jnp.pad(jnp.concatenate([cos, cos], axis=1), widths),
                            jnp.pad(jnp.concatenate([sin, sin], axis=1), widths)], axis=1)


def _prep_weights(mla_w_dq, mla_w_uq, mla_w_dkv, mla_w_uk, mla_w_uv, mla_w_o, w_kv_shared, swa_w_q, swa_w_o, scale):
    ql, qcols = mla_w_uq.shape
    c_dim, heads, nope = mla_w_uk.shape
    rope = qcols // heads - nope
    half = rope // 2
    d = mla_w_dq.shape[0]
    pad = LANES - nope - rope
    wq3 = mla_w_uq.reshape(ql, heads, nope + rope)
    w_nope, r1, r2 = wq3[:, :, :nope], wq3[:, :, nope:nope + half], wq3[:, :, nope + half:]
    zq = lambda n: jnp.zeros((ql, heads, n), F32)
    w = {}
    w["wq"] = jnp.concatenate([w_nope, r1, r2, zq(pad)], axis=-1).reshape(ql, heads * LANES) * scale
    w["wq_rot"] = jnp.concatenate([zq(nope), -r2, r1, zq(pad)], axis=-1).reshape(ql, heads * LANES) * scale
    w["wdq"] = mla_w_dq
    w["wlat"] = mla_w_dkv[:, :c_dim]
    k1, k2 = mla_w_dkv[:, c_dim:c_dim + half], mla_w_dkv[:, c_dim + half:]
    zk = lambda n: jnp.zeros((d, n), F32)
    w["wkr"] = jnp.concatenate([zk(nope), k1, k2, zk(pad)], axis=1)
    w["wkr_rot"] = jnp.concatenate([zk(nope), -k2, k1, zk(pad)], axis=1)
    zc = jnp.zeros((c_dim, heads, LANES - nope), F32)
    w["wuk"] = jnp.concatenate([mla_w_uk, zc], axis=-1).reshape(c_dim, heads * LANES)
    w["wuv_t"] = jnp.pad(mla_w_uv, ((0, 0), (0, 0), (0, MLA_VT_ROWS - mla_w_uv.shape[2]))).reshape(
        c_dim, heads * MLA_VT_ROWS).T
    w["wuk_dec"] = jnp.concatenate([jnp.transpose(mla_w_uk, (1, 2, 0)),
                                    jnp.zeros((heads, LANES - nope, c_dim), F32)], axis=1)
    w["wuv_dec"] = jnp.concatenate([jnp.transpose(mla_w_uv, (1, 0, 2)),
                                    jnp.zeros((heads, c_dim, LANES - nope), F32)], axis=-1)
    vdim = mla_w_uv.shape[2]
    assert vdim == MLA_V and nope == MLA_NOPE and rope == MLA_ROPE
    w["wo"] = mla_w_o
    w["wo_pad"] = jnp.concatenate([mla_w_o.reshape(heads, vdim, d), jnp.zeros((heads, LANES - vdim, d), F32)],
                                  axis=1).reshape(heads * LANES, d)
    kvw = w_kv_shared.shape[1] // 2
    groups = kvw // SWA_HEAD_DIM
    s_heads = swa_w_q.shape[1] // SWA_HEAD_DIM
    rep = s_heads // groups
    w["wkv"] = w_kv_shared
    eye = jnp.eye(SWA_HEAD_DIM, dtype=F32)
    zero = jnp.zeros_like(eye)
    w["rep_k2"] = jnp.kron(jnp.eye(groups, dtype=F32), jnp.concatenate([eye, zero, zero, eye], axis=1))
    w["eye_kv"] = jnp.eye(kvw, dtype=F32)
    w["swa_wq_t"] = swa_w_q.T
    w["swa_wo"] = swa_w_o
    w["rep_kv"] = jnp.kron(jnp.eye(groups, dtype=F32), jnp.tile(eye, (1, rep)))
    w["swa_wq"] = swa_w_q
    return {k: v.astype(BF16) for k, v in w.items()}, dict(heads=heads, c_dim=c_dim, groups=groups, rep=rep,
                                                            s_heads=s_heads, kvw=kvw)


def _rows_a(x, P, w, tab, heads, c_dim, absorbed):
    t, d = x.shape
    tm = _row_tile(t)
    hw = heads * LANES
    row = lambda v: v.reshape(1, -1)
    ins = ([_rows(x, tm)] + _ffn_ins(P, w, 1, 0)
           + [_const(row(P["mix_norm"][0])), _const(w["wdq"]), _const(row(P["mla_q_norm"][0])), _const(w["wq"]),
              _const(w["wq_rot"]), _const(w["wlat"]), _const(row(P["mla_kv_norm"][0])), _const(w["wkr"]),
              _const(w["wkr_rot"]), _cycle(tab, tm)]
           + ([_const(w["wuk_dec"])] if absorbed else [_const(w["wuk"]), _const(w["wuv_t"])]))
    outs = [_rows_out(t, tm, d, F32), _rows_out(t, tm, hw, BF16)]
    if absorbed:
        outs.append(_rows_out(t, tm, heads * c_dim, BF16))
    else:
        outs += [_rows_out(t, tm, hw, BF16),
                 (jax.ShapeDtypeStruct((heads * MLA_VT_ROWS, t), BF16),
                  pl.BlockSpec((heads * MLA_VT_ROWS, tm), lambda i: (0, i)))]
    outs += [_rows_out(t, tm, c_dim, F32), _rows_out(t, tm, MLA_ROPE, F32)]
    return _rows_call(functools.partial(_rows_a_kernel, heads=heads, c_dim=c_dim, absorbed=absorbed),
                      "rows_a_absorbed" if absorbed else "rows_a", t, tm, ins, outs)


def _rows_b(a, x, P, w, meta, decode):
    t, d = x.shape
    tm = _row_tile(t, ROW_TILE_LIGHT)
    kvw = meta["kvw"]
    ins = [_rows(a, tm)]
    if decode:
        ins += [_const(w["wuv_dec"]), _const(w["wo_pad"])]
    else:
        ins += [_const(w["wo"])]
    ins += [_rows(x, tm)] + _ffn_ins(P, w, 2, 0) + [_const(P["kv_norm"].reshape(1, -1)), _const(w["wkv"])]
    outs = [_rows_out(t, tm, d, F32), _rows_out(t, tm, 2 * kvw, F32)]
    if not decode:
        ins += [_const(w["rep_k2"]), _const(w["eye_kv"])]
        outs += [_rows_out(t, tm, w["rep_k2"].shape[1], BF16),
                 (jax.ShapeDtypeStruct((kvw, t), BF16), pl.BlockSpec((kvw, tm), lambda i: (0, i)))]
    return _rows_call(functools.partial(_rows_b_kernel, dec_heads=meta["heads"] if decode else 0,
                                        c_dim=meta["c_dim"], tiled_kv=not decode),
                      "rows_b_decode" if decode else "rows_b", t, tm, ins, outs)


def _rows_c(x, P, w, wq, name, transposed):
    t, d = x.shape
    tm = _row_tile(t, ROW_TILE_LIGHT)
    ins = [_rows(x, tm)] + _ffn_ins(P, w, 1, 1) + [_const(P["mix_norm"][1].reshape(1, -1)), _const(wq)]
    if transposed:
        q_out = (jax.ShapeDtypeStruct((wq.shape[0], t), BF16), pl.BlockSpec((wq.shape[0], tm), lambda i: (0, i)))
    else:
        q_out = _rows_out(t, tm, wq.shape[1], BF16)
    return _rows_call(functools.partial(_rows_c_kernel, scale=SWA_HEAD_DIM ** -0.5 * LOG2E, transposed=transposed),
                      name, t, tm, ins, [_rows_out(t, tm, d, F32), q_out])


def _rows_d(a, x, P, w, wo, name):
    t, d = x.shape
    tm = _row_tile(t, ROW_TILE_LIGHT)
    ins = ([_rows(a, tm), _const(wo), _rows(x, tm)] + _ffn_ins(P, w, 2, 1)
           + [_const(P["final_norm"].reshape(1, -1))])
    return _rows_call(_rows_d_kernel, name, t, tm, ins, [_rows_out(t, tm, d, F32)])[0]


def _trunk_prompt(x3, P, w, meta):
    batch, seq, d = x3.shape
    heads, c_dim, kvw = meta["heads"], meta["c_dim"], meta["kvw"]
    x = x3.reshape(batch * seq, d)
    x, q, k, vt, lat, kr = _rows_a(x, P, w, _rope_table(jnp.arange(seq)), heads, c_dim, absorbed=False)
    o = _mla_attn_prompt(q, k, vt, batch, seq, heads)
    x, kv, k2, v_t = _rows_b(o, x, P, w, meta, decode=False)
    x, q_t = _rows_c(x, P, w, w["swa_wq_t"], "rows_c", transposed=True)
    blk = 2 * CHUNK
    rel = (jnp.arange(2 * blk) - blk)[:, None] - jnp.arange(blk)[None, :]
    bias_t = _bias_table(P["rel_bias"], _rel_bucket(rel).astype(jnp.int32))
    o = _swa_prompt(q_t, k2, v_t, bias_t, P["swa_sinks"][0], batch, seq, meta["s_heads"], meta["rep"], blk)
    y = _rows_d(o, x, P, w, w["swa_wo"], "rows_d")
    keep = min(WINDOW, seq)
    kv3 = kv.reshape(batch, seq, 2 * kvw)[:, seq - keep:]
    new_k = kv3[:, :, :kvw].reshape(batch, keep, meta["groups"], SWA_HEAD_DIM)
    new_v = kv3[:, :, kvw:].reshape(batch, keep, meta["groups"], SWA_HEAD_DIM)
    return (y.reshape(batch, seq, d), lat.reshape(1, batch, seq, c_dim), kr.reshape(1, batch, seq, MLA_ROPE),
            new_k, new_v)


def _trunk_decode(x3, cache_lat, cache_rope, cache_k, cache_v, P, w, meta):
    batch, n_new, d = x3.shape
    heads, c_dim, kvw, s_heads = meta["heads"], meta["c_dim"], meta["kvw"], meta["s_heads"]
    past = cache_lat.shape[2]
    w_c = cache_k.shape[1]
    qpos = past + jnp.arange(n_new)
    assert past % CHUNK == 0 and n_new <= CHUNK and w_c <= WINDOW_CHUNKS * CHUNK and w_c <= past
    t = batch * n_new
    x = x3.reshape(t, d)
    tm = _row_tile(t)
    assert tm % n_new == 0
    tab = jnp.tile(_rope_table(qpos), (tm // n_new, 1))
    x, q, ql, lat, kr = _rows_a(x, P, w, tab, heads, c_dim, absorbed=True)
    ol = _mla_attn_decode(ql, q, cache_lat, jnp.swapaxes(cache_rope, 2, 3), lat, kr, heads, n_new)
    x, kv = _rows_b(ol, x, P, w, meta, decode=True)
    x, qs = _rows_c(x, P, w, w["swa_wq"], "rows_c_decode", transposed=False)
    kpos = jnp.arange(past - w_c, past + n_new)
    bias_h = _bias_table(P["rel_bias"], _rel_bucket(kpos[None, :] - qpos[:, None]).astype(jnp.int32))
    bias = bias_h.reshape(s_heads * n_new, w_c + n_new)
    sink_rows = jnp.repeat(P["swa_sinks"][0], n_new).reshape(s_heads * n_new, 1)
    const2 = lambda a: pl.BlockSpec(a.shape, lambda b: (0, 0), pipeline_mode=pl.Buffered(1))
    o = pl.pallas_call(
        functools.partial(_swa_decode_kernel, kv_w=kvw, groups=meta["groups"], rep=meta["rep"]),
        grid=(batch,),
        in_specs=[pl.BlockSpec((n_new, qs.shape[1]), lambda b: (b, 0)),
                  pl.BlockSpec((None, w_c, kvw), lambda b: (b, 0, 0)),
                  pl.BlockSpec((None, w_c, kvw), lambda b: (b, 0, 0)),
                  pl.BlockSpec((n_new, 2 * kvw), lambda b: (b, 0)),
                  const2(w["rep_kv"]), const2(bias), const2(sink_rows)],
        out_specs=pl.BlockSpec((n_new, qs.shape[1]), lambda b: (b, 0)),
        out_shape=jax.ShapeDtypeStruct(qs.shape, BF16),
        compiler_params=_params(1), name="swa_decode",
    )(qs, cache_k.reshape(batch, w_c, kvw), cache_v.reshape(batch, w_c, kvw), kv, w["rep_kv"], bias, sink_rows)
    y = _rows_d(o, x, P, w, w["swa_wo"], "rows_d_decode")
    kv3 = kv.reshape(batch, n_new, 2 * kvw)
    new_k = kv3[:, :, :kvw].reshape(batch, n_new, meta["groups"], SWA_HEAD_DIM)
    new_v = kv3[:, :, kvw:].reshape(batch, n_new, meta["groups"], SWA_HEAD_DIM)
    return (y.reshape(batch, n_new, d), lat.reshape(1, batch, n_new, c_dim), kr.reshape(1, batch, n_new, MLA_ROPE),
            new_k, new_v)


def kernel(x_prompt, x_sample, cache_mla_latent, cache_mla_krope, cache_swa_k, cache_swa_v, ffn_norm1, ffn1_w_gate, ffn1_w_up, ffn1_w_down, mix_norm, ffn_norm2, ffn2_w_gate, ffn2_w_up, ffn2_w_down, mla_w_dq, mla_q_norm, mla_w_uq, mla_w_dkv, mla_kv_norm, mla_w_uk, mla_w_uv, mla_w_o, kv_norm, w_kv_shared, swa_w_q, swa_sinks, swa_w_o, rel_bias, final_norm):
    assert ffn_norm1.shape[0] == 2 and mla_w_dq.shape[0] == 1 and swa_w_q.shape[0] == 1
    scale = (MLA_NOPE + MLA_ROPE) ** -0.5 * LOG2E
    w, meta = _prep_weights(mla_w_dq[0], mla_w_uq[0], mla_w_dkv[0], mla_w_uk[0], mla_w_uv[0], mla_w_o[0],
                            w_kv_shared, swa_w_q[0], swa_w_o[0], scale)
    w.update(f1g=ffn1_w_gate.astype(BF16), f1u=ffn1_w_up.astype(BF16), f1d=ffn1_w_down.astype(BF16),
             f2g=ffn2_w_gate.astype(BF16), f2u=ffn2_w_up.astype(BF16), f2d=ffn2_w_down.astype(BF16))
    P = dict(ffn_norm1=ffn_norm1, mix_norm=mix_norm, ffn_norm2=ffn_norm2, mla_q_norm=mla_q_norm,
             mla_kv_norm=mla_kv_norm, kv_norm=kv_norm, swa_sinks=swa_sinks, rel_bias=rel_bias, final_norm=final_norm)
    y_p, lat_p, rope_p, k_p, v_p = _trunk_prompt(x_prompt, P, w, meta)
    y_s, lat_s, rope_s, k_s, v_s = _trunk_decode(x_sample, cache_mla_latent, cache_mla_krope,
                                                 cache_swa_k, cache_swa_v, P, w, meta)
    return (y_p, y_s, lat_p, rope_p, k_p, v_p, lat_s, rope_s, k_s, v_s)
```

```python
import functools
import math

import jax
import jax.numpy as jnp
from jax import lax
from jax.experimental import pallas as pl
from jax.experimental.pallas import tpu as pltpu

F32 = jnp.float32
BF16 = jnp.bfloat16

CHUNK = 64
RMS_EPS = 1e-6
FFN_RES = 0.5
ROPE_BASE = 10000.0
WINDOW = 128
WINDOW_CHUNKS = WINDOW // CHUNK
N_BUCKETS = 32
MAX_DISTANCE = 128
NEG_INF = -1e30
LOG2E = math.log2(math.e)
MLA_NOPE = 64
MLA_ROPE = 32
MLA_V = 64
MLA_VT_ROWS = 80
SWA_HEAD_DIM = 64

LANES = 128
ROW_TILE = 512
ROW_TILE_LIGHT = 512
MLA_BQ = 512
MLA_BK = 512
MLA_DIAG = 256
MLA_DECODE_CHUNK = 1024
MLA_HEADS_PER_STEP = 4
VMEM_LIMIT = 60 * 1024 * 1024


def _params(n_axes, vmem=VMEM_LIMIT):
    return pltpu.CompilerParams(dimension_semantics=("arbitrary",) * n_axes, vmem_limit_bytes=vmem)


def _rms(xf, g):
    return xf * lax.rsqrt(jnp.mean(xf * xf, axis=-1, keepdims=True) + RMS_EPS) * g


def _dot(a, b):
    return jnp.dot(a, b, preferred_element_type=F32)


def _dot_nt(a, b):
    return lax.dot_general(a, b, (((1,), (1,)), ((), ())), preferred_element_type=F32)


def _row_tile(t, tile=None):
    tm = min(tile or ROW_TILE, t)
    assert t % tm == 0, (t, tm)
    return tm


def _rows(a, tm):
    return a, pl.BlockSpec((tm, a.shape[1]), lambda i: (i, 0))


def _const(a):
    nd = a.ndim
    return a, pl.BlockSpec(a.shape, lambda i: (0,) * nd, pipeline_mode=pl.Buffered(1))


def _layer(a, layer):
    nd = a.ndim - 1
    return a, pl.BlockSpec((None,) + tuple(a.shape[1:]), lambda i: (layer,) + (0,) * nd,
                           pipeline_mode=pl.Buffered(1))


def _cycle(a, tm):
    assert a.shape[0] % tm == 0
    nb = a.shape[0] // tm
    return a, pl.BlockSpec((tm, a.shape[1]), lambda i: (i % nb, 0))


def _rows_out(t, tm, ncols, dtype):
    return jax.ShapeDtypeStruct((t, ncols), dtype), pl.BlockSpec((tm, ncols), lambda i: (i, 0))


def _rows_call(body, name, t, tm, ins, outs):
    return pl.pallas_call(
        body, grid=(t // tm,),
        in_specs=[s for _, s in ins], out_specs=[s for _, s in outs], out_shape=[o for o, _ in outs],
        compiler_params=_params(1), name=name,
    )(*[a for a, _ in ins])


def _ffn_ins(P, w, which, layer):
    return [_const(P["ffn_norm%d" % which][layer].reshape(1, -1)), _layer(w["f%dg" % which], layer),
            _layer(w["f%du" % which], layer), _layer(w["f%dd" % which], layer)]


def _ffn_apply(x, g_ref, wg_ref, wu_ref, wd_ref):
    h = _rms(x, g_ref[...]).astype(BF16)
    gate = _dot(h, wg_ref[...])
    up = _dot(h, wu_ref[...])
    a = (gate * jax.nn.sigmoid(gate) * up).astype(BF16)
    return x + FFN_RES * _dot(a, wd_ref[...])


def _mla_proj_apply(x, mg_ref, wdq_ref, qn_ref, wq_ref, wlat_ref, kvn_ref, wkr_ref, tab_ref,
                    refs, heads, c_dim, absorbed):
    cos_t, sin_t = tab_ref[:, 0:LANES], tab_ref[:, LANES:2 * LANES]
    lane = lax.broadcasted_iota(jnp.int32, (1, LANES), 1)
    cos_q = jnp.where(lane < MLA_NOPE, 1.0, cos_t)
    first_half = lane < MLA_NOPE + MLA_ROPE // 2

    def swap_halves(v):
        return jnp.where(first_half, pltpu.roll(v, LANES - MLA_ROPE // 2, 1), pltpu.roll(v, MLA_ROPE // 2, 1))

    h = _rms(x, mg_ref[...]).astype(BF16)
    cq = _rms(_dot(h, wdq_ref[...]), qn_ref[...]).astype(BF16)
    qa = _dot(cq, wq_ref[...])
    lat = _rms(_dot(h, wlat_ref[...]), kvn_ref[...])
    kr = _dot(h, wkr_ref[...])
    kr = kr * cos_t + swap_halves(kr) * sin_t
    latb = lat.astype(BF16)
    if absorbed:
        wukd_ref, q_ref, ql_ref, lat_ref, kr_ref = refs
    else:
        wuk_ref, wuvt_ref, q_ref, k_ref, vt_ref, lat_ref, kr_ref = refs
        kn = _dot(latb, wuk_ref[...])
    lat_ref[...] = lat
    kr_ref[...] = kr[:, MLA_NOPE:MLA_NOPE + MLA_ROPE]
    for hh in range(heads):
        sl = slice(hh * LANES, (hh + 1) * LANES)
        qh = (qa[:, sl] * cos_q + swap_halves(qa[:, sl]) * sin_t).astype(BF16)
        q_ref[:, sl] = qh
        if absorbed:
            ql_ref[:, hh * c_dim:(hh + 1) * c_dim] = _dot(qh, wukd_ref[hh]).astype(BF16)
        else:
            k_ref[:, sl] = (kn[:, sl] + kr).astype(BF16)
    if not absorbed:
        vt = _dot_nt(wuvt_ref[...], latb)
        ones_row = lax.broadcasted_iota(jnp.int32, vt.shape, 0) % MLA_VT_ROWS == MLA_V
        vt_ref[...] = jnp.where(ones_row, 1.0, vt).astype(BF16)


def _rows_a_kernel(x_ref, fg, wg, wu, wd, mg, wdq, qn, wq, wlat, kvn, wkr, tab, *refs,
                   heads, c_dim, absorbed):
    n_extra = 1 if absorbed else 2
    x1_ref = refs[n_extra]
    x1 = _ffn_apply(x_ref[...], fg, wg, wu, wd)
    x1_ref[...] = x1
    _mla_proj_apply(x1, mg, wdq, qn, wq, wlat, kvn, wkr, tab, refs[:n_extra] + refs[n_extra + 1:],
                    heads, c_dim, absorbed)


def _rows_b_kernel(a_ref, *refs, dec_heads, c_dim, tiled_kv):
    if dec_heads:
        wuv_ref, refs = refs[0], refs[1:]
        a = jnp.concatenate([_dot(a_ref[:, hh * c_dim:(hh + 1) * c_dim], wuv_ref[hh]).astype(BF16)
                             for hh in range(dec_heads)], axis=1)
    else:
        a = a_ref[...]
    wo, x_ref, fg, wg, wu, wd, kvg, wkv = refs[:8]
    x = _ffn_apply(x_ref[...] + _dot(a, wo[...]), fg, wg, wu, wd)
    kv = _dot(_rms(x, kvg[...]).astype(BF16), wkv[...])
    if tiled_kv:
        repk, eye, x_out, kv_out, k2_out, vt_out = refs[8:]
        half = kv.shape[1] // 2
        k2_out[...] = _dot(kv[:, :half].astype(BF16), repk[...]).astype(BF16)
        vt_out[...] = _dot_nt(eye[...], kv[:, half:].astype(BF16)).astype(BF16)
    else:
        x_out, kv_out = refs[8:]
    x_out[...] = x
    kv_out[...] = kv


def _rows_c_kernel(x_ref, fg, wg, wu, wd, mg, wq, x_out, q_out, *, scale, transposed):
    x = _ffn_apply(x_ref[...], fg, wg, wu, wd)
    x_out[...] = x
    h = _rms(x, mg[...]).astype(BF16)
    q = _dot_nt(wq[...], h) if transposed else _dot(h, wq[...])
    q_out[...] = (q * scale).astype(BF16)


def _rows_d_kernel(a_ref, wo, x_ref, fg, wg, wu, wd, fin_g, y_out):
    x = _ffn_apply(x_ref[...] + _dot(a_ref[...], wo[...]), fg, wg, wu, wd)
    y_out[...] = _rms(x, fin_g[...])


def _mla_attn_kernel(q_ref, k_ref, vt_ref, o_ref, m_sc, acc_sc, *, bq, bk, hp):
    i = pl.program_id(2)
    qs = [q_ref[:, hh * LANES:(hh + 1) * LANES] for hh in range(hp)]
    sls = [slice(hh * LANES, (hh + 1) * LANES) for hh in range(hp)]

    def block(start, width, q_lo, masked):
        ql = slice(q_lo, bq)
        ss = [_dot_nt(k_ref[pl.ds(start, width), sls[hh]], qs[hh][ql]) for hh in range(hp)]
        if masked:
            kc = (start + lax.broadcasted_iota(jnp.int32, (width, 1), 0)) // CHUNK
            qc = (i * bq + q_lo + lax.broadcasted_iota(jnp.int32, (1, bq - q_lo), 1)) // CHUNK
            ss = [jnp.where(kc <= qc, s, NEG_INF) for s in ss]
        for hh in range(hp):
            m = m_sc[hh, :, ql]
            m_new = jnp.maximum(m, jnp.max(ss[hh], axis=0, keepdims=True))
            alpha = jnp.exp2(m - m_new)
            p = jnp.exp2(ss[hh] - m_new).astype(BF16)
            vt = vt_ref[hh * MLA_VT_ROWS:(hh + 1) * MLA_VT_ROWS, pl.ds(start, width)]
            acc_sc[hh, :, ql] = alpha * acc_sc[hh, :, ql] + _dot(vt, p)
            m_sc[hh, :, ql] = m_new

    m_sc[...] = jnp.full(m_sc.shape, NEG_INF, F32)
    acc_sc[...] = jnp.zeros(acc_sc.shape, F32)
    n_full = (i * bq) // bk

    @pl.loop(0, n_full)
    def _(j):
        block(pl.multiple_of(j * bk, bk), bk, 0, False)

    diag = min(MLA_DIAG, bq)
    for t in range(bq // diag):
        block(pl.multiple_of(i * bq + t * diag, diag), diag, t * diag, True)
    for pair in range(hp // 2):
        halves = []
        for hh in (2 * pair, 2 * pair + 1):
            acc = acc_sc[hh]
            halves.append(acc[:MLA_V] * (1.0 / acc[MLA_V:MLA_V + 1]))
        o_ref[:, pair * LANES:(pair + 1) * LANES] = jnp.concatenate(halves, axis=0).T.astype(BF16)


def _mla_attn_prompt(q, k, vt, batch, seq, heads, bq=MLA_BQ, bk=MLA_BK, hp=MLA_HEADS_PER_STEP):
    bq, bk = min(bq, seq), min(bk, seq)
    nq = seq // bq
    assert seq % bq == 0 and seq % bk == 0 and bq % CHUNK == 0 and heads % hp == 0 and hp % 2 == 0
    assert bq % bk == 0 and bq % min(MLA_DIAG, bq) == 0 and MLA_DIAG % CHUNK == 0
    w = hp * LANES
    return pl.pallas_call(
        functools.partial(_mla_attn_kernel, bq=bq, bk=bk, hp=hp),
        grid=(batch, heads // hp, nq),
        in_specs=[pl.BlockSpec((bq, w), lambda b, h, i: (b * nq + i, h)),
                  pl.BlockSpec((seq, w), lambda b, h, i: (b, h)),
                  pl.BlockSpec((hp * MLA_VT_ROWS, seq), lambda b, h, i: (h, b))],
        out_specs=pl.BlockSpec((bq, hp * MLA_V), lambda b, h, i: (b * nq + i, h)),
        out_shape=jax.ShapeDtypeStruct((q.shape[0], heads * MLA_V), BF16),
        scratch_shapes=[pltpu.VMEM((hp, 1, bq), F32), pltpu.VMEM((hp, MLA_VT_ROWS, bq), F32)],
        compiler_params=_params(3), name="mla_attn_prompt",
    )(q, k, vt)


def _mla_attn_decode_kernel(ql_ref, q_ref, cl_ref, crt_ref, nl_ref, nr_ref, o_ref, *, heads, c_dim):
    n = ql_ref.shape[0]
    ql = jnp.concatenate([ql_ref[:, hh * c_dim:(hh + 1) * c_dim] for hh in range(heads)], axis=0)
    qr = jnp.concatenate([q_ref[:, hh * LANES + MLA_NOPE:hh * LANES + MLA_NOPE + MLA_ROPE] for hh in range(heads)],
                         axis=0)
    past = cl_ref.shape[0]
    chunk = min(past, MLA_DECODE_CHUNK)
    parts = [(cl_ref[c:c + chunk, :], crt_ref[:, c:c + chunk], _dot) for c in range(0, past, chunk)]
    parts.append((nl_ref[...], nr_ref[...], _dot_nt))
    m = jnp.full((heads * n, 1), NEG_INF, F32)
    l = jnp.zeros((heads * n, 1), F32)
    acc = jnp.zeros((heads * n, c_dim), F32)
    for kl, kr, rope_dot in parts:
        kl, kr = kl.astype(BF16), kr.astype(BF16)
        s = _dot_nt(ql, kl) + rope_dot(qr, kr)
        m_new = jnp.maximum(m, jnp.max(s, axis=-1, keepdims=True))
        alpha = jnp.exp2(m - m_new)
        p = jnp.exp2(s - m_new)
        l = alpha * l + jnp.sum(p, axis=-1, keepdims=True)
        acc = alpha * acc + _dot(p.astype(BF16), kl)
        m = m_new
    o = acc / l
    for hh in range(heads):
        o_ref[:, hh * c_dim:(hh + 1) * c_dim] = o[hh * n:(hh + 1) * n].astype(BF16)


def _mla_attn_decode(ql, q, cache_lat, cache_rope_t, new_lat, new_rope, heads, n_new):
    _, batch, past, c_dim = cache_lat.shape
    r_dim = cache_rope_t.shape[2]
    return pl.pallas_call(
        functools.partial(_mla_attn_decode_kernel, heads=heads, c_dim=c_dim),
        grid=(batch,),
        in_specs=[pl.BlockSpec((n_new, heads * c_dim), lambda b: (b, 0)),
                  pl.BlockSpec((n_new, heads * LANES), lambda b: (b, 0)),
                  pl.BlockSpec((None, None, past, c_dim), lambda b: (0, b, 0, 0)),
                  pl.BlockSpec((None, None, r_dim, past), lambda b: (0, b, 0, 0)),
                  pl.BlockSpec((n_new, c_dim), lambda b: (b, 0)),
                  pl.BlockSpec((n_new, r_dim), lambda b: (b, 0))],
        out_specs=pl.BlockSpec((n_new, heads * c_dim), lambda b: (b, 0)),
        out_shape=jax.ShapeDtypeStruct(ql.shape, BF16),
        compiler_params=_params(1), name="mla_attn_decode",
    )(ql, q, cache_lat, cache_rope_t, new_lat, new_rope)


def _bias_kernel(tab_ref, idx_ref, o_ref, *, heads):
    idx = idx_ref[...]
    for hh in range(heads):
        acc = jnp.zeros(idx.shape, F32)
        for b in range(N_BUCKETS):
            acc = jnp.where(idx == b, tab_ref[b, hh] * LOG2E, acc)
        o_ref[hh] = acc


def _bias_table(rel_bias, idx):
    heads = rel_bias.shape[1]
    return pl.pallas_call(
        functools.partial(_bias_kernel, heads=heads),
        in_specs=[pl.BlockSpec(memory_space=pltpu.SMEM), pl.BlockSpec(idx.shape, lambda: (0, 0))],
        out_specs=pl.BlockSpec((heads,) + idx.shape, lambda: (0, 0, 0)),
        out_shape=jax.ShapeDtypeStruct((heads,) + idx.shape, F32), name="rel_bias_table",
    )(rel_bias, idx)


def _rel_bucket(rel):
    half = N_BUCKETS // 2
    max_exact = half // 2
    base = jnp.where(rel > 0, half, 0)
    n = jnp.abs(rel)
    nf = jnp.maximum(n, 1).astype(jnp.float32)
    large = max_exact + (jnp.log(nf / max_exact) / math.log(MAX_DISTANCE / max_exact)
                         * (half - max_exact)).astype(jnp.int32)
    large = jnp.minimum(large, half - 1)
    return base + jnp.where(n < max_exact, n, large)


def _swa_prompt_kernel(sink_ref, qt_ref, kp_ref, kc_ref, vtp_ref, vtc_ref, bias_ref, o_ref, *, s_heads, rep, blk):
    i = pl.program_id(1)
    key_chunk = lax.broadcasted_iota(jnp.int32, (2 * blk, 1), 0) // CHUNK
    row_chunk = lax.broadcasted_iota(jnp.int32, (1, blk), 1) // CHUNK
    first = blk // CHUNK
    valid = ((key_chunk >= row_chunk) & (key_chunk <= row_chunk + WINDOW_CHUNKS)
             & ((i > 0) | (key_chunk >= first)))
    dh = SWA_HEAD_DIM
    ss = []
    for hh in range(s_heads):
        g, pair, slot = hh // rep, hh // 2, hh % 2
        ksl = slice(g * 4 * dh + slot * 2 * dh, g * 4 * dh + (slot + 1) * 2 * dh)
        k = jnp.concatenate([kp_ref[:, ksl], kc_ref[:, ksl]], axis=0)
        ss.append(_dot(k, qt_ref[pair * 2 * dh:(pair + 1) * 2 * dh, :]))
    ps, invs = [], []
    for hh, s in enumerate(ss):
        s = jnp.where(valid, s + bias_ref[hh], NEG_INF)
        sink = sink_ref[hh] * LOG2E
        m = jnp.maximum(jnp.max(s, axis=0, keepdims=True), sink)
        p = jnp.exp2(s - m)
        invs.append(1.0 / (jnp.sum(p, axis=0, keepdims=True) + jnp.exp2(sink - m)))
        ps.append(p.astype(BF16))
    outs = []
    for hh, p in enumerate(ps):
        g = hh // rep
        vt = jnp.concatenate([vtp_ref[g * dh:(g + 1) * dh, :], vtc_ref[g * dh:(g + 1) * dh, :]], axis=1)
        outs.append(_dot(vt, p) * invs[hh])
    for pair in range(s_heads // 2):
        o_ref[:, pair * 2 * dh:(pair + 1) * 2 * dh] = (
            jnp.concatenate(outs[2 * pair:2 * pair + 2], axis=0).T.astype(BF16))


def _swa_prompt(qt, k2, vt, bias_t, sinks, batch, seq, s_heads, rep, blk=128):
    nq = seq // blk
    kw = k2.shape[1]
    cur = lambda b, i: (b * nq + i, 0)
    prev = lambda b, i: (b * nq + jnp.maximum(i - 1, 0), 0)
    cur_t = lambda b, i: (0, b * nq + i)
    prev_t = lambda b, i: (0, b * nq + jnp.maximum(i - 1, 0))
    assert rep == 4 and kw == (s_heads // rep) * 4 * SWA_HEAD_DIM
    return pl.pallas_call(
        functools.partial(_swa_prompt_kernel, s_heads=s_heads, rep=rep, blk=blk),
        grid=(batch, nq),
        in_specs=[pl.BlockSpec(memory_space=pltpu.SMEM),
                  pl.BlockSpec((qt.shape[0], blk), cur_t),
                  pl.BlockSpec((blk, kw), prev), pl.BlockSpec((blk, kw), cur),
                  pl.BlockSpec((vt.shape[0], blk), prev_t), pl.BlockSpec((vt.shape[0], blk), cur_t),
                  pl.BlockSpec(bias_t.shape, lambda b, i: (0, 0, 0), pipeline_mode=pl.Buffered(1))],
        out_specs=pl.BlockSpec((blk, qt.shape[0]), cur),
        out_shape=jax.ShapeDtypeStruct((qt.shape[1], qt.shape[0]), BF16),
        compiler_params=_params(2), name="swa_prompt",
    )(sinks, qt, k2, k2, vt, vt, bias_t)


def _swa_decode_kernel(q_ref, ck_ref, cv_ref, nkv_ref, rep_ref, bias_ref, sink_ref, o_ref, *, kv_w, groups, rep):
    n = q_ref.shape[0]
    gw = rep * SWA_HEAD_DIM
    k = jnp.concatenate([ck_ref[...], nkv_ref[:, 0:kv_w]], axis=0).astype(BF16)
    v = jnp.concatenate([cv_ref[...], nkv_ref[:, kv_w:2 * kv_w]], axis=0).astype(BF16)
    k4 = _dot(k, rep_ref[...]).astype(BF16)
    v4 = _dot(v, rep_ref[...]).astype(BF16)
    lane_head = lax.broadcasted_iota(jnp.int32, (1, gw), 1) // SWA_HEAD_DIM
    sls = [slice(g * gw, (g + 1) * gw) for g in range(groups)]
    ss = []
    for g in range(groups):
        qg = q_ref[:, sls[g]]
        qs = jnp.concatenate([jnp.where(lane_head == r, qg, jnp.zeros_like(qg)) for r in range(rep)], axis=0)
        ss.append(_dot_nt(qs, k4[:, sls[g]]))
    ps = []
    for g in range(groups):
        rows = slice(g * rep * n, (g + 1) * rep * n)
        s = ss[g] + bias_ref[rows, :]
        sink = sink_ref[rows, :] * LOG2E
        m = jnp.maximum(jnp.max(s, axis=-1, keepdims=True), sink)
        p = jnp.exp2(s - m)
        inv = 1.0 / (jnp.sum(p, axis=-1, keepdims=True) + jnp.exp2(sink - m))
        ps.append((p * inv).astype(BF16))
    for g in range(groups):
        res = _dot(ps[g], v4[:, sls[g]])
        og = jnp.zeros((n, gw), F32)
        for r in range(rep):
            og = jnp.where(lane_head == r, res[r * n:(r + 1) * n], og)
        o_ref[:, sls[g]] = og.astype(BF16)


def _rope_table(pos):
    inv = ROPE_BASE ** (-jnp.arange(0, MLA_ROPE, 2, dtype=jnp.float32) / MLA_ROPE)
    ang = pos.astype(jnp.float32)[:, None] * inv[None, :]
    cos, sin = jnp.cos(ang), jnp.sin(ang)
    widths = ((0, 0), (MLA_NOPE, LANES - MLA_NOPE - MLA_ROPE))
    return jnp.concatenate([jnp.pad(jnp.concatenate([cos, cos], axis=1), widths),
                            jnp.pad(jnp.concatenate([-sin, sin], axis=1), widths)], axis=1)


def _prep_weights(mla_w_dq, mla_w_uq, mla_w_dkv, mla_w_uk, mla_w_uv, mla_w_o, w_kv_shared, swa_w_q, swa_w_o, scale):
    ql, qcols = mla_w_uq.shape
    c_dim, heads, nope = mla_w_uk.shape
    rope = qcols // heads - nope
    d = mla_w_dq.shape[0]
    pad = LANES - nope - rope
    w = {}
    w["wq"] = jnp.pad(mla_w_uq.reshape(ql, heads, nope + rope), ((0, 0), (0, 0), (0, pad))).reshape(
        ql, heads * LANES) * scale
    w["wdq"] = mla_w_dq
    w["wlat"] = mla_w_dkv[:, :c_dim]
    w["wkr"] = jnp.pad(mla_w_dkv[:, c_dim:], ((0, 0), (nope, pad)))
    zc = jnp.zeros((c_dim, heads, LANES - nope), F32)
    w["wuk"] = jnp.concatenate([mla_w_uk, zc], axis=-1).reshape(c_dim, heads * LANES)
    w["wuv_t"] = jnp.pad(mla_w_uv, ((0, 0), (0, 0), (0, MLA_VT_ROWS - mla_w_uv.shape[2]))).reshape(
        c_dim, heads * MLA_VT_ROWS).T
    w["wuk_dec"] = jnp.concatenate([jnp.transpose(mla_w_uk, (1, 2, 0)),
                                    jnp.zeros((heads, LANES - nope, c_dim), F32)], axis=1)
    w["wuv_dec"] = jnp.concatenate([jnp.transpose(mla_w_uv, (1, 0, 2)),
                                    jnp.zeros((heads, c_dim, LANES - nope), F32)], axis=-1)
    vdim = mla_w_uv.shape[2]
    assert vdim == MLA_V and nope == MLA_NOPE and rope == MLA_ROPE
    w["wo"] = mla_w_o
    w["wo_pad"] = jnp.concatenate([mla_w_o.reshape(heads, vdim, d), jnp.zeros((heads, LANES - vdim, d), F32)],
                                  axis=1).reshape(heads * LANES, d)
    kvw = w_kv_shared.shape[1] // 2
    groups = kvw // SWA_HEAD_DIM
    s_heads = swa_w_q.shape[1] // SWA_HEAD_DIM
    rep = s_heads // groups
    w["wkv"] = w_kv_shared
    eye = jnp.eye(SWA_HEAD_DIM, dtype=F32)
    zero = jnp.zeros_like(eye)
    w["rep_k2"] = jnp.kron(jnp.eye(groups, dtype=F32), jnp.concatenate([eye, zero, zero, eye], axis=1))
    w["eye_kv"] = jnp.eye(kvw, dtype=F32)
    w["swa_wq_t"] = swa_w_q.T
    w["swa_wo"] = swa_w_o
    w["rep_kv"] = jnp.kron(jnp.eye(groups, dtype=F32), jnp.tile(eye, (1, rep)))
    w["swa_wq"] = swa_w_q
    return {k: v.astype(BF16) for k, v in w.items()}, dict(heads=heads, c_dim=c_dim, groups=groups, rep=rep,
                                                            s_heads=s_heads, kvw=kvw)


def _rows_a(x, P, w, tab, heads, c_dim, absorbed):
    t, d = x.shape
    tm = _row_tile(t)
    hw = heads * LANES
    row = lambda v: v.reshape(1, -1)
    ins = ([_rows(x, tm)] + _ffn_ins(P, w, 1, 0)
           + [_const(row(P["mix_norm"][0])), _const(w["wdq"]), _const(row(P["mla_q_norm"][0])), _const(w["wq"]),
              _const(w["wlat"]), _const(row(P["mla_kv_norm"][0])), _const(w["wkr"]), _cycle(tab, tm)]
           + ([_const(w["wuk_dec"])] if absorbed else [_const(w["wuk"]), _const(w["wuv_t"])]))
    outs = [_rows_out(t, tm, d, F32), _rows_out(t, tm, hw, BF16)]
    if absorbed:
        outs.append(_rows_out(t, tm, heads * c_dim, BF16))
    else:
        outs += [_rows_out(t, tm, hw, BF16),
                 (jax.ShapeDtypeStruct((heads * MLA_VT_ROWS, t), BF16),
                  pl.BlockSpec((heads * MLA_VT_ROWS, tm), lambda i: (0, i)))]
    outs += [_rows_out(t, tm, c_dim, F32), _rows_out(t, tm, MLA_ROPE, F32)]
    return _rows_call(functools.partial(_rows_a_kernel, heads=heads, c_dim=c_dim, absorbed=absorbed),
                      "rows_a_absorbed" if absorbed else "rows_a", t, tm, ins, outs)


def _rows_b(a, x, P, w, meta, decode):
    t, d = x.shape
    tm = _row_tile(t, ROW_TILE_LIGHT)
    kvw = meta["kvw"]
    ins = [_rows(a, tm)]
    if decode:
        ins += [_const(w["wuv_dec"]), _const(w["wo_pad"])]
    else:
        ins += [_const(w["wo"])]
    ins += [_rows(x, tm)] + _ffn_ins(P, w, 2, 0) + [_const(P["kv_norm"].reshape(1, -1)), _const(w["wkv"])]
    outs = [_rows_out(t, tm, d, F32), _rows_out(t, tm, 2 * kvw, F32)]
    if not decode:
        ins += [_const(w["rep_k2"]), _const(w["eye_kv"])]
        outs += [_rows_out(t, tm, w["rep_k2"].shape[1], BF16),
                 (jax.ShapeDtypeStruct((kvw, t), BF16), pl.BlockSpec((kvw, tm), lambda i: (0, i)))]
    return _rows_call(functools.partial(_rows_b_kernel, dec_heads=meta["heads"] if decode else 0,
                                        c_dim=meta["c_dim"], tiled_kv=not decode),
                      "rows_b_decode" if decode else "rows_b", t, tm, ins, outs)


def _rows_c(x, P, w, wq, name, transposed):
    t, d = x.shape
    tm = _row_tile(t, ROW_TILE_LIGHT)
    ins = [_rows(x, tm)] + _ffn_ins(P, w, 1, 1) + [_const(P["mix_norm"][1].reshape(1, -1)), _const(wq)]
    if transposed:
        q_out = (jax.ShapeDtypeStruct((wq.shape[0], t), BF16), pl.BlockSpec((wq.shape[0], tm), lambda i: (0, i)))
    else:
        q_out = _rows_out(t, tm, wq.shape[1], BF16)
    return _rows_call(functools.partial(_rows_c_kernel, scale=SWA_HEAD_DIM ** -0.5 * LOG2E, transposed=transposed),
                      name, t, tm, ins, [_rows_out(t, tm, d, F32), q_out])


def _rows_d(a, x, P, w, wo, name):
    t, d = x.shape
    tm = _row_tile(t, ROW_TILE_LIGHT)
    ins = ([_rows(a, tm), _const(wo), _rows(x, tm)] + _ffn_ins(P, w, 2, 1)
           + [_const(P["final_norm"].reshape(1, -1))])
    return _rows_call(_rows_d_kernel, name, t, tm, ins, [_rows_out(t, tm, d, F32)])[0]


def _trunk_prompt(x3, P, w, meta):
    batch, seq, d = x3.shape
    heads, c_dim, kvw = meta["heads"], meta["c_dim"], meta["kvw"]
    x = x3.reshape(batch * seq, d)
    x, q, k, vt, lat, kr = _rows_a(x, P, w, _rope_table(jnp.arange(seq)), heads, c_dim, absorbed=False)
    o = _mla_attn_prompt(q, k, vt, batch, seq, heads)
    x, kv, k2, v_t = _rows_b(o, x, P, w, meta, decode=False)
    x, q_t = _rows_c(x, P, w, w["swa_wq_t"], "rows_c", transposed=True)
    blk = 2 * CHUNK
    rel = (jnp.arange(2 * blk) - blk)[:, None] - jnp.arange(blk)[None, :]
    bias_t = _bias_table(P["rel_bias"], _rel_bucket(rel).astype(jnp.int32))
    o = _swa_prompt(q_t, k2, v_t, bias_t, P["swa_sinks"][0], batch, seq, meta["s_heads"], meta["rep"], blk)
    y = _rows_d(o, x, P, w, w["swa_wo"], "rows_d")
    keep = min(WINDOW, seq)
    kv3 = kv.reshape(batch, seq, 2 * kvw)[:, seq - keep:]
    new_k = kv3[:, :, :kvw].reshape(batch, keep, meta["groups"], SWA_HEAD_DIM)
    new_v = kv3[:, :, kvw:].reshape(batch, keep, meta["groups"], SWA_HEAD_DIM)
    return (y.reshape(batch, seq, d), lat.reshape(1, batch, seq, c_dim), kr.reshape(1, batch, seq, MLA_ROPE),
            new_k, new_v)


def _trunk_decode(x3, cache_lat, cache_rope, cache_k, cache_v, P, w, meta):
    batch, n_new, d = x3.shape
    heads, c_dim, kvw, s_heads = meta["heads"], meta["c_dim"], meta["kvw"], meta["s_heads"]
    past = cache_lat.shape[2]
    w_c = cache_k.shape[1]
    qpos = past + jnp.arange(n_new)
    assert past % CHUNK == 0 and n_new <= CHUNK and w_c <= WINDOW_CHUNKS * CHUNK and w_c <= past
    t = batch * n_new
    x = x3.reshape(t, d)
    tm = _row_tile(t)
    assert tm % n_new == 0
    tab = jnp.tile(_rope_table(qpos), (tm // n_new, 1))
    x, q, ql, lat, kr = _rows_a(x, P, w, tab, heads, c_dim, absorbed=True)
    ol = _mla_attn_decode(ql, q, cache_lat, jnp.swapaxes(cache_rope, 2, 3), lat, kr, heads, n_new)
    x, kv = _rows_b(ol, x, P, w, meta, decode=True)
    x, qs = _rows_c(x, P, w, w["swa_wq"], "rows_c_decode", transposed=False)
    kpos = jnp.arange(past - w_c, past + n_new)
    bias_h = _bias_table(P["rel_bias"], _rel_bucket(kpos[None, :] - qpos[:, None]).astype(jnp.int32))
    bias = bias_h.reshape(s_heads * n_new, w_c + n_new)
    sink_rows = jnp.repeat(P["swa_sinks"][0], n_new).reshape(s_heads * n_new, 1)
    const2 = lambda a: pl.BlockSpec(a.shape, lambda b: (0, 0), pipeline_mode=pl.Buffered(1))
    o = pl.pallas_call(
        functools.partial(_swa_decode_kernel, kv_w=kvw, groups=meta["groups"], rep=meta["rep"]),
        grid=(batch,),
        in_specs=[pl.BlockSpec((n_new, qs.shape[1]), lambda b: (b, 0)),
                  pl.BlockSpec((None, w_c, kvw), lambda b: (b, 0, 0)),
                  pl.BlockSpec((None, w_c, kvw), lambda b: (b, 0, 0)),
                  pl.BlockSpec((n_new, 2 * kvw), lambda b: (b, 0)),
                  const2(w["rep_kv"]), const2(bias), const2(sink_rows)],
        out_specs=pl.BlockSpec((n_new, qs.shape[1]), lambda b: (b, 0)),
        out_shape=jax.ShapeDtypeStruct(qs.shape, BF16),
        compiler_params=_params(1), name="swa_decode",
    )(qs, cache_k.reshape(batch, w_c, kvw), cache_v.reshape(batch, w_c, kvw), kv, w["rep_kv"], bias, sink_rows)
    y = _rows_d(o, x, P, w, w["swa_wo"], "rows_d_decode")
    kv3 = kv.reshape(batch, n_new, 2 * kvw)
    new_k = kv3[:, :, :kvw].reshape(batch, n_new, meta["groups"], SWA_HEAD_DIM)
    new_v = kv3[:, :, kvw:].reshape(batch, n_new, meta["groups"], SWA_HEAD_DIM)
    return (y.reshape(batch, n_new, d), lat.reshape(1, batch, n_new, c_dim), kr.reshape(1, batch, n_new, MLA_ROPE),
            new_k, new_v)


def kernel(x_prompt, x_sample, cache_mla_latent, cache_mla_krope, cache_swa_k, cache_swa_v, ffn_norm1, ffn1_w_gate, ffn1_w_up, ffn1_w_down, mix_norm, ffn_norm2, ffn2_w_gate, ffn2_w_up, ffn2_w_down, mla_w_dq, mla_q_norm, mla_w_uq, mla_w_dkv, mla_kv_norm, mla_w_uk, mla_w_uv, mla_w_o, kv_norm, w_kv_shared, swa_w_q, swa_sinks, swa_w_o, rel_bias, final_norm):
    assert ffn_norm1.shape[0] == 2 and mla_w_dq.shape[0] == 1 and swa_w_q.shape[0] == 1
    scale = (MLA_NOPE + MLA_ROPE) ** -0.5 * LOG2E
    w, meta = _prep_weights(mla_w_dq[0], mla_w_uq[0], mla_w_dkv[0], mla_w_uk[0], mla_w_uv[0], mla_w_o[0],
                            w_kv_shared, swa_w_q[0], swa_w_o[0], scale)
    w.update(f1g=ffn1_w_gate.astype(BF16), f1u=ffn1_w_up.astype(BF16), f1d=ffn1_w_down.astype(BF16),
             f2g=ffn2_w_gate.astype(BF16), f2u=ffn2_w_up.astype(BF16), f2d=ffn2_w_down.astype(BF16))
    P = dict(ffn_norm1=ffn_norm1, mix_norm=mix_norm, ffn_norm2=ffn_norm2, mla_q_norm=mla_q_norm,
             mla_kv_norm=mla_kv_norm, kv_norm=kv_norm, swa_sinks=swa_sinks, rel_bias=rel_bias, final_norm=final_norm)
    y_p, lat_p, rope_p, k_p, v_p = _trunk_prompt(x_prompt, P, w, meta)
    y_s, lat_s, rope_s, k_s, v_s = _trunk_decode(x_sample, cache_mla_latent, cache_mla_krope,
                                                 cache_swa_k, cache_swa_v, P, w, meta)
    return (y_p, y_s, lat_p, rope_p, k_p, v_p, lat_s, rope_s, k_s, v_s)
```

```python
import functools
import math

import jax
import jax.numpy as jnp
from jax import lax
from jax.experimental import pallas as pl
from jax.experimental.pallas import tpu as pltpu

F32 = jnp.float32
BF16 = jnp.bfloat16

CHUNK = 64
RMS_EPS = 1e-6
FFN_RES = 0.5
ROPE_BASE = 10000.0
WINDOW = 128
WINDOW_CHUNKS = WINDOW // CHUNK
N_BUCKETS = 32
MAX_DISTANCE = 128
NEG_INF = -1e30
LOG2E = math.log2(math.e)
MLA_NOPE = 64
MLA_ROPE = 32
MLA_V = 64
MLA_VT_ROWS = 80
SWA_HEAD_DIM = 64

LANES = 128
ROW_TILE = 512
ROW_TILE_LIGHT = 512
MLA_BQ = 512
MLA_BK = 512
MLA_DIAG = 256
MLA_DECODE_CHUNK = 1024
MLA_HEADS_PER_STEP = 4
VMEM_LIMIT = 60 * 1024 * 1024


def _params(n_axes, vmem=VMEM_LIMIT):
    return pltpu.CompilerParams(dimension_semantics=("arbitrary",) * n_axes, vmem_limit_bytes=vmem)


def _rms(xf, g):
    return xf * lax.rsqrt(jnp.mean(xf * xf, axis=-1, keepdims=True) + RMS_EPS) * g


def _dot(a, b):
    return jnp.dot(a, b, preferred_element_type=F32)


def _dot_nt(a, b):
    return lax.dot_general(a, b, (((1,), (1,)), ((), ())), preferred_element_type=F32)


def _row_tile(t, tile=None):
    tm = min(tile or ROW_TILE, t)
    assert t % tm == 0, (t, tm)
    return tm


def _rows(a, tm):
    return a, pl.BlockSpec((tm, a.shape[1]), lambda i: (i, 0))


def _const(a):
    nd = a.ndim
    return a, pl.BlockSpec(a.shape, lambda i: (0,) * nd, pipeline_mode=pl.Buffered(1))


def _layer(a, layer):
    nd = a.ndim - 1
    return a, pl.BlockSpec((None,) + tuple(a.shape[1:]), lambda i: (layer,) + (0,) * nd,
                           pipeline_mode=pl.Buffered(1))


def _cycle(a, tm):
    assert a.shape[0] % tm == 0
    nb = a.shape[0] // tm
    return a, pl.BlockSpec((tm, a.shape[1]), lambda i: (i % nb, 0))


def _rows_out(t, tm, ncols, dtype):
    return jax.ShapeDtypeStruct((t, ncols), dtype), pl.BlockSpec((tm, ncols), lambda i: (i, 0))


def _rows_call(body, name, t, tm, ins, outs):
    return pl.pallas_call(
        body, grid=(t // tm,),
        in_specs=[s for _, s in ins], out_specs=[s for _, s in outs], out_shape=[o for o, _ in outs],
        compiler_params=_params(1), name=name,
    )(*[a for a, _ in ins])


def _ffn_ins(P, w, which, layer):
    return [_const(P["ffn_norm%d" % which][layer].reshape(1, -1)), _layer(w["f%dg" % which], layer),
            _layer(w["f%du" % which], layer), _layer(w["f%dd" % which], layer)]


def _ffn_apply(x, g_ref, wg_ref, wu_ref, wd_ref):
    h = _rms(x, g_ref[...]).astype(BF16)
    gate = _dot(h, wg_ref[...])
    up = _dot(h, wu_ref[...])
    a = (gate * jax.nn.sigmoid(gate) * up).astype(BF16)
    return x + FFN_RES * _dot(a, wd_ref[...])


def _mla_proj_apply(x, mg_ref, wdq_ref, qn_ref, wq_ref, wlat_ref, kvn_ref, wkr_ref, tab_ref,
                    refs, heads, c_dim, absorbed):
    cos_t, sin_t = tab_ref[:, 0:LANES], tab_ref[:, LANES:2 * LANES]
    lane = lax.broadcasted_iota(jnp.int32, (1, LANES), 1)
    cos_q = jnp.where(lane < MLA_NOPE, 1.0, cos_t)
    first_half = lane < MLA_NOPE + MLA_ROPE // 2

    def swap_halves(v):
        return jnp.where(first_half, pltpu.roll(v, LANES - MLA_ROPE // 2, 1), pltpu.roll(v, MLA_ROPE // 2, 1))

    h = _rms(x, mg_ref[...]).astype(BF16)
    cq = _rms(_dot(h, wdq_ref[...]), qn_ref[...]).astype(BF16)
    qa = _dot(cq, wq_ref[...])
    lat = _rms(_dot(h, wlat_ref[...]), kvn_ref[...])
    kr = _dot(h, wkr_ref[...])
    kr = kr * cos_t + swap_halves(kr) * sin_t
    latb = lat.astype(BF16)
    if absorbed:
        wukd_ref, q_ref, ql_ref, lat_ref, kr_ref = refs
    else:
        wuk_ref, wuvt_ref, q_ref, k_ref, vt_ref, lat_ref, kr_ref = refs
        kn = _dot(latb, wuk_ref[...])
    lat_ref[...] = lat
    kr_ref[...] = kr[:, MLA_NOPE:MLA_NOPE + MLA_ROPE]
    for hh in range(heads):
        sl = slice(hh * LANES, (hh + 1) * LANES)
        qh = (qa[:, sl] * cos_q + swap_halves(qa[:, sl]) * sin_t).astype(BF16)
        q_ref[:, sl] = qh
        if absorbed:
            ql_ref[:, hh * c_dim:(hh + 1) * c_dim] = _dot(qh, wukd_ref[hh]).astype(BF16)
        else:
            k_ref[:, sl] = (kn[:, sl] + kr).astype(BF16)
    if not absorbed:
        vt = _dot_nt(wuvt_ref[...], latb)
        ones_row = lax.broadcasted_iota(jnp.int32, vt.shape, 0) % MLA_VT_ROWS == MLA_V
        vt_ref[...] = jnp.where(ones_row, 1.0, vt).astype(BF16)


def _rows_a_kernel(x_ref, fg, wg, wu, wd, mg, wdq, qn, wq, wlat, kvn, wkr, tab, *refs,
                   heads, c_dim, absorbed):
    n_extra = 1 if absorbed else 2
    x1_ref = refs[n_extra]
    x1 = _ffn_apply(x_ref[...], fg, wg, wu, wd)
    x1_ref[...] = x1
    _mla_proj_apply(x1, mg, wdq, qn, wq, wlat, kvn, wkr, tab, refs[:n_extra] + refs[n_extra + 1:],
                    heads, c_dim, absorbed)


def _rows_b_kernel(a_ref, *refs, dec_heads, c_dim, tiled_kv):
    if dec_heads:
        wuv_ref, refs = refs[0], refs[1:]
        a = jnp.concatenate([_dot(a_ref[:, hh * c_dim:(hh + 1) * c_dim], wuv_ref[hh]).astype(BF16)
                             for hh in range(dec_heads)], axis=1)
    else:
        a = a_ref[...]
    wo, x_ref, fg, wg, wu, wd, kvg, wkv = refs[:8]
    x = _ffn_apply(x_ref[...] + _dot(a, wo[...]), fg, wg, wu, wd)
    kv = _dot(_rms(x, kvg[...]).astype(BF16), wkv[...])
    if tiled_kv:
        repk, eye, x_out, kv_out, k2_out, vt_out = refs[8:]
        half = kv.shape[1] // 2
        k2_out[...] = _dot(kv[:, :half].astype(BF16), repk[...]).astype(BF16)
        vt_out[...] = _dot_nt(eye[...], kv[:, half:].astype(BF16)).astype(BF16)
    else:
        x_out, kv_out = refs[8:]
    x_out[...] = x
    kv_out[...] = kv


def _rows_c_kernel(x_ref, fg, wg, wu, wd, mg, wq, x_out, q_out, *, scale, transposed):
    x = _ffn_apply(x_ref[...], fg, wg, wu, wd)
    x_out[...] = x
    h = _rms(x, mg[...]).astype(BF16)
    q = _dot_nt(wq[...], h) if transposed else _dot(h, wq[...])
    q_out[...] = (q * scale).astype(BF16)


def _rows_d_kernel(a_ref, wo, x_ref, fg, wg, wu, wd, fin_g, y_out):
    x = _ffn_apply(x_ref[...] + _dot(a_ref[...], wo[...]), fg, wg, wu, wd)
    y_out[...] = _rms(x, fin_g[...])


def _mla_attn_kernel(q_ref, k_ref, vt_ref, o_ref, m_sc, acc_sc, sa_sc, sb_sc, *, bq, bk, hp):
    i = pl.program_id(2)
    qs = [q_ref[:, hh * LANES:(hh + 1) * LANES] for hh in range(hp)]
    sls = [slice(hh * LANES, (hh + 1) * LANES) for hh in range(hp)]

    def block(start, width, q_lo, masked):
        ql = slice(q_lo, bq)
        ss = [_dot_nt(k_ref[pl.ds(start, width), sls[hh]], qs[hh][ql]) for hh in range(hp)]
        if masked:
            kc = (start + lax.broadcasted_iota(jnp.int32, (width, 1), 0)) // CHUNK
            qc = (i * bq + q_lo + lax.broadcasted_iota(jnp.int32, (1, bq - q_lo), 1)) // CHUNK
            ss = [jnp.where(kc <= qc, s, NEG_INF) for s in ss]
        for hh in range(hp):
            m = m_sc[hh, :, ql]
            m_new = jnp.maximum(m, jnp.max(ss[hh], axis=0, keepdims=True))
            alpha = jnp.exp2(m - m_new)
            p = jnp.exp2(ss[hh] - m_new).astype(BF16)
            vt = vt_ref[hh * MLA_VT_ROWS:(hh + 1) * MLA_VT_ROWS, pl.ds(start, width)]
            acc_sc[hh, :, ql] = alpha * acc_sc[hh, :, ql] + _dot(vt, p)
            m_sc[hh, :, ql] = m_new

    def produce(hh, j, dst):
        dst[hh] = _dot_nt(k_ref[pl.ds(pl.multiple_of(j * bk, bk), bk), sls[hh]], qs[hh])

    def consume(hh, j, src):
        s = src[hh]
        m = m_sc[hh]
        m_new = jnp.maximum(m, jnp.max(s, axis=0, keepdims=True))
        alpha = jnp.exp2(m - m_new)
        p = jnp.exp2(s - m_new).astype(BF16)
        vt = vt_ref[hh * MLA_VT_ROWS:(hh + 1) * MLA_VT_ROWS, pl.ds(pl.multiple_of(j * bk, bk), bk)]
        acc_sc[hh] = alpha * acc_sc[hh] + _dot(vt, p)
        m_sc[hh] = m_new

    def transition(j, src, dst):
        produce(0, j + 1, dst)
        for hh in range(hp):
            consume(hh, j, src)
            if hh + 1 < hp:
                produce(hh + 1, j + 1, dst)

    m_sc[...] = jnp.full(m_sc.shape, NEG_INF, F32)
    acc_sc[...] = jnp.zeros(acc_sc.shape, F32)
    n_full = (i * bq) // bk

    @pl.when(n_full > 0)
    def _():
        for hh in range(hp):
            produce(hh, 0, sa_sc)
        last = n_full - 1

        @pl.loop(0, last // 2)
        def _(t):
            transition(2 * t, sa_sc, sb_sc)
            transition(2 * t + 1, sb_sc, sa_sc)

        @pl.when(last % 2 == 1)
        def _():
            transition(last - 1, sa_sc, sb_sc)
            for hh in range(hp):
                consume(hh, last, sb_sc)

        @pl.when(last % 2 == 0)
        def _():
            for hh in range(hp):
                consume(hh, last, sa_sc)

    diag = min(MLA_DIAG, bq)
    for t in range(bq // diag):
        block(pl.multiple_of(i * bq + t * diag, diag), diag, t * diag, True)
    for pair in range(hp // 2):
        halves = []
        for hh in (2 * pair, 2 * pair + 1):
            acc = acc_sc[hh]
            halves.append(acc[:MLA_V] * (1.0 / acc[MLA_V:MLA_V + 1]))
        o_ref[:, pair * LANES:(pair + 1) * LANES] = jnp.concatenate(halves, axis=0).T.astype(BF16)


def _mla_attn_prompt(q, k, vt, batch, seq, heads, bq=MLA_BQ, bk=MLA_BK, hp=MLA_HEADS_PER_STEP):
    bq, bk = min(bq, seq), min(bk, seq)
    nq = seq // bq
    assert seq % bq == 0 and seq % bk == 0 and bq % CHUNK == 0 and heads % hp == 0 and hp % 2 == 0
    assert bq % bk == 0 and bq % min(MLA_DIAG, bq) == 0 and MLA_DIAG % CHUNK == 0
    w = hp * LANES
    return pl.pallas_call(
        functools.partial(_mla_attn_kernel, bq=bq, bk=bk, hp=hp),
        grid=(batch, heads // hp, nq),
        in_specs=[pl.BlockSpec((bq, w), lambda b, h, i: (b * nq + i, h)),
                  pl.BlockSpec((seq, w), lambda b, h, i: (b, h)),
                  pl.BlockSpec((hp * MLA_VT_ROWS, seq), lambda b, h, i: (h, b))],
        out_specs=pl.BlockSpec((bq, hp * MLA_V), lambda b, h, i: (b * nq + i, h)),
        out_shape=jax.ShapeDtypeStruct((q.shape[0], heads * MLA_V), BF16),
        scratch_shapes=[pltpu.VMEM((hp, 1, bq), F32), pltpu.VMEM((hp, MLA_VT_ROWS, bq), F32),
                        pltpu.VMEM((hp, bk, bq), F32), pltpu.VMEM((hp, bk, bq), F32)],
        compiler_params=_params(3), name="mla_attn_prompt",
    )(q, k, vt)


def _mla_attn_decode_kernel(ql_ref, q_ref, cl_ref, crt_ref, nl_ref, nr_ref, o_ref, *, heads, c_dim):
    n = ql_ref.shape[0]
    ql = jnp.concatenate([ql_ref[:, hh * c_dim:(hh + 1) * c_dim] for hh in range(heads)], axis=0)
    qr = jnp.concatenate([q_ref[:, hh * LANES + MLA_NOPE:hh * LANES + MLA_NOPE + MLA_ROPE] for hh in range(heads)],
                         axis=0)
    past = cl_ref.shape[0]
    chunk = min(past, MLA_DECODE_CHUNK)
    parts = [(cl_ref[c:c + chunk, :], crt_ref[:, c:c + chunk], _dot) for c in range(0, past, chunk)]
    parts.append((nl_ref[...], nr_ref[...], _dot_nt))
    m = jnp.full((heads * n, 1), NEG_INF, F32)
    l = jnp.zeros((heads * n, 1), F32)
    acc = jnp.zeros((heads * n, c_dim), F32)
    for kl, kr, rope_dot in parts:
        kl, kr = kl.astype(BF16), kr.astype(BF16)
        s = _dot_nt(ql, kl) + rope_dot(qr, kr)
        m_new = jnp.maximum(m, jnp.max(s, axis=-1, keepdims=True))
        alpha = jnp.exp2(m - m_new)
        p = jnp.exp2(s - m_new)
        l = alpha * l + jnp.sum(p, axis=-1, keepdims=True)
        acc = alpha * acc + _dot(p.astype(BF16), kl)
        m = m_new
    o = acc / l
    for hh in range(heads):
        o_ref[:, hh * c_dim:(hh + 1) * c_dim] = o[hh * n:(hh + 1) * n].astype(BF16)


def _mla_attn_decode(ql, q, cache_lat, cache_rope_t, new_lat, new_rope, heads, n_new):
    _, batch, past, c_dim = cache_lat.shape
    r_dim = cache_rope_t.shape[2]
    return pl.pallas_call(
        functools.partial(_mla_attn_decode_kernel, heads=heads, c_dim=c_dim),
        grid=(batch,),
        in_specs=[pl.BlockSpec((n_new, heads * c_dim), lambda b: (b, 0)),
                  pl.BlockSpec((n_new, heads * LANES), lambda b: (b, 0)),
                  pl.BlockSpec((None, None, past, c_dim), lambda b: (0, b, 0, 0)),
                  pl.BlockSpec((None, None, r_dim, past), lambda b: (0, b, 0, 0)),
                  pl.BlockSpec((n_new, c_dim), lambda b: (b, 0)),
                  pl.BlockSpec((n_new, r_dim), lambda b: (b, 0))],
        out_specs=pl.BlockSpec((n_new, heads * c_dim), lambda b: (b, 0)),
        out_shape=jax.ShapeDtypeStruct(ql.shape, BF16),
        compiler_params=_params(1), name="mla_attn_decode",
    )(ql, q, cache_lat, cache_rope_t, new_lat, new_rope)


def _bias_kernel(tab_ref, idx_ref, o_ref, *, heads):
    idx = idx_ref[...]
    for hh in range(heads):
        acc = jnp.zeros(idx.shape, F32)
        for b in range(N_BUCKETS):
            acc = jnp.where(idx == b, tab_ref[b, hh] * LOG2E, acc)
        o_ref[hh] = acc


def _bias_table(rel_bias, idx):
    heads = rel_bias.shape[1]
    return pl.pallas_call(
        functools.partial(_bias_kernel, heads=heads),
        in_specs=[pl.BlockSpec(memory_space=pltpu.SMEM), pl.BlockSpec(idx.shape, lambda: (0, 0))],
        out_specs=pl.BlockSpec((heads,) + idx.shape, lambda: (0, 0, 0)),
        out_shape=jax.ShapeDtypeStruct((heads,) + idx.shape, F32), name="rel_bias_table",
    )(rel_bias, idx)


def _rel_bucket(rel):
    half = N_BUCKETS // 2
    max_exact = half // 2
    base = jnp.where(rel > 0, half, 0)
    n = jnp.abs(rel)
    nf = jnp.maximum(n, 1).astype(jnp.float32)
    large = max_exact + (jnp.log(nf / max_exact) / math.log(MAX_DISTANCE / max_exact)
                         * (half - max_exact)).astype(jnp.int32)
    large = jnp.minimum(large, half - 1)
    return base + jnp.where(n < max_exact, n, large)


def _swa_prompt_kernel(sink_ref, qt_ref, kp_ref, kc_ref, vtp_ref, vtc_ref, bias_ref, o_ref, *, s_heads, rep, blk):
    i = pl.program_id(1)
    key_chunk = lax.broadcasted_iota(jnp.int32, (2 * blk, 1), 0) // CHUNK
    row_chunk = lax.broadcasted_iota(jnp.int32, (1, blk), 1) // CHUNK
    first = blk // CHUNK
    valid = ((key_chunk >= row_chunk) & (key_chunk <= row_chunk + WINDOW_CHUNKS)
             & ((i > 0) | (key_chunk >= first)))
    dh = SWA_HEAD_DIM
    ss = []
    for hh in range(s_heads):
        g, pair, slot = hh // rep, hh // 2, hh % 2
        ksl = slice(g * 4 * dh + slot * 2 * dh, g * 4 * dh + (slot + 1) * 2 * dh)
        k = jnp.concatenate([kp_ref[:, ksl], kc_ref[:, ksl]], axis=0)
        ss.append(_dot(k, qt_ref[pair * 2 * dh:(pair + 1) * 2 * dh, :]))
    ps, invs = [], []
    for hh, s in enumerate(ss):
        s = jnp.where(valid, s + bias_ref[hh], NEG_INF)
        sink = sink_ref[hh] * LOG2E
        m = jnp.maximum(jnp.max(s, axis=0, keepdims=True), sink)
        p = jnp.exp2(s - m)
        invs.append(1.0 / (jnp.sum(p, axis=0, keepdims=True) + jnp.exp2(sink - m)))
        ps.append(p.astype(BF16))
    outs = []
    for hh, p in enumerate(ps):
        g = hh // rep
        vt = jnp.concatenate([vtp_ref[g * dh:(g + 1) * dh, :], vtc_ref[g * dh:(g + 1) * dh, :]], axis=1)
        outs.append(_dot(vt, p) * invs[hh])
    for pair in range(s_heads // 2):
        o_ref[:, pair * 2 * dh:(pair + 1) * 2 * dh] = (
            jnp.concatenate(outs[2 * pair:2 * pair + 2], axis=0).T.astype(BF16))


def _swa_prompt(qt, k2, vt, bias_t, sinks, batch, seq, s_heads, rep, blk=128):
    nq = seq // blk
    kw = k2.shape[1]
    cur = lambda b, i: (b * nq + i, 0)
    prev = lambda b, i: (b * nq + jnp.maximum(i - 1, 0), 0)
    cur_t = lambda b, i: (0, b * nq + i)
    prev_t = lambda b, i: (0, b * nq + jnp.maximum(i - 1, 0))
    assert rep == 4 and kw == (s_heads // rep) * 4 * SWA_HEAD_DIM
    return pl.pallas_call(
        functools.partial(_swa_prompt_kernel, s_heads=s_heads, rep=rep, blk=blk),
        grid=(batch, nq),
        in_specs=[pl.BlockSpec(memory_space=pltpu.SMEM),
                  pl.BlockSpec((qt.shape[0], blk), cur_t),
                  pl.BlockSpec((blk, kw), prev), pl.BlockSpec((blk, kw), cur),
                  pl.BlockSpec((vt.shape[0], blk), prev_t), pl.BlockSpec((vt.shape[0], blk), cur_t),
                  pl.BlockSpec(bias_t.shape, lambda b, i: (0, 0, 0), pipeline_mode=pl.Buffered(1))],
        out_specs=pl.BlockSpec((blk, qt.shape[0]), cur),
        out_shape=jax.ShapeDtypeStruct((qt.shape[1], qt.shape[0]), BF16),
        compiler_params=_params(2), name="swa_prompt",
    )(sinks, qt, k2, k2, vt, vt, bias_t)


def _swa_decode_kernel(q_ref, ck_ref, cv_ref, nkv_ref, rep_ref, bias_ref, sink_ref, o_ref, *, kv_w, groups, rep):
    n = q_ref.shape[0]
    gw = rep * SWA_HEAD_DIM
    k = jnp.concatenate([ck_ref[...], nkv_ref[:, 0:kv_w]], axis=0).astype(BF16)
    v = jnp.concatenate([cv_ref[...], nkv_ref[:, kv_w:2 * kv_w]], axis=0).astype(BF16)
    k4 = _dot(k, rep_ref[...]).astype(BF16)
    v4 = _dot(v, rep_ref[...]).astype(BF16)
    lane_head = lax.broadcasted_iota(jnp.int32, (1, gw), 1) // SWA_HEAD_DIM
    sls = [slice(g * gw, (g + 1) * gw) for g in range(groups)]
    ss = []
    for g in range(groups):
        qg = q_ref[:, sls[g]]
        qs = jnp.concatenate([jnp.where(lane_head == r, qg, jnp.zeros_like(qg)) for r in range(rep)], axis=0)
        ss.append(_dot_nt(qs, k4[:, sls[g]]))
    ps = []
    for g in range(groups):
        rows = slice(g * rep * n, (g + 1) * rep * n)
        s = ss[g] + bias_ref[rows, :]
        sink = sink_ref[rows, :] * LOG2E
        m = jnp.maximum(jnp.max(s, axis=-1, keepdims=True), sink)
        p = jnp.exp2(s - m)
        inv = 1.0 / (jnp.sum(p, axis=-1, keepdims=True) + jnp.exp2(sink - m))
        ps.append((p * inv).astype(BF16))
    for g in range(groups):
        res = _dot(ps[g], v4[:, sls[g]])
        og = jnp.zeros((n, gw), F32)
        for r in range(rep):
            og = jnp.where(lane_head == r, res[r * n:(r + 1) * n], og)
        o_ref[:, sls[g]] = og.astype(BF16)


def _rope_table(pos):
    inv = ROPE_BASE ** (-jnp.arange(0, MLA_ROPE, 2, dtype=jnp.float32) / MLA_ROPE)
    ang = pos.astype(jnp.float32)[:, None] * inv[None, :]
    cos, sin = jnp.cos(ang), jnp.sin(ang)
    widths = ((0, 0), (MLA_NOPE, LANES - MLA_NOPE - MLA_ROPE))
    return jnp.concatenate([jnp.pad(jnp.concatenate([cos, cos], axis=1), widths),
                            jnp.pad(jnp.concatenate([-sin, sin], axis=1), widths)], axis=1)


def _prep_weights(mla_w_dq, mla_w_uq, mla_w_dkv, mla_w_uk, mla_w_uv, mla_w_o, w_kv_shared, swa_w_q, swa_w_o, scale):
    ql, qcols = mla_w_uq.shape
    c_dim, heads, nope = mla_w_uk.shape
    rope = qcols // heads - nope
    d = mla_w_dq.shape[0]
    pad = LANES - nope - rope
    w = {}
    w["wq"] = jnp.pad(mla_w_uq.reshape(ql, heads, nope + rope), ((0, 0), (0, 0), (0, pad))).reshape(
        ql, heads * LANES) * scale
    w["wdq"] = mla_w_dq
    w["wlat"] = mla_w_dkv[:, :c_dim]
    w["wkr"] = jnp.pad(mla_w_dkv[:, c_dim:], ((0, 0), (nope, pad)))
    zc = jnp.zeros((c_dim, heads, LANES - nope), F32)
    w["wuk"] = jnp.concatenate([mla_w_uk, zc], axis=-1).reshape(c_dim, heads * LANES)
    w["wuv_t"] = jnp.pad(mla_w_uv, ((0, 0), (0, 0), (0, MLA_VT_ROWS - mla_w_uv.shape[2]))).reshape(
        c_dim, heads * MLA_VT_ROWS).T
    w["wuk_dec"] = jnp.concatenate([jnp.transpose(mla_w_uk, (1, 2, 0)),
                                    jnp.zeros((heads, LANES - nope, c_dim), F32)], axis=1)
    w["wuv_dec"] = jnp.concatenate([jnp.transpose(mla_w_uv, (1, 0, 2)),
                                    jnp.zeros((heads, c_dim, LANES - nope), F32)], axis=-1)
    vdim = mla_w_uv.shape[2]
    assert vdim == MLA_V and nope == MLA_NOPE and rope == MLA_ROPE
    w["wo"] = mla_w_o
    w["wo_pad"] = jnp.concatenate([mla_w_o.reshape(heads, vdim, d), jnp.zeros((heads, LANES - vdim, d), F32)],
                                  axis=1).reshape(heads * LANES, d)
    kvw = w_kv_shared.shape[1] // 2
    groups = kvw // SWA_HEAD_DIM
    s_heads = swa_w_q.shape[1] // SWA_HEAD_DIM
    rep = s_heads // groups
    w["wkv"] = w_kv_shared
    eye = jnp.eye(SWA_HEAD_DIM, dtype=F32)
    zero = jnp.zeros_like(eye)
    w["rep_k2"] = jnp.kron(jnp.eye(groups, dtype=F32), jnp.concatenate([eye, zero, zero, eye], axis=1))
    w["eye_kv"] = jnp.eye(kvw, dtype=F32)
    w["swa_wq_t"] = swa_w_q.T
    w["swa_wo"] = swa_w_o
    w["rep_kv"] = jnp.kron(jnp.eye(groups, dtype=F32), jnp.tile(eye, (1, rep)))
    w["swa_wq"] = swa_w_q
    return {k: v.astype(BF16) for k, v in w.items()}, dict(heads=heads, c_dim=c_dim, groups=groups, rep=rep,
                                                            s_heads=s_heads, kvw=kvw)


def _rows_a(x, P, w, tab, heads, c_dim, absorbed):
    t, d = x.shape
    tm = _row_tile(t)
    hw = heads * LANES
    row = lambda v: v.reshape(1, -1)
    ins = ([_rows(x, tm)] + _ffn_ins(P, w, 1, 0)
           + [_const(row(P["mix_norm"][0])), _const(w["wdq"]), _const(row(P["mla_q_norm"][0])), _const(w["wq"]),
              _const(w["wlat"]), _const(row(P["mla_kv_norm"][0])), _const(w["wkr"]), _cycle(tab, tm)]
           + ([_const(w["wuk_dec"])] if absorbed else [_const(w["wuk"]), _const(w["wuv_t"])]))
    outs = [_rows_out(t, tm, d, F32), _rows_out(t, tm, hw, BF16)]
    if absorbed:
        outs.append(_rows_out(t, tm, heads * c_dim, BF16))
    else:
        outs += [_rows_out(t, tm, hw, BF16),
                 (jax.ShapeDtypeStruct((heads * MLA_VT_ROWS, t), BF16),
                  pl.BlockSpec((heads * MLA_VT_ROWS, tm), lambda i: (0, i)))]
    outs += [_rows_out(t, tm, c_dim, F32), _rows_out(t, tm, MLA_ROPE, F32)]
    return _rows_call(functools.partial(_rows_a_kernel, heads=heads, c_dim=c_dim, absorbed=absorbed),
                      "rows_a_absorbed" if absorbed else "rows_a", t, tm, ins, outs)


def _rows_b(a, x, P, w, meta, decode):
    t, d = x.shape
    tm = _row_tile(t, ROW_TILE_LIGHT)
    kvw = meta["kvw"]
    ins = [_rows(a, tm)]
    if decode:
        ins += [_const(w["wuv_dec"]), _const(w["wo_pad"])]
    else:
        ins += [_const(w["wo"])]
    ins += [_rows(x, tm)] + _ffn_ins(P, w, 2, 0) + [_const(P["kv_norm"].reshape(1, -1)), _const(w["wkv"])]
    outs = [_rows_out(t, tm, d, F32), _rows_out(t, tm, 2 * kvw, F32)]
    if not decode:
        ins += [_const(w["rep_k2"]), _const(w["eye_kv"])]
        outs += [_rows_out(t, tm, w["rep_k2"].shape[1], BF16),
                 (jax.ShapeDtypeStruct((kvw, t), BF16), pl.BlockSpec((kvw, tm), lambda i: (0, i)))]
    return _rows_call(functools.partial(_rows_b_kernel, dec_heads=meta["heads"] if decode else 0,
                                        c_dim=meta["c_dim"], tiled_kv=not decode),
                      "rows_b_decode" if decode else "rows_b", t, tm, ins, outs)


def _rows_c(x, P, w, wq, name, transposed):
    t, d = x.shape
    tm = _row_tile(t, ROW_TILE_LIGHT)
    ins = [_rows(x, tm)] + _ffn_ins(P, w, 1, 1) + [_const(P["mix_norm"][1].reshape(1, -1)), _const(wq)]
    if transposed:
        q_out = (jax.ShapeDtypeStruct((wq.shape[0], t), BF16), pl.BlockSpec((wq.shape[0], tm), lambda i: (0, i)))
    else:
        q_out = _rows_out(t, tm, wq.shape[1], BF16)
    return _rows_call(functools.partial(_rows_c_kernel, scale=SWA_HEAD_DIM ** -0.5 * LOG2E, transposed=transposed),
                      name, t, tm, ins, [_rows_out(t, tm, d, F32), q_out])


def _rows_d(a, x, P, w, wo, name):
    t, d = x.shape
    tm = _row_tile(t, ROW_TILE_LIGHT)
    ins = ([_rows(a, tm), _const(wo), _rows(x, tm)] + _ffn_ins(P, w, 2, 1)
           + [_const(P["final_norm"].reshape(1, -1))])
    return _rows_call(_rows_d_kernel, name, t, tm, ins, [_rows_out(t, tm, d, F32)])[0]


def _trunk_prompt(x3, P, w, meta):
    batch, seq, d = x3.shape
    heads, c_dim, kvw = meta["heads"], meta["c_dim"], meta["kvw"]
    x = x3.reshape(batch * seq, d)
    x, q, k, vt, lat, kr = _rows_a(x, P, w, _rope_table(jnp.arange(seq)), heads, c_dim, absorbed=False)
    o = _mla_attn_prompt(q, k, vt, batch, seq, heads)
    x, kv, k2, v_t = _rows_b(o, x, P, w, meta, decode=False)
    x, q_t = _rows_c(x, P, w, w["swa_wq_t"], "rows_c", transposed=True)
    blk = 2 * CHUNK
    rel = (jnp.arange(2 * blk) - blk)[:, None] - jnp.arange(blk)[None, :]
    bias_t = _bias_table(P["rel_bias"], _rel_bucket(rel).astype(jnp.int32))
    o = _swa_prompt(q_t, k2, v_t, bias_t, P["swa_sinks"][0], batch, seq, meta["s_heads"], meta["rep"], blk)
    y = _rows_d(o, x, P, w, w["swa_wo"], "rows_d")
    keep = min(WINDOW, seq)
    kv3 = kv.reshape(batch, seq, 2 * kvw)[:, seq - keep:]
    new_k = kv3[:, :, :kvw].reshape(batch, keep, meta["groups"], SWA_HEAD_DIM)
    new_v = kv3[:, :, kvw:].reshape(batch, keep, meta["groups"], SWA_HEAD_DIM)
    return (y.reshape(batch, seq, d), lat.reshape(1, batch, seq, c_dim), kr.reshape(1, batch, seq, MLA_ROPE),
            new_k, new_v)


def _trunk_decode(x3, cache_lat, cache_rope, cache_k, cache_v, P, w, meta):
    batch, n_new, d = x3.shape
    heads, c_dim, kvw, s_heads = meta["heads"], meta["c_dim"], meta["kvw"], meta["s_heads"]
    past = cache_lat.shape[2]
    w_c = cache_k.shape[1]
    qpos = past + jnp.arange(n_new)
    assert past % CHUNK == 0 and n_new <= CHUNK and w_c <= WINDOW_CHUNKS * CHUNK and w_c <= past
    t = batch * n_new
    x = x3.reshape(t, d)
    tm = _row_tile(t)
    assert tm % n_new == 0
    tab = jnp.tile(_rope_table(qpos), (tm // n_new, 1))
    x, q, ql, lat, kr = _rows_a(x, P, w, tab, heads, c_dim, absorbed=True)
    ol = _mla_attn_decode(ql, q, cache_lat, jnp.swapaxes(cache_rope, 2, 3), lat, kr, heads, n_new)
    x, kv = _rows_b(ol, x, P, w, meta, decode=True)
    x, qs = _rows_c(x, P, w, w["swa_wq"], "rows_c_decode", transposed=False)
    kpos = jnp.arange(past - w_c, past + n_new)
    bias_h = _bias_table(P["rel_bias"], _rel_bucket(kpos[None, :] - qpos[:, None]).astype(jnp.int32))
    bias = bias_h.reshape(s_heads * n_new, w_c + n_new)
    sink_rows = jnp.repeat(P["swa_sinks"][0], n_new).reshape(s_heads * n_new, 1)
    const2 = lambda a: pl.BlockSpec(a.shape, lambda b: (0, 0), pipeline_mode=pl.Buffered(1))
    o = pl.pallas_call(
        functools.partial(_swa_decode_kernel, kv_w=kvw, groups=meta["groups"], rep=meta["rep"]),
        grid=(batch,),
        in_specs=[pl.BlockSpec((n_new, qs.shape[1]), lambda b: (b, 0)),
                  pl.BlockSpec((None, w_c, kvw), lambda b: (b, 0, 0)),
                  pl.BlockSpec((None, w_c, kvw), lambda b: (b, 0, 0)),
                  pl.BlockSpec((n_new, 2 * kvw), lambda b: (b, 0)),
                  const2(w["rep_kv"]), const2(bias), const2(sink_rows)],
        out_specs=pl.BlockSpec((n_new, qs.shape[1]), lambda b: (b, 0)),
        out_shape=jax.ShapeDtypeStruct(qs.shape, BF16),
        compiler_params=_params(1), name="swa_decode",
    )(qs, cache_k.reshape(batch, w_c, kvw), cache_v.reshape(batch, w_c, kvw), kv, w["rep_kv"], bias, sink_rows)
    y = _rows_d(o, x, P, w, w["swa_wo"], "rows_d_decode")
    kv3 = kv.reshape(batch, n_new, 2 * kvw)
    new_k = kv3[:, :, :kvw].reshape(batch, n_new, meta["groups"], SWA_HEAD_DIM)
    new_v = kv3[:, :, kvw:].reshape(batch, n_new, meta["groups"], SWA_HEAD_DIM)
    return (y.reshape(batch, n_new, d), lat.reshape(1, batch, n_new, c_dim), kr.reshape(1, batch, n_new, MLA_ROPE),
            new_k, new_v)


def kernel(x_prompt, x_sample, cache_mla_latent, cache_mla_krope, cache_swa_k, cache_swa_v, ffn_norm1, ffn1_w_gate, ffn1_w_up, ffn1_w_down, mix_norm, ffn_norm2, ffn2_w_gate, ffn2_w_up, ffn2_w_down, mla_w_dq, mla_q_norm, mla_w_uq, mla_w_dkv, mla_kv_norm, mla_w_uk, mla_w_uv, mla_w_o, kv_norm, w_kv_shared, swa_w_q, swa_sinks, swa_w_o, rel_bias, final_norm):
    assert ffn_norm1.shape[0] == 2 and mla_w_dq.shape[0] == 1 and swa_w_q.shape[0] == 1
    scale = (MLA_NOPE + MLA_ROPE) ** -0.5 * LOG2E
    w, meta = _prep_weights(mla_w_dq[0], mla_w_uq[0], mla_w_dkv[0], mla_w_uk[0], mla_w_uv[0], mla_w_o[0],
                            w_kv_shared, swa_w_q[0], swa_w_o[0], scale)
    w.update(f1g=ffn1_w_gate.astype(BF16), f1u=ffn1_w_up.astype(BF16), f1d=ffn1_w_down.astype(BF16),
             f2g=ffn2_w_gate.astype(BF16), f2u=ffn2_w_up.astype(BF16), f2d=ffn2_w_down.astype(BF16))
    P = dict(ffn_norm1=ffn_norm1, mix_norm=mix_norm, ffn_norm2=ffn_norm2, mla_q_norm=mla_q_norm,
             mla_kv_norm=mla_kv_norm, kv_norm=kv_norm, swa_sinks=swa_sinks, rel_bias=rel_bias, final_norm=final_norm)
    y_p, lat_p, rope_p, k_p, v_p = _trunk_prompt(x_prompt, P, w, meta)
    y_s, lat_s, rope_s, k_s, v_s = _trunk_decode(x_sample, cache_mla_latent, cache_mla_krope,
                                                 cache_swa_k, cache_swa_v, P, w, meta)
    return (y_p, y_s, lat_p, rope_p, k_p, v_p, lat_s, rope_s, k_s, v_s)
```

```python
import functools
import math

import jax
import jax.numpy as jnp
from jax import lax
from jax.experimental import pallas as pl
from jax.experimental.pallas import tpu as pltpu

F32 = jnp.float32
BF16 = jnp.bfloat16

CHUNK = 64
RMS_EPS = 1e-6
FFN_RES = 0.5
ROPE_BASE = 10000.0
WINDOW = 128
WINDOW_CHUNKS = WINDOW // CHUNK
N_BUCKETS = 32
MAX_DISTANCE = 128
NEG_INF = -1e30
LOG2E = math.log2(math.e)
MLA_NOPE = 64
MLA_ROPE = 32
MLA_V = 64
MLA_VT_ROWS = 80
SWA_HEAD_DIM = 64

LANES = 128
ROW_TILE = 512
ROW_TILE_LIGHT = 512
MLA_BQ = 512
MLA_BK = 512
MLA_DIAG = 256
MLA_DECODE_CHUNK = 1024
MLA_HEADS_PER_STEP = 4
VMEM_LIMIT = 60 * 1024 * 1024


def _params(n_axes, vmem=VMEM_LIMIT):
    return pltpu.CompilerParams(dimension_semantics=("arbitrary",) * n_axes, vmem_limit_bytes=vmem)


def _rms(xf, g):
    return xf * lax.rsqrt(jnp.mean(xf * xf, axis=-1, keepdims=True) + RMS_EPS) * g


def _dot(a, b):
    return jnp.dot(a, b, preferred_element_type=F32)


def _dot_nt(a, b):
    return lax.dot_general(a, b, (((1,), (1,)), ((), ())), preferred_element_type=F32)


def _row_tile(t, tile=None):
    tm = min(tile or ROW_TILE, t)
    assert t % tm == 0, (t, tm)
    return tm


def _rows(a, tm):
    return a, pl.BlockSpec((tm, a.shape[1]), lambda i: (i, 0))


def _const(a):
    nd = a.ndim
    return a, pl.BlockSpec(a.shape, lambda i: (0,) * nd, pipeline_mode=pl.Buffered(1))


def _layer(a, layer):
    nd = a.ndim - 1
    return a, pl.BlockSpec((None,) + tuple(a.shape[1:]), lambda i: (layer,) + (0,) * nd,
                           pipeline_mode=pl.Buffered(1))


def _cycle(a, tm):
    assert a.shape[0] % tm == 0
    nb = a.shape[0] // tm
    return a, pl.BlockSpec((tm, a.shape[1]), lambda i: (i % nb, 0))


def _rows_out(t, tm, ncols, dtype):
    return jax.ShapeDtypeStruct((t, ncols), dtype), pl.BlockSpec((tm, ncols), lambda i: (i, 0))


def _rows_call(body, name, t, tm, ins, outs):
    return pl.pallas_call(
        body, grid=(t // tm,),
        in_specs=[s for _, s in ins], out_specs=[s for _, s in outs], out_shape=[o for o, _ in outs],
        compiler_params=_params(1), name=name,
    )(*[a for a, _ in ins])


def _ffn_ins(P, w, which, layer):
    return [_const(P["ffn_norm%d" % which][layer].reshape(1, -1)), _layer(w["f%dg" % which], layer),
            _layer(w["f%du" % which], layer), _layer(w["f%dd" % which], layer)]


def _ffn_apply(x, g_ref, wg_ref, wu_ref, wd_ref):
    h = _rms(x, g_ref[...]).astype(BF16)
    gate = _dot(h, wg_ref[...])
    up = _dot(h, wu_ref[...])
    a = (gate * jax.nn.sigmoid(gate) * up).astype(BF16)
    return x + FFN_RES * _dot(a, wd_ref[...])


def _mla_proj_apply(x, mg_ref, wdq_ref, qn_ref, wq_ref, wlat_ref, kvn_ref, wkr_ref, tab_ref,
                    refs, heads, c_dim, absorbed):
    cos_t, sin_t = tab_ref[:, 0:LANES], tab_ref[:, LANES:2 * LANES]
    lane = lax.broadcasted_iota(jnp.int32, (1, LANES), 1)
    cos_q = jnp.where(lane < MLA_NOPE, 1.0, cos_t)
    first_half = lane < MLA_NOPE + MLA_ROPE // 2

    def swap_halves(v):
        return jnp.where(first_half, pltpu.roll(v, LANES - MLA_ROPE // 2, 1), pltpu.roll(v, MLA_ROPE // 2, 1))

    h = _rms(x, mg_ref[...]).astype(BF16)
    cq = _rms(_dot(h, wdq_ref[...]), qn_ref[...]).astype(BF16)
    qa = _dot(cq, wq_ref[...])
    lat = _rms(_dot(h, wlat_ref[...]), kvn_ref[...])
    kr = _dot(h, wkr_ref[...])
    kr = kr * cos_t + swap_halves(kr) * sin_t
    latb = lat.astype(BF16)
    if absorbed:
        wukd_ref, q_ref, ql_ref, lat_ref, kr_ref = refs
    else:
        wuk_ref, wuvt_ref, q_ref, k_ref, vt_ref, lat_ref, kr_ref = refs
        kn = _dot(latb, wuk_ref[...])
    lat_ref[...] = lat
    kr_ref[...] = kr[:, MLA_NOPE:MLA_NOPE + MLA_ROPE]
    for hh in range(heads):
        sl = slice(hh * LANES, (hh + 1) * LANES)
        qh = (qa[:, sl] * cos_q + swap_halves(qa[:, sl]) * sin_t).astype(BF16)
        q_ref[:, sl] = qh
        if absorbed:
            ql_ref[:, hh * c_dim:(hh + 1) * c_dim] = _dot(qh, wukd_ref[hh]).astype(BF16)
        else:
            k_ref[:, sl] = (kn[:, sl] + kr).astype(BF16)
    if not absorbed:
        vt = _dot_nt(wuvt_ref[...], latb)
        ones_row = lax.broadcasted_iota(jnp.int32, vt.shape, 0) % MLA_VT_ROWS == MLA_V
        vt_ref[...] = jnp.where(ones_row, 1.0, vt).astype(BF16)


def _rows_a_kernel(x_ref, fg, wg, wu, wd, mg, wdq, qn, wq, wlat, kvn, wkr, tab, *refs,
                   heads, c_dim, absorbed):
    n_extra = 1 if absorbed else 2
    x1_ref = refs[n_extra]
    x1 = _ffn_apply(x_ref[...], fg, wg, wu, wd)
    x1_ref[...] = x1
    _mla_proj_apply(x1, mg, wdq, qn, wq, wlat, kvn, wkr, tab, refs[:n_extra] + refs[n_extra + 1:],
                    heads, c_dim, absorbed)


def _rows_b_kernel(a_ref, *refs, dec_heads, c_dim, tiled_kv):
    if dec_heads:
        wuv_ref, refs = refs[0], refs[1:]
        a = jnp.concatenate([_dot(a_ref[:, hh * c_dim:(hh + 1) * c_dim], wuv_ref[hh]).astype(BF16)
                             for hh in range(dec_heads)], axis=1)
    else:
        a = a_ref[...]
    wo, x_ref, fg, wg, wu, wd, kvg, wkv = refs[:8]
    x = _ffn_apply(x_ref[...] + _dot(a, wo[...]), fg, wg, wu, wd)
    kv = _dot(_rms(x, kvg[...]).astype(BF16), wkv[...])
    if tiled_kv:
        repk, eye, x_out, kv_out, k2_out, vt_out = refs[8:]
        half = kv.shape[1] // 2
        k2_out[...] = _dot(kv[:, :half].astype(BF16), repk[...]).astype(BF16)
        vt_out[...] = _dot_nt(eye[...], kv[:, half:].astype(BF16)).astype(BF16)
    else:
        x_out, kv_out = refs[8:]
    x_out[...] = x
    kv_out[...] = kv


def _rows_c_kernel(x_ref, fg, wg, wu, wd, mg, wq, x_out, q_out, *, scale, transposed):
    x = _ffn_apply(x_ref[...], fg, wg, wu, wd)
    x_out[...] = x
    h = _rms(x, mg[...]).astype(BF16)
    q = _dot_nt(wq[...], h) if transposed else _dot(h, wq[...])
    q_out[...] = (q * scale).astype(BF16)


def _rows_d_kernel(a_ref, wo, x_ref, fg, wg, wu, wd, fin_g, y_out):
    x = _ffn_apply(x_ref[...] + _dot(a_ref[...], wo[...]), fg, wg, wu, wd)
    y_out[...] = _rms(x, fin_g[...])


def _mla_attn_kernel(q_ref, k_ref, vt_ref, o_ref, m_sc, acc_sc, sa_sc, sb_sc, *, bq, bk, hp):
    i = pl.program_id(2)
    qs = [q_ref[:, hh * LANES:(hh + 1) * LANES] for hh in range(hp)]
    sls = [slice(hh * LANES, (hh + 1) * LANES) for hh in range(hp)]

    def block(start, width, q_lo, masked, after_head=None):
        ql = slice(q_lo, bq)
        ss = [_dot_nt(k_ref[pl.ds(start, width), sls[hh]], qs[hh][ql]) for hh in range(hp)]
        if masked:
            kc = (start + lax.broadcasted_iota(jnp.int32, (width, 1), 0)) // CHUNK
            qc = (i * bq + q_lo + lax.broadcasted_iota(jnp.int32, (1, bq - q_lo), 1)) // CHUNK
            ss = [jnp.where(kc <= qc, s, NEG_INF) for s in ss]
        for hh in range(hp):
            m = m_sc[hh, :, ql]
            m_new = jnp.maximum(m, jnp.max(ss[hh], axis=0, keepdims=True))
            alpha = jnp.exp2(m - m_new)
            p = jnp.exp2(ss[hh] - m_new).astype(BF16)
            vt = vt_ref[hh * MLA_VT_ROWS:(hh + 1) * MLA_VT_ROWS, pl.ds(start, width)]
            acc_sc[hh, :, ql] = alpha * acc_sc[hh, :, ql] + _dot(vt, p)
            m_sc[hh, :, ql] = m_new
            if after_head is not None:
                after_head(hh)

    def produce(hh, j, dst):
        dst[hh] = _dot_nt(k_ref[pl.ds(pl.multiple_of(j * bk, bk), bk), sls[hh]], qs[hh])

    def consume(hh, j, src):
        s = src[hh]
        m = m_sc[hh]
        m_new = jnp.maximum(m, jnp.max(s, axis=0, keepdims=True))
        alpha = jnp.exp2(m - m_new)
        p = jnp.exp2(s - m_new).astype(BF16)
        vt = vt_ref[hh * MLA_VT_ROWS:(hh + 1) * MLA_VT_ROWS, pl.ds(pl.multiple_of(j * bk, bk), bk)]
        acc_sc[hh] = alpha * acc_sc[hh] + _dot(vt, p)
        m_sc[hh] = m_new

    def transition(j, src, dst):
        produce(0, j + 1, dst)
        for hh in range(hp):
            consume(hh, j, src)
            if hh + 1 < hp:
                produce(hh + 1, j + 1, dst)

    m_sc[...] = jnp.full(m_sc.shape, NEG_INF, F32)
    acc_sc[...] = jnp.zeros(acc_sc.shape, F32)
    n_full = (i * bq) // bk

    diag = min(MLA_DIAG, bq)
    for t in range(bq // diag):
        block(pl.multiple_of(i * bq + t * diag, diag), diag, t * diag, True,
              after_head=(lambda hh: produce(hh, 0, sa_sc)) if t == 0 else None)

    @pl.when(n_full > 0)
    def _():
        last = n_full - 1

        @pl.loop(0, last // 2)
        def _(t):
            transition(2 * t, sa_sc, sb_sc)
            transition(2 * t + 1, sb_sc, sa_sc)

        @pl.when(last % 2 == 1)
        def _():
            transition(last - 1, sa_sc, sb_sc)
            for hh in range(hp):
                consume(hh, last, sb_sc)

        @pl.when(last % 2 == 0)
        def _():
            for hh in range(hp):
                consume(hh, last, sa_sc)

    for pair in range(hp // 2):
        halves = []
        for hh in (2 * pair, 2 * pair + 1):
            acc = acc_sc[hh]
            halves.append(acc[:MLA_V] * (1.0 / acc[MLA_V:MLA_V + 1]))
        o_ref[:, pair * LANES:(pair + 1) * LANES] = jnp.concatenate(halves, axis=0).T.astype(BF16)


def _mla_attn_prompt(q, k, vt, batch, seq, heads, bq=MLA_BQ, bk=MLA_BK, hp=MLA_HEADS_PER_STEP):
    bq, bk = min(bq, seq), min(bk, seq)
    nq = seq // bq
    assert seq % bq == 0 and seq % bk == 0 and bq % CHUNK == 0 and heads % hp == 0 and hp % 2 == 0
    assert bq % bk == 0 and bq % min(MLA_DIAG, bq) == 0 and MLA_DIAG % CHUNK == 0
    w = hp * LANES
    return pl.pallas_call(
        functools.partial(_mla_attn_kernel, bq=bq, bk=bk, hp=hp),
        grid=(batch, heads // hp, nq),
        in_specs=[pl.BlockSpec((bq, w), lambda b, h, i: (b * nq + i, h)),
                  pl.BlockSpec((seq, w), lambda b, h, i: (b, h)),
                  pl.BlockSpec((hp * MLA_VT_ROWS, seq), lambda b, h, i: (h, b))],
        out_specs=pl.BlockSpec((bq, hp * MLA_V), lambda b, h, i: (b * nq + i, h)),
        out_shape=jax.ShapeDtypeStruct((q.shape[0], heads * MLA_V), BF16),
        scratch_shapes=[pltpu.VMEM((hp, 1, bq), F32), pltpu.VMEM((hp, MLA_VT_ROWS, bq), F32),
                        pltpu.VMEM((hp, bk, bq), F32), pltpu.VMEM((hp, bk, bq), F32)],
        compiler_params=_params(3), name="mla_attn_prompt",
    )(q, k, vt)


def _mla_attn_decode_kernel(ql_ref, q_ref, cl_ref, crt_ref, nl_ref, nr_ref, o_ref, *, heads, c_dim):
    n = ql_ref.shape[0]
    ql = jnp.concatenate([ql_ref[:, hh * c_dim:(hh + 1) * c_dim] for hh in range(heads)], axis=0)
    qr = jnp.concatenate([q_ref[:, hh * LANES + MLA_NOPE:hh * LANES + MLA_NOPE + MLA_ROPE] for hh in range(heads)],
                         axis=0)
    past = cl_ref.shape[0]
    chunk = min(past, MLA_DECODE_CHUNK)
    parts = [(cl_ref[c:c + chunk, :], crt_ref[:, c:c + chunk], _dot) for c in range(0, past, chunk)]
    parts.append((nl_ref[...], nr_ref[...], _dot_nt))
    m = jnp.full((heads * n, 1), NEG_INF, F32)
    l = jnp.zeros((heads * n, 1), F32)
    acc = jnp.zeros((heads * n, c_dim), F32)
    def scores(part):
        kl, kr, rope_dot = part
        kl, kr = kl.astype(BF16), kr.astype(BF16)
        return kl, _dot_nt(ql, kl) + rope_dot(qr, kr)

    nxt = scores(parts[0])
    for c in range(len(parts)):
        kl, s = nxt
        if c + 1 < len(parts):
            nxt = scores(parts[c + 1])
        m_new = jnp.maximum(m, jnp.max(s, axis=-1, keepdims=True))
        alpha = jnp.exp2(m - m_new)
        p = jnp.exp2(s - m_new)
        l = alpha * l + jnp.sum(p, axis=-1, keepdims=True)
        acc = alpha * acc + _dot(p.astype(BF16), kl)
        m = m_new
    o = acc / l
    for hh in range(heads):
        o_ref[:, hh * c_dim:(hh + 1) * c_dim] = o[hh * n:(hh + 1) * n].astype(BF16)


def _mla_attn_decode(ql, q, cache_lat, cache_rope_t, new_lat, new_rope, heads, n_new):
    _, batch, past, c_dim = cache_lat.shape
    r_dim = cache_rope_t.shape[2]
    return pl.pallas_call(
        functools.partial(_mla_attn_decode_kernel, heads=heads, c_dim=c_dim),
        grid=(batch,),
        in_specs=[pl.BlockSpec((n_new, heads * c_dim), lambda b: (b, 0)),
                  pl.BlockSpec((n_new, heads * LANES), lambda b: (b, 0)),
                  pl.BlockSpec((None, None, past, c_dim), lambda b: (0, b, 0, 0)),
                  pl.BlockSpec((None, None, r_dim, past), lambda b: (0, b, 0, 0)),
                  pl.BlockSpec((n_new, c_dim), lambda b: (b, 0)),
                  pl.BlockSpec((n_new, r_dim), lambda b: (b, 0))],
        out_specs=pl.BlockSpec((n_new, heads * c_dim), lambda b: (b, 0)),
        out_shape=jax.ShapeDtypeStruct(ql.shape, BF16),
        compiler_params=_params(1), name="mla_attn_decode",
    )(ql, q, cache_lat, cache_rope_t, new_lat, new_rope)


def _bias_kernel(tab_ref, idx_ref, o_ref, *, heads):
    idx = idx_ref[...]
    for hh in range(heads):
        acc = jnp.zeros(idx.shape, F32)
        for b in range(N_BUCKETS):
            acc = jnp.where(idx == b, tab_ref[b, hh] * LOG2E, acc)
        o_ref[hh] = acc


def _bias_table(rel_bias, idx):
    heads = rel_bias.shape[1]
    return pl.pallas_call(
        functools.partial(_bias_kernel, heads=heads),
        in_specs=[pl.BlockSpec(memory_space=pltpu.SMEM), pl.BlockSpec(idx.shape, lambda: (0, 0))],
        out_specs=pl.BlockSpec((heads,) + idx.shape, lambda: (0, 0, 0)),
        out_shape=jax.ShapeDtypeStruct((heads,) + idx.shape, F32), name="rel_bias_table",
    )(rel_bias, idx)


def _rel_bucket(rel):
    half = N_BUCKETS // 2
    max_exact = half // 2
    base = jnp.where(rel > 0, half, 0)
    n = jnp.abs(rel)
    nf = jnp.maximum(n, 1).astype(jnp.float32)
    large = max_exact + (jnp.log(nf / max_exact) / math.log(MAX_DISTANCE / max_exact)
                         * (half - max_exact)).astype(jnp.int32)
    large = jnp.minimum(large, half - 1)
    return base + jnp.where(n < max_exact, n, large)


def _swa_prompt_kernel(sink_ref, qt_ref, kp_ref, kc_ref, vtp_ref, vtc_ref, bias_ref, o_ref, *, s_heads, rep, blk):
    i = pl.program_id(1)
    key_chunk = lax.broadcasted_iota(jnp.int32, (2 * blk, 1), 0) // CHUNK
    row_chunk = lax.broadcasted_iota(jnp.int32, (1, blk), 1) // CHUNK
    first = blk // CHUNK
    valid = ((key_chunk >= row_chunk) & (key_chunk <= row_chunk + WINDOW_CHUNKS)
             & ((i > 0) | (key_chunk >= first)))
    dh = SWA_HEAD_DIM
    ss = []
    for hh in range(s_heads):
        g, pair, slot = hh // rep, hh // 2, hh % 2
        ksl = slice(g * 4 * dh + slot * 2 * dh, g * 4 * dh + (slot + 1) * 2 * dh)
        k = jnp.concatenate([kp_ref[:, ksl], kc_ref[:, ksl]], axis=0)
        ss.append(_dot(k, qt_ref[pair * 2 * dh:(pair + 1) * 2 * dh, :]))
    ps, invs = [], []
    for hh, s in enumerate(ss):
        s = jnp.where(valid, s + bias_ref[hh], NEG_INF)
        sink = sink_ref[hh] * LOG2E
        m = jnp.maximum(jnp.max(s, axis=0, keepdims=True), sink)
        p = jnp.exp2(s - m)
        invs.append(1.0 / (jnp.sum(p, axis=0, keepdims=True) + jnp.exp2(sink - m)))
        ps.append(p.astype(BF16))
    outs = []
    for hh, p in enumerate(ps):
        g = hh // rep
        vt = jnp.concatenate([vtp_ref[g * dh:(g + 1) * dh, :], vtc_ref[g * dh:(g + 1) * dh, :]], axis=1)
        outs.append(_dot(vt, p) * invs[hh])
    for pair in range(s_heads // 2):
        o_ref[:, pair * 2 * dh:(pair + 1) * 2 * dh] = (
            jnp.concatenate(outs[2 * pair:2 * pair + 2], axis=0).T.astype(BF16))


def _swa_prompt(qt, k2, vt, bias_t, sinks, batch, seq, s_heads, rep, blk=128):
    nq = seq // blk
    kw = k2.shape[1]
    cur = lambda b, i: (b * nq + i, 0)
    prev = lambda b, i: (b * nq + jnp.maximum(i - 1, 0), 0)
    cur_t = lambda b, i: (0, b * nq + i)
    prev_t = lambda b, i: (0, b * nq + jnp.maximum(i - 1, 0))
    assert rep == 4 and kw == (s_heads // rep) * 4 * SWA_HEAD_DIM
    return pl.pallas_call(
        functools.partial(_swa_prompt_kernel, s_heads=s_heads, rep=rep, blk=blk),
        grid=(batch, nq),
        in_specs=[pl.BlockSpec(memory_space=pltpu.SMEM),
                  pl.BlockSpec((qt.shape[0], blk), cur_t),
                  pl.BlockSpec((blk, kw), prev), pl.BlockSpec((blk, kw), cur),
                  pl.BlockSpec((vt.shape[0], blk), prev_t), pl.BlockSpec((vt.shape[0], blk), cur_t),
                  pl.BlockSpec(bias_t.shape, lambda b, i: (0, 0, 0), pipeline_mode=pl.Buffered(1))],
        out_specs=pl.BlockSpec((blk, qt.shape[0]), cur),
        out_shape=jax.ShapeDtypeStruct((qt.shape[1], qt.shape[0]), BF16),
        compiler_params=_params(2), name="swa_prompt",
    )(sinks, qt, k2, k2, vt, vt, bias_t)


def _swa_decode_kernel(q_ref, ck_ref, cv_ref, nkv_ref, rep_ref, bias_ref, sink_ref, o_ref, *, kv_w, groups, rep):
    n = q_ref.shape[0]
    gw = rep * SWA_HEAD_DIM
    k = jnp.concatenate([ck_ref[...], nkv_ref[:, 0:kv_w]], axis=0).astype(BF16)
    v = jnp.concatenate([cv_ref[...], nkv_ref[:, kv_w:2 * kv_w]], axis=0).astype(BF16)
    k4 = _dot(k, rep_ref[...]).astype(BF16)
    v4 = _dot(v, rep_ref[...]).astype(BF16)
    lane_head = lax.broadcasted_iota(jnp.int32, (1, gw), 1) // SWA_HEAD_DIM
    sls = [slice(g * gw, (g + 1) * gw) for g in range(groups)]
    ss = []
    for g in range(groups):
        qg = q_ref[:, sls[g]]
        qs = jnp.concatenate([jnp.where(lane_head == r, qg, jnp.zeros_like(qg)) for r in range(rep)], axis=0)
        ss.append(_dot_nt(qs, k4[:, sls[g]]))
    ps = []
    for g in range(groups):
        rows = slice(g * rep * n, (g + 1) * rep * n)
        s = ss[g] + bias_ref[rows, :]
        sink = sink_ref[rows, :] * LOG2E
        m = jnp.maximum(jnp.max(s, axis=-1, keepdims=True), sink)
        p = jnp.exp2(s - m)
        inv = 1.0 / (jnp.sum(p, axis=-1, keepdims=True) + jnp.exp2(sink - m))
        ps.append((p * inv).astype(BF16))
    for g in range(groups):
        res = _dot(ps[g], v4[:, sls[g]])
        og = jnp.zeros((n, gw), F32)
        for r in range(rep):
            og = jnp.where(lane_head == r, res[r * n:(r + 1) * n], og)
        o_ref[:, sls[g]] = og.astype(BF16)


def _rope_table(pos):
    inv = ROPE_BASE ** (-jnp.arange(0, MLA_ROPE, 2, dtype=jnp.float32) / MLA_ROPE)
    ang = pos.astype(jnp.float32)[:, None] * inv[None, :]
    cos, sin = jnp.cos(ang), jnp.sin(ang)
    widths = ((0, 0), (MLA_NOPE, LANES - MLA_NOPE - MLA_ROPE))
    return jnp.concatenate([jnp.pad(jnp.concatenate([cos, cos], axis=1), widths),
                            jnp.pad(jnp.concatenate([-sin, sin], axis=1), widths)], axis=1)


def _prep_weights(mla_w_dq, mla_w_uq, mla_w_dkv, mla_w_uk, mla_w_uv, mla_w_o, w_kv_shared, swa_w_q, swa_w_o, scale):
    ql, qcols = mla_w_uq.shape
    c_dim, heads, nope = mla_w_uk.shape
    rope = qcols // heads - nope
    d = mla_w_dq.shape[0]
    pad = LANES - nope - rope
    w = {}
    w["wq"] = jnp.pad(mla_w_uq.reshape(ql, heads, nope + rope), ((0, 0), (0, 0), (0, pad))).reshape(
        ql, heads * LANES) * scale
    w["wdq"] = mla_w_dq
    w["wlat"] = mla_w_dkv[:, :c_dim]
    w["wkr"] = jnp.pad(mla_w_dkv[:, c_dim:], ((0, 0), (nope, pad)))
    zc = jnp.zeros((c_dim, heads, LANES - nope), F32)
    w["wuk"] = jnp.concatenate([mla_w_uk, zc], axis=-1).reshape(c_dim, heads * LANES)
    w["wuv_t"] = jnp.pad(mla_w_uv, ((0, 0), (0, 0), (0, MLA_VT_ROWS - mla_w_uv.shape[2]))).reshape(
        c_dim, heads * MLA_VT_ROWS).T
    w["wuk_dec"] = jnp.concatenate([jnp.transpose(mla_w_uk, (1, 2, 0)),
                                    jnp.zeros((heads, LANES - nope, c_dim), F32)], axis=1)
    w["wuv_dec"] = jnp.concatenate([jnp.transpose(mla_w_uv, (1, 0, 2)),
                                    jnp.zeros((heads, c_dim, LANES - nope), F32)], axis=-1)
    vdim = mla_w_uv.shape[2]
    assert vdim == MLA_V and nope == MLA_NOPE and rope == MLA_ROPE
    w["wo"] = mla_w_o
    w["wo_pad"] = jnp.concatenate([mla_w_o.reshape(heads, vdim, d), jnp.zeros((heads, LANES - vdim, d), F32)],
                                  axis=1).reshape(heads * LANES, d)
    kvw = w_kv_shared.shape[1] // 2
    groups = kvw // SWA_HEAD_DIM
    s_heads = swa_w_q.shape[1] // SWA_HEAD_DIM
    rep = s_heads // groups
    w["wkv"] = w_kv_shared
    eye = jnp.eye(SWA_HEAD_DIM, dtype=F32)
    zero = jnp.zeros_like(eye)
    w["rep_k2"] = jnp.kron(jnp.eye(groups, dtype=F32), jnp.concatenate([eye, zero, zero, eye], axis=1))
    w["eye_kv"] = jnp.eye(kvw, dtype=F32)
    w["swa_wq_t"] = swa_w_q.T
    w["swa_wo"] = swa_w_o
    w["rep_kv"] = jnp.kron(jnp.eye(groups, dtype=F32), jnp.tile(eye, (1, rep)))
    w["swa_wq"] = swa_w_q
    return {k: v.astype(BF16) for k, v in w.items()}, dict(heads=heads, c_dim=c_dim, groups=groups, rep=rep,
                                                            s_heads=s_heads, kvw=kvw)


def _rows_a(x, P, w, tab, heads, c_dim, absorbed):
    t, d = x.shape
    tm = _row_tile(t)
    hw = heads * LANES
    row = lambda v: v.reshape(1, -1)
    ins = ([_rows(x, tm)] + _ffn_ins(P, w, 1, 0)
           + [_const(row(P["mix_norm"][0])), _const(w["wdq"]), _const(row(P["mla_q_norm"][0])), _const(w["wq"]),
              _const(w["wlat"]), _const(row(P["mla_kv_norm"][0])), _const(w["wkr"]), _cycle(tab, tm)]
           + ([_const(w["wuk_dec"])] if absorbed else [_const(w["wuk"]), _const(w["wuv_t"])]))
    outs = [_rows_out(t, tm, d, F32), _rows_out(t, tm, hw, BF16)]
    if absorbed:
        outs.append(_rows_out(t, tm, heads * c_dim, BF16))
    else:
        outs += [_rows_out(t, tm, hw, BF16),
                 (jax.ShapeDtypeStruct((heads * MLA_VT_ROWS, t), BF16),
                  pl.BlockSpec((heads * MLA_VT_ROWS, tm), lambda i: (0, i)))]
    outs += [_rows_out(t, tm, c_dim, F32), _rows_out(t, tm, MLA_ROPE, F32)]
    return _rows_call(functools.partial(_rows_a_kernel, heads=heads, c_dim=c_dim, absorbed=absorbed),
                      "rows_a_absorbed" if absorbed else "rows_a", t, tm, ins, outs)


def _rows_b(a, x, P, w, meta, decode):
    t, d = x.shape
    tm = _row_tile(t, ROW_TILE_LIGHT)
    kvw = meta["kvw"]
    ins = [_rows(a, tm)]
    if decode:
        ins += [_const(w["wuv_dec"]), _const(w["wo_pad"])]
    else:
        ins += [_const(w["wo"])]
    ins += [_rows(x, tm)] + _ffn_ins(P, w, 2, 0) + [_const(P["kv_norm"].reshape(1, -1)), _const(w["wkv"])]
    outs = [_rows_out(t, tm, d, F32), _rows_out(t, tm, 2 * kvw, F32)]
    if not decode:
        ins += [_const(w["rep_k2"]), _const(w["eye_kv"])]
        outs += [_rows_out(t, tm, w["rep_k2"].shape[1], BF16),
                 (jax.ShapeDtypeStruct((kvw, t), BF16), pl.BlockSpec((kvw, tm), lambda i: (0, i)))]
    return _rows_call(functools.partial(_rows_b_kernel, dec_heads=meta["heads"] if decode else 0,
                                        c_dim=meta["c_dim"], tiled_kv=not decode),
                      "rows_b_decode" if decode else "rows_b", t, tm, ins, outs)


def _rows_c(x, P, w, wq, name, transposed):
    t, d = x.shape
    tm = _row_tile(t, ROW_TILE_LIGHT)
    ins = [_rows(x, tm)] + _ffn_ins(P, w, 1, 1) + [_const(P["mix_norm"][1].reshape(1, -1)), _const(wq)]
    if transposed:
        q_out = (jax.ShapeDtypeStruct((wq.shape[0], t), BF16), pl.BlockSpec((wq.shape[0], tm), lambda i: (0, i)))
    else:
        q_out = _rows_out(t, tm, wq.shape[1], BF16)
    return _rows_call(functools.partial(_rows_c_kernel, scale=SWA_HEAD_DIM ** -0.5 * LOG2E, transposed=transposed),
                      name, t, tm, ins, [_rows_out(t, tm, d, F32), q_out])


def _rows_d(a, x, P, w, wo, name):
    t, d = x.shape
    tm = _row_tile(t, ROW_TILE_LIGHT)
    ins = ([_rows(a, tm), _const(wo), _rows(x, tm)] + _ffn_ins(P, w, 2, 1)
           + [_const(P["final_norm"].reshape(1, -1))])
    return _rows_call(_rows_d_kernel, name, t, tm, ins, [_rows_out(t, tm, d, F32)])[0]


def _trunk_prompt(x3, P, w, meta):
    batch, seq, d = x3.shape
    heads, c_dim, kvw = meta["heads"], meta["c_dim"], meta["kvw"]
    x = x3.reshape(batch * seq, d)
    x, q, k, vt, lat, kr = _rows_a(x, P, w, _rope_table(jnp.arange(seq)), heads, c_dim, absorbed=False)
    o = _mla_attn_prompt(q, k, vt, batch, seq, heads)
    x, kv, k2, v_t = _rows_b(o, x, P, w, meta, decode=False)
    x, q_t = _rows_c(x, P, w, w["swa_wq_t"], "rows_c", transposed=True)
    blk = 2 * CHUNK
    rel = (jnp.arange(2 * blk) - blk)[:, None] - jnp.arange(blk)[None, :]
    bias_t = _bias_table(P["rel_bias"], _rel_bucket(rel).astype(jnp.int32))
    o = _swa_prompt(q_t, k2, v_t, bias_t, P["swa_sinks"][0], batch, seq, meta["s_heads"], meta["rep"], blk)
    y = _rows_d(o, x, P, w, w["swa_wo"], "rows_d")
    keep = min(WINDOW, seq)
    kv3 = kv.reshape(batch, seq, 2 * kvw)[:, seq - keep:]
    new_k = kv3[:, :, :kvw].reshape(batch, keep, meta["groups"], SWA_HEAD_DIM)
    new_v = kv3[:, :, kvw:].reshape(batch, keep, meta["groups"], SWA_HEAD_DIM)
    return (y.reshape(batch, seq, d), lat.reshape(1, batch, seq, c_dim), kr.reshape(1, batch, seq, MLA_ROPE),
            new_k, new_v)


def _trunk_decode(x3, cache_lat, cache_rope, cache_k, cache_v, P, w, meta):
    batch, n_new, d = x3.shape
    heads, c_dim, kvw, s_heads = meta["heads"], meta["c_dim"], meta["kvw"], meta["s_heads"]
    past = cache_lat.shape[2]
    w_c = cache_k.shape[1]
    qpos = past + jnp.arange(n_new)
    assert past % CHUNK == 0 and n_new <= CHUNK and w_c <= WINDOW_CHUNKS * CHUNK and w_c <= past
    t = batch * n_new
    x = x3.reshape(t, d)
    tm = _row_tile(t)
    assert tm % n_new == 0
    tab = jnp.tile(_rope_table(qpos), (tm // n_new, 1))
    x, q, ql, lat, kr = _rows_a(x, P, w, tab, heads, c_dim, absorbed=True)
    ol = _mla_attn_decode(ql, q, cache_lat, jnp.swapaxes(cache_rope, 2, 3), lat, kr, heads, n_new)
    x, kv = _rows_b(ol, x, P, w, meta, decode=True)
    x, qs = _rows_c(x, P, w, w["swa_wq"], "rows_c_decode", transposed=False)
    kpos = jnp.arange(past - w_c, past + n_new)
    bias_h = _bias_table(P["rel_bias"], _rel_bucket(kpos[None, :] - qpos[:, None]).astype(jnp.int32))
    bias = bias_h.reshape(s_heads * n_new, w_c + n_new)
    sink_rows = jnp.repeat(P["swa_sinks"][0], n_new).reshape(s_heads * n_new, 1)
    const2 = lambda a: pl.BlockSpec(a.shape, lambda b: (0, 0), pipeline_mode=pl.Buffered(1))
    o = pl.pallas_call(
        functools.partial(_swa_decode_kernel, kv_w=kvw, groups=meta["groups"], rep=meta["rep"]),
        grid=(batch,),
        in_specs=[pl.BlockSpec((n_new, qs.shape[1]), lambda b: (b, 0)),
                  pl.BlockSpec((None, w_c, kvw), lambda b: (b, 0, 0)),
                  pl.BlockSpec((None, w_c, kvw), lambda b: (b, 0, 0)),
                  pl.BlockSpec((n_new, 2 * kvw), lambda b: (b, 0)),
                  const2(w["rep_kv"]), const2(bias), const2(sink_rows)],
        out_specs=pl.BlockSpec((n_new, qs.shape[1]), lambda b: (b, 0)),
        out_shape=jax.ShapeDtypeStruct(qs.shape, BF16),
        compiler_params=_params(1), name="swa_decode",
    )(qs, cache_k.reshape(batch, w_c, kvw), cache_v.reshape(batch, w_c, kvw), kv, w["rep_kv"], bias, sink_rows)
    y = _rows_d(o, x, P, w, w["swa_wo"], "rows_d_decode")
    kv3 = kv.reshape(batch, n_new, 2 * kvw)
    new_k = kv3[:, :, :kvw].reshape(batch, n_new, meta["groups"], SWA_HEAD_DIM)
    new_v = kv3[:, :, kvw:].reshape(batch, n_new, meta["groups"], SWA_HEAD_DIM)
    return (y.reshape(batch, n_new, d), lat.reshape(1, batch, n_new, c_dim), kr.reshape(1, batch, n_new, MLA_ROPE),
            new_k, new_v)


def kernel(x_prompt, x_sample, cache_mla_latent, cache_mla_krope, cache_swa_k, cache_swa_v, ffn_norm1, ffn1_w_gate, ffn1_w_up, ffn1_w_down, mix_norm, ffn_norm2, ffn2_w_gate, ffn2_w_up, ffn2_w_down, mla_w_dq, mla_q_norm, mla_w_uq, mla_w_dkv, mla_kv_norm, mla_w_uk, mla_w_uv, mla_w_o, kv_norm, w_kv_shared, swa_w_q, swa_sinks, swa_w_o, rel_bias, final_norm):
    assert ffn_norm1.shape[0] == 2 and mla_w_dq.shape[0] == 1 and swa_w_q.shape[0] == 1
    scale = (MLA_NOPE + MLA_ROPE) ** -0.5 * LOG2E
    w, meta = _prep_weights(mla_w_dq[0], mla_w_uq[0], mla_w_dkv[0], mla_w_uk[0], mla_w_uv[0], mla_w_o[0],
                            w_kv_shared, swa_w_q[0], swa_w_o[0], scale)
    w.update(f1g=ffn1_w_gate.astype(BF16), f1u=ffn1_w_up.astype(BF16), f1d=ffn1_w_down.astype(BF16),
             f2g=ffn2_w_gate.astype(BF16), f2u=ffn2_w_up.astype(BF16), f2d=ffn2_w_down.astype(BF16))
    P = dict(ffn_norm1=ffn_norm1, mix_norm=mix_norm, ffn_norm2=ffn_norm2, mla_q_norm=mla_q_norm,
             mla_kv_norm=mla_kv_norm, kv_norm=kv_norm, swa_sinks=swa_sinks, rel_bias=rel_bias, final_norm=final_norm)
    y_p, lat_p, rope_p, k_p, v_p = _trunk_prompt(x_prompt, P, w, meta)
    y_s, lat_s, rope_s, k_s, v_s = _trunk_decode(x_sample, cache_mla_latent, cache_mla_krope,
                                                 cache_swa_k, cache_swa_v, P, w, meta)
    return (y_p, y_s, lat_p, rope_p, k_p, v_p, lat_s, rope_s, k_s, v_s)
```

```python
import functools
import math

import jax
import jax.numpy as jnp
from jax import lax
from jax.experimental import pallas as pl
from jax.experimental.pallas import tpu as pltpu

F32 = jnp.float32
BF16 = jnp.bfloat16

CHUNK = 64
RMS_EPS = 1e-6
FFN_RES = 0.5
ROPE_BASE = 10000.0
WINDOW = 128
WINDOW_CHUNKS = WINDOW // CHUNK
N_BUCKETS = 32
MAX_DISTANCE = 128
NEG_INF = -1e30
LOG2E = math.log2(math.e)
MLA_NOPE = 64
MLA_ROPE = 32
MLA_V = 64
MLA_VT_ROWS = 80
SWA_HEAD_DIM = 64

LANES = 128
ROW_TILE = 512
ROW_TILE_LIGHT = 512
MLA_BQ = 512
MLA_BK = 512
MLA_DIAG = 256
MLA_DECODE_CHUNK = 1024
MLA_HEADS_PER_STEP = 8
VMEM_LIMIT = 60 * 1024 * 1024


def _params(n_axes, vmem=VMEM_LIMIT):
    return pltpu.CompilerParams(dimension_semantics=("arbitrary",) * n_axes, vmem_limit_bytes=vmem)


def _rms(xf, g):
    return xf * lax.rsqrt(jnp.mean(xf * xf, axis=-1, keepdims=True) + RMS_EPS) * g


def _dot(a, b):
    return jnp.dot(a, b, preferred_element_type=F32)


def _dot_nt(a, b):
    return lax.dot_general(a, b, (((1,), (1,)), ((), ())), preferred_element_type=F32)


def _row_tile(t, tile=None):
    tm = min(tile or ROW_TILE, t)
    assert t % tm == 0, (t, tm)
    return tm


def _rows(a, tm):
    return a, pl.BlockSpec((tm, a.shape[1]), lambda i: (i, 0))


def _const(a):
    nd = a.ndim
    return a, pl.BlockSpec(a.shape, lambda i: (0,) * nd, pipeline_mode=pl.Buffered(1))


def _layer(a, layer):
    nd = a.ndim - 1
    return a, pl.BlockSpec((None,) + tuple(a.shape[1:]), lambda i: (layer,) + (0,) * nd,
                           pipeline_mode=pl.Buffered(1))


def _cycle(a, tm):
    assert a.shape[0] % tm == 0
    nb = a.shape[0] // tm
    return a, pl.BlockSpec((tm, a.shape[1]), lambda i: (i % nb, 0))


def _rows_out(t, tm, ncols, dtype):
    return jax.ShapeDtypeStruct((t, ncols), dtype), pl.BlockSpec((tm, ncols), lambda i: (i, 0))


def _rows_call(body, name, t, tm, ins, outs):
    return pl.pallas_call(
        body, grid=(t // tm,),
        in_specs=[s for _, s in ins], out_specs=[s for _, s in outs], out_shape=[o for o, _ in outs],
        compiler_params=_params(1), name=name,
    )(*[a for a, _ in ins])


def _ffn_ins(P, w, which, layer):
    return [_const(P["ffn_norm%d" % which][layer].reshape(1, -1)), _layer(w["f%dg" % which], layer),
            _layer(w["f%du" % which], layer), _layer(w["f%dd" % which], layer)]


def _ffn_apply(x, g_ref, wg_ref, wu_ref, wd_ref):
    h = _rms(x, g_ref[...]).astype(BF16)
    gate = _dot(h, wg_ref[...])
    up = _dot(h, wu_ref[...])
    a = (gate * jax.nn.sigmoid(gate) * up).astype(BF16)
    return x + FFN_RES * _dot(a, wd_ref[...])


def _mla_proj_apply(x, mg_ref, wdq_ref, qn_ref, wq_ref, wlat_ref, kvn_ref, wkr_ref, tab_ref,
                    refs, heads, c_dim, absorbed):
    cos_t, sin_t = tab_ref[:, 0:LANES], tab_ref[:, LANES:2 * LANES]
    lane = lax.broadcasted_iota(jnp.int32, (1, LANES), 1)
    cos_q = jnp.where(lane < MLA_NOPE, 1.0, cos_t)
    first_half = lane < MLA_NOPE + MLA_ROPE // 2

    def swap_halves(v):
        return jnp.where(first_half, pltpu.roll(v, LANES - MLA_ROPE // 2, 1), pltpu.roll(v, MLA_ROPE // 2, 1))

    h = _rms(x, mg_ref[...]).astype(BF16)
    cq = _rms(_dot(h, wdq_ref[...]), qn_ref[...]).astype(BF16)
    qa = _dot(cq, wq_ref[...])
    lat = _rms(_dot(h, wlat_ref[...]), kvn_ref[...])
    kr = _dot(h, wkr_ref[...])
    kr = kr * cos_t + swap_halves(kr) * sin_t
    latb = lat.astype(BF16)
    if absorbed:
        wukd_ref, q_ref, ql_ref, lat_ref, kr_ref = refs
    else:
        wuk_ref, wuvt_ref, q_ref, k_ref, vt_ref, lat_ref, kr_ref = refs
        kn = _dot(latb, wuk_ref[...])
    lat_ref[...] = lat
    kr_ref[...] = kr[:, MLA_NOPE:MLA_NOPE + MLA_ROPE]
    for hh in range(heads):
        sl = slice(hh * LANES, (hh + 1) * LANES)
        qh = (qa[:, sl] * cos_q + swap_halves(qa[:, sl]) * sin_t).astype(BF16)
        q_ref[:, sl] = qh
        if absorbed:
            ql_ref[:, hh * c_dim:(hh + 1) * c_dim] = _dot(qh, wukd_ref[hh]).astype(BF16)
        else:
            k_ref[:, sl] = (kn[:, sl] + kr).astype(BF16)
    if not absorbed:
        vt = _dot_nt(wuvt_ref[...], latb)
        ones_row = lax.broadcasted_iota(jnp.int32, vt.shape, 0) % MLA_VT_ROWS == MLA_V
        vt_ref[...] = jnp.where(ones_row, 1.0, vt).astype(BF16)


def _rows_a_kernel(x_ref, fg, wg, wu, wd, mg, wdq, qn, wq, wlat, kvn, wkr, tab, *refs,
                   heads, c_dim, absorbed):
    n_extra = 1 if absorbed else 2
    x1_ref = refs[n_extra]
    x1 = _ffn_apply(x_ref[...], fg, wg, wu, wd)
    x1_ref[...] = x1
    _mla_proj_apply(x1, mg, wdq, qn, wq, wlat, kvn, wkr, tab, refs[:n_extra] + refs[n_extra + 1:],
                    heads, c_dim, absorbed)


def _rows_b_kernel(a_ref, *refs, dec_heads, c_dim, tiled_kv):
    if dec_heads:
        wuv_ref, refs = refs[0], refs[1:]
        a = jnp.concatenate([_dot(a_ref[:, hh * c_dim:(hh + 1) * c_dim], wuv_ref[hh]).astype(BF16)
                             for hh in range(dec_heads)], axis=1)
    else:
        a = a_ref[...]
    wo, x_ref, fg, wg, wu, wd, kvg, wkv = refs[:8]
    x = _ffn_apply(x_ref[...] + _dot(a, wo[...]), fg, wg, wu, wd)
    kv = _dot(_rms(x, kvg[...]).astype(BF16), wkv[...])
    if tiled_kv:
        repk, eye, x_out, kv_out, k2_out, vt_out = refs[8:]
        half = kv.shape[1] // 2
        k2_out[...] = _dot(kv[:, :half].astype(BF16), repk[...]).astype(BF16)
        vt_out[...] = _dot_nt(eye[...], kv[:, half:].astype(BF16)).astype(BF16)
    else:
        x_out, kv_out = refs[8:]
    x_out[...] = x
    kv_out[...] = kv


def _rows_c_kernel(x_ref, fg, wg, wu, wd, mg, wq, x_out, q_out, *, scale, transposed):
    x = _ffn_apply(x_ref[...], fg, wg, wu, wd)
    x_out[...] = x
    h = _rms(x, mg[...]).astype(BF16)
    q = _dot_nt(wq[...], h) if transposed else _dot(h, wq[...])
    q_out[...] = (q * scale).astype(BF16)


def _rows_d_kernel(a_ref, wo, x_ref, fg, wg, wu, wd, fin_g, y_out):
    x = _ffn_apply(x_ref[...] + _dot(a_ref[...], wo[...]), fg, wg, wu, wd)
    y_out[...] = _rms(x, fin_g[...])


def _mla_attn_kernel(q_ref, k_ref, vt_ref, o_ref, m_sc, acc_sc, sa_sc, sb_sc, *, bq, bk, hp):
    i = pl.program_id(2)
    qs = [q_ref[:, hh * LANES:(hh + 1) * LANES] for hh in range(hp)]
    sls = [slice(hh * LANES, (hh + 1) * LANES) for hh in range(hp)]

    def own_scores(start, width, q_lo):
        ss = [_dot_nt(k_ref[pl.ds(start, width), sls[hh]], qs[hh][q_lo:]) for hh in range(hp)]
        kc = (start + lax.broadcasted_iota(jnp.int32, (width, 1), 0)) // CHUNK
        qc = (i * bq + q_lo + lax.broadcasted_iota(jnp.int32, (1, bq - q_lo), 1)) // CHUNK
        return [jnp.where(kc <= qc, s, NEG_INF) for s in ss]

    def own_update(ss, start, width, q_lo, after_head=None):
        ql = slice(q_lo, bq)
        for hh in range(hp):
            m = m_sc[hh, :, ql]
            m_new = jnp.maximum(m, jnp.max(ss[hh], axis=0, keepdims=True))
            alpha = jnp.exp2(m - m_new)
            p = jnp.exp2(ss[hh] - m_new).astype(BF16)
            vt = vt_ref[hh * MLA_VT_ROWS:(hh + 1) * MLA_VT_ROWS, pl.ds(start, width)]
            acc_sc[hh, :, ql] = alpha * acc_sc[hh, :, ql] + _dot(vt, p)
            m_sc[hh, :, ql] = m_new
            if after_head is not None:
                after_head(hh)

    def produce(hh, j, dst):
        dst[hh] = _dot_nt(k_ref[pl.ds(pl.multiple_of(j * bk, bk), bk), sls[hh]], qs[hh])

    def consume(hh, j, src):
        s = src[hh]
        m = m_sc[hh]
        m_new = jnp.maximum(m, jnp.max(s, axis=0, keepdims=True))
        alpha = jnp.exp2(m - m_new)
        p = jnp.exp2(s - m_new).astype(BF16)
        vt = vt_ref[hh * MLA_VT_ROWS:(hh + 1) * MLA_VT_ROWS, pl.ds(pl.multiple_of(j * bk, bk), bk)]
        acc_sc[hh] = alpha * acc_sc[hh] + _dot(vt, p)
        m_sc[hh] = m_new

    def transition(j, src, dst):
        produce(0, j + 1, dst)
        for hh in range(hp):
            consume(hh, j, src)
            if hh + 1 < hp:
                produce(hh + 1, j + 1, dst)

    m_sc[...] = jnp.full(m_sc.shape, NEG_INF, F32)
    acc_sc[...] = jnp.zeros(acc_sc.shape, F32)
    n_full = (i * bq) // bk

    diag = min(MLA_DIAG, bq)
    starts = [pl.multiple_of(i * bq + t * diag, diag) for t in range(bq // diag)]
    own = [own_scores(starts[t], diag, t * diag) for t in range(bq // diag)]
    for t in range(bq // diag):
        own_update(own[t], starts[t], diag, t * diag,
                   after_head=(lambda hh: produce(hh, 0, sa_sc)) if t == 0 else None)

    @pl.when(n_full > 0)
    def _():
        last = n_full - 1

        @pl.loop(0, last // 2)
        def _(t):
            transition(2 * t, sa_sc, sb_sc)
            transition(2 * t + 1, sb_sc, sa_sc)

        @pl.when(last % 2 == 1)
        def _():
            transition(last - 1, sa_sc, sb_sc)
            for hh in range(hp):
                consume(hh, last, sb_sc)

        @pl.when(last % 2 == 0)
        def _():
            for hh in range(hp):
                consume(hh, last, sa_sc)

    for pair in range(hp // 2):
        halves = []
        for hh in (2 * pair, 2 * pair + 1):
            acc = acc_sc[hh]
            halves.append(acc[:MLA_V] * (1.0 / acc[MLA_V:MLA_V + 1]))
        o_ref[:, pair * LANES:(pair + 1) * LANES] = jnp.concatenate(halves, axis=0).T.astype(BF16)


def _mla_attn_prompt(q, k, vt, batch, seq, heads, bq=MLA_BQ, bk=MLA_BK, hp=MLA_HEADS_PER_STEP):
    bq, bk = min(bq, seq), min(bk, seq)
    nq = seq // bq
    assert seq % bq == 0 and seq % bk == 0 and bq % CHUNK == 0 and heads % hp == 0 and hp % 2 == 0
    assert bq % bk == 0 and bq % min(MLA_DIAG, bq) == 0 and MLA_DIAG % CHUNK == 0
    w = hp * LANES
    return pl.pallas_call(
        functools.partial(_mla_attn_kernel, bq=bq, bk=bk, hp=hp),
        grid=(batch, heads // hp, nq),
        in_specs=[pl.BlockSpec((bq, w), lambda b, h, i: (b * nq + i, h)),
                  pl.BlockSpec((seq, w), lambda b, h, i: (b, h)),
                  pl.BlockSpec((hp * MLA_VT_ROWS, seq), lambda b, h, i: (h, b))],
        out_specs=pl.BlockSpec((bq, hp * MLA_V), lambda b, h, i: (b * nq + i, h)),
        out_shape=jax.ShapeDtypeStruct((q.shape[0], heads * MLA_V), BF16),
        scratch_shapes=[pltpu.VMEM((hp, 1, bq), F32), pltpu.VMEM((hp, MLA_VT_ROWS, bq), F32),
                        pltpu.VMEM((hp, bk, bq), F32), pltpu.VMEM((hp, bk, bq), F32)],
        compiler_params=_params(3), name="mla_attn_prompt",
    )(q, k, vt)


def _mla_attn_decode_kernel(ql_ref, q_ref, cl_ref, crt_ref, nl_ref, nr_ref, o_ref, *, heads, c_dim):
    n = ql_ref.shape[0]
    ql = jnp.concatenate([ql_ref[:, hh * c_dim:(hh + 1) * c_dim] for hh in range(heads)], axis=0)
    qr = jnp.concatenate([q_ref[:, hh * LANES + MLA_NOPE:hh * LANES + MLA_NOPE + MLA_ROPE] for hh in range(heads)],
                         axis=0)
    past = cl_ref.shape[0]
    chunk = min(past, MLA_DECODE_CHUNK)
    parts = [(cl_ref[c:c + chunk, :], crt_ref[:, c:c + chunk], _dot) for c in range(0, past, chunk)]
    parts.append((nl_ref[...], nr_ref[...], _dot_nt))
    m = jnp.full((heads * n, 1), NEG_INF, F32)
    l = jnp.zeros((heads * n, 1), F32)
    acc = jnp.zeros((heads * n, c_dim), F32)
    def scores(part):
        kl, kr, rope_dot = part
        kl, kr = kl.astype(BF16), kr.astype(BF16)
        return kl, _dot_nt(ql, kl) + rope_dot(qr, kr)

    nxt = scores(parts[0])
    for c in range(len(parts)):
        kl, s = nxt
        if c + 1 < len(parts):
            nxt = scores(parts[c + 1])
        m_new = jnp.maximum(m, jnp.max(s, axis=-1, keepdims=True))
        alpha = jnp.exp2(m - m_new)
        p = jnp.exp2(s - m_new)
        l = alpha * l + jnp.sum(p, axis=-1, keepdims=True)
        acc = alpha * acc + _dot(p.astype(BF16), kl)
        m = m_new
    o = acc / l
    for hh in range(heads):
        o_ref[:, hh * c_dim:(hh + 1) * c_dim] = o[hh * n:(hh + 1) * n].astype(BF16)


def _mla_attn_decode(ql, q, cache_lat, cache_rope_t, new_lat, new_rope, heads, n_new):
    _, batch, past, c_dim = cache_lat.shape
    r_dim = cache_rope_t.shape[2]
    return pl.pallas_call(
        functools.partial(_mla_attn_decode_kernel, heads=heads, c_dim=c_dim),
        grid=(batch,),
        in_specs=[pl.BlockSpec((n_new, heads * c_dim), lambda b: (b, 0)),
                  pl.BlockSpec((n_new, heads * LANES), lambda b: (b, 0)),
                  pl.BlockSpec((None, None, past, c_dim), lambda b: (0, b, 0, 0)),
                  pl.BlockSpec((None, None, r_dim, past), lambda b: (0, b, 0, 0)),
                  pl.BlockSpec((n_new, c_dim), lambda b: (b, 0)),
                  pl.BlockSpec((n_new, r_dim), lambda b: (b, 0))],
        out_specs=pl.BlockSpec((n_new, heads * c_dim), lambda b: (b, 0)),
        out_shape=jax.ShapeDtypeStruct(ql.shape, BF16),
        compiler_params=_params(1), name="mla_attn_decode",
    )(ql, q, cache_lat, cache_rope_t, new_lat, new_rope)


def _bias_kernel(tab_ref, idx_ref, o_ref, *, heads):
    idx = idx_ref[...]
    for hh in range(heads):
        acc = jnp.zeros(idx.shape, F32)
        for b in range(N_BUCKETS):
            acc = jnp.where(idx == b, tab_ref[b, hh] * LOG2E, acc)
        o_ref[hh] = acc


def _bias_table(rel_bias, idx):
    heads = rel_bias.shape[1]
    return pl.pallas_call(
        functools.partial(_bias_kernel, heads=heads),
        in_specs=[pl.BlockSpec(memory_space=pltpu.SMEM), pl.BlockSpec(idx.shape, lambda: (0, 0))],
        out_specs=pl.BlockSpec((heads,) + idx.shape, lambda: (0, 0, 0)),
        out_shape=jax.ShapeDtypeStruct((heads,) + idx.shape, F32), name="rel_bias_table",
    )(rel_bias, idx)


def _rel_bucket(rel):
    half = N_BUCKETS // 2
    max_exact = half // 2
    base = jnp.where(rel > 0, half, 0)
    n = jnp.abs(rel)
    nf = jnp.maximum(n, 1).astype(jnp.float32)
    large = max_exact + (jnp.log(nf / max_exact) / math.log(MAX_DISTANCE / max_exact)
                         * (half - max_exact)).astype(jnp.int32)
    large = jnp.minimum(large, half - 1)
    return base + jnp.where(n < max_exact, n, large)


def _swa_prompt_kernel(sink_ref, qt_ref, kp_ref, kc_ref, vtp_ref, vtc_ref, bias_ref, o_ref, *, s_heads, rep, blk):
    i = pl.program_id(1)
    key_chunk = lax.broadcasted_iota(jnp.int32, (2 * blk, 1), 0) // CHUNK
    row_chunk = lax.broadcasted_iota(jnp.int32, (1, blk), 1) // CHUNK
    first = blk // CHUNK
    valid = ((key_chunk >= row_chunk) & (key_chunk <= row_chunk + WINDOW_CHUNKS)
             & ((i > 0) | (key_chunk >= first)))
    dh = SWA_HEAD_DIM
    ss = []
    for hh in range(s_heads):
        g, pair, slot = hh // rep, hh // 2, hh % 2
        ksl = slice(g * 4 * dh + slot * 2 * dh, g * 4 * dh + (slot + 1) * 2 * dh)
        k = jnp.concatenate([kp_ref[:, ksl], kc_ref[:, ksl]], axis=0)
        ss.append(_dot(k, qt_ref[pair * 2 * dh:(pair + 1) * 2 * dh, :]))
    ps, invs = [], []
    for hh, s in enumerate(ss):
        s = jnp.where(valid, s + bias_ref[hh], NEG_INF)
        sink = sink_ref[hh] * LOG2E
        m = jnp.maximum(jnp.max(s, axis=0, keepdims=True), sink)
        p = jnp.exp2(s - m)
        invs.append(1.0 / (jnp.sum(p, axis=0, keepdims=True) + jnp.exp2(sink - m)))
        ps.append(p.astype(BF16))
    outs = []
    for hh, p in enumerate(ps):
        g = hh // rep
        vt = jnp.concatenate([vtp_ref[g * dh:(g + 1) * dh, :], vtc_ref[g * dh:(g + 1) * dh, :]], axis=1)
        outs.append(_dot(vt, p) * invs[hh])
    for pair in range(s_heads // 2):
        o_ref[:, pair * 2 * dh:(pair + 1) * 2 * dh] = (
            jnp.concatenate(outs[2 * pair:2 * pair + 2], axis=0).T.astype(BF16))


def _swa_prompt(qt, k2, vt, bias_t, sinks, batch, seq, s_heads, rep, blk=128):
    nq = seq // blk
    kw = k2.shape[1]
    cur = lambda b, i: (b * nq + i, 0)
    prev = lambda b, i: (b * nq + jnp.maximum(i - 1, 0), 0)
    cur_t = lambda b, i: (0, b * nq + i)
    prev_t = lambda b, i: (0, b * nq + jnp.maximum(i - 1, 0))
    assert rep == 4 and kw == (s_heads // rep) * 4 * SWA_HEAD_DIM
    return pl.pallas_call(
        functools.partial(_swa_prompt_kernel, s_heads=s_heads, rep=rep, blk=blk),
        grid=(batch, nq),
        in_specs=[pl.BlockSpec(memory_space=pltpu.SMEM),
                  pl.BlockSpec((qt.shape[0], blk), cur_t),
                  pl.BlockSpec((blk, kw), prev), pl.BlockSpec((blk, kw), cur),
                  pl.BlockSpec((vt.shape[0], blk), prev_t), pl.BlockSpec((vt.shape[0], blk), cur_t),
                  pl.BlockSpec(bias_t.shape, lambda b, i: (0, 0, 0), pipeline_mode=pl.Buffered(1))],
        out_specs=pl.BlockSpec((blk, qt.shape[0]), cur),
        out_shape=jax.ShapeDtypeStruct((qt.shape[1], qt.shape[0]), BF16),
        compiler_params=_params(2), name="swa_prompt",
    )(sinks, qt, k2, k2, vt, vt, bias_t)


def _swa_decode_kernel(q_ref, ck_ref, cv_ref, nkv_ref, rep_ref, bias_ref, sink_ref, o_ref, *, kv_w, groups, rep):
    n = q_ref.shape[0]
    gw = rep * SWA_HEAD_DIM
    k = jnp.concatenate([ck_ref[...], nkv_ref[:, 0:kv_w]], axis=0).astype(BF16)
    v = jnp.concatenate([cv_ref[...], nkv_ref[:, kv_w:2 * kv_w]], axis=0).astype(BF16)
    k4 = _dot(k, rep_ref[...]).astype(BF16)
    v4 = _dot(v, rep_ref[...]).astype(BF16)
    lane_head = lax.broadcasted_iota(jnp.int32, (1, gw), 1) // SWA_HEAD_DIM
    sls = [slice(g * gw, (g + 1) * gw) for g in range(groups)]
    ss = []
    for g in range(groups):
        qg = q_ref[:, sls[g]]
        qs = jnp.concatenate([jnp.where(lane_head == r, qg, jnp.zeros_like(qg)) for r in range(rep)], axis=0)
        ss.append(_dot_nt(qs, k4[:, sls[g]]))
    ps = []
    for g in range(groups):
        rows = slice(g * rep * n, (g + 1) * rep * n)
        s = ss[g] + bias_ref[rows, :]
        sink = sink_ref[rows, :] * LOG2E
        m = jnp.maximum(jnp.max(s, axis=-1, keepdims=True), sink)
        p = jnp.exp2(s - m)
        inv = 1.0 / (jnp.sum(p, axis=-1, keepdims=True) + jnp.exp2(sink - m))
        ps.append((p * inv).astype(BF16))
    for g in range(groups):
        res = _dot(ps[g], v4[:, sls[g]])
        og = jnp.zeros((n, gw), F32)
        for r in range(rep):
            og = jnp.where(lane_head == r, res[r * n:(r + 1) * n], og)
        o_ref[:, sls[g]] = og.astype(BF16)


def _rope_table(pos):
    inv = ROPE_BASE ** (-jnp.arange(0, MLA_ROPE, 2, dtype=jnp.float32) / MLA_ROPE)
    ang = pos.astype(jnp.float32)[:, None] * inv[None, :]
    cos, sin = jnp.cos(ang), jnp.sin(ang)
    widths = ((0, 0), (MLA_NOPE, LANES - MLA_NOPE - MLA_ROPE))
    return jnp.concatenate([jnp.pad(jnp.concatenate([cos, cos], axis=1), widths),
                            jnp.pad(jnp.concatenate([-sin, sin], axis=1), widths)], axis=1)


def _prep_weights(mla_w_dq, mla_w_uq, mla_w_dkv, mla_w_uk, mla_w_uv, mla_w_o, w_kv_shared, swa_w_q, swa_w_o, scale):
    ql, qcols = mla_w_uq.shape
    c_dim, heads, nope = mla_w_uk.shape
    rope = qcols // heads - nope
    d = mla_w_dq.shape[0]
    pad = LANES - nope - rope
    w = {}
    w["wq"] = jnp.pad(mla_w_uq.reshape(ql, heads, nope + rope), ((0, 0), (0, 0), (0, pad))).reshape(
        ql, heads * LANES) * scale
    w["wdq"] = mla_w_dq
    w["wlat"] = mla_w_dkv[:, :c_dim]
    w["wkr"] = jnp.pad(mla_w_dkv[:, c_dim:], ((0, 0), (nope, pad)))
    zc = jnp.zeros((c_dim, heads, LANES - nope), F32)
    w["wuk"] = jnp.concatenate([mla_w_uk, zc], axis=-1).reshape(c_dim, heads * LANES)
    w["wuv_t"] = jnp.pad(mla_w_uv, ((0, 0), (0, 0), (0, MLA_VT_ROWS - mla_w_uv.shape[2]))).reshape(
        c_dim, heads * MLA_VT_ROWS).T
    w["wuk_dec"] = jnp.concatenate([jnp.transpose(mla_w_uk, (1, 2, 0)),
                                    jnp.zeros((heads, LANES - nope, c_dim), F32)], axis=1)
    w["wuv_dec"] = jnp.concatenate([jnp.transpose(mla_w_uv, (1, 0, 2)),
                                    jnp.zeros((heads, c_dim, LANES - nope), F32)], axis=-1)
    vdim = mla_w_uv.shape[2]
    assert vdim == MLA_V and nope == MLA_NOPE and rope == MLA_ROPE
    w["wo"] = mla_w_o
    w["wo_pad"] = jnp.concatenate([mla_w_o.reshape(heads, vdim, d), jnp.zeros((heads, LANES - vdim, d), F32)],
                                  axis=1).reshape(heads * LANES, d)
    kvw = w_kv_shared.shape[1] // 2
    groups = kvw // SWA_HEAD_DIM
    s_heads = swa_w_q.shape[1] // SWA_HEAD_DIM
    rep = s_heads // groups
    w["wkv"] = w_kv_shared
    eye = jnp.eye(SWA_HEAD_DIM, dtype=F32)
    zero = jnp.zeros_like(eye)
    w["rep_k2"] = jnp.kron(jnp.eye(groups, dtype=F32), jnp.concatenate([eye, zero, zero, eye], axis=1))
    w["eye_kv"] = jnp.eye(kvw, dtype=F32)
    w["swa_wq_t"] = swa_w_q.T
    w["swa_wo"] = swa_w_o
    w["rep_kv"] = jnp.kron(jnp.eye(groups, dtype=F32), jnp.tile(eye, (1, rep)))
    w["swa_wq"] = swa_w_q
    return {k: v.astype(BF16) for k, v in w.items()}, dict(heads=heads, c_dim=c_dim, groups=groups, rep=rep,
                                                            s_heads=s_heads, kvw=kvw)


def _rows_a(x, P, w, tab, heads, c_dim, absorbed):
    t, d = x.shape
    tm = _row_tile(t)
    hw = heads * LANES
    row = lambda v: v.reshape(1, -1)
    ins = ([_rows(x, tm)] + _ffn_ins(P, w, 1, 0)
           + [_const(row(P["mix_norm"][0])), _const(w["wdq"]), _const(row(P["mla_q_norm"][0])), _const(w["wq"]),
              _const(w["wlat"]), _const(row(P["mla_kv_norm"][0])), _const(w["wkr"]), _cycle(tab, tm)]
           + ([_const(w["wuk_dec"])] if absorbed else [_const(w["wuk"]), _const(w["wuv_t"])]))
    outs = [_rows_out(t, tm, d, F32), _rows_out(t, tm, hw, BF16)]
    if absorbed:
        outs.append(_rows_out(t, tm, heads * c_dim, BF16))
    else:
        outs += [_rows_out(t, tm, hw, BF16),
                 (jax.ShapeDtypeStruct((heads * MLA_VT_ROWS, t), BF16),
                  pl.BlockSpec((heads * MLA_VT_ROWS, tm), lambda i: (0, i)))]
    outs += [_rows_out(t, tm, c_dim, F32), _rows_out(t, tm, MLA_ROPE, F32)]
    return _rows_call(functools.partial(_rows_a_kernel, heads=heads, c_dim=c_dim, absorbed=absorbed),
                      "rows_a_absorbed" if absorbed else "rows_a", t, tm, ins, outs)


def _rows_b(a, x, P, w, meta, decode):
    t, d = x.shape
    tm = _row_tile(t, ROW_TILE_LIGHT)
    kvw = meta["kvw"]
    ins = [_rows(a, tm)]
    if decode:
        ins += [_const(w["wuv_dec"]), _const(w["wo_pad"])]
    else:
        ins += [_const(w["wo"])]
    ins += [_rows(x, tm)] + _ffn_ins(P, w, 2, 0) + [_const(P["kv_norm"].reshape(1, -1)), _const(w["wkv"])]
    outs = [_rows_out(t, tm, d, F32), _rows_out(t, tm, 2 * kvw, F32)]
    if not decode:
        ins += [_const(w["rep_k2"]), _const(w["eye_kv"])]
        outs += [_rows_out(t, tm, w["rep_k2"].shape[1], BF16),
                 (jax.ShapeDtypeStruct((kvw, t), BF16), pl.BlockSpec((kvw, tm), lambda i: (0, i)))]
    return _rows_call(functools.partial(_rows_b_kernel, dec_heads=meta["heads"] if decode else 0,
                                        c_dim=meta["c_dim"], tiled_kv=not decode),
                      "rows_b_decode" if decode else "rows_b", t, tm, ins, outs)


def _rows_c(x, P, w, wq, name, transposed):
    t, d = x.shape
    tm = _row_tile(t, ROW_TILE_LIGHT)
    ins = [_rows(x, tm)] + _ffn_ins(P, w, 1, 1) + [_const(P["mix_norm"][1].reshape(1, -1)), _const(wq)]
    if transposed:
        q_out = (jax.ShapeDtypeStruct((wq.shape[0], t), BF16), pl.BlockSpec((wq.shape[0], tm), lambda i: (0, i)))
    else:
        q_out = _rows_out(t, tm, wq.shape[1], BF16)
    return _rows_call(functools.partial(_rows_c_kernel, scale=SWA_HEAD_DIM ** -0.5 * LOG2E, transposed=transposed),
                      name, t, tm, ins, [_rows_out(t, tm, d, F32), q_out])


def _rows_d(a, x, P, w, wo, name):
    t, d = x.shape
    tm = _row_tile(t, ROW_TILE_LIGHT)
    ins = ([_rows(a, tm), _const(wo), _rows(x, tm)] + _ffn_ins(P, w, 2, 1)
           + [_const(P["final_norm"].reshape(1, -1))])
    return _rows_call(_rows_d_kernel, name, t, tm, ins, [_rows_out(t, tm, d, F32)])[0]


def _trunk_prompt(x3, P, w, meta):
    batch, seq, d = x3.shape
    heads, c_dim, kvw = meta["heads"], meta["c_dim"], meta["kvw"]
    x = x3.reshape(batch * seq, d)
    x, q, k, vt, lat, kr = _rows_a(x, P, w, _rope_table(jnp.arange(seq)), heads, c_dim, absorbed=False)
    o = _mla_attn_prompt(q, k, vt, batch, seq, heads)
    x, kv, k2, v_t = _rows_b(o, x, P, w, meta, decode=False)
    x, q_t = _rows_c(x, P, w, w["swa_wq_t"], "rows_c", transposed=True)
    blk = 2 * CHUNK
    rel = (jnp.arange(2 * blk) - blk)[:, None] - jnp.arange(blk)[None, :]
    bias_t = _bias_table(P["rel_bias"], _rel_bucket(rel).astype(jnp.int32))
    o = _swa_prompt(q_t, k2, v_t, bias_t, P["swa_sinks"][0], batch, seq, meta["s_heads"], meta["rep"], blk)
    y = _rows_d(o, x, P, w, w["swa_wo"], "rows_d")
    keep = min(WINDOW, seq)
    kv3 = kv.reshape(batch, seq, 2 * kvw)[:, seq - keep:]
    new_k = kv3[:, :, :kvw].reshape(batch, keep, meta["groups"], SWA_HEAD_DIM)
    new_v = kv3[:, :, kvw:].reshape(batch, keep, meta["groups"], SWA_HEAD_DIM)
    return (y.reshape(batch, seq, d), lat.reshape(1, batch, seq, c_dim), kr.reshape(1, batch, seq, MLA_ROPE),
            new_k, new_v)


def _trunk_decode(x3, cache_lat, cache_rope, cache_k, cache_v, P, w, meta):
    batch, n_new, d = x3.shape
    heads, c_dim, kvw, s_heads = meta["heads"], meta["c_dim"], meta["kvw"], meta["s_heads"]
    past = cache_lat.shape[2]
    w_c = cache_k.shape[1]
    qpos = past + jnp.arange(n_new)
    assert past % CHUNK == 0 and n_new <= CHUNK and w_c <= WINDOW_CHUNKS * CHUNK and w_c <= past
    t = batch * n_new
    x = x3.reshape(t, d)
    tm = _row_tile(t)
    assert tm % n_new == 0
    tab = jnp.tile(_rope_table(qpos), (tm // n_new, 1))
    x, q, ql, lat, kr = _rows_a(x, P, w, tab, heads, c_dim, absorbed=True)
    ol = _mla_attn_decode(ql, q, cache_lat, jnp.swapaxes(cache_rope, 2, 3), lat, kr, heads, n_new)
    x, kv = _rows_b(ol, x, P, w, meta, decode=True)
    x, qs = _rows_c(x, P, w, w["swa_wq"], "rows_c_decode", transposed=False)
    kpos = jnp.arange(past - w_c, past + n_new)
    bias_h = _bias_table(P["rel_bias"], _rel_bucket(kpos[None, :] - qpos[:, None]).astype(jnp.int32))
    bias = bias_h.reshape(s_heads * n_new, w_c + n_new)
    sink_rows = jnp.repeat(P["swa_sinks"][0], n_new).reshape(s_heads * n_new, 1)
    const2 = lambda a: pl.BlockSpec(a.shape, lambda b: (0, 0), pipeline_mode=pl.Buffered(1))
    o = pl.pallas_call(
        functools.partial(_swa_decode_kernel, kv_w=kvw, groups=meta["groups"], rep=meta["rep"]),
        grid=(batch,),
        in_specs=[pl.BlockSpec((n_new, qs.shape[1]), lambda b: (b, 0)),
                  pl.BlockSpec((None, w_c, kvw), lambda b: (b, 0, 0)),
                  pl.BlockSpec((None, w_c, kvw), lambda b: (b, 0, 0)),
                  pl.BlockSpec((n_new, 2 * kvw), lambda b: (b, 0)),
                  const2(w["rep_kv"]), const2(bias), const2(sink_rows)],
        out_specs=pl.BlockSpec((n_new, qs.shape[1]), lambda b: (b, 0)),
        out_shape=jax.ShapeDtypeStruct(qs.shape, BF16),
        compiler_params=_params(1), name="swa_decode",
    )(qs, cache_k.reshape(batch, w_c, kvw), cache_v.reshape(batch, w_c, kvw), kv, w["rep_kv"], bias, sink_rows)
    y = _rows_d(o, x, P, w, w["swa_wo"], "rows_d_decode")
    kv3 = kv.reshape(batch, n_new, 2 * kvw)
    new_k = kv3[:, :, :kvw].reshape(batch, n_new, meta["groups"], SWA_HEAD_DIM)
    new_v = kv3[:, :, kvw:].reshape(batch, n_new, meta["groups"], SWA_HEAD_DIM)
    return (y.reshape(batch, n_new, d), lat.reshape(1, batch, n_new, c_dim), kr.reshape(1, batch, n_new, MLA_ROPE),
            new_k, new_v)


def kernel(x_prompt, x_sample, cache_mla_latent, cache_mla_krope, cache_swa_k, cache_swa_v, ffn_norm1, ffn1_w_gate, ffn1_w_up, ffn1_w_down, mix_norm, ffn_norm2, ffn2_w_gate, ffn2_w_up, ffn2_w_down, mla_w_dq, mla_q_norm, mla_w_uq, mla_w_dkv, mla_kv_norm, mla_w_uk, mla_w_uv, mla_w_o, kv_norm, w_kv_shared, swa_w_q, swa_sinks, swa_w_o, rel_bias, final_norm):
    assert ffn_norm1.shape[0] == 2 and mla_w_dq.shape[0] == 1 and swa_w_q.shape[0] == 1
    scale = (MLA_NOPE + MLA_ROPE) ** -0.5 * LOG2E
    w, meta = _prep_weights(mla_w_dq[0], mla_w_uq[0], mla_w_dkv[0], mla_w_uk[0], mla_w_uv[0], mla_w_o[0],
                            w_kv_shared, swa_w_q[0], swa_w_o[0], scale)
    w.update(f1g=ffn1_w_gate.astype(BF16), f1u=ffn1_w_up.astype(BF16), f1d=ffn1_w_down.astype(BF16),
             f2g=ffn2_w_gate.astype(BF16), f2u=ffn2_w_up.astype(BF16), f2d=ffn2_w_down.astype(BF16))
    P = dict(ffn_norm1=ffn_norm1, mix_norm=mix_norm, ffn_norm2=ffn_norm2, mla_q_norm=mla_q_norm,
             mla_kv_norm=mla_kv_norm, kv_norm=kv_norm, swa_sinks=swa_sinks, rel_bias=rel_bias, final_norm=final_norm)
    y_p, lat_p, rope_p, k_p, v_p = _trunk_prompt(x_prompt, P, w, meta)
    y_s, lat_s, rope_s, k_s, v_s = _trunk_decode(x_sample, cache_mla_latent, cache_mla_krope,
                                                 cache_swa_k, cache_swa_v, P, w, meta)
    return (y_p, y_s, lat_p, rope_p, k_p, v_p, lat_s, rope_s, k_s, v_s)
```

```python
import functools
import math

import jax
import jax.numpy as jnp
from jax import lax
from jax.experimental import pallas as pl
from jax.experimental.pallas import tpu as pltpu

F32 = jnp.float32
BF16 = jnp.bfloat16

CHUNK = 64
RMS_EPS = 1e-6
FFN_RES = 0.5
ROPE_BASE = 10000.0
WINDOW = 128
WINDOW_CHUNKS = WINDOW // CHUNK
N_BUCKETS = 32
MAX_DISTANCE = 128
NEG_INF = -1e30
LOG2E = math.log2(math.e)
MLA_NOPE = 64
MLA_ROPE = 32
MLA_V = 64
MLA_VT_ROWS = 80
SWA_HEAD_DIM = 64

LANES = 128
ROW_TILE = 512
ROW_TILE_LIGHT = 512
MLA_BQ = 512
MLA_BK = 512
SWA_BLOCKS_PER_STEP = 4
MLA_DIAG = 256
MLA_DECODE_CHUNK = 1024
MLA_HEADS_PER_STEP = 8
VMEM_LIMIT = 60 * 1024 * 1024


def _params(n_axes, vmem=VMEM_LIMIT):
    return pltpu.CompilerParams(dimension_semantics=("arbitrary",) * n_axes, vmem_limit_bytes=vmem)


def _rms(xf, g):
    return xf * lax.rsqrt(jnp.mean(xf * xf, axis=-1, keepdims=True) + RMS_EPS) * g


def _dot(a, b):
    return jnp.dot(a, b, preferred_element_type=F32)


def _dot_nt(a, b):
    return lax.dot_general(a, b, (((1,), (1,)), ((), ())), preferred_element_type=F32)


def _row_tile(t, tile=None):
    tm = min(tile or ROW_TILE, t)
    assert t % tm == 0, (t, tm)
    return tm


def _rows(a, tm):
    return a, pl.BlockSpec((tm, a.shape[1]), lambda i: (i, 0))


def _const(a):
    nd = a.ndim
    return a, pl.BlockSpec(a.shape, lambda i: (0,) * nd, pipeline_mode=pl.Buffered(1))


def _layer(a, layer):
    nd = a.ndim - 1
    return a, pl.BlockSpec((None,) + tuple(a.shape[1:]), lambda i: (layer,) + (0,) * nd,
                           pipeline_mode=pl.Buffered(1))


def _cycle(a, tm):
    assert a.shape[0] % tm == 0
    nb = a.shape[0] // tm
    return a, pl.BlockSpec((tm, a.shape[1]), lambda i: (i % nb, 0))


def _rows_out(t, tm, ncols, dtype):
    return jax.ShapeDtypeStruct((t, ncols), dtype), pl.BlockSpec((tm, ncols), lambda i: (i, 0))


def _rows_call(body, name, t, tm, ins, outs):
    return pl.pallas_call(
        body, grid=(t // tm,),
        in_specs=[s for _, s in ins], out_specs=[s for _, s in outs], out_shape=[o for o, _ in outs],
        compiler_params=_params(1), name=name,
    )(*[a for a, _ in ins])


def _ffn_ins(P, w, which, layer):
    return [_const(P["ffn_norm%d" % which][layer].reshape(1, -1)), _layer(w["f%dg" % which], layer),
            _layer(w["f%du" % which], layer), _layer(w["f%dd" % which], layer)]


def _ffn_apply(x, g_ref, wg_ref, wu_ref, wd_ref):
    h = _rms(x, g_ref[...]).astype(BF16)
    gate = _dot(h, wg_ref[...])
    up = _dot(h, wu_ref[...])
    a = (gate * jax.nn.sigmoid(gate) * up).astype(BF16)
    return x + FFN_RES * _dot(a, wd_ref[...])


def _mla_proj_apply(x, mg_ref, wdq_ref, qn_ref, wq_ref, wlat_ref, kvn_ref, wkr_ref, tab_ref,
                    refs, heads, c_dim, absorbed):
    cos_t, sin_t = tab_ref[:, 0:LANES], tab_ref[:, LANES:2 * LANES]
    lane = lax.broadcasted_iota(jnp.int32, (1, LANES), 1)
    cos_q = jnp.where(lane < MLA_NOPE, 1.0, cos_t)
    first_half = lane < MLA_NOPE + MLA_ROPE // 2

    def swap_halves(v):
        return jnp.where(first_half, pltpu.roll(v, LANES - MLA_ROPE // 2, 1), pltpu.roll(v, MLA_ROPE // 2, 1))

    h = _rms(x, mg_ref[...]).astype(BF16)
    cq = _rms(_dot(h, wdq_ref[...]), qn_ref[...]).astype(BF16)
    qa = _dot(cq, wq_ref[...])
    lat = _rms(_dot(h, wlat_ref[...]), kvn_ref[...])
    kr = _dot(h, wkr_ref[...])
    kr = kr * cos_t + swap_halves(kr) * sin_t
    latb = lat.astype(BF16)
    if absorbed:
        wukd_ref, q_ref, ql_ref, lat_ref, kr_ref = refs
    else:
        wuk_ref, wuvt_ref, q_ref, k_ref, vt_ref, lat_ref, kr_ref = refs
        kn = _dot(latb, wuk_ref[...])
    lat_ref[...] = lat
    kr_ref[...] = kr[:, MLA_NOPE:MLA_NOPE + MLA_ROPE]
    for hh in range(heads):
        sl = slice(hh * LANES, (hh + 1) * LANES)
        qh = (qa[:, sl] * cos_q + swap_halves(qa[:, sl]) * sin_t).astype(BF16)
        q_ref[:, sl] = qh
        if absorbed:
            ql_ref[:, hh * c_dim:(hh + 1) * c_dim] = _dot(qh, wukd_ref[hh]).astype(BF16)
        else:
            k_ref[:, sl] = (kn[:, sl] + kr).astype(BF16)
    if not absorbed:
        vt = _dot_nt(wuvt_ref[...], latb)
        ones_row = lax.broadcasted_iota(jnp.int32, vt.shape, 0) % MLA_VT_ROWS == MLA_V
        vt_ref[...] = jnp.where(ones_row, 1.0, vt).astype(BF16)


def _rows_a_kernel(x_ref, fg, wg, wu, wd, mg, wdq, qn, wq, wlat, kvn, wkr, tab, *refs,
                   heads, c_dim, absorbed):
    n_extra = 1 if absorbed else 2
    x1_ref = refs[n_extra]
    x1 = _ffn_apply(x_ref[...], fg, wg, wu, wd)
    x1_ref[...] = x1
    _mla_proj_apply(x1, mg, wdq, qn, wq, wlat, kvn, wkr, tab, refs[:n_extra] + refs[n_extra + 1:],
                    heads, c_dim, absorbed)


def _rows_b_kernel(a_ref, *refs, dec_heads, c_dim, tiled_kv):
    if dec_heads:
        wuv_ref, refs = refs[0], refs[1:]
        a = jnp.concatenate([_dot(a_ref[:, hh * c_dim:(hh + 1) * c_dim], wuv_ref[hh]).astype(BF16)
                             for hh in range(dec_heads)], axis=1)
    else:
        a = a_ref[...]
    wo, x_ref, fg, wg, wu, wd, kvg, wkv = refs[:8]
    x = _ffn_apply(x_ref[...] + _dot(a, wo[...]), fg, wg, wu, wd)
    kv = _dot(_rms(x, kvg[...]).astype(BF16), wkv[...])
    if tiled_kv:
        repk, eye, x_out, kv_out, k2_out, vt_out = refs[8:]
        half = kv.shape[1] // 2
        k2_out[...] = _dot(kv[:, :half].astype(BF16), repk[...]).astype(BF16)
        vt_out[...] = _dot_nt(eye[...], kv[:, half:].astype(BF16)).astype(BF16)
    else:
        x_out, kv_out = refs[8:]
    x_out[...] = x
    kv_out[...] = kv


def _rows_c_kernel(x_ref, fg, wg, wu, wd, mg, wq, x_out, q_out, *, scale, transposed):
    x = _ffn_apply(x_ref[...], fg, wg, wu, wd)
    x_out[...] = x
    h = _rms(x, mg[...]).astype(BF16)
    q = _dot_nt(wq[...], h) if transposed else _dot(h, wq[...])
    q_out[...] = (q * scale).astype(BF16)


def _rows_d_kernel(a_ref, wo, x_ref, fg, wg, wu, wd, fin_g, y_out):
    x = _ffn_apply(x_ref[...] + _dot(a_ref[...], wo[...]), fg, wg, wu, wd)
    y_out[...] = _rms(x, fin_g[...])


def _mla_attn_kernel(q_ref, k_ref, vt_ref, o_ref, m_sc, acc_sc, sa_sc, sb_sc, *, bq, bk, hp):
    i = pl.program_id(2)
    qs = [q_ref[:, hh * LANES:(hh + 1) * LANES] for hh in range(hp)]
    sls = [slice(hh * LANES, (hh + 1) * LANES) for hh in range(hp)]

    def own_scores(start, width, q_lo):
        ss = [_dot_nt(k_ref[pl.ds(start, width), sls[hh]], qs[hh][q_lo:]) for hh in range(hp)]
        kc = (start + lax.broadcasted_iota(jnp.int32, (width, 1), 0)) // CHUNK
        qc = (i * bq + q_lo + lax.broadcasted_iota(jnp.int32, (1, bq - q_lo), 1)) // CHUNK
        return [jnp.where(kc <= qc, s, NEG_INF) for s in ss]

    def own_update(ss, start, width, q_lo, after_head=None):
        ql = slice(q_lo, bq)
        for hh in range(hp):
            m = m_sc[hh, :, ql]
            m_new = jnp.maximum(m, jnp.max(ss[hh], axis=0, keepdims=True))
            alpha = jnp.exp2(m - m_new)
            p = jnp.exp2(ss[hh] - m_new).astype(BF16)
            vt = vt_ref[hh * MLA_VT_ROWS:(hh + 1) * MLA_VT_ROWS, pl.ds(start, width)]
            acc_sc[hh, :, ql] = alpha * acc_sc[hh, :, ql] + _dot(vt, p)
            m_sc[hh, :, ql] = m_new
            if after_head is not None:
                after_head(hh)

    def produce(hh, j, dst):
        dst[hh] = _dot_nt(k_ref[pl.ds(pl.multiple_of(j * bk, bk), bk), sls[hh]], qs[hh])

    def consume(hh, j, src):
        s = src[hh]
        m = m_sc[hh]
        m_new = jnp.maximum(m, jnp.max(s, axis=0, keepdims=True))
        alpha = jnp.exp2(m - m_new)
        p = jnp.exp2(s - m_new).astype(BF16)
        vt = vt_ref[hh * MLA_VT_ROWS:(hh + 1) * MLA_VT_ROWS, pl.ds(pl.multiple_of(j * bk, bk), bk)]
        acc_sc[hh] = alpha * acc_sc[hh] + _dot(vt, p)
        m_sc[hh] = m_new

    def transition(j, src, dst):
        produce(0, j + 1, dst)
        for hh in range(hp):
            consume(hh, j, src)
            if hh + 1 < hp:
                produce(hh + 1, j + 1, dst)

    m_sc[...] = jnp.full(m_sc.shape, NEG_INF, F32)
    acc_sc[...] = jnp.zeros(acc_sc.shape, F32)
    n_full = (i * bq) // bk

    diag = min(MLA_DIAG, bq)
    starts = [pl.multiple_of(i * bq + t * diag, diag) for t in range(bq // diag)]
    own = [own_scores(starts[t], diag, t * diag) for t in range(bq // diag)]
    for t in range(bq // diag):
        own_update(own[t], starts[t], diag, t * diag,
                   after_head=(lambda hh: produce(hh, 0, sa_sc)) if t == 0 else None)

    @pl.when(n_full > 0)
    def _():
        last = n_full - 1

        @pl.loop(0, last // 2)
        def _(t):
            transition(2 * t, sa_sc, sb_sc)
            transition(2 * t + 1, sb_sc, sa_sc)

        @pl.when(last % 2 == 1)
        def _():
            transition(last - 1, sa_sc, sb_sc)
            for hh in range(hp):
                consume(hh, last, sb_sc)

        @pl.when(last % 2 == 0)
        def _():
            for hh in range(hp):
                consume(hh, last, sa_sc)

    for pair in range(hp // 2):
        halves = []
        for hh in (2 * pair, 2 * pair + 1):
            acc = acc_sc[hh]
            halves.append(acc[:MLA_V] * (1.0 / acc[MLA_V:MLA_V + 1]))
        o_ref[:, pair * LANES:(pair + 1) * LANES] = jnp.concatenate(halves, axis=0).T.astype(BF16)


def _mla_attn_prompt(q, k, vt, batch, seq, heads, bq=MLA_BQ, bk=MLA_BK, hp=MLA_HEADS_PER_STEP):
    bq, bk = min(bq, seq), min(bk, seq)
    nq = seq // bq
    assert seq % bq == 0 and seq % bk == 0 and bq % CHUNK == 0 and heads % hp == 0 and hp % 2 == 0
    assert bq % bk == 0 and bq % min(MLA_DIAG, bq) == 0 and MLA_DIAG % CHUNK == 0
    w = hp * LANES
    return pl.pallas_call(
        functools.partial(_mla_attn_kernel, bq=bq, bk=bk, hp=hp),
        grid=(batch, heads // hp, nq),
        in_specs=[pl.BlockSpec((bq, w), lambda b, h, i: (b * nq + i, h)),
                  pl.BlockSpec((seq, w), lambda b, h, i: (b, h)),
                  pl.BlockSpec((hp * MLA_VT_ROWS, seq), lambda b, h, i: (h, b))],
        out_specs=pl.BlockSpec((bq, hp * MLA_V), lambda b, h, i: (b * nq + i, h)),
        out_shape=jax.ShapeDtypeStruct((q.shape[0], heads * MLA_V), BF16),
        scratch_shapes=[pltpu.VMEM((hp, 1, bq), F32), pltpu.VMEM((hp, MLA_VT_ROWS, bq), F32),
                        pltpu.VMEM((hp, bk, bq), F32), pltpu.VMEM((hp, bk, bq), F32)],
        compiler_params=_params(3), name="mla_attn_prompt",
    )(q, k, vt)


def _mla_attn_decode_kernel(ql_ref, q_ref, cl_ref, crt_ref, nl_ref, nr_ref, o_ref, *, heads, c_dim):
    n = ql_ref.shape[0]
    ql = jnp.concatenate([ql_ref[:, hh * c_dim:(hh + 1) * c_dim] for hh in range(heads)], axis=0)
    qr = jnp.concatenate([q_ref[:, hh * LANES + MLA_NOPE:hh * LANES + MLA_NOPE + MLA_ROPE] for hh in range(heads)],
                         axis=0)
    past = cl_ref.shape[0]
    chunk = min(past, MLA_DECODE_CHUNK)
    parts = [(cl_ref[c:c + chunk, :], crt_ref[:, c:c + chunk], _dot) for c in range(0, past, chunk)]
    parts.append((nl_ref[...], nr_ref[...], _dot_nt))
    m = jnp.full((heads * n, 1), NEG_INF, F32)
    l = jnp.zeros((heads * n, 1), F32)
    acc = jnp.zeros((heads * n, c_dim), F32)
    def scores(part):
        kl, kr, rope_dot = part
        kl, kr = kl.astype(BF16), kr.astype(BF16)
        return kl, _dot_nt(ql, kl) + rope_dot(qr, kr)

    nxt = scores(parts[0])
    for c in range(len(parts)):
        kl, s = nxt
        if c + 1 < len(parts):
            nxt = scores(parts[c + 1])
        m_new = jnp.maximum(m, jnp.max(s, axis=-1, keepdims=True))
        alpha = jnp.exp2(m - m_new)
        p = jnp.exp2(s - m_new)
        l = alpha * l + jnp.sum(p, axis=-1, keepdims=True)
        acc = alpha * acc + _dot(p.astype(BF16), kl)
        m = m_new
    o = acc / l
    for hh in range(heads):
        o_ref[:, hh * c_dim:(hh + 1) * c_dim] = o[hh * n:(hh + 1) * n].astype(BF16)


def _mla_attn_decode(ql, q, cache_lat, cache_rope_t, new_lat, new_rope, heads, n_new):
    _, batch, past, c_dim = cache_lat.shape
    r_dim = cache_rope_t.shape[2]
    return pl.pallas_call(
        functools.partial(_mla_attn_decode_kernel, heads=heads, c_dim=c_dim),
        grid=(batch,),
        in_specs=[pl.BlockSpec((n_new, heads * c_dim), lambda b: (b, 0)),
                  pl.BlockSpec((n_new, heads * LANES), lambda b: (b, 0)),
                  pl.BlockSpec((None, None, past, c_dim), lambda b: (0, b, 0, 0)),
                  pl.BlockSpec((None, None, r_dim, past), lambda b: (0, b, 0, 0)),
                  pl.BlockSpec((n_new, c_dim), lambda b: (b, 0)),
                  pl.BlockSpec((n_new, r_dim), lambda b: (b, 0))],
        out_specs=pl.BlockSpec((n_new, heads * c_dim), lambda b: (b, 0)),
        out_shape=jax.ShapeDtypeStruct(ql.shape, BF16),
        compiler_params=_params(1), name="mla_attn_decode",
    )(ql, q, cache_lat, cache_rope_t, new_lat, new_rope)


def _bias_kernel(tab_ref, idx_ref, o_ref, *, heads):
    idx = idx_ref[...]
    for hh in range(heads):
        acc = jnp.zeros(idx.shape, F32)
        for b in range(N_BUCKETS):
            acc = jnp.where(idx == b, tab_ref[b, hh] * LOG2E, acc)
        o_ref[hh] = acc


def _bias_table(rel_bias, idx):
    heads = rel_bias.shape[1]
    return pl.pallas_call(
        functools.partial(_bias_kernel, heads=heads),
        in_specs=[pl.BlockSpec(memory_space=pltpu.SMEM), pl.BlockSpec(idx.shape, lambda: (0, 0))],
        out_specs=pl.BlockSpec((heads,) + idx.shape, lambda: (0, 0, 0)),
        out_shape=jax.ShapeDtypeStruct((heads,) + idx.shape, F32), name="rel_bias_table",
    )(rel_bias, idx)


def _rel_bucket(rel):
    half = N_BUCKETS // 2
    max_exact = half // 2
    base = jnp.where(rel > 0, half, 0)
    n = jnp.abs(rel)
    nf = jnp.maximum(n, 1).astype(jnp.float32)
    large = max_exact + (jnp.log(nf / max_exact) / math.log(MAX_DISTANCE / max_exact)
                         * (half - max_exact)).astype(jnp.int32)
    large = jnp.minimum(large, half - 1)
    return base + jnp.where(n < max_exact, n, large)


def _swa_prompt_kernel(sink_ref, qt_ref, kp_ref, kc_ref, vtp_ref, vtc_ref, bias_ref, o_ref, *, s_heads, rep, blk, nsub):
    i = pl.program_id(1)
    key_chunk = lax.broadcasted_iota(jnp.int32, (2 * blk, 1), 0) // CHUNK
    row_chunk = lax.broadcasted_iota(jnp.int32, (1, blk), 1) // CHUNK
    first = blk // CHUNK
    window = (key_chunk >= row_chunk) & (key_chunk <= row_chunk + WINDOW_CHUNKS)
    dh = SWA_HEAD_DIM
    units = [(u, hh) for u in range(nsub) for hh in range(s_heads)]
    ss = []
    for u, hh in units:
        g, pair, slot = hh // rep, hh // 2, hh % 2
        ksl = slice(g * 4 * dh + slot * 2 * dh, g * 4 * dh + (slot + 1) * 2 * dh)
        k = jnp.concatenate([kp_ref[:, ksl], kc_ref[:, ksl]], axis=0)[u * blk:(u + 2) * blk]
        ss.append(_dot(k, qt_ref[pair * 2 * dh:(pair + 1) * 2 * dh, u * blk:(u + 1) * blk]))
    ps, invs = [], []
    for (u, hh), s in zip(units, ss):
        valid = window if u > 0 else window & ((i > 0) | (key_chunk >= first))
        s = jnp.where(valid, s + bias_ref[hh], NEG_INF)
        sink = sink_ref[hh] * LOG2E
        m = jnp.maximum(jnp.max(s, axis=0, keepdims=True), sink)
        p = jnp.exp2(s - m)
        invs.append(1.0 / (jnp.sum(p, axis=0, keepdims=True) + jnp.exp2(sink - m)))
        ps.append(p.astype(BF16))
    outs = []
    for n, (u, hh) in enumerate(units):
        g = hh // rep
        vt = jnp.concatenate([vtp_ref[g * dh:(g + 1) * dh, :], vtc_ref[g * dh:(g + 1) * dh, :]],
                             axis=1)[:, u * blk:(u + 2) * blk]
        outs.append(_dot(vt, ps[n]) * invs[n])
    for u in range(nsub):
        for pair in range(s_heads // 2):
            n = u * s_heads + 2 * pair
            o_ref[u * blk:(u + 1) * blk, pair * 2 * dh:(pair + 1) * 2 * dh] = (
                jnp.concatenate(outs[n:n + 2], axis=0).T.astype(BF16))


def _swa_prompt(qt, k2, vt, bias_t, sinks, batch, seq, s_heads, rep, blk=128, nsub=SWA_BLOCKS_PER_STEP):
    assert seq % (blk * nsub) == 0
    nq = seq // (blk * nsub)
    kw = k2.shape[1]
    cur = lambda b, i: (b * nq + i, 0)
    prev = lambda b, i: ((b * nq + i) * nsub - jnp.minimum(i, 1), 0)
    cur_t = lambda b, i: (0, b * nq + i)
    prev_t = lambda b, i: (0, (b * nq + i) * nsub - jnp.minimum(i, 1))
    assert rep == 4 and kw == (s_heads // rep) * 4 * SWA_HEAD_DIM
    return pl.pallas_call(
        functools.partial(_swa_prompt_kernel, s_heads=s_heads, rep=rep, blk=blk, nsub=nsub),
        grid=(batch, nq),
        in_specs=[pl.BlockSpec(memory_space=pltpu.SMEM),
                  pl.BlockSpec((qt.shape[0], blk * nsub), cur_t),
                  pl.BlockSpec((blk, kw), prev), pl.BlockSpec((blk * nsub, kw), cur),
                  pl.BlockSpec((vt.shape[0], blk), prev_t), pl.BlockSpec((vt.shape[0], blk * nsub), cur_t),
                  pl.BlockSpec(bias_t.shape, lambda b, i: (0, 0, 0), pipeline_mode=pl.Buffered(1))],
        out_specs=pl.BlockSpec((blk * nsub, qt.shape[0]), cur),
        out_shape=jax.ShapeDtypeStruct((qt.shape[1], qt.shape[0]), BF16),
        compiler_params=_params(2), name="swa_prompt",
    )(sinks, qt, k2, k2, vt, vt, bias_t)


def _swa_decode_kernel(q_ref, ck_ref, cv_ref, nkv_ref, rep_ref, bias_ref, sink_ref, o_ref, *, kv_w, groups, rep):
    n = q_ref.shape[0]
    gw = rep * SWA_HEAD_DIM
    k = jnp.concatenate([ck_ref[...], nkv_ref[:, 0:kv_w]], axis=0).astype(BF16)
    v = jnp.concatenate([cv_ref[...], nkv_ref[:, kv_w:2 * kv_w]], axis=0).astype(BF16)
    k4 = _dot(k, rep_ref[...]).astype(BF16)
    v4 = _dot(v, rep_ref[...]).astype(BF16)
    lane_head = lax.broadcasted_iota(jnp.int32, (1, gw), 1) // SWA_HEAD_DIM
    sls = [slice(g * gw, (g + 1) * gw) for g in range(groups)]
    ss = []
    for g in range(groups):
        qg = q_ref[:, sls[g]]
        qs = jnp.concatenate([jnp.where(lane_head == r, qg, jnp.zeros_like(qg)) for r in range(rep)], axis=0)
        ss.append(_dot_nt(qs, k4[:, sls[g]]))
    ps = []
    for g in range(groups):
        rows = slice(g * rep * n, (g + 1) * rep * n)
        s = ss[g] + bias_ref[rows, :]
        sink = sink_ref[rows, :] * LOG2E
        m = jnp.maximum(jnp.max(s, axis=-1, keepdims=True), sink)
        p = jnp.exp2(s - m)
        inv = 1.0 / (jnp.sum(p, axis=-1, keepdims=True) + jnp.exp2(sink - m))
        ps.append((p * inv).astype(BF16))
    for g in range(groups):
        res = _dot(ps[g], v4[:, sls[g]])
        og = jnp.zeros((n, gw), F32)
        for r in range(rep):
            og = jnp.where(lane_head == r, res[r * n:(r + 1) * n], og)
        o_ref[:, sls[g]] = og.astype(BF16)


def _rope_table(pos):
    inv = ROPE_BASE ** (-jnp.arange(0, MLA_ROPE, 2, dtype=jnp.float32) / MLA_ROPE)
    ang = pos.astype(jnp.float32)[:, None] * inv[None, :]
    cos, sin = jnp.cos(ang), jnp.sin(ang)
    widths = ((0, 0), (MLA_NOPE, LANES - MLA_NOPE - MLA_ROPE))
    return jnp.concatenate([jnp.pad(jnp.concatenate([cos, cos], axis=1), widths),
                            jnp.pad(jnp.concatenate([-sin, sin], axis=1), widths)], axis=1)


def _prep_weights(mla_w_dq, mla_w_uq, mla_w_dkv, mla_w_uk, mla_w_uv, mla_w_o, w_kv_shared, swa_w_q, swa_w_o, scale):
    ql, qcols = mla_w_uq.shape
    c_dim, heads, nope = mla_w_uk.shape
    rope = qcols // heads - nope
    d = mla_w_dq.shape[0]
    pad = LANES - nope - rope
    w = {}
    w["wq"] = jnp.pad(mla_w_uq.reshape(ql, heads, nope + rope), ((0, 0), (0, 0), (0, pad))).reshape(
        ql, heads * LANES) * scale
    w["wdq"] = mla_w_dq
    w["wlat"] = mla_w_dkv[:, :c_dim]
    w["wkr"] = jnp.pad(mla_w_dkv[:, c_dim:], ((0, 0), (nope, pad)))
    zc = jnp.zeros((c_dim, heads, LANES - nope), F32)
    w["wuk"] = jnp.concatenate([mla_w_uk, zc], axis=-1).reshape(c_dim, heads * LANES)
    w["wuv_t"] = jnp.pad(mla_w_uv, ((0, 0), (0, 0), (0, MLA_VT_ROWS - mla_w_uv.shape[2]))).reshape(
        c_dim, heads * MLA_VT_ROWS).T
    w["wuk_dec"] = jnp.concatenate([jnp.transpose(mla_w_uk, (1, 2, 0)),
                                    jnp.zeros((heads, LANES - nope, c_dim), F32)], axis=1)
    w["wuv_dec"] = jnp.concatenate([jnp.transpose(mla_w_uv, (1, 0, 2)),
                                    jnp.zeros((heads, c_dim, LANES - nope), F32)], axis=-1)
    vdim = mla_w_uv.shape[2]
    assert vdim == MLA_V and nope == MLA_NOPE and rope == MLA_ROPE
    w["wo"] = mla_w_o
    w["wo_pad"] = jnp.concatenate([mla_w_o.reshape(heads, vdim, d), jnp.zeros((heads, LANES - vdim, d), F32)],
                                  axis=1).reshape(heads * LANES, d)
    kvw = w_kv_shared.shape[1] // 2
    groups = kvw // SWA_HEAD_DIM
    s_heads = swa_w_q.shape[1] // SWA_HEAD_DIM
    rep = s_heads // groups
    w["wkv"] = w_kv_shared
    eye = jnp.eye(SWA_HEAD_DIM, dtype=F32)
    zero = jnp.zeros_like(eye)
    w["rep_k2"] = jnp.kron(jnp.eye(groups, dtype=F32), jnp.concatenate([eye, zero, zero, eye], axis=1))
    w["eye_kv"] = jnp.eye(kvw, dtype=F32)
    w["swa_wq_t"] = swa_w_q.T
    w["swa_wo"] = swa_w_o
    w["rep_kv"] = jnp.kron(jnp.eye(groups, dtype=F32), jnp.tile(eye, (1, rep)))
    w["swa_wq"] = swa_w_q
    return {k: v.astype(BF16) for k, v in w.items()}, dict(heads=heads, c_dim=c_dim, groups=groups, rep=rep,
                                                            s_heads=s_heads, kvw=kvw)


def _rows_a(x, P, w, tab, heads, c_dim, absorbed):
    t, d = x.shape
    tm = _row_tile(t)
    hw = heads * LANES
    row = lambda v: v.reshape(1, -1)
    ins = ([_rows(x, tm)] + _ffn_ins(P, w, 1, 0)
           + [_const(row(P["mix_norm"][0])), _const(w["wdq"]), _const(row(P["mla_q_norm"][0])), _const(w["wq"]),
              _const(w["wlat"]), _const(row(P["mla_kv_norm"][0])), _const(w["wkr"]), _cycle(tab, tm)]
           + ([_const(w["wuk_dec"])] if absorbed else [_const(w["wuk"]), _const(w["wuv_t"])]))
    outs = [_rows_out(t, tm, d, F32), _rows_out(t, tm, hw, BF16)]
    if absorbed:
        outs.append(_rows_out(t, tm, heads * c_dim, BF16))
    else:
        outs += [_rows_out(t, tm, hw, BF16),
                 (jax.ShapeDtypeStruct((heads * MLA_VT_ROWS, t), BF16),
                  pl.BlockSpec((heads * MLA_VT_ROWS, tm), lambda i: (0, i)))]
    outs += [_rows_out(t, tm, c_dim, F32), _rows_out(t, tm, MLA_ROPE, F32)]
    return _rows_call(functools.partial(_rows_a_kernel, heads=heads, c_dim=c_dim, absorbed=absorbed),
                      "rows_a_absorbed" if absorbed else "rows_a", t, tm, ins, outs)


def _rows_b(a, x, P, w, meta, decode):
    t, d = x.shape
    tm = _row_tile(t, ROW_TILE_LIGHT)
    kvw = meta["kvw"]
    ins = [_rows(a, tm)]
    if decode:
        ins += [_const(w["wuv_dec"]), _const(w["wo_pad"])]
    else:
        ins += [_const(w["wo"])]
    ins += [_rows(x, tm)] + _ffn_ins(P, w, 2, 0) + [_const(P["kv_norm"].reshape(1, -1)), _const(w["wkv"])]
    outs = [_rows_out(t, tm, d, F32), _rows_out(t, tm, 2 * kvw, F32)]
    if not decode:
        ins += [_const(w["rep_k2"]), _const(w["eye_kv"])]
        outs += [_rows_out(t, tm, w["rep_k2"].shape[1], BF16),
                 (jax.ShapeDtypeStruct((kvw, t), BF16), pl.BlockSpec((kvw, tm), lambda i: (0, i)))]
    return _rows_call(functools.partial(_rows_b_kernel, dec_heads=meta["heads"] if decode else 0,
                                        c_dim=meta["c_dim"], tiled_kv=not decode),
                      "rows_b_decode" if decode else "rows_b", t, tm, ins, outs)


def _rows_c(x, P, w, wq, name, transposed):
    t, d = x.shape
    tm = _row_tile(t, ROW_TILE_LIGHT)
    ins = [_rows(x, tm)] + _ffn_ins(P, w, 1, 1) + [_const(P["mix_norm"][1].reshape(1, -1)), _const(wq)]
    if transposed:
        q_out = (jax.ShapeDtypeStruct((wq.shape[0], t), BF16), pl.BlockSpec((wq.shape[0], tm), lambda i: (0, i)))
    else:
        q_out = _rows_out(t, tm, wq.shape[1], BF16)
    return _rows_call(functools.partial(_rows_c_kernel, scale=SWA_HEAD_DIM ** -0.5 * LOG2E, transposed=transposed),
                      name, t, tm, ins, [_rows_out(t, tm, d, F32), q_out])


def _rows_d(a, x, P, w, wo, name):
    t, d = x.shape
    tm = _row_tile(t, ROW_TILE_LIGHT)
    ins = ([_rows(a, tm), _const(wo), _rows(x, tm)] + _ffn_ins(P, w, 2, 1)
           + [_const(P["final_norm"].reshape(1, -1))])
    return _rows_call(_rows_d_kernel, name, t, tm, ins, [_rows_out(t, tm, d, F32)])[0]


def _trunk_prompt(x3, P, w, meta):
    batch, seq, d = x3.shape
    heads, c_dim, kvw = meta["heads"], meta["c_dim"], meta["kvw"]
    x = x3.reshape(batch * seq, d)
    x, q, k, vt, lat, kr = _rows_a(x, P, w, _rope_table(jnp.arange(seq)), heads, c_dim, absorbed=False)
    o = _mla_attn_prompt(q, k, vt, batch, seq, heads)
    x, kv, k2, v_t = _rows_b(o, x, P, w, meta, decode=False)
    x, q_t = _rows_c(x, P, w, w["swa_wq_t"], "rows_c", transposed=True)
    blk = 2 * CHUNK
    rel = (jnp.arange(2 * blk) - blk)[:, None] - jnp.arange(blk)[None, :]
    bias_t = _bias_table(P["rel_bias"], _rel_bucket(rel).astype(jnp.int32))
    o = _swa_prompt(q_t, k2, v_t, bias_t, P["swa_sinks"][0], batch, seq, meta["s_heads"], meta["rep"], blk)
    y = _rows_d(o, x, P, w, w["swa_wo"], "rows_d")
    keep = min(WINDOW, seq)
    kv3 = kv.reshape(batch, seq, 2 * kvw)[:, seq - keep:]
    new_k = kv3[:, :, :kvw].reshape(batch, keep, meta["groups"], SWA_HEAD_DIM)
    new_v = kv3[:, :, kvw:].reshape(batch, keep, meta["groups"], SWA_HEAD_DIM)
    return (y.reshape(batch, seq, d), lat.reshape(1, batch, seq, c_dim), kr.reshape(1, batch, seq, MLA_ROPE),
            new_k, new_v)


def _trunk_decode(x3, cache_lat, cache_rope, cache_k, cache_v, P, w, meta):
    batch, n_new, d = x3.shape
    heads, c_dim, kvw, s_heads = meta["heads"], meta["c_dim"], meta["kvw"], meta["s_heads"]
    past = cache_lat.shape[2]
    w_c = cache_k.shape[1]
    qpos = past + jnp.arange(n_new)
    assert past % CHUNK == 0 and n_new <= CHUNK and w_c <= WINDOW_CHUNKS * CHUNK and w_c <= past
    t = batch * n_new
    x = x3.reshape(t, d)
    tm = _row_tile(t)
    assert tm % n_new == 0
    tab = jnp.tile(_rope_table(qpos), (tm // n_new, 1))
    x, q, ql, lat, kr = _rows_a(x, P, w, tab, heads, c_dim, absorbed=True)
    ol = _mla_attn_decode(ql, q, cache_lat, jnp.swapaxes(cache_rope, 2, 3), lat, kr, heads, n_new)
    x, kv = _rows_b(ol, x, P, w, meta, decode=True)
    x, qs = _rows_c(x, P, w, w["swa_wq"], "rows_c_decode", transposed=False)
    kpos = jnp.arange(past - w_c, past + n_new)
    bias_h = _bias_table(P["rel_bias"], _rel_bucket(kpos[None, :] - qpos[:, None]).astype(jnp.int32))
    bias = bias_h.reshape(s_heads * n_new, w_c + n_new)
    sink_rows = jnp.repeat(P["swa_sinks"][0], n_new).reshape(s_heads * n_new, 1)
    const2 = lambda a: pl.BlockSpec(a.shape, lambda b: (0, 0), pipeline_mode=pl.Buffered(1))
    o = pl.pallas_call(
        functools.partial(_swa_decode_kernel, kv_w=kvw, groups=meta["groups"], rep=meta["rep"]),
        grid=(batch,),
        in_specs=[pl.BlockSpec((n_new, qs.shape[1]), lambda b: (b, 0)),
                  pl.BlockSpec((None, w_c, kvw), lambda b: (b, 0, 0)),
                  pl.BlockSpec((None, w_c, kvw), lambda b: (b, 0, 0)),
                  pl.BlockSpec((n_new, 2 * kvw), lambda b: (b, 0)),
                  const2(w["rep_kv"]), const2(bias), const2(sink_rows)],
        out_specs=pl.BlockSpec((n_new, qs.shape[1]), lambda b: (b, 0)),
        out_shape=jax.ShapeDtypeStruct(qs.shape, BF16),
        compiler_params=_params(1), name="swa_decode",
    )(qs, cache_k.reshape(batch, w_c, kvw), cache_v.reshape(batch, w_c, kvw), kv, w["rep_kv"], bias, sink_rows)
    y = _rows_d(o, x, P, w, w["swa_wo"], "rows_d_decode")
    kv3 = kv.reshape(batch, n_new, 2 * kvw)
    new_k = kv3[:, :, :kvw].reshape(batch, n_new, meta["groups"], SWA_HEAD_DIM)
    new_v = kv3[:, :, kvw:].reshape(batch, n_new, meta["groups"], SWA_HEAD_DIM)
    return (y.reshape(batch, n_new, d), lat.reshape(1, batch, n_new, c_dim), kr.reshape(1, batch, n_new, MLA_ROPE),
            new_k, new_v)


def kernel(x_prompt, x_sample, cache_mla_latent, cache_mla_krope, cache_swa_k, cache_swa_v, ffn_norm1, ffn1_w_gate, ffn1_w_up, ffn1_w_down, mix_norm, ffn_norm2, ffn2_w_gate, ffn2_w_up, ffn2_w_down, mla_w_dq, mla_q_norm, mla_w_uq, mla_w_dkv, mla_kv_norm, mla_w_uk, mla_w_uv, mla_w_o, kv_norm, w_kv_shared, swa_w_q, swa_sinks, swa_w_o, rel_bias, final_norm):
    assert ffn_norm1.shape[0] == 2 and mla_w_dq.shape[0] == 1 and swa_w_q.shape[0] == 1
    scale = (MLA_NOPE + MLA_ROPE) ** -0.5 * LOG2E
    w, meta = _prep_weights(mla_w_dq[0], mla_w_uq[0], mla_w_dkv[0], mla_w_uk[0], mla_w_uv[0], mla_w_o[0],
                            w_kv_shared, swa_w_q[0], swa_w_o[0], scale)
    w.update(f1g=ffn1_w_gate.astype(BF16), f1u=ffn1_w_up.astype(BF16), f1d=ffn1_w_down.astype(BF16),
             f2g=ffn2_w_gate.astype(BF16), f2u=ffn2_w_up.astype(BF16), f2d=ffn2_w_down.astype(BF16))
    P = dict(ffn_norm1=ffn_norm1, mix_norm=mix_norm, ffn_norm2=ffn_norm2, mla_q_norm=mla_q_norm,
             mla_kv_norm=mla_kv_norm, kv_norm=kv_norm, swa_sinks=swa_sinks, rel_bias=rel_bias, final_norm=final_norm)
    y_p, lat_p, rope_p, k_p, v_p = _trunk_prompt(x_prompt, P, w, meta)
    y_s, lat_s, rope_s, k_s, v_s = _trunk_decode(x_sample, cache_mla_latent, cache_mla_krope,
                                                 cache_swa_k, cache_swa_v, P, w, meta)
    return (y_p, y_s, lat_p, rope_p, k_p, v_p, lat_s, rope_s, k_s, v_s)
```

```python
import functools
import math

import jax
import jax.numpy as jnp
from jax import lax
from jax.experimental import pallas as pl
from jax.experimental.pallas import tpu as pltpu

F32 = jnp.float32
BF16 = jnp.bfloat16

CHUNK = 64
RMS_EPS = 1e-6
FFN_RES = 0.5
ROPE_BASE = 10000.0
WINDOW = 128
WINDOW_CHUNKS = WINDOW // CHUNK
N_BUCKETS = 32
MAX_DISTANCE = 128
NEG_INF = -1e30
LOG2E = math.log2(math.e)
MLA_NOPE = 64
MLA_ROPE = 32
MLA_V = 64
MLA_VT_ROWS = 80
SWA_HEAD_DIM = 64

LANES = 128
ROW_TILE = 512
MLA_BQ = 512
MLA_BK = 512
SWA_BLOCKS_PER_STEP = 4
MLA_DIAG = 256
MLA_DECODE_CHUNK = 1024
MLA_HEADS_PER_STEP = 8
VMEM_LIMIT = 60 * 1024 * 1024


def _params(n_axes, vmem=VMEM_LIMIT):
    return pltpu.CompilerParams(dimension_semantics=("arbitrary",) * n_axes, vmem_limit_bytes=vmem)


def _rms(xf, g):
    return xf * lax.rsqrt(jnp.mean(xf * xf, axis=-1, keepdims=True) + RMS_EPS) * g


def _dot(a, b):
    return jnp.dot(a, b, preferred_element_type=F32)


def _dot_nt(a, b):
    return lax.dot_general(a, b, (((1,), (1,)), ((), ())), preferred_element_type=F32)


def _row_tile(t):
    tm = min(ROW_TILE, t)
    assert t % tm == 0, (t, tm)
    return tm


def _tile_index(kind, n):
    return (lambda i: jnp.minimum(i, n - 1)) if kind == "now" else (lambda i: jnp.maximum(i - 1, 0))


def _rows(a, tm, tile=lambda i: i):
    return a, pl.BlockSpec((tm, a.shape[1]), lambda i: (tile(i), 0))


def _const(a):
    nd = a.ndim
    return a, pl.BlockSpec(a.shape, lambda i: (0,) * nd, pipeline_mode=pl.Buffered(1))


def _layer(a, layer):
    nd = a.ndim - 1
    return a, pl.BlockSpec((None,) + tuple(a.shape[1:]), lambda i: (layer,) + (0,) * nd,
                           pipeline_mode=pl.Buffered(1))


def _cycle(a, tm, tile=lambda i: i):
    assert a.shape[0] % tm == 0
    nb = a.shape[0] // tm
    return a, pl.BlockSpec((tm, a.shape[1]), lambda i: (tile(i) % nb, 0))


def _rows_out(t, tm, ncols, dtype, tile=lambda i: i):
    return jax.ShapeDtypeStruct((t, ncols), dtype), pl.BlockSpec((tm, ncols), lambda i: (tile(i), 0))


def _rows_call(body, name, steps, ins, outs, scratch=()):
    return pl.pallas_call(
        body, grid=(steps,),
        in_specs=[s for _, s in ins], out_specs=[s for _, s in outs], out_shape=[o for o, _ in outs],
        scratch_shapes=list(scratch), compiler_params=_params(1), name=name,
    )(*[a for a, _ in ins])


def _ffn_ins(P, w, which, layer):
    return [_const(P["ffn_norm%d" % which][layer].reshape(1, -1)), _layer(w["f%dg" % which], layer),
            _layer(w["f%du" % which], layer), _layer(w["f%dd" % which], layer)]


def _ffn_apply(x, g_ref, wg_ref, wu_ref, wd_ref):
    h = _rms(x, g_ref[...]).astype(BF16)
    gate = _dot(h, wg_ref[...])
    up = _dot(h, wu_ref[...])
    a = (gate * jax.nn.sigmoid(gate) * up).astype(BF16)
    return x + FFN_RES * _dot(a, wd_ref[...])


def _mla_proj_apply(x, mg_ref, wdq_ref, qn_ref, wq_ref, wlat_ref, kvn_ref, wkr_ref, tab_ref,
                    refs, heads, c_dim, absorbed):
    cos_t, sin_t = tab_ref[:, 0:LANES], tab_ref[:, LANES:2 * LANES]
    lane = lax.broadcasted_iota(jnp.int32, (1, LANES), 1)
    cos_q = jnp.where(lane < MLA_NOPE, 1.0, cos_t)
    first_half = lane < MLA_NOPE + MLA_ROPE // 2

    def swap_halves(v):
        return jnp.where(first_half, pltpu.roll(v, LANES - MLA_ROPE // 2, 1), pltpu.roll(v, MLA_ROPE // 2, 1))

    h = _rms(x, mg_ref[...]).astype(BF16)
    cq = _rms(_dot(h, wdq_ref[...]), qn_ref[...]).astype(BF16)
    qa = _dot(cq, wq_ref[...])
    lat = _rms(_dot(h, wlat_ref[...]), kvn_ref[...])
    kr = _dot(h, wkr_ref[...])
    kr = kr * cos_t + swap_halves(kr) * sin_t
    latb = lat.astype(BF16)
    if absorbed:
        wukd_ref, q_ref, ql_ref, lat_ref, kr_ref = refs
    else:
        wuk_ref, wuvt_ref, q_ref, k_ref, vt_ref, lat_ref, kr_ref = refs
        kn = _dot(latb, wuk_ref[...])
    lat_ref[...] = lat
    kr_ref[...] = kr[:, MLA_NOPE:MLA_NOPE + MLA_ROPE]
    for hh in range(heads):
        sl = slice(hh * LANES, (hh + 1) * LANES)
        qh = (qa[:, sl] * cos_q + swap_halves(qa[:, sl]) * sin_t).astype(BF16)
        q_ref[:, sl] = qh
        if absorbed:
            ql_ref[:, hh * c_dim:(hh + 1) * c_dim] = _dot(qh, wukd_ref[hh]).astype(BF16)
        else:
            k_ref[:, sl] = (kn[:, sl] + kr).astype(BF16)
    if not absorbed:
        vt = _dot_nt(wuvt_ref[...], latb)
        ones_row = lax.broadcasted_iota(jnp.int32, vt.shape, 0) % MLA_VT_ROWS == MLA_V
        vt_ref[...] = jnp.where(ones_row, 1.0, vt).astype(BF16)


def _lagged(n_tiles, prev_sc, main, epilogue):
    i = pl.program_id(0)

    @pl.when(i == 0)
    def _():
        prev_sc[...] = jnp.zeros(prev_sc.shape, F32)

    @pl.when(i < n_tiles)
    def _():
        epilogue(prev_sc[...])
        prev_sc[...] = main()

    @pl.when(i == n_tiles)
    def _():
        epilogue(prev_sc[...])


def _rows_a_kernel(x_ref, fg, wg, wu, wd, mg, wdq, qn, wq, wlat, kvn, wkr, tab, *refs,
                   heads, c_dim, absorbed, n_tiles):
    n_extra = 1 if absorbed else 2
    x1_ref, prev_sc = refs[n_extra], refs[-1]
    proj_refs = refs[:n_extra] + refs[n_extra + 1:-1]

    def main():
        x1 = _ffn_apply(x_ref[...], fg, wg, wu, wd)
        x1_ref[...] = x1
        return x1

    def epilogue(x1):
        _mla_proj_apply(x1, mg, wdq, qn, wq, wlat, kvn, wkr, tab, proj_refs, heads, c_dim, absorbed)

    _lagged(n_tiles, prev_sc, main, epilogue)


def _rows_b_kernel(a_ref, *refs, dec_heads, c_dim, tiled_kv, n_tiles):
    prev_sc, refs = refs[-1], refs[:-1]
    if dec_heads:
        wuv_ref, refs = refs[0], refs[1:]
    wo, x_ref, fg, wg, wu, wd, kvg, wkv = refs[:8]
    if tiled_kv:
        repk, eye, x_out, kv_out, k2_out, vt_out = refs[8:]
    else:
        x_out, kv_out = refs[8:]

    def main():
        if dec_heads:
            a = jnp.concatenate([_dot(a_ref[:, hh * c_dim:(hh + 1) * c_dim], wuv_ref[hh]).astype(BF16)
                                 for hh in range(dec_heads)], axis=1)
        else:
            a = a_ref[...]
        x = _ffn_apply(x_ref[...] + _dot(a, wo[...]), fg, wg, wu, wd)
        x_out[...] = x
        return x

    def epilogue(x):
        kv = _dot(_rms(x, kvg[...]).astype(BF16), wkv[...])
        kv_out[...] = kv
        if tiled_kv:
            half = kv.shape[1] // 2
            k2_out[...] = _dot(kv[:, :half].astype(BF16), repk[...]).astype(BF16)
            vt_out[...] = _dot_nt(eye[...], kv[:, half:].astype(BF16)).astype(BF16)

    _lagged(n_tiles, prev_sc, main, epilogue)


def _rows_c_kernel(x_ref, fg, wg, wu, wd, mg, wq, x_out, q_out, prev_sc, *, scale, transposed, n_tiles):
    def main():
        x = _ffn_apply(x_ref[...], fg, wg, wu, wd)
        x_out[...] = x
        return x

    def epilogue(x):
        h = _rms(x, mg[...]).astype(BF16)
        q = _dot_nt(wq[...], h) if transposed else _dot(h, wq[...])
        q_out[...] = (q * scale).astype(BF16)

    _lagged(n_tiles, prev_sc, main, epilogue)


def _rows_d_kernel(a_ref, wo, x_ref, fg, wg, wu, wd, fin_g, y_out, prev_sc, *, n_tiles):
    def main():
        return _ffn_apply(x_ref[...] + _dot(a_ref[...], wo[...]), fg, wg, wu, wd)

    def epilogue(x):
        y_out[...] = _rms(x, fin_g[...])

    _lagged(n_tiles, prev_sc, main, epilogue)


def _mla_attn_kernel(q_ref, k_ref, vt_ref, o_ref, m_sc, acc_sc, sa_sc, sb_sc, *, bq, bk, hp):
    i = pl.program_id(2)
    qs = [q_ref[:, hh * LANES:(hh + 1) * LANES] for hh in range(hp)]
    sls = [slice(hh * LANES, (hh + 1) * LANES) for hh in range(hp)]

    def own_scores(start, width, q_lo):
        ss = [_dot_nt(k_ref[pl.ds(start, width), sls[hh]], qs[hh][q_lo:]) for hh in range(hp)]
        kc = (start + lax.broadcasted_iota(jnp.int32, (width, 1), 0)) // CHUNK
        qc = (i * bq + q_lo + lax.broadcasted_iota(jnp.int32, (1, bq - q_lo), 1)) // CHUNK
        return [jnp.where(kc <= qc, s, NEG_INF) for s in ss]

    def own_update(ss, start, width, q_lo, after_head=None):
        ql = slice(q_lo, bq)
        for hh in range(hp):
            m = m_sc[hh, :, ql]
            m_new = jnp.maximum(m, jnp.max(ss[hh], axis=0, keepdims=True))
            alpha = jnp.exp2(m - m_new)
            p = jnp.exp2(ss[hh] - m_new).astype(BF16)
            vt = vt_ref[hh * MLA_VT_ROWS:(hh + 1) * MLA_VT_ROWS, pl.ds(start, width)]
            acc_sc[hh, :, ql] = alpha * acc_sc[hh, :, ql] + _dot(vt, p)
            m_sc[hh, :, ql] = m_new
            if after_head is not None:
                after_head(hh)

    def produce(hh, j, dst):
        dst[hh] = _dot_nt(k_ref[pl.ds(pl.multiple_of(j * bk, bk), bk), sls[hh]], qs[hh])

    def consume(hh, j, src):
        s = src[hh]
        m = m_sc[hh]
        m_new = jnp.maximum(m, jnp.max(s, axis=0, keepdims=True))
        alpha = jnp.exp2(m - m_new)
        p = jnp.exp2(s - m_new).astype(BF16)
        vt = vt_ref[hh * MLA_VT_ROWS:(hh + 1) * MLA_VT_ROWS, pl.ds(pl.multiple_of(j * bk, bk), bk)]
        acc_sc[hh] = alpha * acc_sc[hh] + _dot(vt, p)
        m_sc[hh] = m_new

    def transition(j, src, dst):
        produce(0, j + 1, dst)
        for hh in range(hp):
            consume(hh, j, src)
            if hh + 1 < hp:
                produce(hh + 1, j + 1, dst)

    m_sc[...] = jnp.full(m_sc.shape, NEG_INF, F32)
    acc_sc[...] = jnp.zeros(acc_sc.shape, F32)
    n_full = (i * bq) // bk

    diag = min(MLA_DIAG, bq)
    starts = [pl.multiple_of(i * bq + t * diag, diag) for t in range(bq // diag)]
    own = [own_scores(starts[t], diag, t * diag) for t in range(bq // diag)]
    for t in range(bq // diag):
        own_update(own[t], starts[t], diag, t * diag,
                   after_head=(lambda hh: produce(hh, 0, sa_sc)) if t == 0 else None)

    @pl.when(n_full > 0)
    def _():
        last = n_full - 1

        @pl.loop(0, last // 2)
        def _(t):
            transition(2 * t, sa_sc, sb_sc)
            transition(2 * t + 1, sb_sc, sa_sc)

        @pl.when(last % 2 == 1)
        def _():
            transition(last - 1, sa_sc, sb_sc)
            for hh in range(hp):
                consume(hh, last, sb_sc)

        @pl.when(last % 2 == 0)
        def _():
            for hh in range(hp):
                consume(hh, last, sa_sc)

    for pair in range(hp // 2):
        halves = []
        for hh in (2 * pair, 2 * pair + 1):
            acc = acc_sc[hh]
            halves.append(acc[:MLA_V] * (1.0 / acc[MLA_V:MLA_V + 1]))
        o_ref[:, pair * LANES:(pair + 1) * LANES] = jnp.concatenate(halves, axis=0).T.astype(BF16)


def _mla_attn_prompt(q, k, vt, batch, seq, heads, bq=MLA_BQ, bk=MLA_BK, hp=MLA_HEADS_PER_STEP):
    bq, bk = min(bq, seq), min(bk, seq)
    nq = seq // bq
    assert seq % bq == 0 and seq % bk == 0 and bq % CHUNK == 0 and heads % hp == 0 and hp % 2 == 0
    assert bq % bk == 0 and bq % min(MLA_DIAG, bq) == 0 and MLA_DIAG % CHUNK == 0
    w = hp * LANES
    return pl.pallas_call(
        functools.partial(_mla_attn_kernel, bq=bq, bk=bk, hp=hp),
        grid=(batch, heads // hp, nq),
        in_specs=[pl.BlockSpec((bq, w), lambda b, h, i: (b * nq + i, h)),
                  pl.BlockSpec((seq, w), lambda b, h, i: (b, h)),
                  pl.BlockSpec((hp * MLA_VT_ROWS, seq), lambda b, h, i: (h, b))],
        out_specs=pl.BlockSpec((bq, hp * MLA_V), lambda b, h, i: (b * nq + i, h)),
        out_shape=jax.ShapeDtypeStruct((q.shape[0], heads * MLA_V), BF16),
        scratch_shapes=[pltpu.VMEM((hp, 1, bq), F32), pltpu.VMEM((hp, MLA_VT_ROWS, bq), F32),
                        pltpu.VMEM((hp, bk, bq), F32), pltpu.VMEM((hp, bk, bq), F32)],
        compiler_params=_params(3), name="mla_attn_prompt",
    )(q, k, vt)


def _mla_attn_decode_kernel(ql_ref, q_ref, cl_ref, crt_ref, nl_ref, nr_ref, o_ref, *, heads, c_dim):
    n = ql_ref.shape[0]
    ql = jnp.concatenate([ql_ref[:, hh * c_dim:(hh + 1) * c_dim] for hh in range(heads)], axis=0)
    qr = jnp.concatenate([q_ref[:, hh * LANES + MLA_NOPE:hh * LANES + MLA_NOPE + MLA_ROPE] for hh in range(heads)],
                         axis=0)
    past = cl_ref.shape[0]
    chunk = min(past, MLA_DECODE_CHUNK)
    parts = [(cl_ref[c:c + chunk, :], crt_ref[:, c:c + chunk], _dot) for c in range(0, past, chunk)]
    parts.append((nl_ref[...], nr_ref[...], _dot_nt))
    m = jnp.full((heads * n, 1), NEG_INF, F32)
    l = jnp.zeros((heads * n, 1), F32)
    acc = jnp.zeros((heads * n, c_dim), F32)
    def scores(part):
        kl, kr, rope_dot = part
        kl, kr = kl.astype(BF16), kr.astype(BF16)
        return kl, _dot_nt(ql, kl) + rope_dot(qr, kr)

    nxt = scores(parts[0])
    for c in range(len(parts)):
        kl, s = nxt
        if c + 1 < len(parts):
            nxt = scores(parts[c + 1])
        m_new = jnp.maximum(m, jnp.max(s, axis=-1, keepdims=True))
        alpha = jnp.exp2(m - m_new)
        p = jnp.exp2(s - m_new)
        l = alpha * l + jnp.sum(p, axis=-1, keepdims=True)
        acc = alpha * acc + _dot(p.astype(BF16), kl)
        m = m_new
    o = acc / l
    for hh in range(heads):
        o_ref[:, hh * c_dim:(hh + 1) * c_dim] = o[hh * n:(hh + 1) * n].astype(BF16)


def _mla_attn_decode(ql, q, cache_lat, cache_rope_t, new_lat, new_rope, heads, n_new):
    _, batch, past, c_dim = cache_lat.shape
    r_dim = cache_rope_t.shape[2]
    return pl.pallas_call(
        functools.partial(_mla_attn_decode_kernel, heads=heads, c_dim=c_dim),
        grid=(batch,),
        in_specs=[pl.BlockSpec((n_new, heads * c_dim), lambda b: (b, 0)),
                  pl.BlockSpec((n_new, heads * LANES), lambda b: (b, 0)),
                  pl.BlockSpec((None, None, past, c_dim), lambda b: (0, b, 0, 0)),
                  pl.BlockSpec((None, None, r_dim, past), lambda b: (0, b, 0, 0)),
                  pl.BlockSpec((n_new, c_dim), lambda b: (b, 0)),
                  pl.BlockSpec((n_new, r_dim), lambda b: (b, 0))],
        out_specs=pl.BlockSpec((n_new, heads * c_dim), lambda b: (b, 0)),
        out_shape=jax.ShapeDtypeStruct(ql.shape, BF16),
        compiler_params=_params(1), name="mla_attn_decode",
    )(ql, q, cache_lat, cache_rope_t, new_lat, new_rope)


def _bias_kernel(tab_ref, idx_ref, o_ref, *, heads):
    idx = idx_ref[...]
    for hh in range(heads):
        acc = jnp.zeros(idx.shape, F32)
        for b in range(N_BUCKETS):
            acc = jnp.where(idx == b, tab_ref[b, hh] * LOG2E, acc)
        o_ref[hh] = acc


def _bias_table(rel_bias, idx):
    heads = rel_bias.shape[1]
    return pl.pallas_call(
        functools.partial(_bias_kernel, heads=heads),
        in_specs=[pl.BlockSpec(memory_space=pltpu.SMEM), pl.BlockSpec(idx.shape, lambda: (0, 0))],
        out_specs=pl.BlockSpec((heads,) + idx.shape, lambda: (0, 0, 0)),
        out_shape=jax.ShapeDtypeStruct((heads,) + idx.shape, F32), name="rel_bias_table",
    )(rel_bias, idx)


def _rel_bucket(rel):
    half = N_BUCKETS // 2
    max_exact = half // 2
    base = jnp.where(rel > 0, half, 0)
    n = jnp.abs(rel)
    nf = jnp.maximum(n, 1).astype(jnp.float32)
    large = max_exact + (jnp.log(nf / max_exact) / math.log(MAX_DISTANCE / max_exact)
                         * (half - max_exact)).astype(jnp.int32)
    large = jnp.minimum(large, half - 1)
    return base + jnp.where(n < max_exact, n, large)


def _swa_prompt_kernel(sink_ref, qt_ref, kp_ref, kc_ref, vtp_ref, vtc_ref, bias_ref, o_ref, *, s_heads, rep, blk, nsub):
    i = pl.program_id(1)
    key_chunk = lax.broadcasted_iota(jnp.int32, (2 * blk, 1), 0) // CHUNK
    row_chunk = lax.broadcasted_iota(jnp.int32, (1, blk), 1) // CHUNK
    first = blk // CHUNK
    window = (key_chunk >= row_chunk) & (key_chunk <= row_chunk + WINDOW_CHUNKS)
    dh = SWA_HEAD_DIM
    units = [(u, hh) for u in range(nsub) for hh in range(s_heads)]
    ss = []
    for u, hh in units:
        g, pair, slot = hh // rep, hh // 2, hh % 2
        ksl = slice(g * 4 * dh + slot * 2 * dh, g * 4 * dh + (slot + 1) * 2 * dh)
        k = jnp.concatenate([kp_ref[:, ksl], kc_ref[:, ksl]], axis=0)[u * blk:(u + 2) * blk]
        ss.append(_dot(k, qt_ref[pair * 2 * dh:(pair + 1) * 2 * dh, u * blk:(u + 1) * blk]))
    ps, invs = [], []
    for (u, hh), s in zip(units, ss):
        valid = window if u > 0 else window & ((i > 0) | (key_chunk >= first))
        s = jnp.where(valid, s + bias_ref[hh], NEG_INF)
        sink = sink_ref[hh] * LOG2E
        m = jnp.maximum(jnp.max(s, axis=0, keepdims=True), sink)
        p = jnp.exp2(s - m)
        invs.append(1.0 / (jnp.sum(p, axis=0, keepdims=True) + jnp.exp2(sink - m)))
        ps.append(p.astype(BF16))
    outs = []
    for n, (u, hh) in enumerate(units):
        g = hh // rep
        vt = jnp.concatenate([vtp_ref[g * dh:(g + 1) * dh, :], vtc_ref[g * dh:(g + 1) * dh, :]],
                             axis=1)[:, u * blk:(u + 2) * blk]
        outs.append(_dot(vt, ps[n]) * invs[n])
    for u in range(nsub):
        for pair in range(s_heads // 2):
            n = u * s_heads + 2 * pair
            o_ref[u * blk:(u + 1) * blk, pair * 2 * dh:(pair + 1) * 2 * dh] = (
                jnp.concatenate(outs[n:n + 2], axis=0).T.astype(BF16))


def _swa_prompt(qt, k2, vt, bias_t, sinks, batch, seq, s_heads, rep, blk=128, nsub=SWA_BLOCKS_PER_STEP):
    assert seq % (blk * nsub) == 0
    nq = seq // (blk * nsub)
    kw = k2.shape[1]
    cur = lambda b, i: (b * nq + i, 0)
    prev = lambda b, i: ((b * nq + i) * nsub - jnp.minimum(i, 1), 0)
    cur_t = lambda b, i: (0, b * nq + i)
    prev_t = lambda b, i: (0, (b * nq + i) * nsub - jnp.minimum(i, 1))
    assert rep == 4 and kw == (s_heads // rep) * 4 * SWA_HEAD_DIM
    return pl.pallas_call(
        functools.partial(_swa_prompt_kernel, s_heads=s_heads, rep=rep, blk=blk, nsub=nsub),
        grid=(batch, nq),
        in_specs=[pl.BlockSpec(memory_space=pltpu.SMEM),
                  pl.BlockSpec((qt.shape[0], blk * nsub), cur_t),
                  pl.BlockSpec((blk, kw), prev), pl.BlockSpec((blk * nsub, kw), cur),
                  pl.BlockSpec((vt.shape[0], blk), prev_t), pl.BlockSpec((vt.shape[0], blk * nsub), cur_t),
                  pl.BlockSpec(bias_t.shape, lambda b, i: (0, 0, 0), pipeline_mode=pl.Buffered(1))],
        out_specs=pl.BlockSpec((blk * nsub, qt.shape[0]), cur),
        out_shape=jax.ShapeDtypeStruct((qt.shape[1], qt.shape[0]), BF16),
        compiler_params=_params(2), name="swa_prompt",
    )(sinks, qt, k2, k2, vt, vt, bias_t)


def _swa_decode_kernel(q_ref, ck_ref, cv_ref, nkv_ref, rep_ref, bias_ref, sink_ref, o_ref, *, kv_w, groups, rep):
    n = q_ref.shape[0]
    gw = rep * SWA_HEAD_DIM
    k = jnp.concatenate([ck_ref[...], nkv_ref[:, 0:kv_w]], axis=0).astype(BF16)
    v = jnp.concatenate([cv_ref[...], nkv_ref[:, kv_w:2 * kv_w]], axis=0).astype(BF16)
    k4 = _dot(k, rep_ref[...]).astype(BF16)
    v4 = _dot(v, rep_ref[...]).astype(BF16)
    lane_head = lax.broadcasted_iota(jnp.int32, (1, gw), 1) // SWA_HEAD_DIM
    sls = [slice(g * gw, (g + 1) * gw) for g in range(groups)]
    ss = []
    for g in range(groups):
        qg = q_ref[:, sls[g]]
        qs = jnp.concatenate([jnp.where(lane_head == r, qg, jnp.zeros_like(qg)) for r in range(rep)], axis=0)
        ss.append(_dot_nt(qs, k4[:, sls[g]]))
    ps = []
    for g in range(groups):
        rows = slice(g * rep * n, (g + 1) * rep * n)
        s = ss[g] + bias_ref[rows, :]
        sink = sink_ref[rows, :] * LOG2E
        m = jnp.maximum(jnp.max(s, axis=-1, keepdims=True), sink)
        p = jnp.exp2(s - m)
        inv = 1.0 / (jnp.sum(p, axis=-1, keepdims=True) + jnp.exp2(sink - m))
        ps.append((p * inv).astype(BF16))
    for g in range(groups):
        res = _dot(ps[g], v4[:, sls[g]])
        og = jnp.zeros((n, gw), F32)
        for r in range(rep):
            og = jnp.where(lane_head == r, res[r * n:(r + 1) * n], og)
        o_ref[:, sls[g]] = og.astype(BF16)


def _rope_table(pos):
    inv = ROPE_BASE ** (-jnp.arange(0, MLA_ROPE, 2, dtype=jnp.float32) / MLA_ROPE)
    ang = pos.astype(jnp.float32)[:, None] * inv[None, :]
    cos, sin = jnp.cos(ang), jnp.sin(ang)
    widths = ((0, 0), (MLA_NOPE, LANES - MLA_NOPE - MLA_ROPE))
    return jnp.concatenate([jnp.pad(jnp.concatenate([cos, cos], axis=1), widths),
                            jnp.pad(jnp.concatenate([-sin, sin], axis=1), widths)], axis=1)


def _prep_weights(mla_w_dq, mla_w_uq, mla_w_dkv, mla_w_uk, mla_w_uv, mla_w_o, w_kv_shared, swa_w_q, swa_w_o, scale):
    ql, qcols = mla_w_uq.shape
    c_dim, heads, nope = mla_w_uk.shape
    rope = qcols // heads - nope
    d = mla_w_dq.shape[0]
    pad = LANES - nope - rope
    w = {}
    w["wq"] = jnp.pad(mla_w_uq.reshape(ql, heads, nope + rope), ((0, 0), (0, 0), (0, pad))).reshape(
        ql, heads * LANES) * scale
    w["wdq"] = mla_w_dq
    w["wlat"] = mla_w_dkv[:, :c_dim]
    w["wkr"] = jnp.pad(mla_w_dkv[:, c_dim:], ((0, 0), (nope, pad)))
    zc = jnp.zeros((c_dim, heads, LANES - nope), F32)
    w["wuk"] = jnp.concatenate([mla_w_uk, zc], axis=-1).reshape(c_dim, heads * LANES)
    w["wuv_t"] = jnp.pad(mla_w_uv, ((0, 0), (0, 0), (0, MLA_VT_ROWS - mla_w_uv.shape[2]))).reshape(
        c_dim, heads * MLA_VT_ROWS).T
    w["wuk_dec"] = jnp.concatenate([jnp.transpose(mla_w_uk, (1, 2, 0)),
                                    jnp.zeros((heads, LANES - nope, c_dim), F32)], axis=1)
    w["wuv_dec"] = jnp.concatenate([jnp.transpose(mla_w_uv, (1, 0, 2)),
                                    jnp.zeros((heads, c_dim, LANES - nope), F32)], axis=-1)
    vdim = mla_w_uv.shape[2]
    assert vdim == MLA_V and nope == MLA_NOPE and rope == MLA_ROPE
    w["wo"] = mla_w_o
    w["wo_pad"] = jnp.concatenate([mla_w_o.reshape(heads, vdim, d), jnp.zeros((heads, LANES - vdim, d), F32)],
                                  axis=1).reshape(heads * LANES, d)
    kvw = w_kv_shared.shape[1] // 2
    groups = kvw // SWA_HEAD_DIM
    s_heads = swa_w_q.shape[1] // SWA_HEAD_DIM
    rep = s_heads // groups
    w["wkv"] = w_kv_shared
    eye = jnp.eye(SWA_HEAD_DIM, dtype=F32)
    zero = jnp.zeros_like(eye)
    w["rep_k2"] = jnp.kron(jnp.eye(groups, dtype=F32), jnp.concatenate([eye, zero, zero, eye], axis=1))
    w["eye_kv"] = jnp.eye(kvw, dtype=F32)
    w["swa_wq_t"] = swa_w_q.T
    w["swa_wo"] = swa_w_o
    w["rep_kv"] = jnp.kron(jnp.eye(groups, dtype=F32), jnp.tile(eye, (1, rep)))
    w["swa_wq"] = swa_w_q
    return {k: v.astype(BF16) for k, v in w.items()}, dict(heads=heads, c_dim=c_dim, groups=groups, rep=rep,
                                                            s_heads=s_heads, kvw=kvw)


def _rows_a(x, P, w, tab, heads, c_dim, absorbed):
    t, d = x.shape
    tm = _row_tile(t)
    hw = heads * LANES
    row = lambda v: v.reshape(1, -1)
    n = t // tm
    now, lag = _tile_index("now", n), _tile_index("lag", n)
    ins = ([_rows(x, tm, now)] + _ffn_ins(P, w, 1, 0)
           + [_const(row(P["mix_norm"][0])), _const(w["wdq"]), _const(row(P["mla_q_norm"][0])), _const(w["wq"]),
              _const(w["wlat"]), _const(row(P["mla_kv_norm"][0])), _const(w["wkr"]), _cycle(tab, tm, lag)]
           + ([_const(w["wuk_dec"])] if absorbed else [_const(w["wuk"]), _const(w["wuv_t"])]))
    outs = [_rows_out(t, tm, d, F32, now), _rows_out(t, tm, hw, BF16, lag)]
    if absorbed:
        outs.append(_rows_out(t, tm, heads * c_dim, BF16, lag))
    else:
        outs += [_rows_out(t, tm, hw, BF16, lag),
                 (jax.ShapeDtypeStruct((heads * MLA_VT_ROWS, t), BF16),
                  pl.BlockSpec((heads * MLA_VT_ROWS, tm), lambda i: (0, lag(i))))]
    outs += [_rows_out(t, tm, c_dim, F32, lag), _rows_out(t, tm, MLA_ROPE, F32, lag)]
    return _rows_call(functools.partial(_rows_a_kernel, heads=heads, c_dim=c_dim, absorbed=absorbed, n_tiles=n),
                      "rows_a_absorbed" if absorbed else "rows_a", n + 1, ins, outs,
                      scratch=[pltpu.VMEM((tm, d), F32)])


def _rows_b(a, x, P, w, meta, decode):
    t, d = x.shape
    tm = _row_tile(t)
    n = t // tm
    now, lag = _tile_index("now", n), _tile_index("lag", n)
    kvw = meta["kvw"]
    ins = [_rows(a, tm, now)]
    if decode:
        ins += [_const(w["wuv_dec"]), _const(w["wo_pad"])]
    else:
        ins += [_const(w["wo"])]
    ins += [_rows(x, tm, now)] + _ffn_ins(P, w, 2, 0) + [_const(P["kv_norm"].reshape(1, -1)), _const(w["wkv"])]
    outs = [_rows_out(t, tm, d, F32, now), _rows_out(t, tm, 2 * kvw, F32, lag)]
    if not decode:
        ins += [_const(w["rep_k2"]), _const(w["eye_kv"])]
        outs += [_rows_out(t, tm, w["rep_k2"].shape[1], BF16, lag),
                 (jax.ShapeDtypeStruct((kvw, t), BF16), pl.BlockSpec((kvw, tm), lambda i: (0, lag(i))))]
    return _rows_call(functools.partial(_rows_b_kernel, dec_heads=meta["heads"] if decode else 0,
                                        c_dim=meta["c_dim"], tiled_kv=not decode, n_tiles=n),
                      "rows_b_decode" if decode else "rows_b", n + 1, ins, outs, scratch=[pltpu.VMEM((tm, d), F32)])


def _rows_c(x, P, w, wq, name, transposed):
    t, d = x.shape
    tm = _row_tile(t)
    n = t // tm
    now, lag = _tile_index("now", n), _tile_index("lag", n)
    ins = [_rows(x, tm, now)] + _ffn_ins(P, w, 1, 1) + [_const(P["mix_norm"][1].reshape(1, -1)), _const(wq)]
    if transposed:
        q_out = (jax.ShapeDtypeStruct((wq.shape[0], t), BF16),
                 pl.BlockSpec((wq.shape[0], tm), lambda i: (0, lag(i))))
    else:
        q_out = _rows_out(t, tm, wq.shape[1], BF16, lag)
    return _rows_call(functools.partial(_rows_c_kernel, scale=SWA_HEAD_DIM ** -0.5 * LOG2E, transposed=transposed,
                                        n_tiles=n),
                      name, n + 1, ins, [_rows_out(t, tm, d, F32, now), q_out], scratch=[pltpu.VMEM((tm, d), F32)])


def _rows_d(a, x, P, w, wo, name):
    t, d = x.shape
    tm = _row_tile(t)
    n = t // tm
    now, lag = _tile_index("now", n), _tile_index("lag", n)
    ins = ([_rows(a, tm, now), _const(wo), _rows(x, tm, now)] + _ffn_ins(P, w, 2, 1)
           + [_const(P["final_norm"].reshape(1, -1))])
    return _rows_call(functools.partial(_rows_d_kernel, n_tiles=n), name, n + 1, ins,
                      [_rows_out(t, tm, d, F32, lag)], scratch=[pltpu.VMEM((tm, d), F32)])[0]


def _trunk_prompt(x3, P, w, meta):
    batch, seq, d = x3.shape
    heads, c_dim, kvw = meta["heads"], meta["c_dim"], meta["kvw"]
    x = x3.reshape(batch * seq, d)
    x, q, k, vt, lat, kr = _rows_a(x, P, w, _rope_table(jnp.arange(seq)), heads, c_dim, absorbed=False)
    o = _mla_attn_prompt(q, k, vt, batch, seq, heads)
    x, kv, k2, v_t = _rows_b(o, x, P, w, meta, decode=False)
    x, q_t = _rows_c(x, P, w, w["swa_wq_t"], "rows_c", transposed=True)
    blk = 2 * CHUNK
    rel = (jnp.arange(2 * blk) - blk)[:, None] - jnp.arange(blk)[None, :]
    bias_t = _bias_table(P["rel_bias"], _rel_bucket(rel).astype(jnp.int32))
    o = _swa_prompt(q_t, k2, v_t, bias_t, P["swa_sinks"][0], batch, seq, meta["s_heads"], meta["rep"], blk)
    y = _rows_d(o, x, P, w, w["swa_wo"], "rows_d")
    keep = min(WINDOW, seq)
    kv3 = kv.reshape(batch, seq, 2 * kvw)[:, seq - keep:]
    new_k = kv3[:, :, :kvw].reshape(batch, keep, meta["groups"], SWA_HEAD_DIM)
    new_v = kv3[:, :, kvw:].reshape(batch, keep, meta["groups"], SWA_HEAD_DIM)
    return (y.reshape(batch, seq, d), lat.reshape(1, batch, seq, c_dim), kr.reshape(1, batch, seq, MLA_ROPE),
            new_k, new_v)


def _trunk_decode(x3, cache_lat, cache_rope, cache_k, cache_v, P, w, meta):
    batch, n_new, d = x3.shape
    heads, c_dim, kvw, s_heads = meta["heads"], meta["c_dim"], meta["kvw"], meta["s_heads"]
    past = cache_lat.shape[2]
    w_c = cache_k.shape[1]
    qpos = past + jnp.arange(n_new)
    assert past % CHUNK == 0 and n_new <= CHUNK and w_c <= WINDOW_CHUNKS * CHUNK and w_c <= past
    t = batch * n_new
    x = x3.reshape(t, d)
    tm = _row_tile(t)
    assert tm % n_new == 0
    tab = jnp.tile(_rope_table(qpos), (tm // n_new, 1))
    x, q, ql, lat, kr = _rows_a(x, P, w, tab, heads, c_dim, absorbed=True)
    ol = _mla_attn_decode(ql, q, cache_lat, jnp.swapaxes(cache_rope, 2, 3), lat, kr, heads, n_new)
    x, kv = _rows_b(ol, x, P, w, meta, decode=True)
    x, qs = _rows_c(x, P, w, w["swa_wq"], "rows_c_decode", transposed=False)
    kpos = jnp.arange(past - w_c, past + n_new)
    bias_h = _bias_table(P["rel_bias"], _rel_bucket(kpos[None, :] - qpos[:, None]).astype(jnp.int32))
    bias = bias_h.reshape(s_heads * n_new, w_c + n_new)
    sink_rows = jnp.repeat(P["swa_sinks"][0], n_new).reshape(s_heads * n_new, 1)
    const2 = lambda a: pl.BlockSpec(a.shape, lambda b: (0, 0), pipeline_mode=pl.Buffered(1))
    o = pl.pallas_call(
        functools.partial(_swa_decode_kernel, kv_w=kvw, groups=meta["groups"], rep=meta["rep"]),
        grid=(batch,),
        in_specs=[pl.BlockSpec((n_new, qs.shape[1]), lambda b: (b, 0)),
                  pl.BlockSpec((None, w_c, kvw), lambda b: (b, 0, 0)),
                  pl.BlockSpec((None, w_c, kvw), lambda b: (b, 0, 0)),
                  pl.BlockSpec((n_new, 2 * kvw), lambda b: (b, 0)),
                  const2(w["rep_kv"]), const2(bias), const2(sink_rows)],
        out_specs=pl.BlockSpec((n_new, qs.shape[1]), lambda b: (b, 0)),
        out_shape=jax.ShapeDtypeStruct(qs.shape, BF16),
        compiler_params=_params(1), name="swa_decode",
    )(qs, cache_k.reshape(batch, w_c, kvw), cache_v.reshape(batch, w_c, kvw), kv, w["rep_kv"], bias, sink_rows)
    y = _rows_d(o, x, P, w, w["swa_wo"], "rows_d_decode")
    kv3 = kv.reshape(batch, n_new, 2 * kvw)
    new_k = kv3[:, :, :kvw].reshape(batch, n_new, meta["groups"], SWA_HEAD_DIM)
    new_v = kv3[:, :, kvw:].reshape(batch, n_new, meta["groups"], SWA_HEAD_DIM)
    return (y.reshape(batch, n_new, d), lat.reshape(1, batch, n_new, c_dim), kr.reshape(1, batch, n_new, MLA_ROPE),
            new_k, new_v)


def kernel(x_prompt, x_sample, cache_mla_latent, cache_mla_krope, cache_swa_k, cache_swa_v, ffn_norm1, ffn1_w_gate, ffn1_w_up, ffn1_w_down, mix_norm, ffn_norm2, ffn2_w_gate, ffn2_w_up, ffn2_w_down, mla_w_dq, mla_q_norm, mla_w_uq, mla_w_dkv, mla_kv_norm, mla_w_uk, mla_w_uv, mla_w_o, kv_norm, w_kv_shared, swa_w_q, swa_sinks, swa_w_o, rel_bias, final_norm):
    assert ffn_norm1.shape[0] == 2 and mla_w_dq.shape[0] == 1 and swa_w_q.shape[0] == 1
    scale = (MLA_NOPE + MLA_ROPE) ** -0.5 * LOG2E
    w, meta = _prep_weights(mla_w_dq[0], mla_w_uq[0], mla_w_dkv[0], mla_w_uk[0], mla_w_uv[0], mla_w_o[0],
                            w_kv_shared, swa_w_q[0], swa_w_o[0], scale)
    w.update(f1g=ffn1_w_gate.astype(BF16), f1u=ffn1_w_up.astype(BF16), f1d=ffn1_w_down.astype(BF16),
             f2g=ffn2_w_gate.astype(BF16), f2u=ffn2_w_up.astype(BF16), f2d=ffn2_w_down.astype(BF16))
    P = dict(ffn_norm1=ffn_norm1, mix_norm=mix_norm, ffn_norm2=ffn_norm2, mla_q_norm=mla_q_norm,
             mla_kv_norm=mla_kv_norm, kv_norm=kv_norm, swa_sinks=swa_sinks, rel_bias=rel_bias, final_norm=final_norm)
    y_p, lat_p, rope_p, k_p, v_p = _trunk_prompt(x_prompt, P, w, meta)
    y_s, lat_s, rope_s, k_s, v_s = _trunk_decode(x_sample, cache_mla_latent, cache_mla_krope,
                                                 cache_swa_k, cache_swa_v, P, w, meta)
    return (y_p, y_s, lat_p, rope_p, k_p, v_p, lat_s, rope_s, k_s, v_s)
```

```python
import functools
import math

import jax
import jax.numpy as jnp
from jax import lax
from jax.experimental import pallas as pl
from jax.experimental.pallas import tpu as pltpu

F32 = jnp.float32
BF16 = jnp.bfloat16

CHUNK = 64
RMS_EPS = 1e-6
FFN_RES = 0.5
ROPE_BASE = 10000.0
WINDOW = 128
WINDOW_CHUNKS = WINDOW // CHUNK
N_BUCKETS = 32
MAX_DISTANCE = 128
NEG_INF = -1e30
LOG2E = math.log2(math.e)
MLA_NOPE = 64
MLA_ROPE = 32
MLA_V = 64
MLA_VT_ROWS = 80
SWA_HEAD_DIM = 64

LANES = 128
ROW_TILE = 512
MLA_BQ = 512
MLA_BK = 512
MLA_DIAG = 256
MLA_HEADS_PER_STEP = 8
MLA_DECODE_CHUNK = 1024
MLA_DECODE_BATCH = 2
SWA_BLOCKS_PER_STEP = 4
VMEM_LIMIT = 60 * 1024 * 1024


def _params(n_axes):
    return pltpu.CompilerParams(dimension_semantics=("arbitrary",) * n_axes, vmem_limit_bytes=VMEM_LIMIT)


def _rms(xf, g):
    return xf * lax.rsqrt(jnp.mean(xf * xf, axis=-1, keepdims=True) + RMS_EPS) * g


def _dot(a, b):
    return jnp.dot(a, b, preferred_element_type=F32)


def _dot_nt(a, b):
    return lax.dot_general(a, b, (((1,), (1,)), ((), ())), preferred_element_type=F32)


def _row_tile(t):
    tm = min(ROW_TILE, t)
    assert t % tm == 0, (t, tm)
    return tm


def _rows(a, tm):
    return a, pl.BlockSpec((tm, a.shape[1]), lambda i: (i, 0))


def _const(a):
    nd = a.ndim
    return a, pl.BlockSpec(a.shape, lambda i: (0,) * nd, pipeline_mode=pl.Buffered(1))


def _layer(a, layer):
    nd = a.ndim - 1
    return a, pl.BlockSpec((None,) + tuple(a.shape[1:]), lambda i: (layer,) + (0,) * nd,
                           pipeline_mode=pl.Buffered(1))


def _cycle(a, tm):
    assert a.shape[0] % tm == 0
    nb = a.shape[0] // tm
    return a, pl.BlockSpec((tm, a.shape[1]), lambda i: (i % nb, 0))


def _rows_out(t, tm, ncols, dtype):
    return jax.ShapeDtypeStruct((t, ncols), dtype), pl.BlockSpec((tm, ncols), lambda i: (i, 0))


def _rows_call(body, name, t, tm, ins, outs):
    return pl.pallas_call(
        body, grid=(t // tm,),
        in_specs=[s for _, s in ins], out_specs=[s for _, s in outs], out_shape=[o for o, _ in outs],
        compiler_params=_params(1), name=name,
    )(*[a for a, _ in ins])


def _ffn_ins(P, w, which, layer):
    return [_const(P["ffn_norm%d" % which][layer].reshape(1, -1)), _layer(w["f%dg" % which], layer),
            _layer(w["f%du" % which], layer), _layer(w["f%dd" % which], layer)]


def _ffn_apply(x, g_ref, wg_ref, wu_ref, wd_ref):
    h = _rms(x, g_ref[...]).astype(BF16)
    gate = _dot(h, wg_ref[...])
    up = _dot(h, wu_ref[...])
    a = (gate * jax.nn.sigmoid(gate) * up).astype(BF16)
    return x + FFN_RES * _dot(a, wd_ref[...])


def _mla_proj_apply(x, mg_ref, wdq_ref, qn_ref, wq_ref, wlat_ref, kvn_ref, wkr_ref, tab_ref,
                    refs, heads, c_dim, absorbed):
    cos_t, sin_t = tab_ref[:, 0:LANES], tab_ref[:, LANES:2 * LANES]
    lane = lax.broadcasted_iota(jnp.int32, (1, LANES), 1)
    cos_q = jnp.where(lane < MLA_NOPE, 1.0, cos_t)
    first_half = lane < MLA_NOPE + MLA_ROPE // 2

    def swap_halves(v):
        return jnp.where(first_half, pltpu.roll(v, LANES - MLA_ROPE // 2, 1), pltpu.roll(v, MLA_ROPE // 2, 1))

    h = _rms(x, mg_ref[...]).astype(BF16)
    cq = _rms(_dot(h, wdq_ref[...]), qn_ref[...]).astype(BF16)
    qa = _dot(cq, wq_ref[...])
    lat = _rms(_dot(h, wlat_ref[...]), kvn_ref[...])
    kr = _dot(h, wkr_ref[...])
    kr = kr * cos_t + swap_halves(kr) * sin_t
    latb = lat.astype(BF16)
    if absorbed:
        wukd_ref, q_ref, ql_ref, lat_ref, kr_ref = refs
    else:
        wuk_ref, wuvt_ref, q_ref, k_ref, vt_ref, lat_ref, kr_ref = refs
        kn = _dot(latb, wuk_ref[...])
    lat_ref[...] = lat
    kr_ref[...] = kr[:, MLA_NOPE:MLA_NOPE + MLA_ROPE]
    for hh in range(heads):
        sl = slice(hh * LANES, (hh + 1) * LANES)
        qh = (qa[:, sl] * cos_q + swap_halves(qa[:, sl]) * sin_t).astype(BF16)
        q_ref[:, sl] = qh
        if absorbed:
            ql_ref[:, hh * c_dim:(hh + 1) * c_dim] = _dot(qh, wukd_ref[hh]).astype(BF16)
        else:
            k_ref[:, sl] = (kn[:, sl] + kr).astype(BF16)
    if not absorbed:
        vt = _dot_nt(wuvt_ref[...], latb)
        ones_row = lax.broadcasted_iota(jnp.int32, vt.shape, 0) % MLA_VT_ROWS == MLA_V
        vt_ref[...] = jnp.where(ones_row, 1.0, vt).astype(BF16)


def _rows_a_kernel(x_ref, fg, wg, wu, wd, mg, wdq, qn, wq, wlat, kvn, wkr, tab, *refs,
                   heads, c_dim, absorbed):
    n_extra = 1 if absorbed else 2
    x1_ref = refs[n_extra]
    x1 = _ffn_apply(x_ref[...], fg, wg, wu, wd)
    x1_ref[...] = x1
    _mla_proj_apply(x1, mg, wdq, qn, wq, wlat, kvn, wkr, tab, refs[:n_extra] + refs[n_extra + 1:],
                    heads, c_dim, absorbed)


def _rows_b_kernel(a_ref, *refs, dec_heads, c_dim, tiled_kv):
    if dec_heads:
        wuv_ref, refs = refs[0], refs[1:]
        a = jnp.concatenate([_dot(a_ref[:, hh * c_dim:(hh + 1) * c_dim], wuv_ref[hh]).astype(BF16)
                             for hh in range(dec_heads)], axis=1)
    else:
        a = a_ref[...]
    wo, x_ref, fg, wg, wu, wd, kvg, wkv = refs[:8]
    x = _ffn_apply(x_ref[...] + _dot(a, wo[...]), fg, wg, wu, wd)
    kv = _dot(_rms(x, kvg[...]).astype(BF16), wkv[...])
    if tiled_kv:
        repk, eye, x_out, kv_out, k2_out, vt_out = refs[8:]
        half = kv.shape[1] // 2
        k2_out[...] = _dot(kv[:, :half].astype(BF16), repk[...]).astype(BF16)
        vt_out[...] = _dot_nt(eye[...], kv[:, half:].astype(BF16)).astype(BF16)
    else:
        x_out, kv_out = refs[8:]
    x_out[...] = x
    kv_out[...] = kv


def _rows_c_kernel(x_ref, fg, wg, wu, wd, mg, wq, x_out, q_out, *, scale, transposed):
    x = _ffn_apply(x_ref[...], fg, wg, wu, wd)
    x_out[...] = x
    h = _rms(x, mg[...]).astype(BF16)
    q = _dot_nt(wq[...], h) if transposed else _dot(h, wq[...])
    q_out[...] = (q * scale).astype(BF16)


def _rows_d_kernel(a_ref, wo, x_ref, fg, wg, wu, wd, fin_g, y_out):
    x = _ffn_apply(x_ref[...] + _dot(a_ref[...], wo[...]), fg, wg, wu, wd)
    y_out[...] = _rms(x, fin_g[...])


def _mla_attn_kernel(q_ref, k_ref, vt_ref, o_ref, m_sc, acc_sc, sa_sc, sb_sc, *, bq, bk, hp):
    i = pl.program_id(2)
    qs = [q_ref[:, hh * LANES:(hh + 1) * LANES] for hh in range(hp)]
    sls = [slice(hh * LANES, (hh + 1) * LANES) for hh in range(hp)]

    def own_scores(start, width, q_lo):
        ss = [_dot_nt(k_ref[pl.ds(start, width), sls[hh]], qs[hh][q_lo:]) for hh in range(hp)]
        kc = (start + lax.broadcasted_iota(jnp.int32, (width, 1), 0)) // CHUNK
        qc = (i * bq + q_lo + lax.broadcasted_iota(jnp.int32, (1, bq - q_lo), 1)) // CHUNK
        return [jnp.where(kc <= qc, s, NEG_INF) for s in ss]

    def own_update(ss, start, width, q_lo, after_head=None):
        ql = slice(q_lo, bq)
        for hh in range(hp):
            m = m_sc[hh, :, ql]
            m_new = jnp.maximum(m, jnp.max(ss[hh], axis=0, keepdims=True))
            alpha = jnp.exp2(m - m_new)
            p = jnp.exp2(ss[hh] - m_new).astype(BF16)
            vt = vt_ref[hh * MLA_VT_ROWS:(hh + 1) * MLA_VT_ROWS, pl.ds(start, width)]
            acc_sc[hh, :, ql] = alpha * acc_sc[hh, :, ql] + _dot(vt, p)
            m_sc[hh, :, ql] = m_new
            if after_head is not None:
                after_head(hh)

    def produce(hh, j, dst):
        dst[hh] = _dot_nt(k_ref[pl.ds(pl.multiple_of(j * bk, bk), bk), sls[hh]], qs[hh])

    def consume(hh, j, src):
        s = src[hh]
        m = m_sc[hh]
        m_new = jnp.maximum(m, jnp.max(s, axis=0, keepdims=True))
        alpha = jnp.exp2(m - m_new)
        p = jnp.exp2(s - m_new).astype(BF16)
        vt = vt_ref[hh * MLA_VT_ROWS:(hh + 1) * MLA_VT_ROWS, pl.ds(pl.multiple_of(j * bk, bk), bk)]
        acc_sc[hh] = alpha * acc_sc[hh] + _dot(vt, p)
        m_sc[hh] = m_new

    def transition(j, src, dst):
        produce(0, j + 1, dst)
        for hh in range(hp):
            consume(hh, j, src)
            if hh + 1 < hp:
                produce(hh + 1, j + 1, dst)

    m_sc[...] = jnp.full(m_sc.shape, NEG_INF, F32)
    acc_sc[...] = jnp.zeros(acc_sc.shape, F32)
    n_full = (i * bq) // bk

    diag = min(MLA_DIAG, bq)
    starts = [pl.multiple_of(i * bq + t * diag, diag) for t in range(bq // diag)]
    own = [own_scores(starts[t], diag, t * diag) for t in range(bq // diag)]
    for t in range(bq // diag):
        own_update(own[t], starts[t], diag, t * diag,
                   after_head=(lambda hh: produce(hh, 0, sa_sc)) if t == 0 else None)

    @pl.when(n_full > 0)
    def _():
        last = n_full - 1

        @pl.loop(0, last // 2)
        def _(t):
            transition(2 * t, sa_sc, sb_sc)
            transition(2 * t + 1, sb_sc, sa_sc)

        @pl.when(last % 2 == 1)
        def _():
            transition(last - 1, sa_sc, sb_sc)
            for hh in range(hp):
                consume(hh, last, sb_sc)

        @pl.when(last % 2 == 0)
        def _():
            for hh in range(hp):
                consume(hh, last, sa_sc)

    for pair in range(hp // 2):
        halves = []
        for hh in (2 * pair, 2 * pair + 1):
            acc = acc_sc[hh]
            halves.append(acc[:MLA_V] * (1.0 / acc[MLA_V:MLA_V + 1]))
        o_ref[:, pair * LANES:(pair + 1) * LANES] = jnp.concatenate(halves, axis=0).T.astype(BF16)


def _mla_attn_prompt(q, k, vt, batch, seq, heads, bq=MLA_BQ, bk=MLA_BK, hp=MLA_HEADS_PER_STEP):
    bq, bk = min(bq, seq), min(bk, seq)
    nq = seq // bq
    assert seq % bq == 0 and seq % bk == 0 and bq % CHUNK == 0 and heads % hp == 0 and hp % 2 == 0
    assert bq % bk == 0 and bq % min(MLA_DIAG, bq) == 0 and MLA_DIAG % CHUNK == 0
    w = hp * LANES
    return pl.pallas_call(
        functools.partial(_mla_attn_kernel, bq=bq, bk=bk, hp=hp),
        grid=(batch, heads // hp, nq),
        in_specs=[pl.BlockSpec((bq, w), lambda b, h, i: (b * nq + i, h)),
                  pl.BlockSpec((seq, w), lambda b, h, i: (b, h)),
                  pl.BlockSpec((hp * MLA_VT_ROWS, seq), lambda b, h, i: (h, b))],
        out_specs=pl.BlockSpec((bq, hp * MLA_V), lambda b, h, i: (b * nq + i, h)),
        out_shape=jax.ShapeDtypeStruct((q.shape[0], heads * MLA_V), BF16),
        scratch_shapes=[pltpu.VMEM((hp, 1, bq), F32), pltpu.VMEM((hp, MLA_VT_ROWS, bq), F32),
                        pltpu.VMEM((hp, bk, bq), F32), pltpu.VMEM((hp, bk, bq), F32)],
        compiler_params=_params(3), name="mla_attn_prompt",
    )(q, k, vt)


def _mla_attn_decode_kernel(ql_ref, q_ref, cl_ref, crt_ref, nl_ref, nr_ref, o_ref, *, heads, c_dim, n_new):
    nb = cl_ref.shape[0]
    past = cl_ref.shape[1]
    chunk = min(past, MLA_DECODE_CHUNK)
    n_parts = past // chunk + 1
    rows = [slice(bb * n_new, (bb + 1) * n_new) for bb in range(nb)]
    qls = [jnp.concatenate([ql_ref[rows[bb], hh * c_dim:(hh + 1) * c_dim] for hh in range(heads)], axis=0)
           for bb in range(nb)]
    qrs = [jnp.concatenate([q_ref[rows[bb], hh * LANES + MLA_NOPE:hh * LANES + MLA_NOPE + MLA_ROPE]
                            for hh in range(heads)], axis=0) for bb in range(nb)]

    def scores(bb, c):
        if c + 1 < n_parts:
            kl = cl_ref[bb, c * chunk:(c + 1) * chunk, :].astype(BF16)
            return kl, _dot_nt(qls[bb], kl) + _dot(qrs[bb], crt_ref[bb, :, c * chunk:(c + 1) * chunk].astype(BF16))
        kl = nl_ref[rows[bb], :].astype(BF16)
        return kl, _dot_nt(qls[bb], kl) + _dot_nt(qrs[bb], nr_ref[rows[bb], :].astype(BF16))

    state = [(jnp.full((heads * n_new, 1), NEG_INF, F32), jnp.zeros((heads * n_new, 1), F32),
              jnp.zeros((heads * n_new, c_dim), F32)) for _ in range(nb)]
    nxt = [scores(bb, 0) for bb in range(nb)]
    for c in range(n_parts):
        for bb in range(nb):
            kl, s = nxt[bb]
            if c + 1 < n_parts:
                nxt[bb] = scores(bb, c + 1)
            m, l, acc = state[bb]
            m_new = jnp.maximum(m, jnp.max(s, axis=-1, keepdims=True))
            alpha = jnp.exp2(m - m_new)
            p = jnp.exp2(s - m_new)
            l = alpha * l + jnp.sum(p, axis=-1, keepdims=True)
            acc = alpha * acc + _dot(p.astype(BF16), kl)
            state[bb] = (m_new, l, acc)
    for bb in range(nb):
        _, l, acc = state[bb]
        o = acc / l
        for hh in range(heads):
            o_ref[rows[bb], hh * c_dim:(hh + 1) * c_dim] = o[hh * n_new:(hh + 1) * n_new].astype(BF16)


def _mla_attn_decode(ql, q, cache_lat, cache_rope_t, new_lat, new_rope, heads, n_new, nb=MLA_DECODE_BATCH):
    _, batch, past, c_dim = cache_lat.shape
    r_dim = cache_rope_t.shape[2]
    nb = math.gcd(nb, batch)
    return pl.pallas_call(
        functools.partial(_mla_attn_decode_kernel, heads=heads, c_dim=c_dim, n_new=n_new),
        grid=(batch // nb,),
        in_specs=[pl.BlockSpec((nb * n_new, heads * c_dim), lambda b: (b, 0)),
                  pl.BlockSpec((nb * n_new, heads * LANES), lambda b: (b, 0)),
                  pl.BlockSpec((None, nb, past, c_dim), lambda b: (0, b, 0, 0)),
                  pl.BlockSpec((None, nb, r_dim, past), lambda b: (0, b, 0, 0)),
                  pl.BlockSpec((nb * n_new, c_dim), lambda b: (b, 0)),
                  pl.BlockSpec((nb * n_new, r_dim), lambda b: (b, 0))],
        out_specs=pl.BlockSpec((nb * n_new, heads * c_dim), lambda b: (b, 0)),
        out_shape=jax.ShapeDtypeStruct(ql.shape, BF16),
        compiler_params=_params(1), name="mla_attn_decode",
    )(ql, q, cache_lat, cache_rope_t, new_lat, new_rope)


def _bias_kernel(tab_ref, idx_ref, o_ref, *, heads):
    idx = idx_ref[...]
    for hh in range(heads):
        acc = jnp.zeros(idx.shape, F32)
        for b in range(N_BUCKETS):
            acc = jnp.where(idx == b, tab_ref[b, hh] * LOG2E, acc)
        o_ref[hh] = acc


def _bias_table(rel_bias, idx):
    heads = rel_bias.shape[1]
    return pl.pallas_call(
        functools.partial(_bias_kernel, heads=heads),
        in_specs=[pl.BlockSpec(memory_space=pltpu.SMEM), pl.BlockSpec(idx.shape, lambda: (0, 0))],
        out_specs=pl.BlockSpec((heads,) + idx.shape, lambda: (0, 0, 0)),
        out_shape=jax.ShapeDtypeStruct((heads,) + idx.shape, F32), name="rel_bias_table",
    )(rel_bias, idx)


def _rel_bucket(rel):
    half = N_BUCKETS // 2
    max_exact = half // 2
    base = jnp.where(rel > 0, half, 0)
    n = jnp.abs(rel)
    nf = jnp.maximum(n, 1).astype(jnp.float32)
    large = max_exact + (jnp.log(nf / max_exact) / math.log(MAX_DISTANCE / max_exact)
                         * (half - max_exact)).astype(jnp.int32)
    large = jnp.minimum(large, half - 1)
    return base + jnp.where(n < max_exact, n, large)


def _swa_prompt_kernel(sink_ref, qt_ref, kp_ref, kc_ref, vtp_ref, vtc_ref, bias_ref, o_ref, *, s_heads, rep, blk, nsub):
    i = pl.program_id(1)
    key_chunk = lax.broadcasted_iota(jnp.int32, (2 * blk, 1), 0) // CHUNK
    row_chunk = lax.broadcasted_iota(jnp.int32, (1, blk), 1) // CHUNK
    first = blk // CHUNK
    window = (key_chunk >= row_chunk) & (key_chunk <= row_chunk + WINDOW_CHUNKS)
    dh = SWA_HEAD_DIM
    units = [(u, hh) for u in range(nsub) for hh in range(s_heads)]
    ss = []
    for u, hh in units:
        g, pair, slot = hh // rep, hh // 2, hh % 2
        ksl = slice(g * 4 * dh + slot * 2 * dh, g * 4 * dh + (slot + 1) * 2 * dh)
        k = jnp.concatenate([kp_ref[:, ksl], kc_ref[:, ksl]], axis=0)[u * blk:(u + 2) * blk]
        ss.append(_dot(k, qt_ref[pair * 2 * dh:(pair + 1) * 2 * dh, u * blk:(u + 1) * blk]))
    ps, invs = [], []
    for (u, hh), s in zip(units, ss):
        valid = window if u > 0 else window & ((i > 0) | (key_chunk >= first))
        s = jnp.where(valid, s + bias_ref[hh], NEG_INF)
        sink = sink_ref[hh] * LOG2E
        m = jnp.maximum(jnp.max(s, axis=0, keepdims=True), sink)
        p = jnp.exp2(s - m)
        invs.append(1.0 / (jnp.sum(p, axis=0, keepdims=True) + jnp.exp2(sink - m)))
        ps.append(p.astype(BF16))
    outs = []
    for n, (u, hh) in enumerate(units):
        g = hh // rep
        vt = jnp.concatenate([vtp_ref[g * dh:(g + 1) * dh, :], vtc_ref[g * dh:(g + 1) * dh, :]],
                             axis=1)[:, u * blk:(u + 2) * blk]
        outs.append(_dot(vt, ps[n]) * invs[n])
    for u in range(nsub):
        for pair in range(s_heads // 2):
            n = u * s_heads + 2 * pair
            o_ref[u * blk:(u + 1) * blk, pair * 2 * dh:(pair + 1) * 2 * dh] = (
                jnp.concatenate(outs[n:n + 2], axis=0).T.astype(BF16))


def _swa_prompt(qt, k2, vt, bias_t, sinks, batch, seq, s_heads, rep, blk=128, nsub=SWA_BLOCKS_PER_STEP):
    assert seq % (blk * nsub) == 0
    nq = seq // (blk * nsub)
    kw = k2.shape[1]
    cur = lambda b, i: (b * nq + i, 0)
    prev = lambda b, i: ((b * nq + i) * nsub - jnp.minimum(i, 1), 0)
    cur_t = lambda b, i: (0, b * nq + i)
    prev_t = lambda b, i: (0, (b * nq + i) * nsub - jnp.minimum(i, 1))
    assert rep == 4 and kw == (s_heads // rep) * 4 * SWA_HEAD_DIM
    return pl.pallas_call(
        functools.partial(_swa_prompt_kernel, s_heads=s_heads, rep=rep, blk=blk, nsub=nsub),
        grid=(batch, nq),
        in_specs=[pl.BlockSpec(memory_space=pltpu.SMEM),
                  pl.BlockSpec((qt.shape[0], blk * nsub), cur_t),
                  pl.BlockSpec((blk, kw), prev), pl.BlockSpec((blk * nsub, kw), cur),
                  pl.BlockSpec((vt.shape[0], blk), prev_t), pl.BlockSpec((vt.shape[0], blk * nsub), cur_t),
                  pl.BlockSpec(bias_t.shape, lambda b, i: (0, 0, 0), pipeline_mode=pl.Buffered(1))],
        out_specs=pl.BlockSpec((blk * nsub, qt.shape[0]), cur),
        out_shape=jax.ShapeDtypeStruct((qt.shape[1], qt.shape[0]), BF16),
        compiler_params=_params(2), name="swa_prompt",
    )(sinks, qt, k2, k2, vt, vt, bias_t)


def _swa_decode_kernel(q_ref, ck_ref, cv_ref, nkv_ref, rep_ref, bias_ref, sink_ref, o_ref, *, kv_w, groups, rep):
    n = q_ref.shape[0]
    gw = rep * SWA_HEAD_DIM
    k = jnp.concatenate([ck_ref[...], nkv_ref[:, 0:kv_w]], axis=0).astype(BF16)
    v = jnp.concatenate([cv_ref[...], nkv_ref[:, kv_w:2 * kv_w]], axis=0).astype(BF16)
    k4 = _dot(k, rep_ref[...]).astype(BF16)
    v4 = _dot(v, rep_ref[...]).astype(BF16)
    lane_head = lax.broadcasted_iota(jnp.int32, (1, gw), 1) // SWA_HEAD_DIM
    sls = [slice(g * gw, (g + 1) * gw) for g in range(groups)]
    ss = []
    for g in range(groups):
        qg = q_ref[:, sls[g]]
        qs = jnp.concatenate([jnp.where(lane_head == r, qg, jnp.zeros_like(qg)) for r in range(rep)], axis=0)
        ss.append(_dot_nt(qs, k4[:, sls[g]]))
    ps = []
    for g in range(groups):
        rows = slice(g * rep * n, (g + 1) * rep * n)
        s = ss[g] + bias_ref[rows, :]
        sink = sink_ref[rows, :] * LOG2E
        m = jnp.maximum(jnp.max(s, axis=-1, keepdims=True), sink)
        p = jnp.exp2(s - m)
        inv = 1.0 / (jnp.sum(p, axis=-1, keepdims=True) + jnp.exp2(sink - m))
        ps.append((p * inv).astype(BF16))
    for g in range(groups):
        res = _dot(ps[g], v4[:, sls[g]])
        og = jnp.zeros((n, gw), F32)
        for r in range(rep):
            og = jnp.where(lane_head == r, res[r * n:(r + 1) * n], og)
        o_ref[:, sls[g]] = og.astype(BF16)


def _rope_table(pos):
    inv = ROPE_BASE ** (-jnp.arange(0, MLA_ROPE, 2, dtype=jnp.float32) / MLA_ROPE)
    ang = pos.astype(jnp.float32)[:, None] * inv[None, :]
    cos, sin = jnp.cos(ang), jnp.sin(ang)
    widths = ((0, 0), (MLA_NOPE, LANES - MLA_NOPE - MLA_ROPE))
    return jnp.concatenate([jnp.pad(jnp.concatenate([cos, cos], axis=1), widths),
                            jnp.pad(jnp.concatenate([-sin, sin], axis=1), widths)], axis=1)


def _prep_weights(mla_w_dq, mla_w_uq, mla_w_dkv, mla_w_uk, mla_w_uv, mla_w_o, w_kv_shared, swa_w_q, swa_w_o, scale):
    ql, qcols = mla_w_uq.shape
    c_dim, heads, nope = mla_w_uk.shape
    rope = qcols // heads - nope
    d = mla_w_dq.shape[0]
    pad = LANES - nope - rope
    w = {}
    w["wq"] = jnp.pad(mla_w_uq.reshape(ql, heads, nope + rope), ((0, 0), (0, 0), (0, pad))).reshape(
        ql, heads * LANES) * scale
    w["wdq"] = mla_w_dq
    w["wlat"] = mla_w_dkv[:, :c_dim]
    w["wkr"] = jnp.pad(mla_w_dkv[:, c_dim:], ((0, 0), (nope, pad)))
    zc = jnp.zeros((c_dim, heads, LANES - nope), F32)
    w["wuk"] = jnp.concatenate([mla_w_uk, zc], axis=-1).reshape(c_dim, heads * LANES)
    w["wuv_t"] = jnp.pad(mla_w_uv, ((0, 0), (0, 0), (0, MLA_VT_ROWS - mla_w_uv.shape[2]))).reshape(
        c_dim, heads * MLA_VT_ROWS).T
    w["wuk_dec"] = jnp.concatenate([jnp.transpose(mla_w_uk, (1, 2, 0)),
                                    jnp.zeros((heads, LANES - nope, c_dim), F32)], axis=1)
    w["wuv_dec"] = jnp.concatenate([jnp.transpose(mla_w_uv, (1, 0, 2)),
                                    jnp.zeros((heads, c_dim, LANES - nope), F32)], axis=-1)
    vdim = mla_w_uv.shape[2]
    assert vdim == MLA_V and nope == MLA_NOPE and rope == MLA_ROPE
    w["wo"] = mla_w_o
    w["wo_pad"] = jnp.concatenate([mla_w_o.reshape(heads, vdim, d), jnp.zeros((heads, LANES - vdim, d), F32)],
                                  axis=1).reshape(heads * LANES, d)
    kvw = w_kv_shared.shape[1] // 2
    groups = kvw // SWA_HEAD_DIM
    s_heads = swa_w_q.shape[1] // SWA_HEAD_DIM
    rep = s_heads // groups
    w["wkv"] = w_kv_shared
    eye = jnp.eye(SWA_HEAD_DIM, dtype=F32)
    zero = jnp.zeros_like(eye)
    w["rep_k2"] = jnp.kron(jnp.eye(groups, dtype=F32), jnp.concatenate([eye, zero, zero, eye], axis=1))
    w["eye_kv"] = jnp.eye(kvw, dtype=F32)
    w["swa_wq_t"] = swa_w_q.T
    w["swa_wo"] = swa_w_o
    w["rep_kv"] = jnp.kron(jnp.eye(groups, dtype=F32), jnp.tile(eye, (1, rep)))
    w["swa_wq"] = swa_w_q
    return {k: v.astype(BF16) for k, v in w.items()}, dict(heads=heads, c_dim=c_dim, groups=groups, rep=rep,
                                                            s_heads=s_heads, kvw=kvw)


def _rows_a(x, P, w, tab, heads, c_dim, absorbed):
    t, d = x.shape
    tm = _row_tile(t)
    hw = heads * LANES
    row = lambda v: v.reshape(1, -1)
    ins = ([_rows(x, tm)] + _ffn_ins(P, w, 1, 0)
           + [_const(row(P["mix_norm"][0])), _const(w["wdq"]), _const(row(P["mla_q_norm"][0])), _const(w["wq"]),
              _const(w["wlat"]), _const(row(P["mla_kv_norm"][0])), _const(w["wkr"]), _cycle(tab, tm)]
           + ([_const(w["wuk_dec"])] if absorbed else [_const(w["wuk"]), _const(w["wuv_t"])]))
    outs = [_rows_out(t, tm, d, F32), _rows_out(t, tm, hw, BF16)]
    if absorbed:
        outs.append(_rows_out(t, tm, heads * c_dim, BF16))
    else:
        outs += [_rows_out(t, tm, hw, BF16),
                 (jax.ShapeDtypeStruct((heads * MLA_VT_ROWS, t), BF16),
                  pl.BlockSpec((heads * MLA_VT_ROWS, tm), lambda i: (0, i)))]
    outs += [_rows_out(t, tm, c_dim, F32), _rows_out(t, tm, MLA_ROPE, F32)]
    return _rows_call(functools.partial(_rows_a_kernel, heads=heads, c_dim=c_dim, absorbed=absorbed),
                      "rows_a_absorbed" if absorbed else "rows_a", t, tm, ins, outs)


def _rows_b(a, x, P, w, meta, decode):
    t, d = x.shape
    tm = _row_tile(t)
    kvw = meta["kvw"]
    ins = [_rows(a, tm)]
    if decode:
        ins += [_const(w["wuv_dec"]), _const(w["wo_pad"])]
    else:
        ins += [_const(w["wo"])]
    ins += [_rows(x, tm)] + _ffn_ins(P, w, 2, 0) + [_const(P["kv_norm"].reshape(1, -1)), _const(w["wkv"])]
    outs = [_rows_out(t, tm, d, F32), _rows_out(t, tm, 2 * kvw, F32)]
    if not decode:
        ins += [_const(w["rep_k2"]), _const(w["eye_kv"])]
        outs += [_rows_out(t, tm, w["rep_k2"].shape[1], BF16),
                 (jax.ShapeDtypeStruct((kvw, t), BF16), pl.BlockSpec((kvw, tm), lambda i: (0, i)))]
    return _rows_call(functools.partial(_rows_b_kernel, dec_heads=meta["heads"] if decode else 0,
                                        c_dim=meta["c_dim"], tiled_kv=not decode),
                      "rows_b_decode" if decode else "rows_b", t, tm, ins, outs)


def _rows_c(x, P, w, wq, name, transposed):
    t, d = x.shape
    tm = _row_tile(t)
    ins = [_rows(x, tm)] + _ffn_ins(P, w, 1, 1) + [_const(P["mix_norm"][1].reshape(1, -1)), _const(wq)]
    if transposed:
        q_out = (jax.ShapeDtypeStruct((wq.shape[0], t), BF16), pl.BlockSpec((wq.shape[0], tm), lambda i: (0, i)))
    else:
        q_out = _rows_out(t, tm, wq.shape[1], BF16)
    return _rows_call(functools.partial(_rows_c_kernel, scale=SWA_HEAD_DIM ** -0.5 * LOG2E, transposed=transposed),
                      name, t, tm, ins, [_rows_out(t, tm, d, F32), q_out])


def _rows_d(a, x, P, w, wo, name):
    t, d = x.shape
    tm = _row_tile(t)
    ins = ([_rows(a, tm), _const(wo), _rows(x, tm)] + _ffn_ins(P, w, 2, 1)
           + [_const(P["final_norm"].reshape(1, -1))])
    return _rows_call(_rows_d_kernel, name, t, tm, ins, [_rows_out(t, tm, d, F32)])[0]


def _trunk_prompt(x3, P, w, meta):
    batch, seq, d = x3.shape
    heads, c_dim, kvw = meta["heads"], meta["c_dim"], meta["kvw"]
    x = x3.reshape(batch * seq, d)
    x, q, k, vt, lat, kr = _rows_a(x, P, w, _rope_table(jnp.arange(seq)), heads, c_dim, absorbed=False)
    o = _mla_attn_prompt(q, k, vt, batch, seq, heads)
    x, kv, k2, v_t = _rows_b(o, x, P, w, meta, decode=False)
    x, q_t = _rows_c(x, P, w, w["swa_wq_t"], "rows_c", transposed=True)
    blk = 2 * CHUNK
    rel = (jnp.arange(2 * blk) - blk)[:, None] - jnp.arange(blk)[None, :]
    bias_t = _bias_table(P["rel_bias"], _rel_bucket(rel).astype(jnp.int32))
    o = _swa_prompt(q_t, k2, v_t, bias_t, P["swa_sinks"][0], batch, seq, meta["s_heads"], meta["rep"], blk)
    y = _rows_d(o, x, P, w, w["swa_wo"], "rows_d")
    keep = min(WINDOW, seq)
    kv3 = kv.reshape(batch, seq, 2 * kvw)[:, seq - keep:]
    new_k = kv3[:, :, :kvw].reshape(batch, keep, meta["groups"], SWA_HEAD_DIM)
    new_v = kv3[:, :, kvw:].reshape(batch, keep, meta["groups"], SWA_HEAD_DIM)
    return (y.reshape(batch, seq, d), lat.reshape(1, batch, seq, c_dim), kr.reshape(1, batch, seq, MLA_ROPE),
            new_k, new_v)


def _trunk_decode(x3, cache_lat, cache_rope, cache_k, cache_v, P, w, meta):
    batch, n_new, d = x3.shape
    heads, c_dim, kvw, s_heads = meta["heads"], meta["c_dim"], meta["kvw"], meta["s_heads"]
    past = cache_lat.shape[2]
    w_c = cache_k.shape[1]
    qpos = past + jnp.arange(n_new)
    assert past % CHUNK == 0 and n_new <= CHUNK and w_c <= WINDOW_CHUNKS * CHUNK and w_c <= past
    t = batch * n_new
    x = x3.reshape(t, d)
    tm = _row_tile(t)
    assert tm % n_new == 0
    tab = jnp.tile(_rope_table(qpos), (tm // n_new, 1))
    x, q, ql, lat, kr = _rows_a(x, P, w, tab, heads, c_dim, absorbed=True)
    ol = _mla_attn_decode(ql, q, cache_lat, jnp.swapaxes(cache_rope, 2, 3), lat, kr, heads, n_new)
    x, kv = _rows_b(ol, x, P, w, meta, decode=True)
    x, qs = _rows_c(x, P, w, w["swa_wq"], "rows_c_decode", transposed=False)
    kpos = jnp.arange(past - w_c, past + n_new)
    bias_h = _bias_table(P["rel_bias"], _rel_bucket(kpos[None, :] - qpos[:, None]).astype(jnp.int32))
    bias = bias_h.reshape(s_heads * n_new, w_c + n_new)
    sink_rows = jnp.repeat(P["swa_sinks"][0], n_new).reshape(s_heads * n_new, 1)
    const2 = lambda a: pl.BlockSpec(a.shape, lambda b: (0, 0), pipeline_mode=pl.Buffered(1))
    o = pl.pallas_call(
        functools.partial(_swa_decode_kernel, kv_w=kvw, groups=meta["groups"], rep=meta["rep"]),
        grid=(batch,),
        in_specs=[pl.BlockSpec((n_new, qs.shape[1]), lambda b: (b, 0)),
                  pl.BlockSpec((None, w_c, kvw), lambda b: (b, 0, 0)),
                  pl.BlockSpec((None, w_c, kvw), lambda b: (b, 0, 0)),
                  pl.BlockSpec((n_new, 2 * kvw), lambda b: (b, 0)),
                  const2(w["rep_kv"]), const2(bias), const2(sink_rows)],
        out_specs=pl.BlockSpec((n_new, qs.shape[1]), lambda b: (b, 0)),
        out_shape=jax.ShapeDtypeStruct(qs.shape, BF16),
        compiler_params=_params(1), name="swa_decode",
    )(qs, cache_k.reshape(batch, w_c, kvw), cache_v.reshape(batch, w_c, kvw), kv, w["rep_kv"], bias, sink_rows)
    y = _rows_d(o, x, P, w, w["swa_wo"], "rows_d_decode")
    kv3 = kv.reshape(batch, n_new, 2 * kvw)
    new_k = kv3[:, :, :kvw].reshape(batch, n_new, meta["groups"], SWA_HEAD_DIM)
    new_v = kv3[:, :, kvw:].reshape(batch, n_new, meta["groups"], SWA_HEAD_DIM)
    return (y.reshape(batch, n_new, d), lat.reshape(1, batch, n_new, c_dim), kr.reshape(1, batch, n_new, MLA_ROPE),
            new_k, new_v)


def kernel(x_prompt, x_sample, cache_mla_latent, cache_mla_krope, cache_swa_k, cache_swa_v, ffn_norm1, ffn1_w_gate, ffn1_w_up, ffn1_w_down, mix_norm, ffn_norm2, ffn2_w_gate, ffn2_w_up, ffn2_w_down, mla_w_dq, mla_q_norm, mla_w_uq, mla_w_dkv, mla_kv_norm, mla_w_uk, mla_w_uv, mla_w_o, kv_norm, w_kv_shared, swa_w_q, swa_sinks, swa_w_o, rel_bias, final_norm):
    assert ffn_norm1.shape[0] == 2 and mla_w_dq.shape[0] == 1 and swa_w_q.shape[0] == 1
    scale = (MLA_NOPE + MLA_ROPE) ** -0.5 * LOG2E
    w, meta = _prep_weights(mla_w_dq[0], mla_w_uq[0], mla_w_dkv[0], mla_w_uk[0], mla_w_uv[0], mla_w_o[0],
                            w_kv_shared, swa_w_q[0], swa_w_o[0], scale)
    w.update(f1g=ffn1_w_gate.astype(BF16), f1u=ffn1_w_up.astype(BF16), f1d=ffn1_w_down.astype(BF16),
             f2g=ffn2_w_gate.astype(BF16), f2u=ffn2_w_up.astype(BF16), f2d=ffn2_w_down.astype(BF16))
    P = dict(ffn_norm1=ffn_norm1, mix_norm=mix_norm, ffn_norm2=ffn_norm2, mla_q_norm=mla_q_norm,
             mla_kv_norm=mla_kv_norm, kv_norm=kv_norm, swa_sinks=swa_sinks, rel_bias=rel_bias, final_norm=final_norm)
    y_p, lat_p, rope_p, k_p, v_p = _trunk_prompt(x_prompt, P, w, meta)
    y_s, lat_s, rope_s, k_s, v_s = _trunk_decode(x_sample, cache_mla_latent, cache_mla_krope,
                                                 cache_swa_k, cache_swa_v, P, w, meta)
    return (y_p, y_s, lat_p, rope_p, k_p, v_p, lat_s, rope_s, k_s, v_s)
```

```python
import functools
import math

import jax
import jax.numpy as jnp
from jax import lax
from jax.experimental import pallas as pl
from jax.experimental.pallas import tpu as pltpu

F32 = jnp.float32
BF16 = jnp.bfloat16

CHUNK = 64
RMS_EPS = 1e-6
FFN_RES = 0.5
ROPE_BASE = 10000.0
WINDOW = 128
WINDOW_CHUNKS = WINDOW // CHUNK
N_BUCKETS = 32
MAX_DISTANCE = 128
NEG_INF = -1e30
LOG2E = math.log2(math.e)
MLA_NOPE = 64
MLA_ROPE = 32
MLA_V = 64
MLA_VT_ROWS = 80
SWA_HEAD_DIM = 64

LANES = 128
ROW_TILE = 512
MLA_BQ = 512
MLA_BK = 512
MLA_DIAG = 256
MLA_HEADS_PER_STEP = 8
MLA_DECODE_CHUNK = 1024
MLA_DECODE_BATCH = 2
SWA_BLOCKS_PER_STEP = 4
VMEM_LIMIT = 60 * 1024 * 1024


def _params(n_axes):
    return pltpu.CompilerParams(dimension_semantics=("arbitrary",) * n_axes, vmem_limit_bytes=VMEM_LIMIT)


def _rms(xf, g):
    return xf * lax.rsqrt(jnp.mean(xf * xf, axis=-1, keepdims=True) + RMS_EPS) * g


def _dot(a, b):
    return jnp.dot(a, b, preferred_element_type=F32)


def _dot_nt(a, b):
    return lax.dot_general(a, b, (((1,), (1,)), ((), ())), preferred_element_type=F32)


def _row_tile(t):
    tm = min(ROW_TILE, t)
    assert t % tm == 0, (t, tm)
    return tm


def _rows(a, tm):
    return a, pl.BlockSpec((tm, a.shape[1]), lambda i: (i, 0))


def _const(a):
    nd = a.ndim
    return a, pl.BlockSpec(a.shape, lambda i: (0,) * nd, pipeline_mode=pl.Buffered(1))


def _layer(a, layer):
    nd = a.ndim - 1
    return a, pl.BlockSpec((None,) + tuple(a.shape[1:]), lambda i: (layer,) + (0,) * nd,
                           pipeline_mode=pl.Buffered(1))


def _cycle(a, tm):
    assert a.shape[0] % tm == 0
    nb = a.shape[0] // tm
    return a, pl.BlockSpec((tm, a.shape[1]), lambda i: (i % nb, 0))


def _rows_out(t, tm, ncols, dtype):
    return jax.ShapeDtypeStruct((t, ncols), dtype), pl.BlockSpec((tm, ncols), lambda i: (i, 0))


def _rows_call(body, name, t, tm, ins, outs):
    return pl.pallas_call(
        body, grid=(t // tm,),
        in_specs=[s for _, s in ins], out_specs=[s for _, s in outs], out_shape=[o for o, _ in outs],
        compiler_params=_params(1), name=name,
    )(*[a for a, _ in ins])


def _ffn_ins(P, w, which, layer):
    return [_const(P["ffn_norm%d" % which][layer].reshape(1, -1)), _layer(w["f%dgu" % which], layer),
            _layer(w["f%dd" % which], layer)]


def _ffn_apply(x, g_ref, wgu_ref, wd_ref):
    h = _rms(x, g_ref[...]).astype(BF16)
    gu = _dot(h, wgu_ref[...])
    f = gu.shape[1] // 2
    gate, up = gu[:, :f], gu[:, f:]
    a = (gate * jax.nn.sigmoid(gate) * up).astype(BF16)
    return x + FFN_RES * _dot(a, wd_ref[...])


def _mla_proj_apply(x, mg_ref, wdq_ref, qn_ref, wq_ref, wlat_ref, kvn_ref, wkr_ref, tab_ref,
                    refs, heads, c_dim, absorbed):
    cos_t, sin_t = tab_ref[:, 0:LANES], tab_ref[:, LANES:2 * LANES]
    lane = lax.broadcasted_iota(jnp.int32, (1, LANES), 1)
    cos_q = jnp.where(lane < MLA_NOPE, 1.0, cos_t)
    first_half = lane < MLA_NOPE + MLA_ROPE // 2

    def swap_halves(v):
        return jnp.where(first_half, pltpu.roll(v, LANES - MLA_ROPE // 2, 1), pltpu.roll(v, MLA_ROPE // 2, 1))

    h = _rms(x, mg_ref[...]).astype(BF16)
    cq = _rms(_dot(h, wdq_ref[...]), qn_ref[...]).astype(BF16)
    qa = _dot(cq, wq_ref[...])
    lat = _rms(_dot(h, wlat_ref[...]), kvn_ref[...])
    kr = _dot(h, wkr_ref[...])
    kr = kr * cos_t + swap_halves(kr) * sin_t
    latb = lat.astype(BF16)
    if absorbed:
        wukd_ref, q_ref, ql_ref, lat_ref, kr_ref = refs
    else:
        wuk_ref, wuvt_ref, q_ref, k_ref, vt_ref, lat_ref, kr_ref = refs
        kn = _dot(latb, wuk_ref[...])
    lat_ref[...] = lat
    kr_ref[...] = kr[:, MLA_NOPE:MLA_NOPE + MLA_ROPE]
    for hh in range(heads):
        sl = slice(hh * LANES, (hh + 1) * LANES)
        qh = (qa[:, sl] * cos_q + swap_halves(qa[:, sl]) * sin_t).astype(BF16)
        q_ref[:, sl] = qh
        if absorbed:
            ql_ref[:, hh * c_dim:(hh + 1) * c_dim] = _dot(qh, wukd_ref[hh]).astype(BF16)
        else:
            k_ref[:, sl] = (kn[:, sl] + kr).astype(BF16)
    if not absorbed:
        vt = _dot_nt(wuvt_ref[...], latb)
        ones_row = lax.broadcasted_iota(jnp.int32, vt.shape, 0) % MLA_VT_ROWS == MLA_V
        vt_ref[...] = jnp.where(ones_row, 1.0, vt).astype(BF16)


def _rows_a_kernel(x_ref, fg, wgu, wd, mg, wdq, qn, wq, wlat, kvn, wkr, tab, *refs,
                   heads, c_dim, absorbed):
    n_extra = 1 if absorbed else 2
    x1_ref = refs[n_extra]
    x1 = _ffn_apply(x_ref[...], fg, wgu, wd)
    x1_ref[...] = x1
    _mla_proj_apply(x1, mg, wdq, qn, wq, wlat, kvn, wkr, tab, refs[:n_extra] + refs[n_extra + 1:],
                    heads, c_dim, absorbed)


def _rows_b_kernel(a_ref, *refs, dec_heads, c_dim, tiled_kv):
    if dec_heads:
        wuv_ref, refs = refs[0], refs[1:]
        a = jnp.concatenate([_dot(a_ref[:, hh * c_dim:(hh + 1) * c_dim], wuv_ref[hh]).astype(BF16)
                             for hh in range(dec_heads)], axis=1)
    else:
        a = a_ref[...]
    wo, x_ref, fg, wgu, wd, kvg, wkv = refs[:7]
    x = _ffn_apply(x_ref[...] + _dot(a, wo[...]), fg, wgu, wd)
    kv = _dot(_rms(x, kvg[...]).astype(BF16), wkv[...])
    if tiled_kv:
        repk, eye, x_out, kv_out, k2_out, vt_out = refs[7:]
        half = kv.shape[1] // 2
        k2_out[...] = _dot(kv[:, :half].astype(BF16), repk[...]).astype(BF16)
        vt_out[...] = _dot_nt(eye[...], kv[:, half:].astype(BF16)).astype(BF16)
    else:
        x_out, kv_out = refs[7:]
    x_out[...] = x
    kv_out[...] = kv


def _rows_c_kernel(x_ref, fg, wgu, wd, mg, wq, x_out, q_out, *, scale, transposed):
    x = _ffn_apply(x_ref[...], fg, wgu, wd)
    x_out[...] = x
    h = _rms(x, mg[...]).astype(BF16)
    q = _dot_nt(wq[...], h) if transposed else _dot(h, wq[...])
    q_out[...] = (q * scale).astype(BF16)


def _rows_d_kernel(a_ref, wo, x_ref, fg, wgu, wd, fin_g, y_out):
    x = _ffn_apply(x_ref[...] + _dot(a_ref[...], wo[...]), fg, wgu, wd)
    y_out[...] = _rms(x, fin_g[...])


def _mla_attn_kernel(q_ref, k_ref, vt_ref, o_ref, m_sc, acc_sc, sa_sc, sb_sc, *, bq, bk, hp):
    i = pl.program_id(2)
    qs = [q_ref[:, hh * LANES:(hh + 1) * LANES] for hh in range(hp)]
    sls = [slice(hh * LANES, (hh + 1) * LANES) for hh in range(hp)]

    def own_scores(start, width, q_lo):
        ss = [_dot_nt(k_ref[pl.ds(start, width), sls[hh]], qs[hh][q_lo:]) for hh in range(hp)]
        kc = (start + lax.broadcasted_iota(jnp.int32, (width, 1), 0)) // CHUNK
        qc = (i * bq + q_lo + lax.broadcasted_iota(jnp.int32, (1, bq - q_lo), 1)) // CHUNK
        return [jnp.where(kc <= qc, s, NEG_INF) for s in ss]

    def own_update(ss, start, width, q_lo, after_head=None):
        ql = slice(q_lo, bq)
        for hh in range(hp):
            m = m_sc[hh, :, ql]
            m_new = jnp.maximum(m, jnp.max(ss[hh], axis=0, keepdims=True))
            alpha = jnp.exp2(m - m_new)
            p = jnp.exp2(ss[hh] - m_new).astype(BF16)
            vt = vt_ref[hh * MLA_VT_ROWS:(hh + 1) * MLA_VT_ROWS, pl.ds(start, width)]
            acc_sc[hh, :, ql] = alpha * acc_sc[hh, :, ql] + _dot(vt, p)
            m_sc[hh, :, ql] = m_new
            if after_head is not None:
                after_head(hh)

    def produce(hh, j, dst):
        dst[hh] = _dot_nt(k_ref[pl.ds(pl.multiple_of(j * bk, bk), bk), sls[hh]], qs[hh])

    def consume(hh, j, src):
        s = src[hh]
        m = m_sc[hh]
        m_new = jnp.maximum(m, jnp.max(s, axis=0, keepdims=True))
        alpha = jnp.exp2(m - m_new)
        p = jnp.exp2(s - m_new).astype(BF16)
        vt = vt_ref[hh * MLA_VT_ROWS:(hh + 1) * MLA_VT_ROWS, pl.ds(pl.multiple_of(j * bk, bk), bk)]
        acc_sc[hh] = alpha * acc_sc[hh] + _dot(vt, p)
        m_sc[hh] = m_new

    def transition(j, src, dst):
        produce(0, j + 1, dst)
        for hh in range(hp):
            consume(hh, j, src)
            if hh + 1 < hp:
                produce(hh + 1, j + 1, dst)

    m_sc[...] = jnp.full(m_sc.shape, NEG_INF, F32)
    acc_sc[...] = jnp.zeros(acc_sc.shape, F32)
    n_full = (i * bq) // bk

    diag = min(MLA_DIAG, bq)
    starts = [pl.multiple_of(i * bq + t * diag, diag) for t in range(bq // diag)]
    own = [own_scores(starts[t], diag, t * diag) for t in range(bq // diag)]
    for t in range(bq // diag):
        own_update(own[t], starts[t], diag, t * diag,
                   after_head=(lambda hh: produce(hh, 0, sa_sc)) if t == 0 else None)

    @pl.when(n_full > 0)
    def _():
        last = n_full - 1

        @pl.loop(0, last // 2)
        def _(t):
            transition(2 * t, sa_sc, sb_sc)
            transition(2 * t + 1, sb_sc, sa_sc)

        @pl.when(last % 2 == 1)
        def _():
            transition(last - 1, sa_sc, sb_sc)
            for hh in range(hp):
                consume(hh, last, sb_sc)

        @pl.when(last % 2 == 0)
        def _():
            for hh in range(hp):
                consume(hh, last, sa_sc)

    for pair in range(hp // 2):
        halves = []
        for hh in (2 * pair, 2 * pair + 1):
            acc = acc_sc[hh]
            halves.append(acc[:MLA_V] * (1.0 / acc[MLA_V:MLA_V + 1]))
        o_ref[:, pair * LANES:(pair + 1) * LANES] = jnp.concatenate(halves, axis=0).T.astype(BF16)


def _mla_attn_prompt(q, k, vt, batch, seq, heads, bq=MLA_BQ, bk=MLA_BK, hp=MLA_HEADS_PER_STEP):
    bq, bk = min(bq, seq), min(bk, seq)
    nq = seq // bq
    assert seq % bq == 0 and seq % bk == 0 and bq % CHUNK == 0 and heads % hp == 0 and hp % 2 == 0
    assert bq % bk == 0 and bq % min(MLA_DIAG, bq) == 0 and MLA_DIAG % CHUNK == 0
    w = hp * LANES
    return pl.pallas_call(
        functools.partial(_mla_attn_kernel, bq=bq, bk=bk, hp=hp),
        grid=(batch, heads // hp, nq),
        in_specs=[pl.BlockSpec((bq, w), lambda b, h, i: (b * nq + i, h)),
                  pl.BlockSpec((seq, w), lambda b, h, i: (b, h)),
                  pl.BlockSpec((hp * MLA_VT_ROWS, seq), lambda b, h, i: (h, b))],
        out_specs=pl.BlockSpec((bq, hp * MLA_V), lambda b, h, i: (b * nq + i, h)),
        out_shape=jax.ShapeDtypeStruct((q.shape[0], heads * MLA_V), BF16),
        scratch_shapes=[pltpu.VMEM((hp, 1, bq), F32), pltpu.VMEM((hp, MLA_VT_ROWS, bq), F32),
                        pltpu.VMEM((hp, bk, bq), F32), pltpu.VMEM((hp, bk, bq), F32)],
        compiler_params=_params(3), name="mla_attn_prompt",
    )(q, k, vt)


def _mla_attn_decode_kernel(ql_ref, q_ref, cl_ref, crt_ref, nl_ref, nr_ref, o_ref, *, heads, c_dim, n_new):
    nb = cl_ref.shape[0]
    past = cl_ref.shape[1]
    chunk = min(past, MLA_DECODE_CHUNK)
    n_parts = past // chunk + 1
    rows = [slice(bb * n_new, (bb + 1) * n_new) for bb in range(nb)]
    qls = [jnp.concatenate([ql_ref[rows[bb], hh * c_dim:(hh + 1) * c_dim] for hh in range(heads)], axis=0)
           for bb in range(nb)]
    qrs = [jnp.concatenate([q_ref[rows[bb], hh * LANES + MLA_NOPE:hh * LANES + MLA_NOPE + MLA_ROPE]
                            for hh in range(heads)], axis=0) for bb in range(nb)]

    def scores(bb, c):
        if c + 1 < n_parts:
            kl = cl_ref[bb, c * chunk:(c + 1) * chunk, :].astype(BF16)
            return kl, _dot_nt(qls[bb], kl) + _dot(qrs[bb], crt_ref[bb, :, c * chunk:(c + 1) * chunk].astype(BF16))
        kl = nl_ref[rows[bb], :].astype(BF16)
        return kl, _dot_nt(qls[bb], kl) + _dot_nt(qrs[bb], nr_ref[rows[bb], :].astype(BF16))

    state = [(jnp.full((heads * n_new, 1), NEG_INF, F32), jnp.zeros((heads * n_new, 1), F32),
              jnp.zeros((heads * n_new, c_dim), F32)) for _ in range(nb)]
    nxt = [scores(bb, 0) for bb in range(nb)]
    for c in range(n_parts):
        for bb in range(nb):
            kl, s = nxt[bb]
            if c + 1 < n_parts:
                nxt[bb] = scores(bb, c + 1)
            m, l, acc = state[bb]
            m_new = jnp.maximum(m, jnp.max(s, axis=-1, keepdims=True))
            alpha = jnp.exp2(m - m_new)
            p = jnp.exp2(s - m_new)
            l = alpha * l + jnp.sum(p, axis=-1, keepdims=True)
            acc = alpha * acc + _dot(p.astype(BF16), kl)
            state[bb] = (m_new, l, acc)
    for bb in range(nb):
        _, l, acc = state[bb]
        o = acc / l
        for hh in range(heads):
            o_ref[rows[bb], hh * c_dim:(hh + 1) * c_dim] = o[hh * n_new:(hh + 1) * n_new].astype(BF16)


def _mla_attn_decode(ql, q, cache_lat, cache_rope_t, new_lat, new_rope, heads, n_new, nb=MLA_DECODE_BATCH):
    _, batch, past, c_dim = cache_lat.shape
    r_dim = cache_rope_t.shape[2]
    nb = math.gcd(nb, batch)
    return pl.pallas_call(
        functools.partial(_mla_attn_decode_kernel, heads=heads, c_dim=c_dim, n_new=n_new),
        grid=(batch // nb,),
        in_specs=[pl.BlockSpec((nb * n_new, heads * c_dim), lambda b: (b, 0)),
                  pl.BlockSpec((nb * n_new, heads * LANES), lambda b: (b, 0)),
                  pl.BlockSpec((None, nb, past, c_dim), lambda b: (0, b, 0, 0)),
                  pl.BlockSpec((None, nb, r_dim, past), lambda b: (0, b, 0, 0)),
                  pl.BlockSpec((nb * n_new, c_dim), lambda b: (b, 0)),
                  pl.BlockSpec((nb * n_new, r_dim), lambda b: (b, 0))],
        out_specs=pl.BlockSpec((nb * n_new, heads * c_dim), lambda b: (b, 0)),
        out_shape=jax.ShapeDtypeStruct(ql.shape, BF16),
        compiler_params=_params(1), name="mla_attn_decode",
    )(ql, q, cache_lat, cache_rope_t, new_lat, new_rope)


def _bias_kernel(tab_ref, idx_ref, o_ref, *, heads):
    idx = idx_ref[...]
    for hh in range(heads):
        acc = jnp.zeros(idx.shape, F32)
        for b in range(N_BUCKETS):
            acc = jnp.where(idx == b, tab_ref[b, hh] * LOG2E, acc)
        o_ref[hh] = acc


def _bias_table(rel_bias, idx):
    heads = rel_bias.shape[1]
    return pl.pallas_call(
        functools.partial(_bias_kernel, heads=heads),
        in_specs=[pl.BlockSpec(memory_space=pltpu.SMEM), pl.BlockSpec(idx.shape, lambda: (0, 0))],
        out_specs=pl.BlockSpec((heads,) + idx.shape, lambda: (0, 0, 0)),
        out_shape=jax.ShapeDtypeStruct((heads,) + idx.shape, F32), name="rel_bias_table",
    )(rel_bias, idx)


def _rel_bucket(rel):
    half = N_BUCKETS // 2
    max_exact = half // 2
    base = jnp.where(rel > 0, half, 0)
    n = jnp.abs(rel)
    nf = jnp.maximum(n, 1).astype(jnp.float32)
    large = max_exact + (jnp.log(nf / max_exact) / math.log(MAX_DISTANCE / max_exact)
                         * (half - max_exact)).astype(jnp.int32)
    large = jnp.minimum(large, half - 1)
    return base + jnp.where(n < max_exact, n, large)


def _swa_prompt_kernel(sink_ref, qt_ref, kp_ref, kc_ref, vtp_ref, vtc_ref, bias_ref, o_ref, *, s_heads, rep, blk, nsub):
    i = pl.program_id(1)
    key_chunk = lax.broadcasted_iota(jnp.int32, (2 * blk, 1), 0) // CHUNK
    row_chunk = lax.broadcasted_iota(jnp.int32, (1, blk), 1) // CHUNK
    first = blk // CHUNK
    window = (key_chunk >= row_chunk) & (key_chunk <= row_chunk + WINDOW_CHUNKS)
    dh = SWA_HEAD_DIM
    units = [(u, hh) for u in range(nsub) for hh in range(s_heads)]
    ss = []
    for u, hh in units:
        g, pair, slot = hh // rep, hh // 2, hh % 2
        ksl = slice(g * 4 * dh + slot * 2 * dh, g * 4 * dh + (slot + 1) * 2 * dh)
        k = jnp.concatenate([kp_ref[:, ksl], kc_ref[:, ksl]], axis=0)[u * blk:(u + 2) * blk]
        ss.append(_dot(k, qt_ref[pair * 2 * dh:(pair + 1) * 2 * dh, u * blk:(u + 1) * blk]))
    ps, invs = [], []
    for (u, hh), s in zip(units, ss):
        valid = window if u > 0 else window & ((i > 0) | (key_chunk >= first))
        s = jnp.where(valid, s + bias_ref[hh], NEG_INF)
        sink = sink_ref[hh] * LOG2E
        m = jnp.maximum(jnp.max(s, axis=0, keepdims=True), sink)
        p = jnp.exp2(s - m)
        invs.append(1.0 / (jnp.sum(p, axis=0, keepdims=True) + jnp.exp2(sink - m)))
        ps.append(p.astype(BF16))
    outs = []
    for n, (u, hh) in enumerate(units):
        g = hh // rep
        vt = jnp.concatenate([vtp_ref[g * dh:(g + 1) * dh, :], vtc_ref[g * dh:(g + 1) * dh, :]],
                             axis=1)[:, u * blk:(u + 2) * blk]
        outs.append(_dot(vt, ps[n]) * invs[n])
    for u in range(nsub):
        for pair in range(s_heads // 2):
            n = u * s_heads + 2 * pair
            o_ref[u * blk:(u + 1) * blk, pair * 2 * dh:(pair + 1) * 2 * dh] = (
                jnp.concatenate(outs[n:n + 2], axis=0).T.astype(BF16))


def _swa_prompt(qt, k2, vt, bias_t, sinks, batch, seq, s_heads, rep, blk=128, nsub=SWA_BLOCKS_PER_STEP):
    assert seq % (blk * nsub) == 0
    nq = seq // (blk * nsub)
    kw = k2.shape[1]
    cur = lambda b, i: (b * nq + i, 0)
    prev = lambda b, i: ((b * nq + i) * nsub - jnp.minimum(i, 1), 0)
    cur_t = lambda b, i: (0, b * nq + i)
    prev_t = lambda b, i: (0, (b * nq + i) * nsub - jnp.minimum(i, 1))
    assert rep == 4 and kw == (s_heads // rep) * 4 * SWA_HEAD_DIM
    return pl.pallas_call(
        functools.partial(_swa_prompt_kernel, s_heads=s_heads, rep=rep, blk=blk, nsub=nsub),
        grid=(batch, nq),
        in_specs=[pl.BlockSpec(memory_space=pltpu.SMEM),
                  pl.BlockSpec((qt.shape[0], blk * nsub), cur_t),
                  pl.BlockSpec((blk, kw), prev), pl.BlockSpec((blk * nsub, kw), cur),
                  pl.BlockSpec((vt.shape[0], blk), prev_t), pl.BlockSpec((vt.shape[0], blk * nsub), cur_t),
                  pl.BlockSpec(bias_t.shape, lambda b, i: (0, 0, 0), pipeline_mode=pl.Buffered(1))],
        out_specs=pl.BlockSpec((blk * nsub, qt.shape[0]), cur),
        out_shape=jax.ShapeDtypeStruct((qt.shape[1], qt.shape[0]), BF16),
        compiler_params=_params(2), name="swa_prompt",
    )(sinks, qt, k2, k2, vt, vt, bias_t)


def _swa_decode_kernel(q_ref, ck_ref, cv_ref, nkv_ref, rep_ref, bias_ref, sink_ref, o_ref, *, kv_w, groups, rep):
    n = q_ref.shape[0]
    gw = rep * SWA_HEAD_DIM
    k = jnp.concatenate([ck_ref[...], nkv_ref[:, 0:kv_w]], axis=0).astype(BF16)
    v = jnp.concatenate([cv_ref[...], nkv_ref[:, kv_w:2 * kv_w]], axis=0).astype(BF16)
    k4 = _dot(k, rep_ref[...]).astype(BF16)
    v4 = _dot(v, rep_ref[...]).astype(BF16)
    lane_head = lax.broadcasted_iota(jnp.int32, (1, gw), 1) // SWA_HEAD_DIM
    sls = [slice(g * gw, (g + 1) * gw) for g in range(groups)]
    ss = []
    for g in range(groups):
        qg = q_ref[:, sls[g]]
        qs = jnp.concatenate([jnp.where(lane_head == r, qg, jnp.zeros_like(qg)) for r in range(rep)], axis=0)
        ss.append(_dot_nt(qs, k4[:, sls[g]]))
    ps = []
    for g in range(groups):
        rows = slice(g * rep * n, (g + 1) * rep * n)
        s = ss[g] + bias_ref[rows, :]
        sink = sink_ref[rows, :] * LOG2E
        m = jnp.maximum(jnp.max(s, axis=-1, keepdims=True), sink)
        p = jnp.exp2(s - m)
        inv = 1.0 / (jnp.sum(p, axis=-1, keepdims=True) + jnp.exp2(sink - m))
        ps.append((p * inv).astype(BF16))
    for g in range(groups):
        res = _dot(ps[g], v4[:, sls[g]])
        og = jnp.zeros((n, gw), F32)
        for r in range(rep):
            og = jnp.where(lane_head == r, res[r * n:(r + 1) * n], og)
        o_ref[:, sls[g]] = og.astype(BF16)


def _rope_table(pos):
    inv = ROPE_BASE ** (-jnp.arange(0, MLA_ROPE, 2, dtype=jnp.float32) / MLA_ROPE)
    ang = pos.astype(jnp.float32)[:, None] * inv[None, :]
    cos, sin = jnp.cos(ang), jnp.sin(ang)
    widths = ((0, 0), (MLA_NOPE, LANES - MLA_NOPE - MLA_ROPE))
    return jnp.concatenate([jnp.pad(jnp.concatenate([cos, cos], axis=1), widths),
                            jnp.pad(jnp.concatenate([-sin, sin], axis=1), widths)], axis=1)


def _prep_weights(mla_w_dq, mla_w_uq, mla_w_dkv, mla_w_uk, mla_w_uv, mla_w_o, w_kv_shared, swa_w_q, swa_w_o, scale):
    ql, qcols = mla_w_uq.shape
    c_dim, heads, nope = mla_w_uk.shape
    rope = qcols // heads - nope
    d = mla_w_dq.shape[0]
    pad = LANES - nope - rope
    w = {}
    w["wq"] = jnp.pad(mla_w_uq.reshape(ql, heads, nope + rope), ((0, 0), (0, 0), (0, pad))).reshape(
        ql, heads * LANES) * scale
    w["wdq"] = mla_w_dq
    w["wlat"] = mla_w_dkv[:, :c_dim]
    w["wkr"] = jnp.pad(mla_w_dkv[:, c_dim:], ((0, 0), (nope, pad)))
    zc = jnp.zeros((c_dim, heads, LANES - nope), F32)
    w["wuk"] = jnp.concatenate([mla_w_uk, zc], axis=-1).reshape(c_dim, heads * LANES)
    w["wuv_t"] = jnp.pad(mla_w_uv, ((0, 0), (0, 0), (0, MLA_VT_ROWS - mla_w_uv.shape[2]))).reshape(
        c_dim, heads * MLA_VT_ROWS).T
    w["wuk_dec"] = jnp.concatenate([jnp.transpose(mla_w_uk, (1, 2, 0)),
                                    jnp.zeros((heads, LANES - nope, c_dim), F32)], axis=1)
    w["wuv_dec"] = jnp.concatenate([jnp.transpose(mla_w_uv, (1, 0, 2)),
                                    jnp.zeros((heads, c_dim, LANES - nope), F32)], axis=-1)
    vdim = mla_w_uv.shape[2]
    assert vdim == MLA_V and nope == MLA_NOPE and rope == MLA_ROPE
    w["wo"] = mla_w_o
    w["wo_pad"] = jnp.concatenate([mla_w_o.reshape(heads, vdim, d), jnp.zeros((heads, LANES - vdim, d), F32)],
                                  axis=1).reshape(heads * LANES, d)
    kvw = w_kv_shared.shape[1] // 2
    groups = kvw // SWA_HEAD_DIM
    s_heads = swa_w_q.shape[1] // SWA_HEAD_DIM
    rep = s_heads // groups
    w["wkv"] = w_kv_shared
    eye = jnp.eye(SWA_HEAD_DIM, dtype=F32)
    zero = jnp.zeros_like(eye)
    w["rep_k2"] = jnp.kron(jnp.eye(groups, dtype=F32), jnp.concatenate([eye, zero, zero, eye], axis=1))
    w["eye_kv"] = jnp.eye(kvw, dtype=F32)
    w["swa_wq_t"] = swa_w_q.T
    w["swa_wo"] = swa_w_o
    w["rep_kv"] = jnp.kron(jnp.eye(groups, dtype=F32), jnp.tile(eye, (1, rep)))
    w["swa_wq"] = swa_w_q
    return {k: v.astype(BF16) for k, v in w.items()}, dict(heads=heads, c_dim=c_dim, groups=groups, rep=rep,
                                                            s_heads=s_heads, kvw=kvw)


def _rows_a(x, P, w, tab, heads, c_dim, absorbed):
    t, d = x.shape
    tm = _row_tile(t)
    hw = heads * LANES
    row = lambda v: v.reshape(1, -1)
    ins = ([_rows(x, tm)] + _ffn_ins(P, w, 1, 0)
           + [_const(row(P["mix_norm"][0])), _const(w["wdq"]), _const(row(P["mla_q_norm"][0])), _const(w["wq"]),
              _const(w["wlat"]), _const(row(P["mla_kv_norm"][0])), _const(w["wkr"]), _cycle(tab, tm)]
           + ([_const(w["wuk_dec"])] if absorbed else [_const(w["wuk"]), _const(w["wuv_t"])]))
    outs = [_rows_out(t, tm, d, F32), _rows_out(t, tm, hw, BF16)]
    if absorbed:
        outs.append(_rows_out(t, tm, heads * c_dim, BF16))
    else:
        outs += [_rows_out(t, tm, hw, BF16),
                 (jax.ShapeDtypeStruct((heads * MLA_VT_ROWS, t), BF16),
                  pl.BlockSpec((heads * MLA_VT_ROWS, tm), lambda i: (0, i)))]
    outs += [_rows_out(t, tm, c_dim, F32), _rows_out(t, tm, MLA_ROPE, F32)]
    return _rows_call(functools.partial(_rows_a_kernel, heads=heads, c_dim=c_dim, absorbed=absorbed),
                      "rows_a_absorbed" if absorbed else "rows_a", t, tm, ins, outs)


def _rows_b(a, x, P, w, meta, decode):
    t, d = x.shape
    tm = _row_tile(t)
    kvw = meta["kvw"]
    ins = [_rows(a, tm)]
    if decode:
        ins += [_const(w["wuv_dec"]), _const(w["wo_pad"])]
    else:
        ins += [_const(w["wo"])]
    ins += [_rows(x, tm)] + _ffn_ins(P, w, 2, 0) + [_const(P["kv_norm"].reshape(1, -1)), _const(w["wkv"])]
    outs = [_rows_out(t, tm, d, F32), _rows_out(t, tm, 2 * kvw, F32)]
    if not decode:
        ins += [_const(w["rep_k2"]), _const(w["eye_kv"])]
        outs += [_rows_out(t, tm, w["rep_k2"].shape[1], BF16),
                 (jax.ShapeDtypeStruct((kvw, t), BF16), pl.BlockSpec((kvw, tm), lambda i: (0, i)))]
    return _rows_call(functools.partial(_rows_b_kernel, dec_heads=meta["heads"] if decode else 0,
                                        c_dim=meta["c_dim"], tiled_kv=not decode),
                      "rows_b_decode" if decode else "rows_b", t, tm, ins, outs)


def _rows_c(x, P, w, wq, name, transposed):
    t, d = x.shape
    tm = _row_tile(t)
    ins = [_rows(x, tm)] + _ffn_ins(P, w, 1, 1) + [_const(P["mix_norm"][1].reshape(1, -1)), _const(wq)]
    if transposed:
        q_out = (jax.ShapeDtypeStruct((wq.shape[0], t), BF16), pl.BlockSpec((wq.shape[0], tm), lambda i: (0, i)))
    else:
        q_out = _rows_out(t, tm, wq.shape[1], BF16)
    return _rows_call(functools.partial(_rows_c_kernel, scale=SWA_HEAD_DIM ** -0.5 * LOG2E, transposed=transposed),
                      name, t, tm, ins, [_rows_out(t, tm, d, F32), q_out])


def _rows_d(a, x, P, w, wo, name):
    t, d = x.shape
    tm = _row_tile(t)
    ins = ([_rows(a, tm), _const(wo), _rows(x, tm)] + _ffn_ins(P, w, 2, 1)
           + [_const(P["final_norm"].reshape(1, -1))])
    return _rows_call(_rows_d_kernel, name, t, tm, ins, [_rows_out(t, tm, d, F32)])[0]


def _trunk_prompt(x3, P, w, meta):
    batch, seq, d = x3.shape
    heads, c_dim, kvw = meta["heads"], meta["c_dim"], meta["kvw"]
    x = x3.reshape(batch * seq, d)
    x, q, k, vt, lat, kr = _rows_a(x, P, w, _rope_table(jnp.arange(seq)), heads, c_dim, absorbed=False)
    o = _mla_attn_prompt(q, k, vt, batch, seq, heads)
    x, kv, k2, v_t = _rows_b(o, x, P, w, meta, decode=False)
    x, q_t = _rows_c(x, P, w, w["swa_wq_t"], "rows_c", transposed=True)
    blk = 2 * CHUNK
    rel = (jnp.arange(2 * blk) - blk)[:, None] - jnp.arange(blk)[None, :]
    bias_t = _bias_table(P["rel_bias"], _rel_bucket(rel).astype(jnp.int32))
    o = _swa_prompt(q_t, k2, v_t, bias_t, P["swa_sinks"][0], batch, seq, meta["s_heads"], meta["rep"], blk)
    y = _rows_d(o, x, P, w, w["swa_wo"], "rows_d")
    keep = min(WINDOW, seq)
    kv3 = kv.reshape(batch, seq, 2 * kvw)[:, seq - keep:]
    new_k = kv3[:, :, :kvw].reshape(batch, keep, meta["groups"], SWA_HEAD_DIM)
    new_v = kv3[:, :, kvw:].reshape(batch, keep, meta["groups"], SWA_HEAD_DIM)
    return (y.reshape(batch, seq, d), lat.reshape(1, batch, seq, c_dim), kr.reshape(1, batch, seq, MLA_ROPE),
            new_k, new_v)


def _trunk_decode(x3, cache_lat, cache_rope, cache_k, cache_v, P, w, meta):
    batch, n_new, d = x3.shape
    heads, c_dim, kvw, s_heads = meta["heads"], meta["c_dim"], meta["kvw"], meta["s_heads"]
    past = cache_lat.shape[2]
    w_c = cache_k.shape[1]
    qpos = past + jnp.arange(n_new)
    assert past % CHUNK == 0 and n_new <= CHUNK and w_c <= WINDOW_CHUNKS * CHUNK and w_c <= past
    t = batch * n_new
    x = x3.reshape(t, d)
    tm = _row_tile(t)
    assert tm % n_new == 0
    tab = jnp.tile(_rope_table(qpos), (tm // n_new, 1))
    x, q, ql, lat, kr = _rows_a(x, P, w, tab, heads, c_dim, absorbed=True)
    ol = _mla_attn_decode(ql, q, cache_lat, jnp.swapaxes(cache_rope, 2, 3), lat, kr, heads, n_new)
    x, kv = _rows_b(ol, x, P, w, meta, decode=True)
    x, qs = _rows_c(x, P, w, w["swa_wq"], "rows_c_decode", transposed=False)
    kpos = jnp.arange(past - w_c, past + n_new)
    bias_h = _bias_table(P["rel_bias"], _rel_bucket(kpos[None, :] - qpos[:, None]).astype(jnp.int32))
    bias = bias_h.reshape(s_heads * n_new, w_c + n_new)
    sink_rows = jnp.repeat(P["swa_sinks"][0], n_new).reshape(s_heads * n_new, 1)
    const2 = lambda a: pl.BlockSpec(a.shape, lambda b: (0, 0), pipeline_mode=pl.Buffered(1))
    o = pl.pallas_call(
        functools.partial(_swa_decode_kernel, kv_w=kvw, groups=meta["groups"], rep=meta["rep"]),
        grid=(batch,),
        in_specs=[pl.BlockSpec((n_new, qs.shape[1]), lambda b: (b, 0)),
                  pl.BlockSpec((None, w_c, kvw), lambda b: (b, 0, 0)),
                  pl.BlockSpec((None, w_c, kvw), lambda b: (b, 0, 0)),
                  pl.BlockSpec((n_new, 2 * kvw), lambda b: (b, 0)),
                  const2(w["rep_kv"]), const2(bias), const2(sink_rows)],
        out_specs=pl.BlockSpec((n_new, qs.shape[1]), lambda b: (b, 0)),
        out_shape=jax.ShapeDtypeStruct(qs.shape, BF16),
        compiler_params=_params(1), name="swa_decode",
    )(qs, cache_k.reshape(batch, w_c, kvw), cache_v.reshape(batch, w_c, kvw), kv, w["rep_kv"], bias, sink_rows)
    y = _rows_d(o, x, P, w, w["swa_wo"], "rows_d_decode")
    kv3 = kv.reshape(batch, n_new, 2 * kvw)
    new_k = kv3[:, :, :kvw].reshape(batch, n_new, meta["groups"], SWA_HEAD_DIM)
    new_v = kv3[:, :, kvw:].reshape(batch, n_new, meta["groups"], SWA_HEAD_DIM)
    return (y.reshape(batch, n_new, d), lat.reshape(1, batch, n_new, c_dim), kr.reshape(1, batch, n_new, MLA_ROPE),
            new_k, new_v)


def kernel(x_prompt, x_sample, cache_mla_latent, cache_mla_krope, cache_swa_k, cache_swa_v, ffn_norm1, ffn1_w_gate, ffn1_w_up, ffn1_w_down, mix_norm, ffn_norm2, ffn2_w_gate, ffn2_w_up, ffn2_w_down, mla_w_dq, mla_q_norm, mla_w_uq, mla_w_dkv, mla_kv_norm, mla_w_uk, mla_w_uv, mla_w_o, kv_norm, w_kv_shared, swa_w_q, swa_sinks, swa_w_o, rel_bias, final_norm):
    assert ffn_norm1.shape[0] == 2 and mla_w_dq.shape[0] == 1 and swa_w_q.shape[0] == 1
    scale = (MLA_NOPE + MLA_ROPE) ** -0.5 * LOG2E
    w, meta = _prep_weights(mla_w_dq[0], mla_w_uq[0], mla_w_dkv[0], mla_w_uk[0], mla_w_uv[0], mla_w_o[0],
                            w_kv_shared, swa_w_q[0], swa_w_o[0], scale)
    gate_up = lambda g, u: jnp.concatenate([g, u], axis=-1).astype(BF16)
    w.update(f1gu=gate_up(ffn1_w_gate, ffn1_w_up), f1d=ffn1_w_down.astype(BF16),
             f2gu=gate_up(ffn2_w_gate, ffn2_w_up), f2d=ffn2_w_down.astype(BF16))
    P = dict(ffn_norm1=ffn_norm1, mix_norm=mix_norm, ffn_norm2=ffn_norm2, mla_q_norm=mla_q_norm,
             mla_kv_norm=mla_kv_norm, kv_norm=kv_norm, swa_sinks=swa_sinks, rel_bias=rel_bias, final_norm=final_norm)
    y_p, lat_p, rope_p, k_p, v_p = _trunk_prompt(x_prompt, P, w, meta)
    y_s, lat_s, rope_s, k_s, v_s = _trunk_decode(x_sample, cache_mla_latent, cache_mla_krope,
                                                 cache_swa_k, cache_swa_v, P, w, meta)
    return (y_p, y_s, lat_p, rope_p, k_p, v_p, lat_s, rope_s, k_s, v_s)
```

```python
import functools
import math

import jax
import jax.numpy as jnp
from jax import lax
from jax.experimental import pallas as pl
from jax.experimental.pallas import tpu as pltpu

F32 = jnp.float32
BF16 = jnp.bfloat16

CHUNK = 64
RMS_EPS = 1e-6
FFN_RES = 0.5
ROPE_BASE = 10000.0
WINDOW = 128
WINDOW_CHUNKS = WINDOW // CHUNK
N_BUCKETS = 32
MAX_DISTANCE = 128
NEG_INF = -1e30
LOG2E = math.log2(math.e)
MLA_NOPE = 64
MLA_ROPE = 32
MLA_V = 64
MLA_VT_ROWS = 80
SWA_HEAD_DIM = 64

LANES = 128
ROW_TILE = 512
MLA_BQ = 512
MLA_BK = 512
MLA_DIAG = 256
MLA_HEADS_PER_STEP = 8
MLA_DECODE_CHUNK = 1024
MLA_DECODE_BATCH = 2
SWA_DECODE_BATCH = 4
SWA_BLOCKS_PER_STEP = 4
VMEM_LIMIT = 60 * 1024 * 1024


def _params(n_axes):
    return pltpu.CompilerParams(dimension_semantics=("arbitrary",) * n_axes, vmem_limit_bytes=VMEM_LIMIT)


def _rms(xf, g):
    return xf * lax.rsqrt(jnp.mean(xf * xf, axis=-1, keepdims=True) + RMS_EPS) * g


def _dot(a, b):
    return jnp.dot(a, b, preferred_element_type=F32)


def _dot_nt(a, b):
    return lax.dot_general(a, b, (((1,), (1,)), ((), ())), preferred_element_type=F32)


def _row_tile(t):
    tm = min(ROW_TILE, t)
    assert t % tm == 0, (t, tm)
    return tm


def _rows(a, tm):
    return a, pl.BlockSpec((tm, a.shape[1]), lambda i: (i, 0))


def _const(a):
    nd = a.ndim
    return a, pl.BlockSpec(a.shape, lambda i: (0,) * nd, pipeline_mode=pl.Buffered(1))


def _layer(a, layer):
    nd = a.ndim - 1
    return a, pl.BlockSpec((None,) + tuple(a.shape[1:]), lambda i: (layer,) + (0,) * nd,
                           pipeline_mode=pl.Buffered(1))


def _cycle(a, tm):
    assert a.shape[0] % tm == 0
    nb = a.shape[0] // tm
    return a, pl.BlockSpec((tm, a.shape[1]), lambda i: (i % nb, 0))


def _rows_out(t, tm, ncols, dtype):
    return jax.ShapeDtypeStruct((t, ncols), dtype), pl.BlockSpec((tm, ncols), lambda i: (i, 0))


def _rows_call(body, name, t, tm, ins, outs):
    return pl.pallas_call(
        body, grid=(t // tm,),
        in_specs=[s for _, s in ins], out_specs=[s for _, s in outs], out_shape=[o for o, _ in outs],
        compiler_params=_params(1), name=name,
    )(*[a for a, _ in ins])


def _ffn_ins(P, w, which, layer):
    return [_const(P["ffn_norm%d" % which][layer].reshape(1, -1)), _layer(w["f%dg" % which], layer),
            _layer(w["f%du" % which], layer), _layer(w["f%dd" % which], layer)]


def _ffn_apply(x, g_ref, wg_ref, wu_ref, wd_ref):
    h = _rms(x, g_ref[...]).astype(BF16)
    gate = _dot(h, wg_ref[...])
    up = _dot(h, wu_ref[...])
    a = (gate * jax.nn.sigmoid(gate) * up).astype(BF16)
    return x + FFN_RES * _dot(a, wd_ref[...])


def _mla_proj_apply(x, mg_ref, wdq_ref, qn_ref, wq_ref, wlat_ref, kvn_ref, wkr_ref, tab_ref,
                    refs, heads, c_dim, absorbed):
    cos_t, sin_t = tab_ref[:, 0:LANES], tab_ref[:, LANES:2 * LANES]
    lane = lax.broadcasted_iota(jnp.int32, (1, LANES), 1)
    cos_q = jnp.where(lane < MLA_NOPE, 1.0, cos_t)
    first_half = lane < MLA_NOPE + MLA_ROPE // 2

    def swap_halves(v):
        return jnp.where(first_half, pltpu.roll(v, LANES - MLA_ROPE // 2, 1), pltpu.roll(v, MLA_ROPE // 2, 1))

    h = _rms(x, mg_ref[...]).astype(BF16)
    cq = _rms(_dot(h, wdq_ref[...]), qn_ref[...]).astype(BF16)
    qa = _dot(cq, wq_ref[...])
    lat = _rms(_dot(h, wlat_ref[...]), kvn_ref[...])
    kr = _dot(h, wkr_ref[...])
    kr = kr * cos_t + swap_halves(kr) * sin_t
    latb = lat.astype(BF16)
    if absorbed:
        wukd_ref, q_ref, ql_ref, lat_ref, kr_ref = refs
    else:
        wuk_ref, wuvt_ref, q_ref, k_ref, vt_ref, lat_ref, kr_ref = refs
        kn = _dot(latb, wuk_ref[...])
    lat_ref[...] = lat
    kr_ref[...] = kr[:, MLA_NOPE:MLA_NOPE + MLA_ROPE]
    for hh in range(heads):
        sl = slice(hh * LANES, (hh + 1) * LANES)
        qh = (qa[:, sl] * cos_q + swap_halves(qa[:, sl]) * sin_t).astype(BF16)
        q_ref[:, sl] = qh
        if absorbed:
            ql_ref[:, hh * c_dim:(hh + 1) * c_dim] = _dot(qh, wukd_ref[hh]).astype(BF16)
        else:
            k_ref[:, sl] = (kn[:, sl] + kr).astype(BF16)
    if not absorbed:
        vt = _dot_nt(wuvt_ref[...], latb)
        ones_row = lax.broadcasted_iota(jnp.int32, vt.shape, 0) % MLA_VT_ROWS == MLA_V
        vt_ref[...] = jnp.where(ones_row, 1.0, vt).astype(BF16)


def _rows_a_kernel(x_ref, fg, wg, wu, wd, mg, wdq, qn, wq, wlat, kvn, wkr, tab, *refs,
                   heads, c_dim, absorbed):
    n_extra = 1 if absorbed else 2
    x1_ref = refs[n_extra]
    x1 = _ffn_apply(x_ref[...], fg, wg, wu, wd)
    x1_ref[...] = x1
    _mla_proj_apply(x1, mg, wdq, qn, wq, wlat, kvn, wkr, tab, refs[:n_extra] + refs[n_extra + 1:],
                    heads, c_dim, absorbed)


def _rows_b_kernel(a_ref, *refs, dec_heads, c_dim, tiled_kv):
    if dec_heads:
        wuv_ref, refs = refs[0], refs[1:]
        a = jnp.concatenate([_dot(a_ref[:, hh * c_dim:(hh + 1) * c_dim], wuv_ref[hh]).astype(BF16)
                             for hh in range(dec_heads)], axis=1)
    else:
        a = a_ref[...]
    wo, x_ref, fg, wg, wu, wd, kvg, wkv = refs[:8]
    x = _ffn_apply(x_ref[...] + _dot(a, wo[...]), fg, wg, wu, wd)
    kv = _dot(_rms(x, kvg[...]).astype(BF16), wkv[...])
    if tiled_kv:
        repk, eye, x_out, kv_out, k2_out, vt_out = refs[8:]
        half = kv.shape[1] // 2
        k2_out[...] = _dot(kv[:, :half].astype(BF16), repk[...]).astype(BF16)
        vt_out[...] = _dot_nt(eye[...], kv[:, half:].astype(BF16)).astype(BF16)
    else:
        x_out, kv_out = refs[8:]
    x_out[...] = x
    kv_out[...] = kv


def _rows_c_kernel(x_ref, fg, wg, wu, wd, mg, wq, x_out, q_out, *, scale, transposed):
    x = _ffn_apply(x_ref[...], fg, wg, wu, wd)
    x_out[...] = x
    h = _rms(x, mg[...]).astype(BF16)
    q = _dot_nt(wq[...], h) if transposed else _dot(h, wq[...])
    q_out[...] = (q * scale).astype(BF16)


def _rows_d_kernel(a_ref, wo, x_ref, fg, wg, wu, wd, fin_g, y_out):
    x = _ffn_apply(x_ref[...] + _dot(a_ref[...], wo[...]), fg, wg, wu, wd)
    y_out[...] = _rms(x, fin_g[...])


def _mla_attn_kernel(q_ref, k_ref, vt_ref, o_ref, m_sc, acc_sc, sa_sc, sb_sc, *, bq, bk, hp):
    i = pl.program_id(2)
    qs = [q_ref[:, hh * LANES:(hh + 1) * LANES] for hh in range(hp)]
    sls = [slice(hh * LANES, (hh + 1) * LANES) for hh in range(hp)]

    def own_scores(start, width, q_lo):
        ss = [_dot_nt(k_ref[pl.ds(start, width), sls[hh]], qs[hh][q_lo:]) for hh in range(hp)]
        kc = (start + lax.broadcasted_iota(jnp.int32, (width, 1), 0)) // CHUNK
        qc = (i * bq + q_lo + lax.broadcasted_iota(jnp.int32, (1, bq - q_lo), 1)) // CHUNK
        return [jnp.where(kc <= qc, s, NEG_INF) for s in ss]

    def own_update(ss, start, width, q_lo, after_head=None):
        ql = slice(q_lo, bq)
        for hh in range(hp):
            m = m_sc[hh, :, ql]
            m_new = jnp.maximum(m, jnp.max(ss[hh], axis=0, keepdims=True))
            alpha = jnp.exp2(m - m_new)
            p = jnp.exp2(ss[hh] - m_new).astype(BF16)
            vt = vt_ref[hh * MLA_VT_ROWS:(hh + 1) * MLA_VT_ROWS, pl.ds(start, width)]
            acc_sc[hh, :, ql] = alpha * acc_sc[hh, :, ql] + _dot(vt, p)
            m_sc[hh, :, ql] = m_new
            if after_head is not None:
                after_head(hh)

    def produce(hh, j, dst):
        dst[hh] = _dot_nt(k_ref[pl.ds(pl.multiple_of(j * bk, bk), bk), sls[hh]], qs[hh])

    def consume(hh, j, src):
        s = src[hh]
        m = m_sc[hh]
        m_new = jnp.maximum(m, jnp.max(s, axis=0, keepdims=True))
        alpha = jnp.exp2(m - m_new)
        p = jnp.exp2(s - m_new).astype(BF16)
        vt = vt_ref[hh * MLA_VT_ROWS:(hh + 1) * MLA_VT_ROWS, pl.ds(pl.multiple_of(j * bk, bk), bk)]
        acc_sc[hh] = alpha * acc_sc[hh] + _dot(vt, p)
        m_sc[hh] = m_new

    def transition(j, src, dst):
        produce(0, j + 1, dst)
        for hh in range(hp):
            consume(hh, j, src)
            if hh + 1 < hp:
                produce(hh + 1, j + 1, dst)

    m_sc[...] = jnp.full(m_sc.shape, NEG_INF, F32)
    acc_sc[...] = jnp.zeros(acc_sc.shape, F32)
    n_full = (i * bq) // bk

    diag = min(MLA_DIAG, bq)
    starts = [pl.multiple_of(i * bq + t * diag, diag) for t in range(bq // diag)]
    own = [own_scores(starts[t], diag, t * diag) for t in range(bq // diag)]
    for t in range(bq // diag):
        own_update(own[t], starts[t], diag, t * diag,
                   after_head=(lambda hh: produce(hh, 0, sa_sc)) if t == 0 else None)

    @pl.when(n_full > 0)
    def _():
        last = n_full - 1

        @pl.loop(0, last // 2)
        def _(t):
            transition(2 * t, sa_sc, sb_sc)
            transition(2 * t + 1, sb_sc, sa_sc)

        @pl.when(last % 2 == 1)
        def _():
            transition(last - 1, sa_sc, sb_sc)
            for hh in range(hp):
                consume(hh, last, sb_sc)

        @pl.when(last % 2 == 0)
        def _():
            for hh in range(hp):
                consume(hh, last, sa_sc)

    for pair in range(hp // 2):
        halves = []
        for hh in (2 * pair, 2 * pair + 1):
            acc = acc_sc[hh]
            halves.append(acc[:MLA_V] * (1.0 / acc[MLA_V:MLA_V + 1]))
        o_ref[:, pair * LANES:(pair + 1) * LANES] = jnp.concatenate(halves, axis=0).T.astype(BF16)


def _mla_attn_prompt(q, k, vt, batch, seq, heads, bq=MLA_BQ, bk=MLA_BK, hp=MLA_HEADS_PER_STEP):
    bq, bk = min(bq, seq), min(bk, seq)
    nq = seq // bq
    assert seq % bq == 0 and seq % bk == 0 and bq % CHUNK == 0 and heads % hp == 0 and hp % 2 == 0
    assert bq % bk == 0 and bq % min(MLA_DIAG, bq) == 0 and MLA_DIAG % CHUNK == 0
    w = hp * LANES
    return pl.pallas_call(
        functools.partial(_mla_attn_kernel, bq=bq, bk=bk, hp=hp),
        grid=(batch, heads // hp, nq),
        in_specs=[pl.BlockSpec((bq, w), lambda b, h, i: (b * nq + i, h)),
                  pl.BlockSpec((seq, w), lambda b, h, i: (b, h)),
                  pl.BlockSpec((hp * MLA_VT_ROWS, seq), lambda b, h, i: (h, b))],
        out_specs=pl.BlockSpec((bq, hp * MLA_V), lambda b, h, i: (b * nq + i, h)),
        out_shape=jax.ShapeDtypeStruct((q.shape[0], heads * MLA_V), BF16),
        scratch_shapes=[pltpu.VMEM((hp, 1, bq), F32), pltpu.VMEM((hp, MLA_VT_ROWS, bq), F32),
                        pltpu.VMEM((hp, bk, bq), F32), pltpu.VMEM((hp, bk, bq), F32)],
        compiler_params=_params(3), name="mla_attn_prompt",
    )(q, k, vt)


def _mla_attn_decode_kernel(ql_ref, q_ref, cl_ref, crt_ref, nl_ref, nr_ref, o_ref, *, heads, c_dim, n_new):
    nb = cl_ref.shape[0]
    past = cl_ref.shape[1]
    chunk = min(past, MLA_DECODE_CHUNK)
    n_parts = past // chunk + 1
    rows = [slice(bb * n_new, (bb + 1) * n_new) for bb in range(nb)]
    qls = [jnp.concatenate([ql_ref[rows[bb], hh * c_dim:(hh + 1) * c_dim] for hh in range(heads)], axis=0)
           for bb in range(nb)]
    qrs = [jnp.concatenate([q_ref[rows[bb], hh * LANES + MLA_NOPE:hh * LANES + MLA_NOPE + MLA_ROPE]
                            for hh in range(heads)], axis=0) for bb in range(nb)]

    def scores(bb, c):
        if c + 1 < n_parts:
            kl = cl_ref[bb, c * chunk:(c + 1) * chunk, :].astype(BF16)
            return kl, _dot_nt(qls[bb], kl) + _dot(qrs[bb], crt_ref[bb, :, c * chunk:(c + 1) * chunk].astype(BF16))
        kl = nl_ref[rows[bb], :].astype(BF16)
        return kl, _dot_nt(qls[bb], kl) + _dot_nt(qrs[bb], nr_ref[rows[bb], :].astype(BF16))

    state = [(jnp.full((heads * n_new, 1), NEG_INF, F32), jnp.zeros((heads * n_new, 1), F32),
              jnp.zeros((heads * n_new, c_dim), F32)) for _ in range(nb)]
    nxt = [scores(bb, 0) for bb in range(nb)]
    for c in range(n_parts):
        for bb in range(nb):
            kl, s = nxt[bb]
            if c + 1 < n_parts:
                nxt[bb] = scores(bb, c + 1)
            m, l, acc = state[bb]
            m_new = jnp.maximum(m, jnp.max(s, axis=-1, keepdims=True))
            alpha = jnp.exp2(m - m_new)
            p = jnp.exp2(s - m_new)
            l = alpha * l + jnp.sum(p, axis=-1, keepdims=True)
            acc = alpha * acc + _dot(p.astype(BF16), kl)
            state[bb] = (m_new, l, acc)
    for bb in range(nb):
        _, l, acc = state[bb]
        o = acc / l
        for hh in range(heads):
            o_ref[rows[bb], hh * c_dim:(hh + 1) * c_dim] = o[hh * n_new:(hh + 1) * n_new].astype(BF16)


def _mla_attn_decode(ql, q, cache_lat, cache_rope_t, new_lat, new_rope, heads, n_new, nb=MLA_DECODE_BATCH):
    _, batch, past, c_dim = cache_lat.shape
    r_dim = cache_rope_t.shape[2]
    nb = math.gcd(nb, batch)
    return pl.pallas_call(
        functools.partial(_mla_attn_decode_kernel, heads=heads, c_dim=c_dim, n_new=n_new),
        grid=(batch // nb,),
        in_specs=[pl.BlockSpec((nb * n_new, heads * c_dim), lambda b: (b, 0)),
                  pl.BlockSpec((nb * n_new, heads * LANES), lambda b: (b, 0)),
                  pl.BlockSpec((None, nb, past, c_dim), lambda b: (0, b, 0, 0)),
                  pl.BlockSpec((None, nb, r_dim, past), lambda b: (0, b, 0, 0)),
                  pl.BlockSpec((nb * n_new, c_dim), lambda b: (b, 0)),
                  pl.BlockSpec((nb * n_new, r_dim), lambda b: (b, 0))],
        out_specs=pl.BlockSpec((nb * n_new, heads * c_dim), lambda b: (b, 0)),
        out_shape=jax.ShapeDtypeStruct(ql.shape, BF16),
        compiler_params=_params(1), name="mla_attn_decode",
    )(ql, q, cache_lat, cache_rope_t, new_lat, new_rope)


def _bias_kernel(tab_ref, idx_ref, o_ref, *, heads):
    idx = idx_ref[...]
    for hh in range(heads):
        acc = jnp.zeros(idx.shape, F32)
        for b in range(N_BUCKETS):
            acc = jnp.where(idx == b, tab_ref[b, hh] * LOG2E, acc)
        o_ref[hh] = acc


def _bias_table(rel_bias, idx):
    heads = rel_bias.shape[1]
    return pl.pallas_call(
        functools.partial(_bias_kernel, heads=heads),
        in_specs=[pl.BlockSpec(memory_space=pltpu.SMEM), pl.BlockSpec(idx.shape, lambda: (0, 0))],
        out_specs=pl.BlockSpec((heads,) + idx.shape, lambda: (0, 0, 0)),
        out_shape=jax.ShapeDtypeStruct((heads,) + idx.shape, F32), name="rel_bias_table",
    )(rel_bias, idx)


def _rel_bucket(rel):
    half = N_BUCKETS // 2
    max_exact = half // 2
    base = jnp.where(rel > 0, half, 0)
    n = jnp.abs(rel)
    nf = jnp.maximum(n, 1).astype(jnp.float32)
    large = max_exact + (jnp.log(nf / max_exact) / math.log(MAX_DISTANCE / max_exact)
                         * (half - max_exact)).astype(jnp.int32)
    large = jnp.minimum(large, half - 1)
    return base + jnp.where(n < max_exact, n, large)


def _swa_prompt_kernel(sink_ref, qt_ref, kp_ref, kc_ref, vtp_ref, vtc_ref, bias_ref, o_ref, *, s_heads, rep, blk, nsub):
    i = pl.program_id(1)
    key_chunk = lax.broadcasted_iota(jnp.int32, (2 * blk, 1), 0) // CHUNK
    row_chunk = lax.broadcasted_iota(jnp.int32, (1, blk), 1) // CHUNK
    first = blk // CHUNK
    window = (key_chunk >= row_chunk) & (key_chunk <= row_chunk + WINDOW_CHUNKS)
    dh = SWA_HEAD_DIM
    units = [(u, hh) for u in range(nsub) for hh in range(s_heads)]
    ss = []
    for u, hh in units:
        g, pair, slot = hh // rep, hh // 2, hh % 2
        ksl = slice(g * 4 * dh + slot * 2 * dh, g * 4 * dh + (slot + 1) * 2 * dh)
        k = jnp.concatenate([kp_ref[:, ksl], kc_ref[:, ksl]], axis=0)[u * blk:(u + 2) * blk]
        ss.append(_dot(k, qt_ref[pair * 2 * dh:(pair + 1) * 2 * dh, u * blk:(u + 1) * blk]))
    ps, invs = [], []
    for (u, hh), s in zip(units, ss):
        valid = window if u > 0 else window & ((i > 0) | (key_chunk >= first))
        s = jnp.where(valid, s + bias_ref[hh], NEG_INF)
        sink = sink_ref[hh] * LOG2E
        m = jnp.maximum(jnp.max(s, axis=0, keepdims=True), sink)
        p = jnp.exp2(s - m)
        invs.append(1.0 / (jnp.sum(p, axis=0, keepdims=True) + jnp.exp2(sink - m)))
        ps.append(p.astype(BF16))
    outs = []
    for n, (u, hh) in enumerate(units):
        g = hh // rep
        vt = jnp.concatenate([vtp_ref[g * dh:(g + 1) * dh, :], vtc_ref[g * dh:(g + 1) * dh, :]],
                             axis=1)[:, u * blk:(u + 2) * blk]
        outs.append(_dot(vt, ps[n]) * invs[n])
    for u in range(nsub):
        for pair in range(s_heads // 2):
            n = u * s_heads + 2 * pair
            o_ref[u * blk:(u + 1) * blk, pair * 2 * dh:(pair + 1) * 2 * dh] = (
                jnp.concatenate(outs[n:n + 2], axis=0).T.astype(BF16))


def _swa_prompt(qt, k2, vt, bias_t, sinks, batch, seq, s_heads, rep, blk=128, nsub=SWA_BLOCKS_PER_STEP):
    assert seq % (blk * nsub) == 0
    nq = seq // (blk * nsub)
    kw = k2.shape[1]
    cur = lambda b, i: (b * nq + i, 0)
    prev = lambda b, i: ((b * nq + i) * nsub - jnp.minimum(i, 1), 0)
    cur_t = lambda b, i: (0, b * nq + i)
    prev_t = lambda b, i: (0, (b * nq + i) * nsub - jnp.minimum(i, 1))
    assert rep == 4 and kw == (s_heads // rep) * 4 * SWA_HEAD_DIM
    return pl.pallas_call(
        functools.partial(_swa_prompt_kernel, s_heads=s_heads, rep=rep, blk=blk, nsub=nsub),
        grid=(batch, nq),
        in_specs=[pl.BlockSpec(memory_space=pltpu.SMEM),
                  pl.BlockSpec((qt.shape[0], blk * nsub), cur_t),
                  pl.BlockSpec((blk, kw), prev), pl.BlockSpec((blk * nsub, kw), cur),
                  pl.BlockSpec((vt.shape[0], blk), prev_t), pl.BlockSpec((vt.shape[0], blk * nsub), cur_t),
                  pl.BlockSpec(bias_t.shape, lambda b, i: (0, 0, 0), pipeline_mode=pl.Buffered(1))],
        out_specs=pl.BlockSpec((blk * nsub, qt.shape[0]), cur),
        out_shape=jax.ShapeDtypeStruct((qt.shape[1], qt.shape[0]), BF16),
        compiler_params=_params(2), name="swa_prompt",
    )(sinks, qt, k2, k2, vt, vt, bias_t)


def _swa_decode_kernel(q_ref, ck_ref, cv_ref, nkv_ref, rep_ref, bias_ref, sink_ref, o_ref, *, kv_w, groups, rep, n_new):
    nb = ck_ref.shape[0]
    n = n_new
    gw = rep * SWA_HEAD_DIM
    lane_head = lax.broadcasted_iota(jnp.int32, (1, gw), 1) // SWA_HEAD_DIM
    sls = [slice(g * gw, (g + 1) * gw) for g in range(groups)]
    units = [(bb, g) for bb in range(nb) for g in range(groups)]
    k4s, v4s = [], []
    for bb in range(nb):
        new = nkv_ref[bb * n:(bb + 1) * n, :]
        k = jnp.concatenate([ck_ref[bb], new[:, 0:kv_w]], axis=0).astype(BF16)
        v = jnp.concatenate([cv_ref[bb], new[:, kv_w:2 * kv_w]], axis=0).astype(BF16)
        k4s.append(_dot(k, rep_ref[...]).astype(BF16))
        v4s.append(_dot(v, rep_ref[...]).astype(BF16))
    ss = []
    for bb, g in units:
        qg = q_ref[bb * n:(bb + 1) * n, sls[g]]
        qs = jnp.concatenate([jnp.where(lane_head == r, qg, jnp.zeros_like(qg)) for r in range(rep)], axis=0)
        ss.append(_dot_nt(qs, k4s[bb][:, sls[g]]))
    ps = []
    for (bb, g), s in zip(units, ss):
        rows = slice(g * rep * n, (g + 1) * rep * n)
        s = s + bias_ref[rows, :]
        sink = sink_ref[rows, :] * LOG2E
        m = jnp.maximum(jnp.max(s, axis=-1, keepdims=True), sink)
        p = jnp.exp2(s - m)
        inv = 1.0 / (jnp.sum(p, axis=-1, keepdims=True) + jnp.exp2(sink - m))
        ps.append((p * inv).astype(BF16))
    for (bb, g), p in zip(units, ps):
        res = _dot(p, v4s[bb][:, sls[g]])
        og = jnp.zeros((n, gw), F32)
        for r in range(rep):
            og = jnp.where(lane_head == r, res[r * n:(r + 1) * n], og)
        o_ref[bb * n:(bb + 1) * n, sls[g]] = og.astype(BF16)


def _rope_table(pos):
    inv = ROPE_BASE ** (-jnp.arange(0, MLA_ROPE, 2, dtype=jnp.float32) / MLA_ROPE)
    ang = pos.astype(jnp.float32)[:, None] * inv[None, :]
    cos, sin = jnp.cos(ang), jnp.sin(ang)
    widths = ((0, 0), (MLA_NOPE, LANES - MLA_NOPE - MLA_ROPE))
    return jnp.concatenate([jnp.pad(jnp.concatenate([cos, cos], axis=1), widths),
                            jnp.pad(jnp.concatenate([-sin, sin], axis=1), widths)], axis=1)


def _prep_weights(mla_w_dq, mla_w_uq, mla_w_dkv, mla_w_uk, mla_w_uv, mla_w_o, w_kv_shared, swa_w_q, swa_w_o, scale):
    ql, qcols = mla_w_uq.shape
    c_dim, heads, nope = mla_w_uk.shape
    rope = qcols // heads - nope
    d = mla_w_dq.shape[0]
    pad = LANES - nope - rope
    w = {}
    w["wq"] = jnp.pad(mla_w_uq.reshape(ql, heads, nope + rope), ((0, 0), (0, 0), (0, pad))).reshape(
        ql, heads * LANES) * scale
    w["wdq"] = mla_w_dq
    w["wlat"] = mla_w_dkv[:, :c_dim]
    w["wkr"] = jnp.pad(mla_w_dkv[:, c_dim:], ((0, 0), (nope, pad)))
    zc = jnp.zeros((c_dim, heads, LANES - nope), F32)
    w["wuk"] = jnp.concatenate([mla_w_uk, zc], axis=-1).reshape(c_dim, heads * LANES)
    w["wuv_t"] = jnp.pad(mla_w_uv, ((0, 0), (0, 0), (0, MLA_VT_ROWS - mla_w_uv.shape[2]))).reshape(
        c_dim, heads * MLA_VT_ROWS).T
    w["wuk_dec"] = jnp.concatenate([jnp.transpose(mla_w_uk, (1, 2, 0)),
                                    jnp.zeros((heads, LANES - nope, c_dim), F32)], axis=1)
    w["wuv_dec"] = jnp.concatenate([jnp.transpose(mla_w_uv, (1, 0, 2)),
                                    jnp.zeros((heads, c_dim, LANES - nope), F32)], axis=-1)
    vdim = mla_w_uv.shape[2]
    assert vdim == MLA_V and nope == MLA_NOPE and rope == MLA_ROPE
    w["wo"] = mla_w_o
    w["wo_pad"] = jnp.concatenate([mla_w_o.reshape(heads, vdim, d), jnp.zeros((heads, LANES - vdim, d), F32)],
                                  axis=1).reshape(heads * LANES, d)
    kvw = w_kv_shared.shape[1] // 2
    groups = kvw // SWA_HEAD_DIM
    s_heads = swa_w_q.shape[1] // SWA_HEAD_DIM
    rep = s_heads // groups
    w["wkv"] = w_kv_shared
    eye = jnp.eye(SWA_HEAD_DIM, dtype=F32)
    zero = jnp.zeros_like(eye)
    w["rep_k2"] = jnp.kron(jnp.eye(groups, dtype=F32), jnp.concatenate([eye, zero, zero, eye], axis=1))
    w["eye_kv"] = jnp.eye(kvw, dtype=F32)
    w["swa_wq_t"] = swa_w_q.T
    w["swa_wo"] = swa_w_o
    w["rep_kv"] = jnp.kron(jnp.eye(groups, dtype=F32), jnp.tile(eye, (1, rep)))
    w["swa_wq"] = swa_w_q
    return {k: v.astype(BF16) for k, v in w.items()}, dict(heads=heads, c_dim=c_dim, groups=groups, rep=rep,
                                                            s_heads=s_heads, kvw=kvw)


def _rows_a(x, P, w, tab, heads, c_dim, absorbed):
    t, d = x.shape
    tm = _row_tile(t)
    hw = heads * LANES
    row = lambda v: v.reshape(1, -1)
    ins = ([_rows(x, tm)] + _ffn_ins(P, w, 1, 0)
           + [_const(row(P["mix_norm"][0])), _const(w["wdq"]), _const(row(P["mla_q_norm"][0])), _const(w["wq"]),
              _const(w["wlat"]), _const(row(P["mla_kv_norm"][0])), _const(w["wkr"]), _cycle(tab, tm)]
           + ([_const(w["wuk_dec"])] if absorbed else [_const(w["wuk"]), _const(w["wuv_t"])]))
    outs = [_rows_out(t, tm, d, F32), _rows_out(t, tm, hw, BF16)]
    if absorbed:
        outs.append(_rows_out(t, tm, heads * c_dim, BF16))
    else:
        outs += [_rows_out(t, tm, hw, BF16),
                 (jax.ShapeDtypeStruct((heads * MLA_VT_ROWS, t), BF16),
                  pl.BlockSpec((heads * MLA_VT_ROWS, tm), lambda i: (0, i)))]
    outs += [_rows_out(t, tm, c_dim, F32), _rows_out(t, tm, MLA_ROPE, F32)]
    return _rows_call(functools.partial(_rows_a_kernel, heads=heads, c_dim=c_dim, absorbed=absorbed),
                      "rows_a_absorbed" if absorbed else "rows_a", t, tm, ins, outs)


def _rows_b(a, x, P, w, meta, decode):
    t, d = x.shape
    tm = _row_tile(t)
    kvw = meta["kvw"]
    ins = [_rows(a, tm)]
    if decode:
        ins += [_const(w["wuv_dec"]), _const(w["wo_pad"])]
    else:
        ins += [_const(w["wo"])]
    ins += [_rows(x, tm)] + _ffn_ins(P, w, 2, 0) + [_const(P["kv_norm"].reshape(1, -1)), _const(w["wkv"])]
    outs = [_rows_out(t, tm, d, F32), _rows_out(t, tm, 2 * kvw, F32)]
    if not decode:
        ins += [_const(w["rep_k2"]), _const(w["eye_kv"])]
        outs += [_rows_out(t, tm, w["rep_k2"].shape[1], BF16),
                 (jax.ShapeDtypeStruct((kvw, t), BF16), pl.BlockSpec((kvw, tm), lambda i: (0, i)))]
    return _rows_call(functools.partial(_rows_b_kernel, dec_heads=meta["heads"] if decode else 0,
                                        c_dim=meta["c_dim"], tiled_kv=not decode),
                      "rows_b_decode" if decode else "rows_b", t, tm, ins, outs)


def _rows_c(x, P, w, wq, name, transposed):
    t, d = x.shape
    tm = _row_tile(t)
    ins = [_rows(x, tm)] + _ffn_ins(P, w, 1, 1) + [_const(P["mix_norm"][1].reshape(1, -1)), _const(wq)]
    if transposed:
        q_out = (jax.ShapeDtypeStruct((wq.shape[0], t), BF16), pl.BlockSpec((wq.shape[0], tm), lambda i: (0, i)))
    else:
        q_out = _rows_out(t, tm, wq.shape[1], BF16)
    return _rows_call(functools.partial(_rows_c_kernel, scale=SWA_HEAD_DIM ** -0.5 * LOG2E, transposed=transposed),
                      name, t, tm, ins, [_rows_out(t, tm, d, F32), q_out])


def _rows_d(a, x, P, w, wo, name):
    t, d = x.shape
    tm = _row_tile(t)
    ins = ([_rows(a, tm), _const(wo), _rows(x, tm)] + _ffn_ins(P, w, 2, 1)
           + [_const(P["final_norm"].reshape(1, -1))])
    return _rows_call(_rows_d_kernel, name, t, tm, ins, [_rows_out(t, tm, d, F32)])[0]


def _trunk_prompt(x3, P, w, meta):
    batch, seq, d = x3.shape
    heads, c_dim, kvw = meta["heads"], meta["c_dim"], meta["kvw"]
    x = x3.reshape(batch * seq, d)
    x, q, k, vt, lat, kr = _rows_a(x, P, w, _rope_table(jnp.arange(seq)), heads, c_dim, absorbed=False)
    o = _mla_attn_prompt(q, k, vt, batch, seq, heads)
    x, kv, k2, v_t = _rows_b(o, x, P, w, meta, decode=False)
    x, q_t = _rows_c(x, P, w, w["swa_wq_t"], "rows_c", transposed=True)
    blk = 2 * CHUNK
    rel = (jnp.arange(2 * blk) - blk)[:, None] - jnp.arange(blk)[None, :]
    bias_t = _bias_table(P["rel_bias"], _rel_bucket(rel).astype(jnp.int32))
    o = _swa_prompt(q_t, k2, v_t, bias_t, P["swa_sinks"][0], batch, seq, meta["s_heads"], meta["rep"], blk)
    y = _rows_d(o, x, P, w, w["swa_wo"], "rows_d")
    keep = min(WINDOW, seq)
    kv3 = kv.reshape(batch, seq, 2 * kvw)[:, seq - keep:]
    new_k = kv3[:, :, :kvw].reshape(batch, keep, meta["groups"], SWA_HEAD_DIM)
    new_v = kv3[:, :, kvw:].reshape(batch, keep, meta["groups"], SWA_HEAD_DIM)
    return (y.reshape(batch, seq, d), lat.reshape(1, batch, seq, c_dim), kr.reshape(1, batch, seq, MLA_ROPE),
            new_k, new_v)


def _trunk_decode(x3, cache_lat, cache_rope, cache_k, cache_v, P, w, meta):
    batch, n_new, d = x3.shape
    heads, c_dim, kvw, s_heads = meta["heads"], meta["c_dim"], meta["kvw"], meta["s_heads"]
    past = cache_lat.shape[2]
    w_c = cache_k.shape[1]
    qpos = past + jnp.arange(n_new)
    assert past % CHUNK == 0 and n_new <= CHUNK and w_c <= WINDOW_CHUNKS * CHUNK and w_c <= past
    t = batch * n_new
    x = x3.reshape(t, d)
    tm = _row_tile(t)
    assert tm % n_new == 0
    tab = jnp.tile(_rope_table(qpos), (tm // n_new, 1))
    x, q, ql, lat, kr = _rows_a(x, P, w, tab, heads, c_dim, absorbed=True)
    ol = _mla_attn_decode(ql, q, cache_lat, jnp.swapaxes(cache_rope, 2, 3), lat, kr, heads, n_new)
    x, kv = _rows_b(ol, x, P, w, meta, decode=True)
    x, qs = _rows_c(x, P, w, w["swa_wq"], "rows_c_decode", transposed=False)
    kpos = jnp.arange(past - w_c, past + n_new)
    bias_h = _bias_table(P["rel_bias"], _rel_bucket(kpos[None, :] - qpos[:, None]).astype(jnp.int32))
    bias = bias_h.reshape(s_heads * n_new, w_c + n_new)
    sink_rows = jnp.repeat(P["swa_sinks"][0], n_new).reshape(s_heads * n_new, 1)
    const2 = lambda a: pl.BlockSpec(a.shape, lambda b: (0, 0), pipeline_mode=pl.Buffered(1))
    nb = math.gcd(SWA_DECODE_BATCH, batch)
    o = pl.pallas_call(
        functools.partial(_swa_decode_kernel, kv_w=kvw, groups=meta["groups"], rep=meta["rep"], n_new=n_new),
        grid=(batch // nb,),
        in_specs=[pl.BlockSpec((nb * n_new, qs.shape[1]), lambda b: (b, 0)),
                  pl.BlockSpec((nb, w_c, kvw), lambda b: (b, 0, 0)),
                  pl.BlockSpec((nb, w_c, kvw), lambda b: (b, 0, 0)),
                  pl.BlockSpec((nb * n_new, 2 * kvw), lambda b: (b, 0)),
                  const2(w["rep_kv"]), const2(bias), const2(sink_rows)],
        out_specs=pl.BlockSpec((nb * n_new, qs.shape[1]), lambda b: (b, 0)),
        out_shape=jax.ShapeDtypeStruct(qs.shape, BF16),
        compiler_params=_params(1), name="swa_decode",
    )(qs, cache_k.reshape(batch, w_c, kvw), cache_v.reshape(batch, w_c, kvw), kv, w["rep_kv"], bias, sink_rows)
    y = _rows_d(o, x, P, w, w["swa_wo"], "rows_d_decode")
    kv3 = kv.reshape(batch, n_new, 2 * kvw)
    new_k = kv3[:, :, :kvw].reshape(batch, n_new, meta["groups"], SWA_HEAD_DIM)
    new_v = kv3[:, :, kvw:].reshape(batch, n_new, meta["groups"], SWA_HEAD_DIM)
    return (y.reshape(batch, n_new, d), lat.reshape(1, batch, n_new, c_dim), kr.reshape(1, batch, n_new, MLA_ROPE),
            new_k, new_v)


def kernel(x_prompt, x_sample, cache_mla_latent, cache_mla_krope, cache_swa_k, cache_swa_v, ffn_norm1, ffn1_w_gate, ffn1_w_up, ffn1_w_down, mix_norm, ffn_norm2, ffn2_w_gate, ffn2_w_up, ffn2_w_down, mla_w_dq, mla_q_norm, mla_w_uq, mla_w_dkv, mla_kv_norm, mla_w_uk, mla_w_uv, mla_w_o, kv_norm, w_kv_shared, swa_w_q, swa_sinks, swa_w_o, rel_bias, final_norm):
    assert ffn_norm1.shape[0] == 2 and mla_w_dq.shape[0] == 1 and swa_w_q.shape[0] == 1
    scale = (MLA_NOPE + MLA_ROPE) ** -0.5 * LOG2E
    w, meta = _prep_weights(mla_w_dq[0], mla_w_uq[0], mla_w_dkv[0], mla_w_uk[0], mla_w_uv[0], mla_w_o[0],
                            w_kv_shared, swa_w_q[0], swa_w_o[0], scale)
    w.update(f1g=ffn1_w_gate.astype(BF16), f1u=ffn1_w_up.astype(BF16), f1d=ffn1_w_down.astype(BF16),
             f2g=ffn2_w_gate.astype(BF16), f2u=ffn2_w_up.astype(BF16), f2d=ffn2_w_down.astype(BF16))
    P = dict(ffn_norm1=ffn_norm1, mix_norm=mix_norm, ffn_norm2=ffn_norm2, mla_q_norm=mla_q_norm,
             mla_kv_norm=mla_kv_norm, kv_norm=kv_norm, swa_sinks=swa_sinks, rel_bias=rel_bias, final_norm=final_norm)
    y_p, lat_p, rope_p, k_p, v_p = _trunk_prompt(x_prompt, P, w, meta)
    y_s, lat_s, rope_s, k_s, v_s = _trunk_decode(x_sample, cache_mla_latent, cache_mla_krope,
                                                 cache_swa_k, cache_swa_v, P, w, meta)
    return (y_p, y_s, lat_p, rope_p, k_p, v_p, lat_s, rope_s, k_s, v_s)
```

```python
import functools
import math

import jax
import jax.numpy as jnp
from jax import lax
from jax.experimental import pallas as pl
from jax.experimental.pallas import tpu as pltpu

F32 = jnp.float32
BF16 = jnp.bfloat16

CHUNK = 64
RMS_EPS = 1e-6
FFN_RES = 0.5
ROPE_BASE = 10000.0
WINDOW = 128
WINDOW_CHUNKS = WINDOW // CHUNK
N_BUCKETS = 32
MAX_DISTANCE = 128
NEG_INF = -1e30
LOG2E = math.log2(math.e)
MLA_NOPE = 64
MLA_ROPE = 32
MLA_V = 64
MLA_VT_ROWS = 80
SWA_HEAD_DIM = 64

LANES = 128
ROW_TILE = 512
MLA_BQ = 512
MLA_BK = 512
MLA_DIAG = 256
MLA_HEADS_PER_STEP = 8
MLA_DECODE_CHUNK = 1024
MLA_DECODE_BATCH = 2
SWA_DECODE_BATCH = 4
SWA_BLOCKS_PER_STEP = 4
VMEM_LIMIT = 60 * 1024 * 1024


def _params(n_axes):
    return pltpu.CompilerParams(dimension_semantics=("arbitrary",) * n_axes, vmem_limit_bytes=VMEM_LIMIT)


def _rms(xf, g):
    return xf * lax.rsqrt(jnp.mean(xf * xf, axis=-1, keepdims=True) + RMS_EPS) * g


def _dot(a, b):
    return jnp.dot(a, b, preferred_element_type=F32)


def _dot_nt(a, b):
    return lax.dot_general(a, b, (((1,), (1,)), ((), ())), preferred_element_type=F32)


def _row_tile(t):
    tm = min(ROW_TILE, t)
    assert t % tm == 0, (t, tm)
    return tm


def _rows(a, tm, tile=lambda i: i):
    return a, pl.BlockSpec((tm, a.shape[1]), lambda i: (tile(i), 0))


def _two_streams(n_first):
    return (lambda i: jnp.minimum(i, n_first - 1)), (lambda i: jnp.maximum(i - n_first, 0))


def _const(a):
    nd = a.ndim
    return a, pl.BlockSpec(a.shape, lambda i: (0,) * nd, pipeline_mode=pl.Buffered(1))


def _layer(a, layer):
    nd = a.ndim - 1
    return a, pl.BlockSpec((None,) + tuple(a.shape[1:]), lambda i: (layer,) + (0,) * nd,
                           pipeline_mode=pl.Buffered(1))


def _cycle(a, tm):
    assert a.shape[0] % tm == 0
    nb = a.shape[0] // tm
    return a, pl.BlockSpec((tm, a.shape[1]), lambda i: (i % nb, 0))


def _rows_out(t, tm, ncols, dtype, tile=lambda i: i):
    return jax.ShapeDtypeStruct((t, ncols), dtype), pl.BlockSpec((tm, ncols), lambda i: (tile(i), 0))


def _rows_call(body, name, t, tm, ins, outs, steps=None):
    return pl.pallas_call(
        body, grid=(steps or t // tm,),
        in_specs=[s for _, s in ins], out_specs=[s for _, s in outs], out_shape=[o for o, _ in outs],
        compiler_params=_params(1), name=name,
    )(*[a for a, _ in ins])


def _ffn_ins(P, w, which, layer):
    return [_const(P["ffn_norm%d" % which][layer].reshape(1, -1)), _layer(w["f%dg" % which], layer),
            _layer(w["f%du" % which], layer), _layer(w["f%dd" % which], layer)]


def _ffn_apply(x, g_ref, wg_ref, wu_ref, wd_ref):
    h = _rms(x, g_ref[...]).astype(BF16)
    gate = _dot(h, wg_ref[...])
    up = _dot(h, wu_ref[...])
    a = (gate * jax.nn.sigmoid(gate) * up).astype(BF16)
    return x + FFN_RES * _dot(a, wd_ref[...])


def _mla_proj_apply(x, mg_ref, wdq_ref, qn_ref, wq_ref, wlat_ref, kvn_ref, wkr_ref, tab_ref,
                    refs, heads, c_dim, absorbed):
    cos_t, sin_t = tab_ref[:, 0:LANES], tab_ref[:, LANES:2 * LANES]
    lane = lax.broadcasted_iota(jnp.int32, (1, LANES), 1)
    cos_q = jnp.where(lane < MLA_NOPE, 1.0, cos_t)
    first_half = lane < MLA_NOPE + MLA_ROPE // 2

    def swap_halves(v):
        return jnp.where(first_half, pltpu.roll(v, LANES - MLA_ROPE // 2, 1), pltpu.roll(v, MLA_ROPE // 2, 1))

    h = _rms(x, mg_ref[...]).astype(BF16)
    cq = _rms(_dot(h, wdq_ref[...]), qn_ref[...]).astype(BF16)
    qa = _dot(cq, wq_ref[...])
    lat = _rms(_dot(h, wlat_ref[...]), kvn_ref[...])
    kr = _dot(h, wkr_ref[...])
    kr = kr * cos_t + swap_halves(kr) * sin_t
    latb = lat.astype(BF16)
    if absorbed:
        wukd_ref, q_ref, ql_ref, lat_ref, kr_ref = refs
    else:
        wuk_ref, wuvt_ref, q_ref, k_ref, vt_ref, lat_ref, kr_ref = refs
        kn = _dot(latb, wuk_ref[...])
    lat_ref[...] = lat
    kr_ref[...] = kr[:, MLA_NOPE:MLA_NOPE + MLA_ROPE]
    for hh in range(heads):
        sl = slice(hh * LANES, (hh + 1) * LANES)
        qh = (qa[:, sl] * cos_q + swap_halves(qa[:, sl]) * sin_t).astype(BF16)
        q_ref[:, sl] = qh
        if absorbed:
            ql_ref[:, hh * c_dim:(hh + 1) * c_dim] = _dot(qh, wukd_ref[hh]).astype(BF16)
        else:
            k_ref[:, sl] = (kn[:, sl] + kr).astype(BF16)
    if not absorbed:
        vt = _dot_nt(wuvt_ref[...], latb)
        ones_row = lax.broadcasted_iota(jnp.int32, vt.shape, 0) % MLA_VT_ROWS == MLA_V
        vt_ref[...] = jnp.where(ones_row, 1.0, vt).astype(BF16)


def _rows_a_kernel(x_ref, fg, wg, wu, wd, mg, wdq, qn, wq, wlat, kvn, wkr, tab, *refs,
                   heads, c_dim, absorbed):
    n_extra = 1 if absorbed else 2
    x1_ref = refs[n_extra]
    x1 = _ffn_apply(x_ref[...], fg, wg, wu, wd)
    x1_ref[...] = x1
    _mla_proj_apply(x1, mg, wdq, qn, wq, wlat, kvn, wkr, tab, refs[:n_extra] + refs[n_extra + 1:],
                    heads, c_dim, absorbed)


def _rows_b_kernel(a_ref, *refs, dec_heads, c_dim, tiled_kv):
    if dec_heads:
        wuv_ref, refs = refs[0], refs[1:]
        a = jnp.concatenate([_dot(a_ref[:, hh * c_dim:(hh + 1) * c_dim], wuv_ref[hh]).astype(BF16)
                             for hh in range(dec_heads)], axis=1)
    else:
        a = a_ref[...]
    wo, x_ref, fg, wg, wu, wd, kvg, wkv = refs[:8]
    x = _ffn_apply(x_ref[...] + _dot(a, wo[...]), fg, wg, wu, wd)
    kv = _dot(_rms(x, kvg[...]).astype(BF16), wkv[...])
    if tiled_kv:
        repk, eye, x_out, kv_out, k2_out, vt_out = refs[8:]
        half = kv.shape[1] // 2
        k2_out[...] = _dot(kv[:, :half].astype(BF16), repk[...]).astype(BF16)
        vt_out[...] = _dot_nt(eye[...], kv[:, half:].astype(BF16)).astype(BF16)
    else:
        x_out, kv_out = refs[8:]
    x_out[...] = x
    kv_out[...] = kv


def _rows_c_kernel(x_ref, fg, wg, wu, wd, mg, wq, x_out, q_out, *, scale, transposed):
    x = _ffn_apply(x_ref[...], fg, wg, wu, wd)
    x_out[...] = x
    h = _rms(x, mg[...]).astype(BF16)
    q = _dot_nt(wq[...], h) if transposed else _dot(h, wq[...])
    q_out[...] = (q * scale).astype(BF16)


def _rows_d_kernel(a_ref, wo, x_ref, fg, wg, wu, wd, fin_g, y_out):
    x = _ffn_apply(x_ref[...] + _dot(a_ref[...], wo[...]), fg, wg, wu, wd)
    y_out[...] = _rms(x, fin_g[...])


def _rows_c_both_kernel(xp_ref, xd_ref, fg, wg, wu, wd, mg, wq_t, wq, xp_out, qt_out, xd_out, qd_out, *, scale, n_p):
    i = pl.program_id(0)

    @pl.when(i < n_p)
    def _():
        _rows_c_kernel(xp_ref, fg, wg, wu, wd, mg, wq_t, xp_out, qt_out, scale=scale, transposed=True)

    @pl.when(i >= n_p)
    def _():
        _rows_c_kernel(xd_ref, fg, wg, wu, wd, mg, wq, xd_out, qd_out, scale=scale, transposed=False)


def _rows_d_both_kernel(ap_ref, xp_ref, ad_ref, xd_ref, wo, fg, wg, wu, wd, fin_g, yp_out, yd_out, *, n_p):
    i = pl.program_id(0)

    @pl.when(i < n_p)
    def _():
        _rows_d_kernel(ap_ref, wo, xp_ref, fg, wg, wu, wd, fin_g, yp_out)

    @pl.when(i >= n_p)
    def _():
        _rows_d_kernel(ad_ref, wo, xd_ref, fg, wg, wu, wd, fin_g, yd_out)


def _mla_attn_kernel(q_ref, k_ref, vt_ref, o_ref, m_sc, acc_sc, sa_sc, sb_sc, *, bq, bk, hp):
    i = pl.program_id(2)
    qs = [q_ref[:, hh * LANES:(hh + 1) * LANES] for hh in range(hp)]
    sls = [slice(hh * LANES, (hh + 1) * LANES) for hh in range(hp)]

    def own_scores(start, width, q_lo):
        ss = [_dot_nt(k_ref[pl.ds(start, width), sls[hh]], qs[hh][q_lo:]) for hh in range(hp)]
        kc = (start + lax.broadcasted_iota(jnp.int32, (width, 1), 0)) // CHUNK
        qc = (i * bq + q_lo + lax.broadcasted_iota(jnp.int32, (1, bq - q_lo), 1)) // CHUNK
        return [jnp.where(kc <= qc, s, NEG_INF) for s in ss]

    def own_update(ss, start, width, q_lo, after_head=None):
        ql = slice(q_lo, bq)
        for hh in range(hp):
            m = m_sc[hh, :, ql]
            m_new = jnp.maximum(m, jnp.max(ss[hh], axis=0, keepdims=True))
            alpha = jnp.exp2(m - m_new)
            p = jnp.exp2(ss[hh] - m_new).astype(BF16)
            vt = vt_ref[hh * MLA_VT_ROWS:(hh + 1) * MLA_VT_ROWS, pl.ds(start, width)]
            acc_sc[hh, :, ql] = alpha * acc_sc[hh, :, ql] + _dot(vt, p)
            m_sc[hh, :, ql] = m_new
            if after_head is not None:
                after_head(hh)

    def produce(hh, j, dst):
        dst[hh] = _dot_nt(k_ref[pl.ds(pl.multiple_of(j * bk, bk), bk), sls[hh]], qs[hh])

    def consume(hh, j, src):
        s = src[hh]
        m = m_sc[hh]
        m_new = jnp.maximum(m, jnp.max(s, axis=0, keepdims=True))
        alpha = jnp.exp2(m - m_new)
        p = jnp.exp2(s - m_new).astype(BF16)
        vt = vt_ref[hh * MLA_VT_ROWS:(hh + 1) * MLA_VT_ROWS, pl.ds(pl.multiple_of(j * bk, bk), bk)]
        acc_sc[hh] = alpha * acc_sc[hh] + _dot(vt, p)
        m_sc[hh] = m_new

    def transition(j, src, dst):
        produce(0, j + 1, dst)
        for hh in range(hp):
            consume(hh, j, src)
            if hh + 1 < hp:
                produce(hh + 1, j + 1, dst)

    m_sc[...] = jnp.full(m_sc.shape, NEG_INF, F32)
    acc_sc[...] = jnp.zeros(acc_sc.shape, F32)
    n_full = (i * bq) // bk

    diag = min(MLA_DIAG, bq)
    starts = [pl.multiple_of(i * bq + t * diag, diag) for t in range(bq // diag)]
    own = [own_scores(starts[t], diag, t * diag) for t in range(bq // diag)]
    for t in range(bq // diag):
        own_update(own[t], starts[t], diag, t * diag,
                   after_head=(lambda hh: produce(hh, 0, sa_sc)) if t == 0 else None)

    @pl.when(n_full > 0)
    def _():
        last = n_full - 1

        @pl.loop(0, last // 2)
        def _(t):
            transition(2 * t, sa_sc, sb_sc)
            transition(2 * t + 1, sb_sc, sa_sc)

        @pl.when(last % 2 == 1)
        def _():
            transition(last - 1, sa_sc, sb_sc)
            for hh in range(hp):
                consume(hh, last, sb_sc)

        @pl.when(last % 2 == 0)
        def _():
            for hh in range(hp):
                consume(hh, last, sa_sc)

    for pair in range(hp // 2):
        halves = []
        for hh in (2 * pair, 2 * pair + 1):
            acc = acc_sc[hh]
            halves.append(acc[:MLA_V] * (1.0 / acc[MLA_V:MLA_V + 1]))
        o_ref[:, pair * LANES:(pair + 1) * LANES] = jnp.concatenate(halves, axis=0).T.astype(BF16)


def _mla_attn_prompt(q, k, vt, batch, seq, heads, bq=MLA_BQ, bk=MLA_BK, hp=MLA_HEADS_PER_STEP):
    bq, bk = min(bq, seq), min(bk, seq)
    nq = seq // bq
    assert seq % bq == 0 and seq % bk == 0 and bq % CHUNK == 0 and heads % hp == 0 and hp % 2 == 0
    assert bq % bk == 0 and bq % min(MLA_DIAG, bq) == 0 and MLA_DIAG % CHUNK == 0
    w = hp * LANES
    return pl.pallas_call(
        functools.partial(_mla_attn_kernel, bq=bq, bk=bk, hp=hp),
        grid=(batch, heads // hp, nq),
        in_specs=[pl.BlockSpec((bq, w), lambda b, h, i: (b * nq + i, h)),
                  pl.BlockSpec((seq, w), lambda b, h, i: (b, h)),
                  pl.BlockSpec((hp * MLA_VT_ROWS, seq), lambda b, h, i: (h, b))],
        out_specs=pl.BlockSpec((bq, hp * MLA_V), lambda b, h, i: (b * nq + i, h)),
        out_shape=jax.ShapeDtypeStruct((q.shape[0], heads * MLA_V), BF16),
        scratch_shapes=[pltpu.VMEM((hp, 1, bq), F32), pltpu.VMEM((hp, MLA_VT_ROWS, bq), F32),
                        pltpu.VMEM((hp, bk, bq), F32), pltpu.VMEM((hp, bk, bq), F32)],
        compiler_params=_params(3), name="mla_attn_prompt",
    )(q, k, vt)


def _mla_attn_decode_kernel(ql_ref, q_ref, cl_ref, crt_ref, nl_ref, nr_ref, o_ref, *, heads, c_dim, n_new):
    nb = cl_ref.shape[0]
    past = cl_ref.shape[1]
    chunk = min(past, MLA_DECODE_CHUNK)
    n_parts = past // chunk + 1
    rows = [slice(bb * n_new, (bb + 1) * n_new) for bb in range(nb)]
    qls = [jnp.concatenate([ql_ref[rows[bb], hh * c_dim:(hh + 1) * c_dim] for hh in range(heads)], axis=0)
           for bb in range(nb)]
    qrs = [jnp.concatenate([q_ref[rows[bb], hh * LANES + MLA_NOPE:hh * LANES + MLA_NOPE + MLA_ROPE]
                            for hh in range(heads)], axis=0) for bb in range(nb)]

    def scores(bb, c):
        if c + 1 < n_parts:
            kl = cl_ref[bb, c * chunk:(c + 1) * chunk, :].astype(BF16)
            return kl, _dot_nt(qls[bb], kl) + _dot(qrs[bb], crt_ref[bb, :, c * chunk:(c + 1) * chunk].astype(BF16))
        kl = nl_ref[rows[bb], :].astype(BF16)
        return kl, _dot_nt(qls[bb], kl) + _dot_nt(qrs[bb], nr_ref[rows[bb], :].astype(BF16))

    state = [(jnp.full((heads * n_new, 1), NEG_INF, F32), jnp.zeros((heads * n_new, 1), F32),
              jnp.zeros((heads * n_new, c_dim), F32)) for _ in range(nb)]
    nxt = [scores(bb, 0) for bb in range(nb)]
    for c in range(n_parts):
        for bb in range(nb):
            kl, s = nxt[bb]
            if c + 1 < n_parts:
                nxt[bb] = scores(bb, c + 1)
            m, l, acc = state[bb]
            m_new = jnp.maximum(m, jnp.max(s, axis=-1, keepdims=True))
            alpha = jnp.exp2(m - m_new)
            p = jnp.exp2(s - m_new)
            l = alpha * l + jnp.sum(p, axis=-1, keepdims=True)
            acc = alpha * acc + _dot(p.astype(BF16), kl)
            state[bb] = (m_new, l, acc)
    for bb in range(nb):
        _, l, acc = state[bb]
        o = acc / l
        for hh in range(heads):
            o_ref[rows[bb], hh * c_dim:(hh + 1) * c_dim] = o[hh * n_new:(hh + 1) * n_new].astype(BF16)


def _mla_attn_decode(ql, q, cache_lat, cache_rope_t, new_lat, new_rope, heads, n_new, nb=MLA_DECODE_BATCH):
    _, batch, past, c_dim = cache_lat.shape
    r_dim = cache_rope_t.shape[2]
    nb = math.gcd(nb, batch)
    return pl.pallas_call(
        functools.partial(_mla_attn_decode_kernel, heads=heads, c_dim=c_dim, n_new=n_new),
        grid=(batch // nb,),
        in_specs=[pl.BlockSpec((nb * n_new, heads * c_dim), lambda b: (b, 0)),
                  pl.BlockSpec((nb * n_new, heads * LANES), lambda b: (b, 0)),
                  pl.BlockSpec((None, nb, past, c_dim), lambda b: (0, b, 0, 0)),
                  pl.BlockSpec((None, nb, r_dim, past), lambda b: (0, b, 0, 0)),
                  pl.BlockSpec((nb * n_new, c_dim), lambda b: (b, 0)),
                  pl.BlockSpec((nb * n_new, r_dim), lambda b: (b, 0))],
        out_specs=pl.BlockSpec((nb * n_new, heads * c_dim), lambda b: (b, 0)),
        out_shape=jax.ShapeDtypeStruct(ql.shape, BF16),
        compiler_params=_params(1), name="mla_attn_decode",
    )(ql, q, cache_lat, cache_rope_t, new_lat, new_rope)


def _bias_kernel(tab_ref, idx_ref, o_ref, *, heads):
    idx = idx_ref[...]
    for hh in range(heads):
        acc = jnp.zeros(idx.shape, F32)
        for b in range(N_BUCKETS):
            acc = jnp.where(idx == b, tab_ref[b, hh] * LOG2E, acc)
        o_ref[hh] = acc


def _bias_table(rel_bias, idx):
    heads = rel_bias.shape[1]
    return pl.pallas_call(
        functools.partial(_bias_kernel, heads=heads),
        in_specs=[pl.BlockSpec(memory_space=pltpu.SMEM), pl.BlockSpec(idx.shape, lambda: (0, 0))],
        out_specs=pl.BlockSpec((heads,) + idx.shape, lambda: (0, 0, 0)),
        out_shape=jax.ShapeDtypeStruct((heads,) + idx.shape, F32), name="rel_bias_table",
    )(rel_bias, idx)


def _rel_bucket(rel):
    half = N_BUCKETS // 2
    max_exact = half // 2
    base = jnp.where(rel > 0, half, 0)
    n = jnp.abs(rel)
    nf = jnp.maximum(n, 1).astype(jnp.float32)
    large = max_exact + (jnp.log(nf / max_exact) / math.log(MAX_DISTANCE / max_exact)
                         * (half - max_exact)).astype(jnp.int32)
    large = jnp.minimum(large, half - 1)
    return base + jnp.where(n < max_exact, n, large)


def _swa_prompt_kernel(sink_ref, qt_ref, kp_ref, kc_ref, vtp_ref, vtc_ref, bias_ref, o_ref, *, s_heads, rep, blk, nsub):
    i = pl.program_id(1)
    key_chunk = lax.broadcasted_iota(jnp.int32, (2 * blk, 1), 0) // CHUNK
    row_chunk = lax.broadcasted_iota(jnp.int32, (1, blk), 1) // CHUNK
    first = blk // CHUNK
    window = (key_chunk >= row_chunk) & (key_chunk <= row_chunk + WINDOW_CHUNKS)
    dh = SWA_HEAD_DIM
    units = [(u, hh) for u in range(nsub) for hh in range(s_heads)]
    ss = []
    for u, hh in units:
        g, pair, slot = hh // rep, hh // 2, hh % 2
        ksl = slice(g * 4 * dh + slot * 2 * dh, g * 4 * dh + (slot + 1) * 2 * dh)
        k = jnp.concatenate([kp_ref[:, ksl], kc_ref[:, ksl]], axis=0)[u * blk:(u + 2) * blk]
        ss.append(_dot(k, qt_ref[pair * 2 * dh:(pair + 1) * 2 * dh, u * blk:(u + 1) * blk]))
    ps, invs = [], []
    for (u, hh), s in zip(units, ss):
        valid = window if u > 0 else window & ((i > 0) | (key_chunk >= first))
        s = jnp.where(valid, s + bias_ref[hh], NEG_INF)
        sink = sink_ref[hh] * LOG2E
        m = jnp.maximum(jnp.max(s, axis=0, keepdims=True), sink)
        p = jnp.exp2(s - m)
        invs.append(1.0 / (jnp.sum(p, axis=0, keepdims=True) + jnp.exp2(sink - m)))
        ps.append(p.astype(BF16))
    outs = []
    for n, (u, hh) in enumerate(units):
        g = hh // rep
        vt = jnp.concatenate([vtp_ref[g * dh:(g + 1) * dh, :], vtc_ref[g * dh:(g + 1) * dh, :]],
                             axis=1)[:, u * blk:(u + 2) * blk]
        outs.append(_dot(vt, ps[n]) * invs[n])
    for u in range(nsub):
        for pair in range(s_heads // 2):
            n = u * s_heads + 2 * pair
            o_ref[u * blk:(u + 1) * blk, pair * 2 * dh:(pair + 1) * 2 * dh] = (
                jnp.concatenate(outs[n:n + 2], axis=0).T.astype(BF16))


def _swa_prompt(qt, k2, vt, bias_t, sinks, batch, seq, s_heads, rep, blk=128, nsub=SWA_BLOCKS_PER_STEP):
    assert seq % (blk * nsub) == 0
    nq = seq // (blk * nsub)
    kw = k2.shape[1]
    cur = lambda b, i: (b * nq + i, 0)
    prev = lambda b, i: ((b * nq + i) * nsub - jnp.minimum(i, 1), 0)
    cur_t = lambda b, i: (0, b * nq + i)
    prev_t = lambda b, i: (0, (b * nq + i) * nsub - jnp.minimum(i, 1))
    assert rep == 4 and kw == (s_heads // rep) * 4 * SWA_HEAD_DIM
    return pl.pallas_call(
        functools.partial(_swa_prompt_kernel, s_heads=s_heads, rep=rep, blk=blk, nsub=nsub),
        grid=(batch, nq),
        in_specs=[pl.BlockSpec(memory_space=pltpu.SMEM),
                  pl.BlockSpec((qt.shape[0], blk * nsub), cur_t),
                  pl.BlockSpec((blk, kw), prev), pl.BlockSpec((blk * nsub, kw), cur),
                  pl.BlockSpec((vt.shape[0], blk), prev_t), pl.BlockSpec((vt.shape[0], blk * nsub), cur_t),
                  pl.BlockSpec(bias_t.shape, lambda b, i: (0, 0, 0), pipeline_mode=pl.Buffered(1))],
        out_specs=pl.BlockSpec((blk * nsub, qt.shape[0]), cur),
        out_shape=jax.ShapeDtypeStruct((qt.shape[1], qt.shape[0]), BF16),
        compiler_params=_params(2), name="swa_prompt",
    )(sinks, qt, k2, k2, vt, vt, bias_t)


def _swa_decode_kernel(q_ref, ck_ref, cv_ref, nkv_ref, rep_ref, bias_ref, sink_ref, o_ref, *, kv_w, groups, rep, n_new):
    nb = ck_ref.shape[0]
    n = n_new
    gw = rep * SWA_HEAD_DIM
    lane_head = lax.broadcasted_iota(jnp.int32, (1, gw), 1) // SWA_HEAD_DIM
    sls = [slice(g * gw, (g + 1) * gw) for g in range(groups)]
    units = [(bb, g) for bb in range(nb) for g in range(groups)]
    k4s, v4s = [], []
    for bb in range(nb):
        new = nkv_ref[bb * n:(bb + 1) * n, :]
        k = jnp.concatenate([ck_ref[bb], new[:, 0:kv_w]], axis=0).astype(BF16)
        v = jnp.concatenate([cv_ref[bb], new[:, kv_w:2 * kv_w]], axis=0).astype(BF16)
        k4s.append(_dot(k, rep_ref[...]).astype(BF16))
        v4s.append(_dot(v, rep_ref[...]).astype(BF16))
    ss = []
    for bb, g in units:
        qg = q_ref[bb * n:(bb + 1) * n, sls[g]]
        qs = jnp.concatenate([jnp.where(lane_head == r, qg, jnp.zeros_like(qg)) for r in range(rep)], axis=0)
        ss.append(_dot_nt(qs, k4s[bb][:, sls[g]]))
    ps = []
    for (bb, g), s in zip(units, ss):
        rows = slice(g * rep * n, (g + 1) * rep * n)
        s = s + bias_ref[rows, :]
        sink = sink_ref[rows, :] * LOG2E
        m = jnp.maximum(jnp.max(s, axis=-1, keepdims=True), sink)
        p = jnp.exp2(s - m)
        inv = 1.0 / (jnp.sum(p, axis=-1, keepdims=True) + jnp.exp2(sink - m))
        ps.append((p * inv).astype(BF16))
    for (bb, g), p in zip(units, ps):
        res = _dot(p, v4s[bb][:, sls[g]])
        og = jnp.zeros((n, gw), F32)
        for r in range(rep):
            og = jnp.where(lane_head == r, res[r * n:(r + 1) * n], og)
        o_ref[bb * n:(bb + 1) * n, sls[g]] = og.astype(BF16)


def _rope_table(pos):
    inv = ROPE_BASE ** (-jnp.arange(0, MLA_ROPE, 2, dtype=jnp.float32) / MLA_ROPE)
    ang = pos.astype(jnp.float32)[:, None] * inv[None, :]
    cos, sin = jnp.cos(ang), jnp.sin(ang)
    widths = ((0, 0), (MLA_NOPE, LANES - MLA_NOPE - MLA_ROPE))
    return jnp.concatenate([jnp.pad(jnp.concatenate([cos, cos], axis=1), widths),
                            jnp.pad(jnp.concatenate([-sin, sin], axis=1), widths)], axis=1)


def _prep_weights(mla_w_dq, mla_w_uq, mla_w_dkv, mla_w_uk, mla_w_uv, mla_w_o, w_kv_shared, swa_w_q, swa_w_o, scale):
    ql, qcols = mla_w_uq.shape
    c_dim, heads, nope = mla_w_uk.shape
    rope = qcols // heads - nope
    d = mla_w_dq.shape[0]
    pad = LANES - nope - rope
    w = {}
    w["wq"] = jnp.pad(mla_w_uq.reshape(ql, heads, nope + rope), ((0, 0), (0, 0), (0, pad))).reshape(
        ql, heads * LANES) * scale
    w["wdq"] = mla_w_dq
    w["wlat"] = mla_w_dkv[:, :c_dim]
    w["wkr"] = jnp.pad(mla_w_dkv[:, c_dim:], ((0, 0), (nope, pad)))
    zc = jnp.zeros((c_dim, heads, LANES - nope), F32)
    w["wuk"] = jnp.concatenate([mla_w_uk, zc], axis=-1).reshape(c_dim, heads * LANES)
    w["wuv_t"] = jnp.pad(mla_w_uv, ((0, 0), (0, 0), (0, MLA_VT_ROWS - mla_w_uv.shape[2]))).reshape(
        c_dim, heads * MLA_VT_ROWS).T
    w["wuk_dec"] = jnp.concatenate([jnp.transpose(mla_w_uk, (1, 2, 0)),
                                    jnp.zeros((heads, LANES - nope, c_dim), F32)], axis=1)
    w["wuv_dec"] = jnp.concatenate([jnp.transpose(mla_w_uv, (1, 0, 2)),
                                    jnp.zeros((heads, c_dim, LANES - nope), F32)], axis=-1)
    vdim = mla_w_uv.shape[2]
    assert vdim == MLA_V and nope == MLA_NOPE and rope == MLA_ROPE
    w["wo"] = mla_w_o
    w["wo_pad"] = jnp.concatenate([mla_w_o.reshape(heads, vdim, d), jnp.zeros((heads, LANES - vdim, d), F32)],
                                  axis=1).reshape(heads * LANES, d)
    kvw = w_kv_shared.shape[1] // 2
    groups = kvw // SWA_HEAD_DIM
    s_heads = swa_w_q.shape[1] // SWA_HEAD_DIM
    rep = s_heads // groups
    w["wkv"] = w_kv_shared
    eye = jnp.eye(SWA_HEAD_DIM, dtype=F32)
    zero = jnp.zeros_like(eye)
    w["rep_k2"] = jnp.kron(jnp.eye(groups, dtype=F32), jnp.concatenate([eye, zero, zero, eye], axis=1))
    w["eye_kv"] = jnp.eye(kvw, dtype=F32)
    w["swa_wq_t"] = swa_w_q.T
    w["swa_wo"] = swa_w_o
    w["rep_kv"] = jnp.kron(jnp.eye(groups, dtype=F32), jnp.tile(eye, (1, rep)))
    w["swa_wq"] = swa_w_q
    return {k: v.astype(BF16) for k, v in w.items()}, dict(heads=heads, c_dim=c_dim, groups=groups, rep=rep,
                                                            s_heads=s_heads, kvw=kvw)


def _rows_a(x, P, w, tab, heads, c_dim, absorbed):
    t, d = x.shape
    tm = _row_tile(t)
    hw = heads * LANES
    row = lambda v: v.reshape(1, -1)
    ins = ([_rows(x, tm)] + _ffn_ins(P, w, 1, 0)
           + [_const(row(P["mix_norm"][0])), _const(w["wdq"]), _const(row(P["mla_q_norm"][0])), _const(w["wq"]),
              _const(w["wlat"]), _const(row(P["mla_kv_norm"][0])), _const(w["wkr"]), _cycle(tab, tm)]
           + ([_const(w["wuk_dec"])] if absorbed else [_const(w["wuk"]), _const(w["wuv_t"])]))
    outs = [_rows_out(t, tm, d, F32), _rows_out(t, tm, hw, BF16)]
    if absorbed:
        outs.append(_rows_out(t, tm, heads * c_dim, BF16))
    else:
        outs += [_rows_out(t, tm, hw, BF16),
                 (jax.ShapeDtypeStruct((heads * MLA_VT_ROWS, t), BF16),
                  pl.BlockSpec((heads * MLA_VT_ROWS, tm), lambda i: (0, i)))]
    outs += [_rows_out(t, tm, c_dim, F32), _rows_out(t, tm, MLA_ROPE, F32)]
    return _rows_call(functools.partial(_rows_a_kernel, heads=heads, c_dim=c_dim, absorbed=absorbed),
                      "rows_a_absorbed" if absorbed else "rows_a", t, tm, ins, outs)


def _rows_b(a, x, P, w, meta, decode):
    t, d = x.shape
    tm = _row_tile(t)
    kvw = meta["kvw"]
    ins = [_rows(a, tm)]
    if decode:
        ins += [_const(w["wuv_dec"]), _const(w["wo_pad"])]
    else:
        ins += [_const(w["wo"])]
    ins += [_rows(x, tm)] + _ffn_ins(P, w, 2, 0) + [_const(P["kv_norm"].reshape(1, -1)), _const(w["wkv"])]
    outs = [_rows_out(t, tm, d, F32), _rows_out(t, tm, 2 * kvw, F32)]
    if not decode:
        ins += [_const(w["rep_k2"]), _const(w["eye_kv"])]
        outs += [_rows_out(t, tm, w["rep_k2"].shape[1], BF16),
                 (jax.ShapeDtypeStruct((kvw, t), BF16), pl.BlockSpec((kvw, tm), lambda i: (0, i)))]
    return _rows_call(functools.partial(_rows_b_kernel, dec_heads=meta["heads"] if decode else 0,
                                        c_dim=meta["c_dim"], tiled_kv=not decode),
                      "rows_b_decode" if decode else "rows_b", t, tm, ins, outs)


def _rows_c_both(xp, xd, P, w):
    (tp, d), td = xp.shape, xd.shape[0]
    tm = _row_tile(tp)
    assert td % tm == 0
    n_p, n_d = tp // tm, td // tm
    first, second = _two_streams(n_p)
    wq_t, wq = w["swa_wq_t"], w["swa_wq"]
    ins = ([_rows(xp, tm, first), _rows(xd, tm, second)] + _ffn_ins(P, w, 1, 1)
           + [_const(P["mix_norm"][1].reshape(1, -1)), _const(wq_t), _const(wq)])
    outs = [_rows_out(tp, tm, d, F32, first),
            (jax.ShapeDtypeStruct((wq_t.shape[0], tp), BF16), pl.BlockSpec((wq_t.shape[0], tm), lambda i: (0, first(i)))),
            _rows_out(td, tm, d, F32, second), _rows_out(td, tm, wq.shape[1], BF16, second)]
    return _rows_call(functools.partial(_rows_c_both_kernel, scale=SWA_HEAD_DIM ** -0.5 * LOG2E, n_p=n_p),
                      "rows_c", tp, tm, ins, outs, steps=n_p + n_d)


def _rows_d_both(ap, xp, ad, xd, P, w):
    (tp, d), td = xp.shape, xd.shape[0]
    tm = _row_tile(tp)
    assert td % tm == 0
    n_p, n_d = tp // tm, td // tm
    first, second = _two_streams(n_p)
    ins = ([_rows(ap, tm, first), _rows(xp, tm, first), _rows(ad, tm, second), _rows(xd, tm, second),
            _const(w["swa_wo"])] + _ffn_ins(P, w, 2, 1) + [_const(P["final_norm"].reshape(1, -1))])
    outs = [_rows_out(tp, tm, d, F32, first), _rows_out(td, tm, d, F32, second)]
    return _rows_call(functools.partial(_rows_d_both_kernel, n_p=n_p), "rows_d", tp, tm, ins, outs, steps=n_p + n_d)


def _prompt_front(x3, P, w, meta):
    batch, seq, d = x3.shape
    heads, c_dim = meta["heads"], meta["c_dim"]
    x = x3.reshape(batch * seq, d)
    x, q, k, vt, lat, kr = _rows_a(x, P, w, _rope_table(jnp.arange(seq)), heads, c_dim, absorbed=False)
    o = _mla_attn_prompt(q, k, vt, batch, seq, heads)
    x, kv, k2, v_t = _rows_b(o, x, P, w, meta, decode=False)
    return x, kv, k2, v_t, lat, kr


def _decode_front(x3, cache_lat, cache_rope, cache_k, P, w, meta):
    batch, n_new, d = x3.shape
    heads, c_dim = meta["heads"], meta["c_dim"]
    past = cache_lat.shape[2]
    w_c = cache_k.shape[1]
    assert past % CHUNK == 0 and n_new <= CHUNK and w_c <= WINDOW_CHUNKS * CHUNK and w_c <= past
    t = batch * n_new
    x = x3.reshape(t, d)
    tm = _row_tile(t)
    assert tm % n_new == 0
    tab = jnp.tile(_rope_table(past + jnp.arange(n_new)), (tm // n_new, 1))
    x, q, ql, lat, kr = _rows_a(x, P, w, tab, heads, c_dim, absorbed=True)
    ol = _mla_attn_decode(ql, q, cache_lat, jnp.swapaxes(cache_rope, 2, 3), lat, kr, heads, n_new)
    x, kv = _rows_b(ol, x, P, w, meta, decode=True)
    return x, kv, lat, kr


def _swa_prompt_attn(q_t, k2, v_t, P, meta, batch, seq):
    blk = 2 * CHUNK
    rel = (jnp.arange(2 * blk) - blk)[:, None] - jnp.arange(blk)[None, :]
    bias_t = _bias_table(P["rel_bias"], _rel_bucket(rel).astype(jnp.int32))
    return _swa_prompt(q_t, k2, v_t, bias_t, P["swa_sinks"][0], batch, seq, meta["s_heads"], meta["rep"], blk)


def _swa_decode_attn(qs, kv, cache_k, cache_v, past, P, w, meta, batch, n_new):
    kvw, s_heads = meta["kvw"], meta["s_heads"]
    w_c = cache_k.shape[1]
    qpos = past + jnp.arange(n_new)
    kpos = jnp.arange(past - w_c, past + n_new)
    bias_h = _bias_table(P["rel_bias"], _rel_bucket(kpos[None, :] - qpos[:, None]).astype(jnp.int32))
    bias = bias_h.reshape(s_heads * n_new, w_c + n_new)
    sink_rows = jnp.repeat(P["swa_sinks"][0], n_new).reshape(s_heads * n_new, 1)
    const2 = lambda a: pl.BlockSpec(a.shape, lambda b: (0, 0), pipeline_mode=pl.Buffered(1))
    nb = math.gcd(SWA_DECODE_BATCH, batch)
    return pl.pallas_call(
        functools.partial(_swa_decode_kernel, kv_w=kvw, groups=meta["groups"], rep=meta["rep"], n_new=n_new),
        grid=(batch // nb,),
        in_specs=[pl.BlockSpec((nb * n_new, qs.shape[1]), lambda b: (b, 0)),
                  pl.BlockSpec((nb, w_c, kvw), lambda b: (b, 0, 0)),
                  pl.BlockSpec((nb, w_c, kvw), lambda b: (b, 0, 0)),
                  pl.BlockSpec((nb * n_new, 2 * kvw), lambda b: (b, 0)),
                  const2(w["rep_kv"]), const2(bias), const2(sink_rows)],
        out_specs=pl.BlockSpec((nb * n_new, qs.shape[1]), lambda b: (b, 0)),
        out_shape=jax.ShapeDtypeStruct(qs.shape, BF16),
        compiler_params=_params(1), name="swa_decode",
    )(qs, cache_k.reshape(batch, w_c, kvw), cache_v.reshape(batch, w_c, kvw), kv, w["rep_kv"], bias, sink_rows)


def _new_kv(kv, batch, rows, keep, meta):
    kvw = meta["kvw"]
    kv3 = kv.reshape(batch, rows, 2 * kvw)[:, rows - keep:]
    return (kv3[:, :, :kvw].reshape(batch, keep, meta["groups"], SWA_HEAD_DIM),
            kv3[:, :, kvw:].reshape(batch, keep, meta["groups"], SWA_HEAD_DIM))


def kernel(x_prompt, x_sample, cache_mla_latent, cache_mla_krope, cache_swa_k, cache_swa_v, ffn_norm1, ffn1_w_gate, ffn1_w_up, ffn1_w_down, mix_norm, ffn_norm2, ffn2_w_gate, ffn2_w_up, ffn2_w_down, mla_w_dq, mla_q_norm, mla_w_uq, mla_w_dkv, mla_kv_norm, mla_w_uk, mla_w_uv, mla_w_o, kv_norm, w_kv_shared, swa_w_q, swa_sinks, swa_w_o, rel_bias, final_norm):
    assert ffn_norm1.shape[0] == 2 and mla_w_dq.shape[0] == 1 and swa_w_q.shape[0] == 1
    scale = (MLA_NOPE + MLA_ROPE) ** -0.5 * LOG2E
    w, meta = _prep_weights(mla_w_dq[0], mla_w_uq[0], mla_w_dkv[0], mla_w_uk[0], mla_w_uv[0], mla_w_o[0],
                            w_kv_shared, swa_w_q[0], swa_w_o[0], scale)
    w.update(f1g=ffn1_w_gate.astype(BF16), f1u=ffn1_w_up.astype(BF16), f1d=ffn1_w_down.astype(BF16),
             f2g=ffn2_w_gate.astype(BF16), f2u=ffn2_w_up.astype(BF16), f2d=ffn2_w_down.astype(BF16))
    P = dict(ffn_norm1=ffn_norm1, mix_norm=mix_norm, ffn_norm2=ffn_norm2, mla_q_norm=mla_q_norm,
             mla_kv_norm=mla_kv_norm, kv_norm=kv_norm, swa_sinks=swa_sinks, rel_bias=rel_bias, final_norm=final_norm)
    batch, seq, d = x_prompt.shape
    dbatch, n_new, _ = x_sample.shape
    c_dim = meta["c_dim"]
    xp, kv_p, k2, v_t, lat_p, rope_p = _prompt_front(x_prompt, P, w, meta)
    xd, kv_s, lat_s, rope_s = _decode_front(x_sample, cache_mla_latent, cache_mla_krope, cache_swa_k, P, w, meta)
    xp, q_t, xd, qs = _rows_c_both(xp, xd, P, w)
    op = _swa_prompt_attn(q_t, k2, v_t, P, meta, batch, seq)
    od = _swa_decode_attn(qs, kv_s, cache_swa_k, cache_swa_v, cache_mla_latent.shape[2], P, w, meta, dbatch, n_new)
    y_p, y_s = _rows_d_both(op, xp, od, xd, P, w)
    k_p, v_p = _new_kv(kv_p, batch, seq, min(WINDOW, seq), meta)
    k_s, v_s = _new_kv(kv_s, dbatch, n_new, n_new, meta)
    return (y_p.reshape(batch, seq, d), y_s.reshape(dbatch, n_new, d),
            lat_p.reshape(1, batch, seq, c_dim), rope_p.reshape(1, batch, seq, MLA_ROPE), k_p, v_p,
            lat_s.reshape(1, dbatch, n_new, c_dim), rope_s.reshape(1, dbatch, n_new, MLA_ROPE), k_s, v_s)
```

```python
import functools
import math

import jax
import jax.numpy as jnp
from jax import lax
from jax.experimental import pallas as pl
from jax.experimental.pallas import tpu as pltpu

F32 = jnp.float32
BF16 = jnp.bfloat16

CHUNK = 64
RMS_EPS = 1e-6
FFN_RES = 0.5
ROPE_BASE = 10000.0
WINDOW = 128
WINDOW_CHUNKS = WINDOW // CHUNK
N_BUCKETS = 32
MAX_DISTANCE = 128
NEG_INF = -1e30
LOG2E = math.log2(math.e)
MLA_NOPE = 64
MLA_ROPE = 32
MLA_V = 64
MLA_VT_ROWS = 80
SWA_HEAD_DIM = 64

LANES = 128
ROW_TILE = 512
MLA_BQ = 512
MLA_BK = 512
MLA_DIAG = 256
MLA_HEADS_PER_STEP = 8
MLA_DECODE_CHUNK = 1024
MLA_DECODE_BATCH = 2
SWA_DECODE_BATCH = 4
SWA_BLOCKS_PER_STEP = 8
VMEM_LIMIT = 60 * 1024 * 1024


def _params(n_axes):
    return pltpu.CompilerParams(dimension_semantics=("arbitrary",) * n_axes, vmem_limit_bytes=VMEM_LIMIT)


def _rms(xf, g):
    return xf * lax.rsqrt(jnp.mean(xf * xf, axis=-1, keepdims=True) + RMS_EPS) * g


def _dot(a, b):
    return jnp.dot(a, b, preferred_element_type=F32)


def _dot_nt(a, b):
    return lax.dot_general(a, b, (((1,), (1,)), ((), ())), preferred_element_type=F32)


def _row_tile(t):
    tm = min(ROW_TILE, t)
    assert t % tm == 0, (t, tm)
    return tm


def _rows(a, tm, tile=lambda i: i):
    return a, pl.BlockSpec((tm, a.shape[1]), lambda i: (tile(i), 0))


def _two_streams(n_first):
    return (lambda i: jnp.minimum(i, n_first - 1)), (lambda i: jnp.maximum(i - n_first, 0))


def _const(a):
    nd = a.ndim
    return a, pl.BlockSpec(a.shape, lambda i: (0,) * nd, pipeline_mode=pl.Buffered(1))


def _layer(a, layer):
    nd = a.ndim - 1
    return a, pl.BlockSpec((None,) + tuple(a.shape[1:]), lambda i: (layer,) + (0,) * nd,
                           pipeline_mode=pl.Buffered(1))


def _cycle(a, tm):
    assert a.shape[0] % tm == 0
    nb = a.shape[0] // tm
    return a, pl.BlockSpec((tm, a.shape[1]), lambda i: (i % nb, 0))


def _rows_out(t, tm, ncols, dtype, tile=lambda i: i):
    return jax.ShapeDtypeStruct((t, ncols), dtype), pl.BlockSpec((tm, ncols), lambda i: (tile(i), 0))


def _rows_call(body, name, t, tm, ins, outs, steps=None):
    return pl.pallas_call(
        body, grid=(steps or t // tm,),
        in_specs=[s for _, s in ins], out_specs=[s for _, s in outs], out_shape=[o for o, _ in outs],
        compiler_params=_params(1), name=name,
    )(*[a for a, _ in ins])


def _ffn_ins(P, w, which, layer):
    return [_const(P["ffn_norm%d" % which][layer].reshape(1, -1)), _layer(w["f%dg" % which], layer),
            _layer(w["f%du" % which], layer), _layer(w["f%dd" % which], layer)]


def _ffn_apply(x, g_ref, wg_ref, wu_ref, wd_ref):
    h = _rms(x, g_ref[...]).astype(BF16)
    gate = _dot(h, wg_ref[...])
    up = _dot(h, wu_ref[...])
    a = (gate * jax.nn.sigmoid(gate) * up).astype(BF16)
    return x + FFN_RES * _dot(a, wd_ref[...])


def _mla_proj_apply(x, mg_ref, wdq_ref, qn_ref, wq_ref, wlat_ref, kvn_ref, wkr_ref, tab_ref,
                    refs, heads, c_dim, absorbed):
    cos_t, sin_t = tab_ref[:, 0:LANES], tab_ref[:, LANES:2 * LANES]
    lane = lax.broadcasted_iota(jnp.int32, (1, LANES), 1)
    cos_q = jnp.where(lane < MLA_NOPE, 1.0, cos_t)
    first_half = lane < MLA_NOPE + MLA_ROPE // 2

    def swap_halves(v):
        return jnp.where(first_half, pltpu.roll(v, LANES - MLA_ROPE // 2, 1), pltpu.roll(v, MLA_ROPE // 2, 1))

    h = _rms(x, mg_ref[...]).astype(BF16)
    cq = _rms(_dot(h, wdq_ref[...]), qn_ref[...]).astype(BF16)
    qa = _dot(cq, wq_ref[...])
    lat = _rms(_dot(h, wlat_ref[...]), kvn_ref[...])
    kr = _dot(h, wkr_ref[...])
    kr = kr * cos_t + swap_halves(kr) * sin_t
    latb = lat.astype(BF16)
    if absorbed:
        wukd_ref, q_ref, ql_ref, lat_ref, kr_ref = refs
    else:
        wuk_ref, wuvt_ref, q_ref, k_ref, vt_ref, lat_ref, kr_ref = refs
        kn = _dot(latb, wuk_ref[...])
    lat_ref[...] = lat
    kr_ref[...] = kr[:, MLA_NOPE:MLA_NOPE + MLA_ROPE]
    for hh in range(heads):
        sl = slice(hh * LANES, (hh + 1) * LANES)
        qh = (qa[:, sl] * cos_q + swap_halves(qa[:, sl]) * sin_t).astype(BF16)
        q_ref[:, sl] = qh
        if absorbed:
            ql_ref[:, hh * c_dim:(hh + 1) * c_dim] = _dot(qh, wukd_ref[hh]).astype(BF16)
        else:
            k_ref[:, sl] = (kn[:, sl] + kr).astype(BF16)
    if not absorbed:
        vt = _dot_nt(wuvt_ref[...], latb)
        ones_row = lax.broadcasted_iota(jnp.int32, vt.shape, 0) % MLA_VT_ROWS == MLA_V
        vt_ref[...] = jnp.where(ones_row, 1.0, vt).astype(BF16)


def _rows_a_kernel(x_ref, fg, wg, wu, wd, mg, wdq, qn, wq, wlat, kvn, wkr, tab, *refs,
                   heads, c_dim, absorbed):
    n_extra = 1 if absorbed else 2
    x1_ref = refs[n_extra]
    x1 = _ffn_apply(x_ref[...], fg, wg, wu, wd)
    x1_ref[...] = x1
    _mla_proj_apply(x1, mg, wdq, qn, wq, wlat, kvn, wkr, tab, refs[:n_extra] + refs[n_extra + 1:],
                    heads, c_dim, absorbed)


def _rows_b_kernel(a_ref, *refs, dec_heads, c_dim, tiled_kv):
    if dec_heads:
        wuv_ref, refs = refs[0], refs[1:]
        a = jnp.concatenate([_dot(a_ref[:, hh * c_dim:(hh + 1) * c_dim], wuv_ref[hh]).astype(BF16)
                             for hh in range(dec_heads)], axis=1)
    else:
        a = a_ref[...]
    wo, x_ref, fg, wg, wu, wd, kvg, wkv = refs[:8]
    x = _ffn_apply(x_ref[...] + _dot(a, wo[...]), fg, wg, wu, wd)
    kv = _dot(_rms(x, kvg[...]).astype(BF16), wkv[...])
    if tiled_kv:
        repk, eye, x_out, kv_out, k2_out, vt_out = refs[8:]
        half = kv.shape[1] // 2
        k2_out[...] = _dot(kv[:, :half].astype(BF16), repk[...]).astype(BF16)
        vt_out[...] = _dot_nt(eye[...], kv[:, half:].astype(BF16)).astype(BF16)
    else:
        x_out, kv_out = refs[8:]
    x_out[...] = x
    kv_out[...] = kv


def _rows_c_kernel(x_ref, fg, wg, wu, wd, mg, wq, x_out, q_out, *, scale, transposed):
    x = _ffn_apply(x_ref[...], fg, wg, wu, wd)
    x_out[...] = x
    h = _rms(x, mg[...]).astype(BF16)
    q = _dot_nt(wq[...], h) if transposed else _dot(h, wq[...])
    q_out[...] = (q * scale).astype(BF16)


def _rows_d_kernel(a_ref, wo, x_ref, fg, wg, wu, wd, fin_g, y_out):
    x = _ffn_apply(x_ref[...] + _dot(a_ref[...], wo[...]), fg, wg, wu, wd)
    y_out[...] = _rms(x, fin_g[...])


def _rows_c_both_kernel(xp_ref, xd_ref, fg, wg, wu, wd, mg, wq_t, wq, xp_out, qt_out, xd_out, qd_out, *, scale, n_p):
    i = pl.program_id(0)

    @pl.when(i < n_p)
    def _():
        _rows_c_kernel(xp_ref, fg, wg, wu, wd, mg, wq_t, xp_out, qt_out, scale=scale, transposed=True)

    @pl.when(i >= n_p)
    def _():
        _rows_c_kernel(xd_ref, fg, wg, wu, wd, mg, wq, xd_out, qd_out, scale=scale, transposed=False)


def _rows_d_both_kernel(ap_ref, xp_ref, ad_ref, xd_ref, wo, fg, wg, wu, wd, fin_g, yp_out, yd_out, *, n_p):
    i = pl.program_id(0)

    @pl.when(i < n_p)
    def _():
        _rows_d_kernel(ap_ref, wo, xp_ref, fg, wg, wu, wd, fin_g, yp_out)

    @pl.when(i >= n_p)
    def _():
        _rows_d_kernel(ad_ref, wo, xd_ref, fg, wg, wu, wd, fin_g, yd_out)


def _mla_attn_kernel(q_ref, k_ref, vt_ref, o_ref, m_sc, acc_sc, sa_sc, sb_sc, *, bq, bk, hp):
    i = pl.program_id(2)
    qs = [q_ref[:, hh * LANES:(hh + 1) * LANES] for hh in range(hp)]
    sls = [slice(hh * LANES, (hh + 1) * LANES) for hh in range(hp)]

    def own_scores(hh, start, width, q_lo):
        s = _dot_nt(k_ref[pl.ds(start, width), sls[hh]], qs[hh][q_lo:])
        kc = (start + lax.broadcasted_iota(jnp.int32, (width, 1), 0)) // CHUNK
        qc = (i * bq + q_lo + lax.broadcasted_iota(jnp.int32, (1, bq - q_lo), 1)) // CHUNK
        return jnp.where(kc <= qc, s, NEG_INF)

    def own_update(ss, start, width, q_lo, after_head=None):
        ql = slice(q_lo, bq)
        for hh in range(hp):
            m = m_sc[hh, :, ql]
            m_new = jnp.maximum(m, jnp.max(ss[hh], axis=0, keepdims=True))
            alpha = jnp.exp2(m - m_new)
            p = jnp.exp2(ss[hh] - m_new).astype(BF16)
            vt = vt_ref[hh * MLA_VT_ROWS:(hh + 1) * MLA_VT_ROWS, pl.ds(start, width)]
            acc_sc[hh, :, ql] = alpha * acc_sc[hh, :, ql] + _dot(vt, p)
            m_sc[hh, :, ql] = m_new
            if after_head is not None:
                after_head(hh)

    def produce(hh, j, dst):
        dst[hh] = _dot_nt(k_ref[pl.ds(pl.multiple_of(j * bk, bk), bk), sls[hh]], qs[hh])

    def consume(hh, j, src):
        s = src[hh]
        m = m_sc[hh]
        m_new = jnp.maximum(m, jnp.max(s, axis=0, keepdims=True))
        alpha = jnp.exp2(m - m_new)
        p = jnp.exp2(s - m_new).astype(BF16)
        vt = vt_ref[hh * MLA_VT_ROWS:(hh + 1) * MLA_VT_ROWS, pl.ds(pl.multiple_of(j * bk, bk), bk)]
        acc_sc[hh] = alpha * acc_sc[hh] + _dot(vt, p)
        m_sc[hh] = m_new

    def transition(j, src, dst):
        produce(0, j + 1, dst)
        for hh in range(hp):
            consume(hh, j, src)
            if hh + 1 < hp:
                produce(hh + 1, j + 1, dst)

    m_sc[...] = jnp.full(m_sc.shape, NEG_INF, F32)
    acc_sc[...] = jnp.zeros(acc_sc.shape, F32)
    n_full = (i * bq) // bk

    diag = min(MLA_DIAG, bq)
    n_diag = bq // diag
    starts = [pl.multiple_of(i * bq + t * diag, diag) for t in range(n_diag)]
    own = [[own_scores(hh, starts[0], diag, 0) for hh in range(hp)]] + [[None] * hp for _ in range(n_diag - 1)]
    for t in range(n_diag):
        def after_head(hh, t=t):
            if t == 0:
                produce(hh, 0, sa_sc)
            if t + 1 < n_diag:
                own[t + 1][hh] = own_scores(hh, starts[t + 1], diag, (t + 1) * diag)

        own_update(own[t], starts[t], diag, t * diag, after_head=after_head)

    @pl.when(n_full > 0)
    def _():
        last = n_full - 1

        @pl.loop(0, last // 2)
        def _(t):
            transition(2 * t, sa_sc, sb_sc)
            transition(2 * t + 1, sb_sc, sa_sc)

        @pl.when(last % 2 == 1)
        def _():
            transition(last - 1, sa_sc, sb_sc)
            for hh in range(hp):
                consume(hh, last, sb_sc)

        @pl.when(last % 2 == 0)
        def _():
            for hh in range(hp):
                consume(hh, last, sa_sc)

    for pair in range(hp // 2):
        halves = []
        for hh in (2 * pair, 2 * pair + 1):
            acc = acc_sc[hh]
            halves.append(acc[:MLA_V] * (1.0 / acc[MLA_V:MLA_V + 1]))
        o_ref[:, pair * LANES:(pair + 1) * LANES] = jnp.concatenate(halves, axis=0).T.astype(BF16)


def _mla_attn_prompt(q, k, vt, batch, seq, heads, bq=MLA_BQ, bk=MLA_BK, hp=MLA_HEADS_PER_STEP):
    bq, bk = min(bq, seq), min(bk, seq)
    nq = seq // bq
    assert seq % bq == 0 and seq % bk == 0 and bq % CHUNK == 0 and heads % hp == 0 and hp % 2 == 0
    assert bq % bk == 0 and bq % min(MLA_DIAG, bq) == 0 and MLA_DIAG % CHUNK == 0
    w = hp * LANES
    return pl.pallas_call(
        functools.partial(_mla_attn_kernel, bq=bq, bk=bk, hp=hp),
        grid=(batch, heads // hp, nq),
        in_specs=[pl.BlockSpec((bq, w), lambda b, h, i: (b * nq + i, h)),
                  pl.BlockSpec((seq, w), lambda b, h, i: (b, h)),
                  pl.BlockSpec((hp * MLA_VT_ROWS, seq), lambda b, h, i: (h, b))],
        out_specs=pl.BlockSpec((bq, hp * MLA_V), lambda b, h, i: (b * nq + i, h)),
        out_shape=jax.ShapeDtypeStruct((q.shape[0], heads * MLA_V), BF16),
        scratch_shapes=[pltpu.VMEM((hp, 1, bq), F32), pltpu.VMEM((hp, MLA_VT_ROWS, bq), F32),
                        pltpu.VMEM((hp, bk, bq), F32), pltpu.VMEM((hp, bk, bq), F32)],
        compiler_params=_params(3), name="mla_attn_prompt",
    )(q, k, vt)


def _mla_attn_decode_kernel(ql_ref, q_ref, cl_ref, crt_ref, nl_ref, nr_ref, o_ref, *, heads, c_dim, n_new):
    nb = cl_ref.shape[0]
    past = cl_ref.shape[1]
    chunk = min(past, MLA_DECODE_CHUNK)
    n_parts = past // chunk + 1
    rows = [slice(bb * n_new, (bb + 1) * n_new) for bb in range(nb)]
    qls = [jnp.concatenate([ql_ref[rows[bb], hh * c_dim:(hh + 1) * c_dim] for hh in range(heads)], axis=0)
           for bb in range(nb)]
    qrs = [jnp.concatenate([q_ref[rows[bb], hh * LANES + MLA_NOPE:hh * LANES + MLA_NOPE + MLA_ROPE]
                            for hh in range(heads)], axis=0) for bb in range(nb)]

    def scores(bb, c):
        if c + 1 < n_parts:
            kl = cl_ref[bb, c * chunk:(c + 1) * chunk, :].astype(BF16)
            return kl, _dot_nt(qls[bb], kl) + _dot(qrs[bb], crt_ref[bb, :, c * chunk:(c + 1) * chunk].astype(BF16))
        kl = nl_ref[rows[bb], :].astype(BF16)
        return kl, _dot_nt(qls[bb], kl) + _dot_nt(qrs[bb], nr_ref[rows[bb], :].astype(BF16))

    state = [(jnp.full((heads * n_new, 1), NEG_INF, F32), jnp.zeros((heads * n_new, 1), F32),
              jnp.zeros((heads * n_new, c_dim), F32)) for _ in range(nb)]
    nxt = [scores(bb, 0) for bb in range(nb)]
    for c in range(n_parts):
        for bb in range(nb):
            kl, s = nxt[bb]
            if c + 1 < n_parts:
                nxt[bb] = scores(bb, c + 1)
            m, l, acc = state[bb]
            m_new = jnp.maximum(m, jnp.max(s, axis=-1, keepdims=True))
            alpha = jnp.exp2(m - m_new)
            p = jnp.exp2(s - m_new)
            l = alpha * l + jnp.sum(p, axis=-1, keepdims=True)
            acc = alpha * acc + _dot(p.astype(BF16), kl)
            state[bb] = (m_new, l, acc)
    for bb in range(nb):
        _, l, acc = state[bb]
        o = acc / l
        for hh in range(heads):
            o_ref[rows[bb], hh * c_dim:(hh + 1) * c_dim] = o[hh * n_new:(hh + 1) * n_new].astype(BF16)


def _mla_attn_decode(ql, q, cache_lat, cache_rope_t, new_lat, new_rope, heads, n_new, nb=MLA_DECODE_BATCH):
    _, batch, past, c_dim = cache_lat.shape
    r_dim = cache_rope_t.shape[2]
    nb = math.gcd(nb, batch)
    return pl.pallas_call(
        functools.partial(_mla_attn_decode_kernel, heads=heads, c_dim=c_dim, n_new=n_new),
        grid=(batch // nb,),
        in_specs=[pl.BlockSpec((nb * n_new, heads * c_dim), lambda b: (b, 0)),
                  pl.BlockSpec((nb * n_new, heads * LANES), lambda b: (b, 0)),
                  pl.BlockSpec((None, nb, past, c_dim), lambda b: (0, b, 0, 0)),
                  pl.BlockSpec((None, nb, r_dim, past), lambda b: (0, b, 0, 0)),
                  pl.BlockSpec((nb * n_new, c_dim), lambda b: (b, 0)),
                  pl.BlockSpec((nb * n_new, r_dim), lambda b: (b, 0))],
        out_specs=pl.BlockSpec((nb * n_new, heads * c_dim), lambda b: (b, 0)),
        out_shape=jax.ShapeDtypeStruct(ql.shape, BF16),
        compiler_params=_params(1), name="mla_attn_decode",
    )(ql, q, cache_lat, cache_rope_t, new_lat, new_rope)


def _bias_kernel(tab_ref, idx_ref, o_ref, *, heads):
    idx = idx_ref[...]
    for hh in range(heads):
        acc = jnp.zeros(idx.shape, F32)
        for b in range(N_BUCKETS):
            acc = jnp.where(idx == b, tab_ref[b, hh] * LOG2E, acc)
        o_ref[hh] = acc


def _bias_table(rel_bias, idx):
    heads = rel_bias.shape[1]
    return pl.pallas_call(
        functools.partial(_bias_kernel, heads=heads),
        in_specs=[pl.BlockSpec(memory_space=pltpu.SMEM), pl.BlockSpec(idx.shape, lambda: (0, 0))],
        out_specs=pl.BlockSpec((heads,) + idx.shape, lambda: (0, 0, 0)),
        out_shape=jax.ShapeDtypeStruct((heads,) + idx.shape, F32), name="rel_bias_table",
    )(rel_bias, idx)


def _rel_bucket(rel):
    half = N_BUCKETS // 2
    max_exact = half // 2
    base = jnp.where(rel > 0, half, 0)
    n = jnp.abs(rel)
    nf = jnp.maximum(n, 1).astype(jnp.float32)
    large = max_exact + (jnp.log(nf / max_exact) / math.log(MAX_DISTANCE / max_exact)
                         * (half - max_exact)).astype(jnp.int32)
    large = jnp.minimum(large, half - 1)
    return base + jnp.where(n < max_exact, n, large)


def _swa_prompt_kernel(sink_ref, qt_ref, kp_ref, kc_ref, vtp_ref, vtc_ref, bias_ref, o_ref, *, s_heads, rep, blk, nsub):
    i = pl.program_id(1)
    key_chunk = lax.broadcasted_iota(jnp.int32, (2 * blk, 1), 0) // CHUNK
    row_chunk = lax.broadcasted_iota(jnp.int32, (1, blk), 1) // CHUNK
    first = blk // CHUNK
    window = (key_chunk >= row_chunk) & (key_chunk <= row_chunk + WINDOW_CHUNKS)
    dh = SWA_HEAD_DIM
    units = [(u, hh) for u in range(nsub) for hh in range(s_heads)]
    ss = []
    for u, hh in units:
        g, pair, slot = hh // rep, hh // 2, hh % 2
        ksl = slice(g * 4 * dh + slot * 2 * dh, g * 4 * dh + (slot + 1) * 2 * dh)
        k = jnp.concatenate([kp_ref[:, ksl], kc_ref[:, ksl]], axis=0)[u * blk:(u + 2) * blk]
        ss.append(_dot(k, qt_ref[pair * 2 * dh:(pair + 1) * 2 * dh, u * blk:(u + 1) * blk]))
    ps, invs = [], []
    for (u, hh), s in zip(units, ss):
        valid = window if u > 0 else window & ((i > 0) | (key_chunk >= first))
        s = jnp.where(valid, s + bias_ref[hh], NEG_INF)
        sink = sink_ref[hh] * LOG2E
        m = jnp.maximum(jnp.max(s, axis=0, keepdims=True), sink)
        p = jnp.exp2(s - m)
        invs.append(1.0 / (jnp.sum(p, axis=0, keepdims=True) + jnp.exp2(sink - m)))
        ps.append(p.astype(BF16))
    outs = []
    for n, (u, hh) in enumerate(units):
        g = hh // rep
        vt = jnp.concatenate([vtp_ref[g * dh:(g + 1) * dh, :], vtc_ref[g * dh:(g + 1) * dh, :]],
                             axis=1)[:, u * blk:(u + 2) * blk]
        outs.append(_dot(vt, ps[n]) * invs[n])
    for u in range(nsub):
        for pair in range(s_heads // 2):
            n = u * s_heads + 2 * pair
            o_ref[u * blk:(u + 1) * blk, pair * 2 * dh:(pair + 1) * 2 * dh] = (
                jnp.concatenate(outs[n:n + 2], axis=0).T.astype(BF16))


def _swa_prompt(qt, k2, vt, bias_t, sinks, batch, seq, s_heads, rep, blk=128, nsub=SWA_BLOCKS_PER_STEP):
    assert seq % (blk * nsub) == 0
    nq = seq // (blk * nsub)
    kw = k2.shape[1]
    cur = lambda b, i: (b * nq + i, 0)
    prev = lambda b, i: ((b * nq + i) * nsub - jnp.minimum(i, 1), 0)
    cur_t = lambda b, i: (0, b * nq + i)
    prev_t = lambda b, i: (0, (b * nq + i) * nsub - jnp.minimum(i, 1))
    assert rep == 4 and kw == (s_heads // rep) * 4 * SWA_HEAD_DIM
    return pl.pallas_call(
        functools.partial(_swa_prompt_kernel, s_heads=s_heads, rep=rep, blk=blk, nsub=nsub),
        grid=(batch, nq),
        in_specs=[pl.BlockSpec(memory_space=pltpu.SMEM),
                  pl.BlockSpec((qt.shape[0], blk * nsub), cur_t),
                  pl.BlockSpec((blk, kw), prev), pl.BlockSpec((blk * nsub, kw), cur),
                  pl.BlockSpec((vt.shape[0], blk), prev_t), pl.BlockSpec((vt.shape[0], blk * nsub), cur_t),
                  pl.BlockSpec(bias_t.shape, lambda b, i: (0, 0, 0), pipeline_mode=pl.Buffered(1))],
        out_specs=pl.BlockSpec((blk * nsub, qt.shape[0]), cur),
        out_shape=jax.ShapeDtypeStruct((qt.shape[1], qt.shape[0]), BF16),
        compiler_params=_params(2), name="swa_prompt",
    )(sinks, qt, k2, k2, vt, vt, bias_t)


def _swa_decode_kernel(q_ref, ck_ref, cv_ref, nkv_ref, rep_ref, bias_ref, sink_ref, o_ref, *, kv_w, groups, rep, n_new):
    nb = ck_ref.shape[0]
    n = n_new
    gw = rep * SWA_HEAD_DIM
    lane_head = lax.broadcasted_iota(jnp.int32, (1, gw), 1) // SWA_HEAD_DIM
    sls = [slice(g * gw, (g + 1) * gw) for g in range(groups)]
    units = [(bb, g) for bb in range(nb) for g in range(groups)]
    k4s, v4s = [], []
    for bb in range(nb):
        new = nkv_ref[bb * n:(bb + 1) * n, :]
        k = jnp.concatenate([ck_ref[bb], new[:, 0:kv_w]], axis=0).astype(BF16)
        v = jnp.concatenate([cv_ref[bb], new[:, kv_w:2 * kv_w]], axis=0).astype(BF16)
        k4s.append(_dot(k, rep_ref[...]).astype(BF16))
        v4s.append(_dot(v, rep_ref[...]).astype(BF16))
    ss = []
    for bb, g in units:
        qg = q_ref[bb * n:(bb + 1) * n, sls[g]]
        qs = jnp.concatenate([jnp.where(lane_head == r, qg, jnp.zeros_like(qg)) for r in range(rep)], axis=0)
        ss.append(_dot_nt(qs, k4s[bb][:, sls[g]]))
    ps = []
    for (bb, g), s in zip(units, ss):
        rows = slice(g * rep * n, (g + 1) * rep * n)
        s = s + bias_ref[rows, :]
        sink = sink_ref[rows, :] * LOG2E
        m = jnp.maximum(jnp.max(s, axis=-1, keepdims=True), sink)
        p = jnp.exp2(s - m)
        inv = 1.0 / (jnp.sum(p, axis=-1, keepdims=True) + jnp.exp2(sink - m))
        ps.append((p * inv).astype(BF16))
    for (bb, g), p in zip(units, ps):
        res = _dot(p, v4s[bb][:, sls[g]])
        og = jnp.zeros((n, gw), F32)
        for r in range(rep):
            og = jnp.where(lane_head == r, res[r * n:(r + 1) * n], og)
        o_ref[bb * n:(bb + 1) * n, sls[g]] = og.astype(BF16)


def _rope_table(pos):
    inv = ROPE_BASE ** (-jnp.arange(0, MLA_ROPE, 2, dtype=jnp.float32) / MLA_ROPE)
    ang = pos.astype(jnp.float32)[:, None] * inv[None, :]
    cos, sin = jnp.cos(ang), jnp.sin(ang)
    widths = ((0, 0), (MLA_NOPE, LANES - MLA_NOPE - MLA_ROPE))
    return jnp.concatenate([jnp.pad(jnp.concatenate([cos, cos], axis=1), widths),
                            jnp.pad(jnp.concatenate([-sin, sin], axis=1), widths)], axis=1)


def _prep_weights(mla_w_dq, mla_w_uq, mla_w_dkv, mla_w_uk, mla_w_uv, mla_w_o, w_kv_shared, swa_w_q, swa_w_o, scale):
    ql, qcols = mla_w_uq.shape
    c_dim, heads, nope = mla_w_uk.shape
    rope = qcols // heads - nope
    d = mla_w_dq.shape[0]
    pad = LANES - nope - rope
    w = {}
    w["wq"] = jnp.pad(mla_w_uq.reshape(ql, heads, nope + rope), ((0, 0), (0, 0), (0, pad))).reshape(
        ql, heads * LANES) * scale
    w["wdq"] = mla_w_dq
    w["wlat"] = mla_w_dkv[:, :c_dim]
    w["wkr"] = jnp.pad(mla_w_dkv[:, c_dim:], ((0, 0), (nope, pad)))
    zc = jnp.zeros((c_dim, heads, LANES - nope), F32)
    w["wuk"] = jnp.concatenate([mla_w_uk, zc], axis=-1).reshape(c_dim, heads * LANES)
    w["wuv_t"] = jnp.pad(mla_w_uv, ((0, 0), (0, 0), (0, MLA_VT_ROWS - mla_w_uv.shape[2]))).reshape(
        c_dim, heads * MLA_VT_ROWS).T
    w["wuk_dec"] = jnp.concatenate([jnp.transpose(mla_w_uk, (1, 2, 0)),
                                    jnp.zeros((heads, LANES - nope, c_dim), F32)], axis=1)
    w["wuv_dec"] = jnp.concatenate([jnp.transpose(mla_w_uv, (1, 0, 2)),
                                    jnp.zeros((heads, c_dim, LANES - nope), F32)], axis=-1)
    vdim = mla_w_uv.shape[2]
    assert vdim == MLA_V and nope == MLA_NOPE and rope == MLA_ROPE
    w["wo"] = mla_w_o
    w["wo_pad"] = jnp.concatenate([mla_w_o.reshape(heads, vdim, d), jnp.zeros((heads, LANES - vdim, d), F32)],
                                  axis=1).reshape(heads * LANES, d)
    kvw = w_kv_shared.shape[1] // 2
    groups = kvw // SWA_HEAD_DIM
    s_heads = swa_w_q.shape[1] // SWA_HEAD_DIM
    rep = s_heads // groups
    w["wkv"] = w_kv_shared
    eye = jnp.eye(SWA_HEAD_DIM, dtype=F32)
    zero = jnp.zeros_like(eye)
    w["rep_k2"] = jnp.kron(jnp.eye(groups, dtype=F32), jnp.concatenate([eye, zero, zero, eye], axis=1))
    w["eye_kv"] = jnp.eye(kvw, dtype=F32)
    w["swa_wq_t"] = swa_w_q.T
    w["swa_wo"] = swa_w_o
    w["rep_kv"] = jnp.kron(jnp.eye(groups, dtype=F32), jnp.tile(eye, (1, rep)))
    w["swa_wq"] = swa_w_q
    return {k: v.astype(BF16) for k, v in w.items()}, dict(heads=heads, c_dim=c_dim, groups=groups, rep=rep,
                                                            s_heads=s_heads, kvw=kvw)


def _rows_a(x, P, w, tab, heads, c_dim, absorbed):
    t, d = x.shape
    tm = _row_tile(t)
    hw = heads * LANES
    row = lambda v: v.reshape(1, -1)
    ins = ([_rows(x, tm)] + _ffn_ins(P, w, 1, 0)
           + [_const(row(P["mix_norm"][0])), _const(w["wdq"]), _const(row(P["mla_q_norm"][0])), _const(w["wq"]),
              _const(w["wlat"]), _const(row(P["mla_kv_norm"][0])), _const(w["wkr"]), _cycle(tab, tm)]
           + ([_const(w["wuk_dec"])] if absorbed else [_const(w["wuk"]), _const(w["wuv_t"])]))
    outs = [_rows_out(t, tm, d, F32), _rows_out(t, tm, hw, BF16)]
    if absorbed:
        outs.append(_rows_out(t, tm, heads * c_dim, BF16))
    else:
        outs += [_rows_out(t, tm, hw, BF16),
                 (jax.ShapeDtypeStruct((heads * MLA_VT_ROWS, t), BF16),
                  pl.BlockSpec((heads * MLA_VT_ROWS, tm), lambda i: (0, i)))]
    outs += [_rows_out(t, tm, c_dim, F32), _rows_out(t, tm, MLA_ROPE, F32)]
    return _rows_call(functools.partial(_rows_a_kernel, heads=heads, c_dim=c_dim, absorbed=absorbed),
                      "rows_a_absorbed" if absorbed else "rows_a", t, tm, ins, outs)


def _rows_b(a, x, P, w, meta, decode):
    t, d = x.shape
    tm = _row_tile(t)
    kvw = meta["kvw"]
    ins = [_rows(a, tm)]
    if decode:
        ins += [_const(w["wuv_dec"]), _const(w["wo_pad"])]
    else:
        ins += [_const(w["wo"])]
    ins += [_rows(x, tm)] + _ffn_ins(P, w, 2, 0) + [_const(P["kv_norm"].reshape(1, -1)), _const(w["wkv"])]
    outs = [_rows_out(t, tm, d, F32), _rows_out(t, tm, 2 * kvw, F32)]
    if not decode:
        ins += [_const(w["rep_k2"]), _const(w["eye_kv"])]
        outs += [_rows_out(t, tm, w["rep_k2"].shape[1], BF16),
                 (jax.ShapeDtypeStruct((kvw, t), BF16), pl.BlockSpec((kvw, tm), lambda i: (0, i)))]
    return _rows_call(functools.partial(_rows_b_kernel, dec_heads=meta["heads"] if decode else 0,
                                        c_dim=meta["c_dim"], tiled_kv=not decode),
                      "rows_b_decode" if decode else "rows_b", t, tm, ins, outs)


def _rows_c_both(xp, xd, P, w):
    (tp, d), td = xp.shape, xd.shape[0]
    tm = _row_tile(tp)
    assert td % tm == 0
    n_p, n_d = tp // tm, td // tm
    first, second = _two_streams(n_p)
    wq_t, wq = w["swa_wq_t"], w["swa_wq"]
    ins = ([_rows(xp, tm, first), _rows(xd, tm, second)] + _ffn_ins(P, w, 1, 1)
           + [_const(P["mix_norm"][1].reshape(1, -1)), _const(wq_t), _const(wq)])
    outs = [_rows_out(tp, tm, d, F32, first),
            (jax.ShapeDtypeStruct((wq_t.shape[0], tp), BF16), pl.BlockSpec((wq_t.shape[0], tm), lambda i: (0, first(i)))),
            _rows_out(td, tm, d, F32, second), _rows_out(td, tm, wq.shape[1], BF16, second)]
    return _rows_call(functools.partial(_rows_c_both_kernel, scale=SWA_HEAD_DIM ** -0.5 * LOG2E, n_p=n_p),
                      "rows_c", tp, tm, ins, outs, steps=n_p + n_d)


def _rows_d_both(ap, xp, ad, xd, P, w):
    (tp, d), td = xp.shape, xd.shape[0]
    tm = _row_tile(tp)
    assert td % tm == 0
    n_p, n_d = tp // tm, td // tm
    first, second = _two_streams(n_p)
    ins = ([_rows(ap, tm, first), _rows(xp, tm, first), _rows(ad, tm, second), _rows(xd, tm, second),
            _const(w["swa_wo"])] + _ffn_ins(P, w, 2, 1) + [_const(P["final_norm"].reshape(1, -1))])
    outs = [_rows_out(tp, tm, d, F32, first), _rows_out(td, tm, d, F32, second)]
    return _rows_call(functools.partial(_rows_d_both_kernel, n_p=n_p), "rows_d", tp, tm, ins, outs, steps=n_p + n_d)


def _prompt_front(x3, P, w, meta):
    batch, seq, d = x3.shape
    heads, c_dim = meta["heads"], meta["c_dim"]
    x = x3.reshape(batch * seq, d)
    x, q, k, vt, lat, kr = _rows_a(x, P, w, _rope_table(jnp.arange(seq)), heads, c_dim, absorbed=False)
    o = _mla_attn_prompt(q, k, vt, batch, seq, heads)
    x, kv, k2, v_t = _rows_b(o, x, P, w, meta, decode=False)
    return x, kv, k2, v_t, lat, kr


def _decode_front(x3, cache_lat, cache_rope, cache_k, P, w, meta):
    batch, n_new, d = x3.shape
    heads, c_dim = meta["heads"], meta["c_dim"]
    past = cache_lat.shape[2]
    w_c = cache_k.shape[1]
    assert past % CHUNK == 0 and n_new <= CHUNK and w_c <= WINDOW_CHUNKS * CHUNK and w_c <= past
    t = batch * n_new
    x = x3.reshape(t, d)
    tm = _row_tile(t)
    assert tm % n_new == 0
    tab = jnp.tile(_rope_table(past + jnp.arange(n_new)), (tm // n_new, 1))
    x, q, ql, lat, kr = _rows_a(x, P, w, tab, heads, c_dim, absorbed=True)
    ol = _mla_attn_decode(ql, q, cache_lat, jnp.swapaxes(cache_rope, 2, 3), lat, kr, heads, n_new)
    x, kv = _rows_b(ol, x, P, w, meta, decode=True)
    return x, kv, lat, kr


def _swa_prompt_attn(q_t, k2, v_t, P, meta, batch, seq):
    blk = 2 * CHUNK
    rel = (jnp.arange(2 * blk) - blk)[:, None] - jnp.arange(blk)[None, :]
    bias_t = _bias_table(P["rel_bias"], _rel_bucket(rel).astype(jnp.int32))
    return _swa_prompt(q_t, k2, v_t, bias_t, P["swa_sinks"][0], batch, seq, meta["s_heads"], meta["rep"], blk)


def _swa_decode_attn(qs, kv, cache_k, cache_v, past, P, w, meta, batch, n_new):
    kvw, s_heads = meta["kvw"], meta["s_heads"]
    w_c = cache_k.shape[1]
    qpos = past + jnp.arange(n_new)
    kpos = jnp.arange(past - w_c, past + n_new)
    bias_h = _bias_table(P["rel_bias"], _rel_bucket(kpos[None, :] - qpos[:, None]).astype(jnp.int32))
    bias = bias_h.reshape(s_heads * n_new, w_c + n_new)
    sink_rows = jnp.repeat(P["swa_sinks"][0], n_new).reshape(s_heads * n_new, 1)
    const2 = lambda a: pl.BlockSpec(a.shape, lambda b: (0, 0), pipeline_mode=pl.Buffered(1))
    nb = math.gcd(SWA_DECODE_BATCH, batch)
    return pl.pallas_call(
        functools.partial(_swa_decode_kernel, kv_w=kvw, groups=meta["groups"], rep=meta["rep"], n_new=n_new),
        grid=(batch // nb,),
        in_specs=[pl.BlockSpec((nb * n_new, qs.shape[1]), lambda b: (b, 0)),
                  pl.BlockSpec((nb, w_c, kvw), lambda b: (b, 0, 0)),
                  pl.BlockSpec((nb, w_c, kvw), lambda b: (b, 0, 0)),
                  pl.BlockSpec((nb * n_new, 2 * kvw), lambda b: (b, 0)),
                  const2(w["rep_kv"]), const2(bias), const2(sink_rows)],
        out_specs=pl.BlockSpec((nb * n_new, qs.shape[1]), lambda b: (b, 0)),
        out_shape=jax.ShapeDtypeStruct(qs.shape, BF16),
        compiler_params=_params(1), name="swa_decode",
    )(qs, cache_k.reshape(batch, w_c, kvw), cache_v.reshape(batch, w_c, kvw), kv, w["rep_kv"], bias, sink_rows)


def _new_kv(kv, batch, rows, keep, meta):
    kvw = meta["kvw"]
    kv3 = kv.reshape(batch, rows, 2 * kvw)[:, rows - keep:]
    return (kv3[:, :, :kvw].reshape(batch, keep, meta["groups"], SWA_HEAD_DIM),
            kv3[:, :, kvw:].reshape(batch, keep, meta["groups"], SWA_HEAD_DIM))


def kernel(x_prompt, x_sample, cache_mla_latent, cache_mla_krope, cache_swa_k, cache_swa_v, ffn_norm1, ffn1_w_gate, ffn1_w_up, ffn1_w_down, mix_norm, ffn_norm2, ffn2_w_gate, ffn2_w_up, ffn2_w_down, mla_w_dq, mla_q_norm, mla_w_uq, mla_w_dkv, mla_kv_norm, mla_w_uk, mla_w_uv, mla_w_o, kv_norm, w_kv_shared, swa_w_q, swa_sinks, swa_w_o, rel_bias, final_norm):
    assert ffn_norm1.shape[0] == 2 and mla_w_dq.shape[0] == 1 and swa_w_q.shape[0] == 1
    scale = (MLA_NOPE + MLA_ROPE) ** -0.5 * LOG2E
    w, meta = _prep_weights(mla_w_dq[0], mla_w_uq[0], mla_w_dkv[0], mla_w_uk[0], mla_w_uv[0], mla_w_o[0],
                            w_kv_shared, swa_w_q[0], swa_w_o[0], scale)
    w.update(f1g=ffn1_w_gate.astype(BF16), f1u=ffn1_w_up.astype(BF16), f1d=ffn1_w_down.astype(BF16),
             f2g=ffn2_w_gate.astype(BF16), f2u=ffn2_w_up.astype(BF16), f2d=ffn2_w_down.astype(BF16))
    P = dict(ffn_norm1=ffn_norm1, mix_norm=mix_norm, ffn_norm2=ffn_norm2, mla_q_norm=mla_q_norm,
             mla_kv_norm=mla_kv_norm, kv_norm=kv_norm, swa_sinks=swa_sinks, rel_bias=rel_bias, final_norm=final_norm)
    batch, seq, d = x_prompt.shape
    dbatch, n_new, _ = x_sample.shape
    c_dim = meta["c_dim"]
    xp, kv_p, k2, v_t, lat_p, rope_p = _prompt_front(x_prompt, P, w, meta)
    xd, kv_s, lat_s, rope_s = _decode_front(x_sample, cache_mla_latent, cache_mla_krope, cache_swa_k, P, w, meta)
    xp, q_t, xd, qs = _rows_c_both(xp, xd, P, w)
    op = _swa_prompt_attn(q_t, k2, v_t, P, meta, batch, seq)
    od = _swa_decode_attn(qs, kv_s, cache_swa_k, cache_swa_v, cache_mla_latent.shape[2], P, w, meta, dbatch, n_new)
    y_p, y_s = _rows_d_both(op, xp, od, xd, P, w)
    k_p, v_p = _new_kv(kv_p, batch, seq, min(WINDOW, seq), meta)
    k_s, v_s = _new_kv(kv_s, dbatch, n_new, n_new, meta)
    return (y_p.reshape(batch, seq, d), y_s.reshape(dbatch, n_new, d),
            lat_p.reshape(1, batch, seq, c_dim), rope_p.reshape(1, batch, seq, MLA_ROPE), k_p, v_p,
            lat_s.reshape(1, dbatch, n_new, c_dim), rope_s.reshape(1, dbatch, n_new, MLA_ROPE), k_s, v_s)
```

```python
import functools
import math

import jax
import jax.numpy as jnp
from jax import lax
from jax.experimental import pallas as pl
from jax.experimental.pallas import tpu as pltpu

F32 = jnp.float32
BF16 = jnp.bfloat16

CHUNK = 64
RMS_EPS = 1e-6
FFN_RES = 0.5
ROPE_BASE = 10000.0
WINDOW = 128
WINDOW_CHUNKS = WINDOW // CHUNK
N_BUCKETS = 32
MAX_DISTANCE = 128
NEG_INF = -1e30
LOG2E = math.log2(math.e)
MLA_NOPE = 64
MLA_ROPE = 32
MLA_V = 64
MLA_VT_ROWS = 80
SWA_HEAD_DIM = 64

LANES = 128
ROW_TILE = 512
MLA_BQ = 512
MLA_BK = 512
MLA_DIAG = 256
MLA_HEADS_PER_STEP = 8
MLA_DECODE_CHUNK = 1024
MLA_DECODE_BATCH = 2
SWA_DECODE_BATCH = 4
SWA_BLOCKS_PER_STEP = 8
VMEM_LIMIT = 60 * 1024 * 1024


def _params(n_axes):
    return pltpu.CompilerParams(dimension_semantics=("arbitrary",) * n_axes, vmem_limit_bytes=VMEM_LIMIT)


def _rms(xf, g):
    return xf * lax.rsqrt(jnp.mean(xf * xf, axis=-1, keepdims=True) + RMS_EPS) * g


def _dot(a, b):
    return jnp.dot(a, b, preferred_element_type=F32)


def _dot_nt(a, b):
    return lax.dot_general(a, b, (((1,), (1,)), ((), ())), preferred_element_type=F32)


def _row_tile(t):
    tm = min(ROW_TILE, t)
    assert t % tm == 0, (t, tm)
    return tm


def _rows(a, tm, tile=lambda i: i):
    return a, pl.BlockSpec((tm, a.shape[1]), lambda i: (tile(i), 0))


def _two_streams(n_first):
    return (lambda i: jnp.minimum(i, n_first - 1)), (lambda i: jnp.maximum(i - n_first, 0))


def _const(a):
    nd = a.ndim
    return a, pl.BlockSpec(a.shape, lambda i: (0,) * nd, pipeline_mode=pl.Buffered(1))


def _layer(a, layer):
    nd = a.ndim - 1
    return a, pl.BlockSpec((None,) + tuple(a.shape[1:]), lambda i: (layer,) + (0,) * nd,
                           pipeline_mode=pl.Buffered(1))


def _cycle(a, tm):
    assert a.shape[0] % tm == 0
    nb = a.shape[0] // tm
    return a, pl.BlockSpec((tm, a.shape[1]), lambda i: (i % nb, 0))


def _rows_out(t, tm, ncols, dtype, tile=lambda i: i):
    return jax.ShapeDtypeStruct((t, ncols), dtype), pl.BlockSpec((tm, ncols), lambda i: (tile(i), 0))


def _rows_call(body, name, t, tm, ins, outs, steps=None):
    return pl.pallas_call(
        body, grid=(steps or t // tm,),
        in_specs=[s for _, s in ins], out_specs=[s for _, s in outs], out_shape=[o for o, _ in outs],
        compiler_params=_params(1), name=name,
    )(*[a for a, _ in ins])


def _ffn_ins(P, w, which, layer):
    return [_const(P["ffn_norm%d" % which][layer].reshape(1, -1)), _layer(w["f%dg" % which], layer),
            _layer(w["f%du" % which], layer), _layer(w["f%dd" % which], layer)]


def _ffn_apply(x, g_ref, wg_ref, wu_ref, wd_ref):
    h = _rms(x, g_ref[...]).astype(BF16)
    gate = _dot(h, wg_ref[...])
    up = _dot(h, wu_ref[...])
    a = (gate * jax.nn.sigmoid(gate) * up).astype(BF16)
    return x + FFN_RES * _dot(a, wd_ref[...])


def _mla_proj_apply(x, mg_ref, wdq_ref, qn_ref, wq_ref, wlat_ref, kvn_ref, wkr_ref, tab_ref,
                    refs, heads, c_dim, absorbed, kr_t=False):
    cos_t, sin_t = tab_ref[:, 0:LANES], tab_ref[:, LANES:2 * LANES]
    lane = lax.broadcasted_iota(jnp.int32, (1, LANES), 1)
    cos_q = jnp.where(lane < MLA_NOPE, 1.0, cos_t)
    first_half = lane < MLA_NOPE + MLA_ROPE // 2

    def swap_halves(v):
        return jnp.where(first_half, pltpu.roll(v, LANES - MLA_ROPE // 2, 1), pltpu.roll(v, MLA_ROPE // 2, 1))

    h = _rms(x, mg_ref[...]).astype(BF16)
    cq = _rms(_dot(h, wdq_ref[...]), qn_ref[...]).astype(BF16)
    qa = _dot(cq, wq_ref[...])
    lat = _rms(_dot(h, wlat_ref[...]), kvn_ref[...])
    kr = _dot(h, wkr_ref[...])
    kr = kr * cos_t + swap_halves(kr) * sin_t
    latb = lat.astype(BF16)
    if absorbed:
        wukd_ref, q_ref, ql_ref, lat_ref, kr_ref = refs
    else:
        wuk_ref, wuvt_ref, q_ref, k_ref, vt_ref, lat_ref, kr_ref = refs
        kn = _dot(latb, wuk_ref[...])
    lat_ref[...] = lat
    if kr_t:
        kr_ref[...] = kr.T[MLA_NOPE:MLA_NOPE + MLA_ROPE, :]
    else:
        kr_ref[...] = kr[:, MLA_NOPE:MLA_NOPE + MLA_ROPE]
    for hh in range(heads):
        sl = slice(hh * LANES, (hh + 1) * LANES)
        qh = (qa[:, sl] * cos_q + swap_halves(qa[:, sl]) * sin_t).astype(BF16)
        q_ref[:, sl] = qh
        if absorbed:
            ql_ref[:, hh * c_dim:(hh + 1) * c_dim] = _dot(qh, wukd_ref[hh]).astype(BF16)
        else:
            k_ref[:, sl] = (kn[:, sl] + kr).astype(BF16)
    if not absorbed:
        vt = _dot_nt(wuvt_ref[...], latb)
        ones_row = lax.broadcasted_iota(jnp.int32, vt.shape, 0) % MLA_VT_ROWS == MLA_V
        vt_ref[...] = jnp.where(ones_row, 1.0, vt).astype(BF16)


def _rows_a_kernel(x_ref, fg, wg, wu, wd, mg, wdq, qn, wq, wlat, kvn, wkr, tab, *refs,
                   heads, c_dim, absorbed, kr_t=False):
    n_extra = 1 if absorbed else 2
    x1_ref = refs[n_extra]
    x1 = _ffn_apply(x_ref[...], fg, wg, wu, wd)
    x1_ref[...] = x1
    _mla_proj_apply(x1, mg, wdq, qn, wq, wlat, kvn, wkr, tab, refs[:n_extra] + refs[n_extra + 1:],
                    heads, c_dim, absorbed, kr_t)


def _rows_b_kernel(a_ref, *refs, dec_heads, c_dim, tiled_kv):
    if dec_heads:
        wuv_ref, refs = refs[0], refs[1:]
        a = jnp.concatenate([_dot(a_ref[:, hh * c_dim:(hh + 1) * c_dim], wuv_ref[hh]).astype(BF16)
                             for hh in range(dec_heads)], axis=1)
    else:
        a = a_ref[...]
    wo, x_ref, fg, wg, wu, wd, kvg, wkv = refs[:8]
    x = _ffn_apply(x_ref[...] + _dot(a, wo[...]), fg, wg, wu, wd)
    kv = _dot(_rms(x, kvg[...]).astype(BF16), wkv[...])
    if tiled_kv:
        repk, eye, x_out, kv_out, k2_out, vt_out = refs[8:]
        half = kv.shape[1] // 2
        k2_out[...] = _dot(kv[:, :half].astype(BF16), repk[...]).astype(BF16)
        vt_out[...] = _dot_nt(eye[...], kv[:, half:].astype(BF16)).astype(BF16)
    else:
        x_out, kv_out = refs[8:]
    x_out[...] = x
    kv_out[...] = kv


def _rows_c_kernel(x_ref, fg, wg, wu, wd, mg, wq, x_out, q_out, *, scale, transposed):
    x = _ffn_apply(x_ref[...], fg, wg, wu, wd)
    x_out[...] = x
    h = _rms(x, mg[...]).astype(BF16)
    q = _dot_nt(wq[...], h) if transposed else _dot(h, wq[...])
    q_out[...] = (q * scale).astype(BF16)


def _rows_d_kernel(a_ref, wo, x_ref, fg, wg, wu, wd, fin_g, y_out):
    x = _ffn_apply(x_ref[...] + _dot(a_ref[...], wo[...]), fg, wg, wu, wd)
    y_out[...] = _rms(x, fin_g[...])


def _rows_c_both_kernel(xp_ref, xd_ref, fg, wg, wu, wd, mg, wq_t, wq, xp_out, qt_out, xd_out, qd_out, *, scale, n_p):
    i = pl.program_id(0)

    @pl.when(i < n_p)
    def _():
        _rows_c_kernel(xp_ref, fg, wg, wu, wd, mg, wq_t, xp_out, qt_out, scale=scale, transposed=True)

    @pl.when(i >= n_p)
    def _():
        _rows_c_kernel(xd_ref, fg, wg, wu, wd, mg, wq, xd_out, qd_out, scale=scale, transposed=False)


def _rows_d_both_kernel(ap_ref, xp_ref, ad_ref, xd_ref, wo, fg, wg, wu, wd, fin_g, yp_out, yd_out, *, n_p):
    i = pl.program_id(0)

    @pl.when(i < n_p)
    def _():
        _rows_d_kernel(ap_ref, wo, xp_ref, fg, wg, wu, wd, fin_g, yp_out)

    @pl.when(i >= n_p)
    def _():
        _rows_d_kernel(ad_ref, wo, xd_ref, fg, wg, wu, wd, fin_g, yd_out)


def _mla_attn_kernel(q_ref, k_ref, vt_ref, o_ref, m_sc, acc_sc, sa_sc, sb_sc, *, bq, bk, hp):
    i = pl.program_id(2)
    qs = [q_ref[:, hh * LANES:(hh + 1) * LANES] for hh in range(hp)]
    sls = [slice(hh * LANES, (hh + 1) * LANES) for hh in range(hp)]

    def own_scores(hh, start, width, q_lo):
        s = _dot_nt(k_ref[pl.ds(start, width), sls[hh]], qs[hh][q_lo:])
        kc = (start + lax.broadcasted_iota(jnp.int32, (width, 1), 0)) // CHUNK
        qc = (i * bq + q_lo + lax.broadcasted_iota(jnp.int32, (1, bq - q_lo), 1)) // CHUNK
        return jnp.where(kc <= qc, s, NEG_INF)

    def own_update(ss, start, width, q_lo, after_head=None):
        ql = slice(q_lo, bq)
        for hh in range(hp):
            m = m_sc[hh, :, ql]
            m_new = jnp.maximum(m, jnp.max(ss[hh], axis=0, keepdims=True))
            alpha = jnp.exp2(m - m_new)
            p = jnp.exp2(ss[hh] - m_new).astype(BF16)
            vt = vt_ref[hh * MLA_VT_ROWS:(hh + 1) * MLA_VT_ROWS, pl.ds(start, width)]
            acc_sc[hh, :, ql] = alpha * acc_sc[hh, :, ql] + _dot(vt, p)
            m_sc[hh, :, ql] = m_new
            if after_head is not None:
                after_head(hh)

    def produce(hh, j, dst):
        dst[hh] = _dot_nt(k_ref[pl.ds(pl.multiple_of(j * bk, bk), bk), sls[hh]], qs[hh])

    def consume(hh, j, src):
        s = src[hh]
        m = m_sc[hh]
        m_new = jnp.maximum(m, jnp.max(s, axis=0, keepdims=True))
        alpha = jnp.exp2(m - m_new)
        p = jnp.exp2(s - m_new).astype(BF16)
        vt = vt_ref[hh * MLA_VT_ROWS:(hh + 1) * MLA_VT_ROWS, pl.ds(pl.multiple_of(j * bk, bk), bk)]
        acc_sc[hh] = alpha * acc_sc[hh] + _dot(vt, p)
        m_sc[hh] = m_new

    def transition(j, src, dst):
        produce(0, j + 1, dst)
        for hh in range(hp):
            consume(hh, j, src)
            if hh + 1 < hp:
                produce(hh + 1, j + 1, dst)

    m_sc[...] = jnp.full(m_sc.shape, NEG_INF, F32)
    acc_sc[...] = jnp.zeros(acc_sc.shape, F32)
    n_full = (i * bq) // bk

    diag = min(MLA_DIAG, bq)
    n_diag = bq // diag
    starts = [pl.multiple_of(i * bq + t * diag, diag) for t in range(n_diag)]
    own = [[own_scores(hh, starts[0], diag, 0) for hh in range(hp)]] + [[None] * hp for _ in range(n_diag - 1)]
    for t in range(n_diag):
        def after_head(hh, t=t):
            if t == 0:
                produce(hh, 0, sa_sc)
            if t + 1 < n_diag:
                own[t + 1][hh] = own_scores(hh, starts[t + 1], diag, (t + 1) * diag)

        own_update(own[t], starts[t], diag, t * diag, after_head=after_head)

    @pl.when(n_full > 0)
    def _():
        last = n_full - 1

        @pl.loop(0, last // 2)
        def _(t):
            transition(2 * t, sa_sc, sb_sc)
            transition(2 * t + 1, sb_sc, sa_sc)

        @pl.when(last % 2 == 1)
        def _():
            transition(last - 1, sa_sc, sb_sc)
            for hh in range(hp):
                consume(hh, last, sb_sc)

        @pl.when(last % 2 == 0)
        def _():
            for hh in range(hp):
                consume(hh, last, sa_sc)

    for pair in range(hp // 2):
        halves = []
        for hh in (2 * pair, 2 * pair + 1):
            acc = acc_sc[hh]
            halves.append(acc[:MLA_V] * (1.0 / acc[MLA_V:MLA_V + 1]))
        o_ref[:, pair * LANES:(pair + 1) * LANES] = jnp.concatenate(halves, axis=0).T.astype(BF16)


def _mla_attn_prompt(q, k, vt, batch, seq, heads, bq=MLA_BQ, bk=MLA_BK, hp=MLA_HEADS_PER_STEP):
    bq, bk = min(bq, seq), min(bk, seq)
    nq = seq // bq
    assert seq % bq == 0 and seq % bk == 0 and bq % CHUNK == 0 and heads % hp == 0 and hp % 2 == 0
    assert bq % bk == 0 and bq % min(MLA_DIAG, bq) == 0 and MLA_DIAG % CHUNK == 0
    w = hp * LANES
    return pl.pallas_call(
        functools.partial(_mla_attn_kernel, bq=bq, bk=bk, hp=hp),
        grid=(batch, heads // hp, nq),
        in_specs=[pl.BlockSpec((bq, w), lambda b, h, i: (b * nq + i, h)),
                  pl.BlockSpec((seq, w), lambda b, h, i: (b, h)),
                  pl.BlockSpec((hp * MLA_VT_ROWS, seq), lambda b, h, i: (h, b))],
        out_specs=pl.BlockSpec((bq, hp * MLA_V), lambda b, h, i: (b * nq + i, h)),
        out_shape=jax.ShapeDtypeStruct((q.shape[0], heads * MLA_V), BF16),
        scratch_shapes=[pltpu.VMEM((hp, 1, bq), F32), pltpu.VMEM((hp, MLA_VT_ROWS, bq), F32),
                        pltpu.VMEM((hp, bk, bq), F32), pltpu.VMEM((hp, bk, bq), F32)],
        compiler_params=_params(3), name="mla_attn_prompt",
    )(q, k, vt)


def _mla_attn_decode_kernel(ql_ref, q_ref, cl_ref, crt_ref, nl_ref, nr_ref, o_ref, *, heads, c_dim, n_new):
    nb = cl_ref.shape[0]
    past = cl_ref.shape[1]
    chunk = min(past, MLA_DECODE_CHUNK)
    n_parts = past // chunk + 1
    rows = [slice(bb * n_new, (bb + 1) * n_new) for bb in range(nb)]
    qls = [jnp.concatenate([ql_ref[rows[bb], hh * c_dim:(hh + 1) * c_dim] for hh in range(heads)], axis=0)
           for bb in range(nb)]
    qrs = [jnp.concatenate([q_ref[rows[bb], hh * LANES + MLA_NOPE:hh * LANES + MLA_NOPE + MLA_ROPE]
                            for hh in range(heads)], axis=0) for bb in range(nb)]

    def scores(bb, c):
        if c + 1 < n_parts:
            kl = cl_ref[bb, c * chunk:(c + 1) * chunk, :].astype(BF16)
            return kl, _dot_nt(qls[bb], kl) + _dot(qrs[bb], crt_ref[bb, :, c * chunk:(c + 1) * chunk].astype(BF16))
        kl = nl_ref[rows[bb], :].astype(BF16)
        return kl, _dot_nt(qls[bb], kl) + _dot_nt(qrs[bb], nr_ref[rows[bb], :].astype(BF16))

    state = [(jnp.full((heads * n_new, 1), NEG_INF, F32), jnp.zeros((heads * n_new, 1), F32),
              jnp.zeros((heads * n_new, c_dim), F32)) for _ in range(nb)]
    nxt = [scores(bb, 0) for bb in range(nb)]
    for c in range(n_parts):
        for bb in range(nb):
            kl, s = nxt[bb]
            if c + 1 < n_parts:
                nxt[bb] = scores(bb, c + 1)
            m, l, acc = state[bb]
            m_new = jnp.maximum(m, jnp.max(s, axis=-1, keepdims=True))
            alpha = jnp.exp2(m - m_new)
            p = jnp.exp2(s - m_new)
            l = alpha * l + jnp.sum(p, axis=-1, keepdims=True)
            acc = alpha * acc + _dot(p.astype(BF16), kl)
            state[bb] = (m_new, l, acc)
    for bb in range(nb):
        _, l, acc = state[bb]
        o = acc / l
        for hh in range(heads):
            o_ref[rows[bb], hh * c_dim:(hh + 1) * c_dim] = o[hh * n_new:(hh + 1) * n_new].astype(BF16)


def _mla_attn_decode(ql, q, cache_lat, cache_rope_t, new_lat, new_rope, heads, n_new, nb=MLA_DECODE_BATCH):
    _, batch, past, c_dim = cache_lat.shape
    r_dim = cache_rope_t.shape[2]
    nb = math.gcd(nb, batch)
    return pl.pallas_call(
        functools.partial(_mla_attn_decode_kernel, heads=heads, c_dim=c_dim, n_new=n_new),
        grid=(batch // nb,),
        in_specs=[pl.BlockSpec((nb * n_new, heads * c_dim), lambda b: (b, 0)),
                  pl.BlockSpec((nb * n_new, heads * LANES), lambda b: (b, 0)),
                  pl.BlockSpec((None, nb, past, c_dim), lambda b: (0, b, 0, 0)),
                  pl.BlockSpec((None, nb, r_dim, past), lambda b: (0, b, 0, 0)),
                  pl.BlockSpec((nb * n_new, c_dim), lambda b: (b, 0)),
                  pl.BlockSpec((nb * n_new, r_dim), lambda b: (b, 0))],
        out_specs=pl.BlockSpec((nb * n_new, heads * c_dim), lambda b: (b, 0)),
        out_shape=jax.ShapeDtypeStruct(ql.shape, BF16),
        compiler_params=_params(1), name="mla_attn_decode",
    )(ql, q, cache_lat, cache_rope_t, new_lat, new_rope)


def _bias_kernel(tab_ref, idx_ref, o_ref, *, heads):
    idx = idx_ref[...]
    for hh in range(heads):
        acc = jnp.zeros(idx.shape, F32)
        for b in range(N_BUCKETS):
            acc = jnp.where(idx == b, tab_ref[b, hh] * LOG2E, acc)
        o_ref[hh] = acc


def _bias_table(rel_bias, idx):
    heads = rel_bias.shape[1]
    return pl.pallas_call(
        functools.partial(_bias_kernel, heads=heads),
        in_specs=[pl.BlockSpec(memory_space=pltpu.SMEM), pl.BlockSpec(idx.shape, lambda: (0, 0))],
        out_specs=pl.BlockSpec((heads,) + idx.shape, lambda: (0, 0, 0)),
        out_shape=jax.ShapeDtypeStruct((heads,) + idx.shape, F32), name="rel_bias_table",
    )(rel_bias, idx)


def _rel_bucket(rel):
    half = N_BUCKETS // 2
    max_exact = half // 2
    base = jnp.where(rel > 0, half, 0)
    n = jnp.abs(rel)
    nf = jnp.maximum(n, 1).astype(jnp.float32)
    large = max_exact + (jnp.log(nf / max_exact) / math.log(MAX_DISTANCE / max_exact)
                         * (half - max_exact)).astype(jnp.int32)
    large = jnp.minimum(large, half - 1)
    return base + jnp.where(n < max_exact, n, large)


def _swa_prompt_kernel(sink_ref, qt_ref, kp_ref, kc_ref, vtp_ref, vtc_ref, bias_ref, o_ref, *, s_heads, rep, blk, nsub):
    i = pl.program_id(1)
    key_chunk = lax.broadcasted_iota(jnp.int32, (2 * blk, 1), 0) // CHUNK
    row_chunk = lax.broadcasted_iota(jnp.int32, (1, blk), 1) // CHUNK
    first = blk // CHUNK
    window = (key_chunk >= row_chunk) & (key_chunk <= row_chunk + WINDOW_CHUNKS)
    dh = SWA_HEAD_DIM
    units = [(u, hh) for u in range(nsub) for hh in range(s_heads)]
    ss = []
    for u, hh in units:
        g, pair, slot = hh // rep, hh // 2, hh % 2
        ksl = slice(g * 4 * dh + slot * 2 * dh, g * 4 * dh + (slot + 1) * 2 * dh)
        k = jnp.concatenate([kp_ref[:, ksl], kc_ref[:, ksl]], axis=0)[u * blk:(u + 2) * blk]
        ss.append(_dot(k, qt_ref[pair * 2 * dh:(pair + 1) * 2 * dh, u * blk:(u + 1) * blk]))
    ps, invs = [], []
    for (u, hh), s in zip(units, ss):
        valid = window if u > 0 else window & ((i > 0) | (key_chunk >= first))
        s = jnp.where(valid, s + bias_ref[hh], NEG_INF)
        sink = sink_ref[hh] * LOG2E
        m = jnp.maximum(jnp.max(s, axis=0, keepdims=True), sink)
        p = jnp.exp2(s - m)
        invs.append(1.0 / (jnp.sum(p, axis=0, keepdims=True) + jnp.exp2(sink - m)))
        ps.append(p.astype(BF16))
    outs = []
    for n, (u, hh) in enumerate(units):
        g = hh // rep
        vt = jnp.concatenate([vtp_ref[g * dh:(g + 1) * dh, :], vtc_ref[g * dh:(g + 1) * dh, :]],
                             axis=1)[:, u * blk:(u + 2) * blk]
        outs.append(_dot(vt, ps[n]) * invs[n])
    for u in range(nsub):
        for pair in range(s_heads // 2):
            n = u * s_heads + 2 * pair
            o_ref[u * blk:(u + 1) * blk, pair * 2 * dh:(pair + 1) * 2 * dh] = (
                jnp.concatenate(outs[n:n + 2], axis=0).T.astype(BF16))


def _swa_prompt(qt, k2, vt, bias_t, sinks, batch, seq, s_heads, rep, blk=128, nsub=SWA_BLOCKS_PER_STEP):
    assert seq % (blk * nsub) == 0
    nq = seq // (blk * nsub)
    kw = k2.shape[1]
    cur = lambda b, i: (b * nq + i, 0)
    prev = lambda b, i: ((b * nq + i) * nsub - jnp.minimum(i, 1), 0)
    cur_t = lambda b, i: (0, b * nq + i)
    prev_t = lambda b, i: (0, (b * nq + i) * nsub - jnp.minimum(i, 1))
    assert rep == 4 and kw == (s_heads // rep) * 4 * SWA_HEAD_DIM
    return pl.pallas_call(
        functools.partial(_swa_prompt_kernel, s_heads=s_heads, rep=rep, blk=blk, nsub=nsub),
        grid=(batch, nq),
        in_specs=[pl.BlockSpec(memory_space=pltpu.SMEM),
                  pl.BlockSpec((qt.shape[0], blk * nsub), cur_t),
                  pl.BlockSpec((blk, kw), prev), pl.BlockSpec((blk * nsub, kw), cur),
                  pl.BlockSpec((vt.shape[0], blk), prev_t), pl.BlockSpec((vt.shape[0], blk * nsub), cur_t),
                  pl.BlockSpec(bias_t.shape, lambda b, i: (0, 0, 0), pipeline_mode=pl.Buffered(1))],
        out_specs=pl.BlockSpec((blk * nsub, qt.shape[0]), cur),
        out_shape=jax.ShapeDtypeStruct((qt.shape[1], qt.shape[0]), BF16),
        compiler_params=_params(2), name="swa_prompt",
    )(sinks, qt, k2, k2, vt, vt, bias_t)


def _swa_decode_kernel(q_ref, ck_ref, cv_ref, nkv_ref, rep_ref, bias_ref, sink_ref, o_ref, *, kv_w, groups, rep, n_new):
    nb = ck_ref.shape[0]
    n = n_new
    gw = rep * SWA_HEAD_DIM
    lane_head = lax.broadcasted_iota(jnp.int32, (1, gw), 1) // SWA_HEAD_DIM
    sls = [slice(g * gw, (g + 1) * gw) for g in range(groups)]
    units = [(bb, g) for bb in range(nb) for g in range(groups)]
    k4s, v4s = [], []
    for bb in range(nb):
        new = nkv_ref[bb * n:(bb + 1) * n, :]
        k = jnp.concatenate([ck_ref[bb], new[:, 0:kv_w]], axis=0).astype(BF16)
        v = jnp.concatenate([cv_ref[bb], new[:, kv_w:2 * kv_w]], axis=0).astype(BF16)
        k4s.append(_dot(k, rep_ref[...]).astype(BF16))
        v4s.append(_dot(v, rep_ref[...]).astype(BF16))
    ss = []
    for bb, g in units:
        qg = q_ref[bb * n:(bb + 1) * n, sls[g]]
        qs = jnp.concatenate([jnp.where(lane_head == r, qg, jnp.zeros_like(qg)) for r in range(rep)], axis=0)
        ss.append(_dot_nt(qs, k4s[bb][:, sls[g]]))
    ps = []
    for (bb, g), s in zip(units, ss):
        rows = slice(g * rep * n, (g + 1) * rep * n)
        s = s + bias_ref[rows, :]
        sink = sink_ref[rows, :] * LOG2E
        m = jnp.maximum(jnp.max(s, axis=-1, keepdims=True), sink)
        p = jnp.exp2(s - m)
        inv = 1.0 / (jnp.sum(p, axis=-1, keepdims=True) + jnp.exp2(sink - m))
        ps.append((p * inv).astype(BF16))
    for (bb, g), p in zip(units, ps):
        res = _dot(p, v4s[bb][:, sls[g]])
        og = jnp.zeros((n, gw), F32)
        for r in range(rep):
            og = jnp.where(lane_head == r, res[r * n:(r + 1) * n], og)
        o_ref[bb * n:(bb + 1) * n, sls[g]] = og.astype(BF16)


def _rope_table(pos):
    inv = ROPE_BASE ** (-jnp.arange(0, MLA_ROPE, 2, dtype=jnp.float32) / MLA_ROPE)
    ang = pos.astype(jnp.float32)[:, None] * inv[None, :]
    cos, sin = jnp.cos(ang), jnp.sin(ang)
    widths = ((0, 0), (MLA_NOPE, LANES - MLA_NOPE - MLA_ROPE))
    return jnp.concatenate([jnp.pad(jnp.concatenate([cos, cos], axis=1), widths),
                            jnp.pad(jnp.concatenate([-sin, sin], axis=1), widths)], axis=1)


def _prep_weights(mla_w_dq, mla_w_uq, mla_w_dkv, mla_w_uk, mla_w_uv, mla_w_o, w_kv_shared, swa_w_q, swa_w_o, scale):
    ql, qcols = mla_w_uq.shape
    c_dim, heads, nope = mla_w_uk.shape
    rope = qcols // heads - nope
    d = mla_w_dq.shape[0]
    pad = LANES - nope - rope
    w = {}
    w["wq"] = jnp.pad(mla_w_uq.reshape(ql, heads, nope + rope), ((0, 0), (0, 0), (0, pad))).reshape(
        ql, heads * LANES) * scale
    w["wdq"] = mla_w_dq
    w["wlat"] = mla_w_dkv[:, :c_dim]
    w["wkr"] = jnp.pad(mla_w_dkv[:, c_dim:], ((0, 0), (nope, pad)))
    zc = jnp.zeros((c_dim, heads, LANES - nope), F32)
    w["wuk"] = jnp.concatenate([mla_w_uk, zc], axis=-1).reshape(c_dim, heads * LANES)
    w["wuv_t"] = jnp.pad(mla_w_uv, ((0, 0), (0, 0), (0, MLA_VT_ROWS - mla_w_uv.shape[2]))).reshape(
        c_dim, heads * MLA_VT_ROWS).T
    w["wuk_dec"] = jnp.concatenate([jnp.transpose(mla_w_uk, (1, 2, 0)),
                                    jnp.zeros((heads, LANES - nope, c_dim), F32)], axis=1)
    w["wuv_dec"] = jnp.concatenate([jnp.transpose(mla_w_uv, (1, 0, 2)),
                                    jnp.zeros((heads, c_dim, LANES - nope), F32)], axis=-1)
    vdim = mla_w_uv.shape[2]
    assert vdim == MLA_V and nope == MLA_NOPE and rope == MLA_ROPE
    w["wo"] = mla_w_o
    w["wo_pad"] = jnp.concatenate([mla_w_o.reshape(heads, vdim, d), jnp.zeros((heads, LANES - vdim, d), F32)],
                                  axis=1).reshape(heads * LANES, d)
    kvw = w_kv_shared.shape[1] // 2
    groups = kvw // SWA_HEAD_DIM
    s_heads = swa_w_q.shape[1] // SWA_HEAD_DIM
    rep = s_heads // groups
    w["wkv"] = w_kv_shared
    eye = jnp.eye(SWA_HEAD_DIM, dtype=F32)
    zero = jnp.zeros_like(eye)
    w["rep_k2"] = jnp.kron(jnp.eye(groups, dtype=F32), jnp.concatenate([eye, zero, zero, eye], axis=1))
    w["eye_kv"] = jnp.eye(kvw, dtype=F32)
    w["swa_wq_t"] = swa_w_q.T
    w["swa_wo"] = swa_w_o
    w["rep_kv"] = jnp.kron(jnp.eye(groups, dtype=F32), jnp.tile(eye, (1, rep)))
    w["swa_wq"] = swa_w_q
    return {k: v.astype(BF16) for k, v in w.items()}, dict(heads=heads, c_dim=c_dim, groups=groups, rep=rep,
                                                            s_heads=s_heads, kvw=kvw)


def _rows_a(x, P, w, tab, heads, c_dim, absorbed, seq=None):
    t, d = x.shape
    tm = _row_tile(t)
    hw = heads * LANES
    row = lambda v: v.reshape(1, -1)
    ins = ([_rows(x, tm)] + _ffn_ins(P, w, 1, 0)
           + [_const(row(P["mix_norm"][0])), _const(w["wdq"]), _const(row(P["mla_q_norm"][0])), _const(w["wq"]),
              _const(w["wlat"]), _const(row(P["mla_kv_norm"][0])), _const(w["wkr"]), _cycle(tab, tm)]
           + ([_const(w["wuk_dec"])] if absorbed else [_const(w["wuk"]), _const(w["wuv_t"])]))
    outs = [_rows_out(t, tm, d, F32), _rows_out(t, tm, hw, BF16)]
    if absorbed:
        outs.append(_rows_out(t, tm, heads * c_dim, BF16))
    else:
        outs += [_rows_out(t, tm, hw, BF16),
                 (jax.ShapeDtypeStruct((heads * MLA_VT_ROWS, t), BF16),
                  pl.BlockSpec((heads * MLA_VT_ROWS, tm), lambda i: (0, i)))]
    kr_t = seq is not None and seq % tm == 0
    if kr_t:
        per = seq // tm
        kr_out = (jax.ShapeDtypeStruct((t // seq, MLA_ROPE, seq), F32),
                  pl.BlockSpec((None, MLA_ROPE, tm), lambda i: (i // per, 0, i % per)))
    else:
        kr_out = _rows_out(t, tm, MLA_ROPE, F32)
    outs += [_rows_out(t, tm, c_dim, F32), kr_out]
    return _rows_call(functools.partial(_rows_a_kernel, heads=heads, c_dim=c_dim, absorbed=absorbed, kr_t=kr_t),
                      "rows_a_absorbed" if absorbed else "rows_a", t, tm, ins, outs)


def _rows_b(a, x, P, w, meta, decode):
    t, d = x.shape
    tm = _row_tile(t)
    kvw = meta["kvw"]
    ins = [_rows(a, tm)]
    if decode:
        ins += [_const(w["wuv_dec"]), _const(w["wo_pad"])]
    else:
        ins += [_const(w["wo"])]
    ins += [_rows(x, tm)] + _ffn_ins(P, w, 2, 0) + [_const(P["kv_norm"].reshape(1, -1)), _const(w["wkv"])]
    outs = [_rows_out(t, tm, d, F32), _rows_out(t, tm, 2 * kvw, F32)]
    if not decode:
        ins += [_const(w["rep_k2"]), _const(w["eye_kv"])]
        outs += [_rows_out(t, tm, w["rep_k2"].shape[1], BF16),
                 (jax.ShapeDtypeStruct((kvw, t), BF16), pl.BlockSpec((kvw, tm), lambda i: (0, i)))]
    return _rows_call(functools.partial(_rows_b_kernel, dec_heads=meta["heads"] if decode else 0,
                                        c_dim=meta["c_dim"], tiled_kv=not decode),
                      "rows_b_decode" if decode else "rows_b", t, tm, ins, outs)


def _rows_c_both(xp, xd, P, w):
    (tp, d), td = xp.shape, xd.shape[0]
    tm = _row_tile(tp)
    assert td % tm == 0
    n_p, n_d = tp // tm, td // tm
    first, second = _two_streams(n_p)
    wq_t, wq = w["swa_wq_t"], w["swa_wq"]
    ins = ([_rows(xp, tm, first), _rows(xd, tm, second)] + _ffn_ins(P, w, 1, 1)
           + [_const(P["mix_norm"][1].reshape(1, -1)), _const(wq_t), _const(wq)])
    outs = [_rows_out(tp, tm, d, F32, first),
            (jax.ShapeDtypeStruct((wq_t.shape[0], tp), BF16), pl.BlockSpec((wq_t.shape[0], tm), lambda i: (0, first(i)))),
            _rows_out(td, tm, d, F32, second), _rows_out(td, tm, wq.shape[1], BF16, second)]
    return _rows_call(functools.partial(_rows_c_both_kernel, scale=SWA_HEAD_DIM ** -0.5 * LOG2E, n_p=n_p),
                      "rows_c", tp, tm, ins, outs, steps=n_p + n_d)


def _rows_d_both(ap, xp, ad, xd, P, w):
    (tp, d), td = xp.shape, xd.shape[0]
    tm = _row_tile(tp)
    assert td % tm == 0
    n_p, n_d = tp // tm, td // tm
    first, second = _two_streams(n_p)
    ins = ([_rows(ap, tm, first), _rows(xp, tm, first), _rows(ad, tm, second), _rows(xd, tm, second),
            _const(w["swa_wo"])] + _ffn_ins(P, w, 2, 1) + [_const(P["final_norm"].reshape(1, -1))])
    outs = [_rows_out(tp, tm, d, F32, first), _rows_out(td, tm, d, F32, second)]
    return _rows_call(functools.partial(_rows_d_both_kernel, n_p=n_p), "rows_d", tp, tm, ins, outs, steps=n_p + n_d)


def _prompt_front(x3, P, w, meta):
    batch, seq, d = x3.shape
    heads, c_dim = meta["heads"], meta["c_dim"]
    x = x3.reshape(batch * seq, d)
    x, q, k, vt, lat, kr = _rows_a(x, P, w, _rope_table(jnp.arange(seq)), heads, c_dim, absorbed=False, seq=seq)
    if kr.ndim == 3:
        kr = jnp.swapaxes(kr, 1, 2)
    o = _mla_attn_prompt(q, k, vt, batch, seq, heads)
    x, kv, k2, v_t = _rows_b(o, x, P, w, meta, decode=False)
    return x, kv, k2, v_t, lat, kr


def _decode_front(x3, cache_lat, cache_rope, cache_k, P, w, meta):
    batch, n_new, d = x3.shape
    heads, c_dim = meta["heads"], meta["c_dim"]
    past = cache_lat.shape[2]
    w_c = cache_k.shape[1]
    assert past % CHUNK == 0 and n_new <= CHUNK and w_c <= WINDOW_CHUNKS * CHUNK and w_c <= past
    t = batch * n_new
    x = x3.reshape(t, d)
    tm = _row_tile(t)
    assert tm % n_new == 0
    tab = jnp.tile(_rope_table(past + jnp.arange(n_new)), (tm // n_new, 1))
    x, q, ql, lat, kr = _rows_a(x, P, w, tab, heads, c_dim, absorbed=True)
    ol = _mla_attn_decode(ql, q, cache_lat, jnp.swapaxes(cache_rope, 2, 3), lat, kr, heads, n_new)
    x, kv = _rows_b(ol, x, P, w, meta, decode=True)
    return x, kv, lat, kr


def _swa_prompt_attn(q_t, k2, v_t, P, meta, batch, seq):
    blk = 2 * CHUNK
    rel = (jnp.arange(2 * blk) - blk)[:, None] - jnp.arange(blk)[None, :]
    bias_t = _bias_table(P["rel_bias"], _rel_bucket(rel).astype(jnp.int32))
    return _swa_prompt(q_t, k2, v_t, bias_t, P["swa_sinks"][0], batch, seq, meta["s_heads"], meta["rep"], blk)


def _swa_decode_attn(qs, kv, cache_k, cache_v, past, P, w, meta, batch, n_new):
    kvw, s_heads = meta["kvw"], meta["s_heads"]
    w_c = cache_k.shape[1]
    qpos = past + jnp.arange(n_new)
    kpos = jnp.arange(past - w_c, past + n_new)
    bias_h = _bias_table(P["rel_bias"], _rel_bucket(kpos[None, :] - qpos[:, None]).astype(jnp.int32))
    bias = bias_h.reshape(s_heads * n_new, w_c + n_new)
    sink_rows = jnp.repeat(P["swa_sinks"][0], n_new).reshape(s_heads * n_new, 1)
    const2 = lambda a: pl.BlockSpec(a.shape, lambda b: (0, 0), pipeline_mode=pl.Buffered(1))
    nb = math.gcd(SWA_DECODE_BATCH, batch)
    return pl.pallas_call(
        functools.partial(_swa_decode_kernel, kv_w=kvw, groups=meta["groups"], rep=meta["rep"], n_new=n_new),
        grid=(batch // nb,),
        in_specs=[pl.BlockSpec((nb * n_new, qs.shape[1]), lambda b: (b, 0)),
                  pl.BlockSpec((nb, w_c, kvw), lambda b: (b, 0, 0)),
                  pl.BlockSpec((nb, w_c, kvw), lambda b: (b, 0, 0)),
                  pl.BlockSpec((nb * n_new, 2 * kvw), lambda b: (b, 0)),
                  const2(w["rep_kv"]), const2(bias), const2(sink_rows)],
        out_specs=pl.BlockSpec((nb * n_new, qs.shape[1]), lambda b: (b, 0)),
        out_shape=jax.ShapeDtypeStruct(qs.shape, BF16),
        compiler_params=_params(1), name="swa_decode",
    )(qs, cache_k.reshape(batch, w_c, kvw), cache_v.reshape(batch, w_c, kvw), kv, w["rep_kv"], bias, sink_rows)


def _new_kv(kv, batch, rows, keep, meta):
    kvw = meta["kvw"]
    kv3 = kv.reshape(batch, rows, 2 * kvw)[:, rows - keep:]
    return (kv3[:, :, :kvw].reshape(batch, keep, meta["groups"], SWA_HEAD_DIM),
            kv3[:, :, kvw:].reshape(batch, keep, meta["groups"], SWA_HEAD_DIM))


def kernel(x_prompt, x_sample, cache_mla_latent, cache_mla_krope, cache_swa_k, cache_swa_v, ffn_norm1, ffn1_w_gate, ffn1_w_up, ffn1_w_down, mix_norm, ffn_norm2, ffn2_w_gate, ffn2_w_up, ffn2_w_down, mla_w_dq, mla_q_norm, mla_w_uq, mla_w_dkv, mla_kv_norm, mla_w_uk, mla_w_uv, mla_w_o, kv_norm, w_kv_shared, swa_w_q, swa_sinks, swa_w_o, rel_bias, final_norm):
    assert ffn_norm1.shape[0] == 2 and mla_w_dq.shape[0] == 1 and swa_w_q.shape[0] == 1
    scale = (MLA_NOPE + MLA_ROPE) ** -0.5 * LOG2E
    w, meta = _prep_weights(mla_w_dq[0], mla_w_uq[0], mla_w_dkv[0], mla_w_uk[0], mla_w_uv[0], mla_w_o[0],
                            w_kv_shared, swa_w_q[0], swa_w_o[0], scale)
    w.update(f1g=ffn1_w_gate.astype(BF16), f1u=ffn1_w_up.astype(BF16), f1d=ffn1_w_down.astype(BF16),
             f2g=ffn2_w_gate.astype(BF16), f2u=ffn2_w_up.astype(BF16), f2d=ffn2_w_down.astype(BF16))
    P = dict(ffn_norm1=ffn_norm1, mix_norm=mix_norm, ffn_norm2=ffn_norm2, mla_q_norm=mla_q_norm,
             mla_kv_norm=mla_kv_norm, kv_norm=kv_norm, swa_sinks=swa_sinks, rel_bias=rel_bias, final_norm=final_norm)
    batch, seq, d = x_prompt.shape
    dbatch, n_new, _ = x_sample.shape
    c_dim = meta["c_dim"]
    xp, kv_p, k2, v_t, lat_p, rope_p = _prompt_front(x_prompt, P, w, meta)
    xd, kv_s, lat_s, rope_s = _decode_front(x_sample, cache_mla_latent, cache_mla_krope, cache_swa_k, P, w, meta)
    xp, q_t, xd, qs = _rows_c_both(xp, xd, P, w)
    op = _swa_prompt_attn(q_t, k2, v_t, P, meta, batch, seq)
    od = _swa_decode_attn(qs, kv_s, cache_swa_k, cache_swa_v, cache_mla_latent.shape[2], P, w, meta, dbatch, n_new)
    y_p, y_s = _rows_d_both(op, xp, od, xd, P, w)
    k_p, v_p = _new_kv(kv_p, batch, seq, min(WINDOW, seq), meta)
    k_s, v_s = _new_kv(kv_s, dbatch, n_new, n_new, meta)
    return (y_p.reshape(batch, seq, d), y_s.reshape(dbatch, n_new, d),
            lat_p.reshape(1, batch, seq, c_dim), rope_p.reshape(1, batch, seq, MLA_ROPE), k_p, v_p,
            lat_s.reshape(1, dbatch, n_new, c_dim), rope_s.reshape(1, dbatch, n_new, MLA_ROPE), k_s, v_s)
```
